```python
import math
import jax, jax.numpy as jnp
from jax import lax
import numpy as np

D_MODEL = 1024
BATCH = 16
SEQ = 2048
DEPTH = 1

N_META = 16
GRID_W = 64
Q_BLOCK = 128
ROPE_THETA = 10000.0

MLA_HEADS = 4
Q_LORA = 256
KV_LORA = 128
NOPE_DIM = 128
ROPE_DIM = 64
V_DIM = 128
QK_DIM = NOPE_DIM + ROPE_DIM

GQA_HEADS = 4
GQA_KV_HEADS = 2
GQA_DIM = 128
AXIAL_HALF = GQA_DIM // 2

MLA_WIDTH = MLA_HEADS * V_DIM
GQA_WIDTH = GQA_HEADS * GQA_DIM
MIX_WIDTH = MLA_WIDTH + GQA_WIDTH

IN_SPLITS = (Q_LORA, KV_LORA, ROPE_DIM, GQA_HEADS * GQA_DIM, GQA_KV_HEADS * GQA_DIM, GQA_KV_HEADS * GQA_DIM)
IN_COLS = sum(IN_SPLITS)
IN_OFFSETS = tuple(int(o) for o in np.cumsum(IN_SPLITS)[:-1])

N_EXPERTS = 32
TOP_K = 4
D_FF = D_MODEL
SWIGLU_LIMIT = 7.0
SWIGLU_ALPHA = 1.702

RMS_EPS = 1e-6
LN_EPS = 1e-5
DEEPNORM_ALPHA = (2.0 * DEPTH) ** 0.25
DEEPNORM_BETA = (8.0 * DEPTH) ** -0.25

kernel_name = 'hymba_mla_axialgqa_moe_deepnorm_encoder'


def _rmsnorm(x, g):
    xf = x.astype(jnp.float32)
    y = xf * lax.rsqrt(jnp.mean(xf * xf, axis=-1, keepdims=True) + RMS_EPS)
    return (y * g.astype(jnp.float32)).astype(x.dtype)


def _layernorm(x, g, b):
    xf = x.astype(jnp.float32)
    mu = jnp.mean(xf, axis=-1, keepdims=True)
    var = jnp.mean(jnp.square(xf - mu), axis=-1, keepdims=True)
    y = (xf - mu) * lax.rsqrt(var + LN_EPS)
    return (y * g.astype(jnp.float32) + b.astype(jnp.float32)).astype(x.dtype)


def _rope_cos_sin(pos, dim):
    inv = ROPE_THETA ** (-jnp.arange(0, dim, 2, dtype=jnp.float32) / dim)
    ang = pos.astype(jnp.float32)[:, None] * inv[None, :]
    return jnp.cos(ang)[:, None, :], jnp.sin(ang)[:, None, :]


def _rotate(x, cos, sin):
    xf = x.astype(jnp.float32)
    x1, x2 = jnp.split(xf, 2, axis=-1)
    return jnp.concatenate([x1 * cos - x2 * sin, x1 * sin + x2 * cos], axis=-1).astype(x.dtype)


def _attend(q, k, v, scale):
    s = jnp.einsum('bqkgd,bskd->bkgqs', q, k).astype(jnp.float32) * scale
    p = jax.nn.softmax(s, axis=-1).astype(v.dtype)
    return jnp.einsum('bkgqs,bskd->bqkgd', p, v)


def _blocked_attention(q, k, v, scale):
    B, L, H, Dq = q.shape
    Hk = k.shape[2]
    G = H // Hk
    Dv = v.shape[-1]
    q = q.reshape(B, L, Hk, G, Dq)
    o_meta = _attend(q[:, :N_META], k, v, scale)
    n_blk = (L - N_META) // Q_BLOCK
    qb = q[:, N_META:].reshape(B, n_blk, Q_BLOCK, Hk, G, Dq).transpose(1, 0, 2, 3, 4, 5)
    ob = lax.map(lambda qi: _attend(qi, k, v, scale), qb)
    ob = ob.transpose(1, 0, 2, 3, 4, 5).reshape(B, L - N_META, Hk, G, Dv)
    return jnp.concatenate([o_meta, ob], axis=1).reshape(B, L, H, Dv)


def _mixer(h, pos_1d, row, col, w_in, g_q_a, w_q_b, g_kv_a, w_kv_b, g_q_gqa, g_k_gqa, g_o_mla, g_o_gqa, w_o):
    B, L, _ = h.shape
    z = jnp.einsum('bld,dc->blc', h, w_in)
    q_a, kv_a, k_pe, q_g, k_g, v_g = jnp.split(z, IN_OFFSETS, axis=-1)

    cos1, sin1 = _rope_cos_sin(pos_1d, ROPE_DIM)
    q = jnp.einsum('blr,rc->blc', _rmsnorm(q_a, g_q_a), w_q_b).reshape(B, L, MLA_HEADS, QK_DIM)
    q_m = jnp.concatenate([q[..., :NOPE_DIM], _rotate(q[..., NOPE_DIM:], cos1, sin1)], axis=-1)
    kv = jnp.einsum('blr,rc->blc', _rmsnorm(kv_a, g_kv_a), w_kv_b).reshape(B, L, MLA_HEADS, NOPE_DIM + V_DIM)
    k_rot = _rotate(k_pe[:, :, None, :], cos1, sin1)
    k_m = jnp.concatenate([kv[..., :NOPE_DIM], jnp.broadcast_to(k_rot, (B, L, MLA_HEADS, ROPE_DIM))], axis=-1)
    v_m = kv[..., NOPE_DIM:]
    o_mla = _blocked_attention(q_m, k_m, v_m, QK_DIM ** -0.5).reshape(B, L, MLA_WIDTH)

    cos_r, sin_r = _rope_cos_sin(row, AXIAL_HALF)
    cos_c, sin_c = _rope_cos_sin(col, AXIAL_HALF)

    def axial(t):
        return jnp.concatenate([_rotate(t[..., :AXIAL_HALF], cos_r, sin_r),
                                _rotate(t[..., AXIAL_HALF:], cos_c, sin_c)], axis=-1)

    qg = axial(_rmsnorm(q_g.reshape(B, L, GQA_HEADS, GQA_DIM), g_q_gqa))
    kg = axial(_rmsnorm(k_g.reshape(B, L, GQA_KV_HEADS, GQA_DIM), g_k_gqa))
    vg = v_g.reshape(B, L, GQA_KV_HEADS, GQA_DIM)
    o_gqa = _blocked_attention(qg, kg, vg, GQA_DIM ** -0.5).reshape(B, L, GQA_WIDTH)

    o = jnp.concatenate([_rmsnorm(o_mla, g_o_mla), _rmsnorm(o_gqa, g_o_gqa)], axis=-1)
    return jnp.einsum('blc,cd->bld', o, w_o)


def _clamped_swiglu(hgu):
    x_glu = jnp.minimum(hgu[..., ::2], SWIGLU_LIMIT)
    x_lin = jnp.clip(hgu[..., 1::2], -SWIGLU_LIMIT, SWIGLU_LIMIT)
    return x_glu * jax.nn.sigmoid(SWIGLU_ALPHA * x_glu) * (x_lin + 1.0)


def _moe_sequence(h, w_router, b_router, w_gate_up, b_gate_up, w_down, b_down):
    L = h.shape[0]
    logits = (jnp.einsum('ld,de->le', h, w_router) + b_router).astype(jnp.float32)
    top_val, top_idx = lax.top_k(logits, TOP_K)
    gates = jax.nn.softmax(top_val, axis=-1).astype(h.dtype)
    flat_e = top_idx.reshape(-1)
    order = jnp.argsort(flat_e)
    tok = order // TOP_K
    e_sorted = flat_e[order]
    group_sizes = jnp.bincount(flat_e, length=N_EXPERTS).astype(jnp.int32)
    xs = h[tok]
    hgu = lax.ragged_dot(xs, w_gate_up, group_sizes) + b_gate_up[e_sorted]
    act = _clamped_swiglu(hgu)
    out = lax.ragged_dot(act, w_down, group_sizes) + b_down[e_sorted]
    out = out * gates.reshape(-1)[order][:, None]
    return jax.ops.segment_sum(out, tok, num_segments=L)


def setup_inputs(seed: int = 0) -> dict:
    key = jax.random.key(seed)
    ks = jax.random.split(key, 28)
    f32 = jnp.float32

    def nrm(k, shape, scale):
        return jax.random.normal(k, shape, f32) * scale

    def gain(k, shape):
        return 1.0 + 0.05 * jax.random.normal(k, shape, f32)

    Dp = DEPTH
    return {
        'x': jax.random.normal(ks[0], (BATCH, SEQ, D_MODEL), f32),
        'meta_tokens': nrm(ks[1], (N_META, D_MODEL), 1.0),
        'ln_emb_g': gain(ks[2], (D_MODEL,)),
        'ln_emb_b': nrm(ks[3], (D_MODEL,), 0.02),
        'w_in': nrm(ks[4], (Dp, D_MODEL, IN_COLS), D_MODEL ** -0.5),
        'g_q_a': gain(ks[5], (Dp, Q_LORA)),
        'w_q_b': nrm(ks[6], (Dp, Q_LORA, MLA_HEADS * QK_DIM), Q_LORA ** -0.5),
        'g_kv_a': gain(ks[7], (Dp, KV_LORA)),
        'w_kv_b': nrm(ks[8], (Dp, KV_LORA, MLA_HEADS * (NOPE_DIM + V_DIM)), KV_LORA ** -0.5),
        'g_q_gqa': gain(ks[9], (Dp, GQA_DIM)),
        'g_k_gqa': gain(ks[10], (Dp, GQA_DIM)),
        'g_o_mla': gain(ks[11], (Dp, MLA_WIDTH)),
        'g_o_gqa': gain(ks[12], (Dp, GQA_WIDTH)),
        'w_o': nrm(ks[13], (Dp, MIX_WIDTH, D_MODEL), DEEPNORM_BETA * MIX_WIDTH ** -0.5),
        'ln1_g': gain(ks[14], (Dp, D_MODEL)),
        'ln1_b': nrm(ks[15], (Dp, D_MODEL), 0.02),
        'w_router': nrm(ks[16], (Dp, D_MODEL, N_EXPERTS), D_MODEL ** -0.5),
        'b_router': nrm(ks[17], (Dp, N_EXPERTS), 0.01),
        'w_gate_up': nrm(ks[18], (Dp, N_EXPERTS, D_MODEL, 2 * D_FF), D_MODEL ** -0.5),
        'b_gate_up': nrm(ks[19], (Dp, N_EXPERTS, 2 * D_FF), 0.02),
        'w_down': nrm(ks[20], (Dp, N_EXPERTS, D_FF, D_MODEL), DEEPNORM_BETA * D_FF ** -0.5),
        'b_down': nrm(ks[21], (Dp, N_EXPERTS, D_MODEL), 0.02),
        'ln2_g': gain(ks[22], (Dp, D_MODEL)),
        'ln2_b': nrm(ks[23], (Dp, D_MODEL), 0.02),
    }


def reference(x, meta_tokens, ln_emb_g, ln_emb_b, w_in, g_q_a, w_q_b, g_kv_a, w_kv_b,
              g_q_gqa, g_k_gqa, g_o_mla, g_o_gqa, w_o, ln1_g, ln1_b,
              w_router, b_router, w_gate_up, b_gate_up, w_down, b_down, ln2_g, ln2_b):
    B, S, D = x.shape
    meta = jnp.broadcast_to(meta_tokens.astype(x.dtype)[None], (B, N_META, D))
    h = _layernorm(jnp.concatenate([meta, x], axis=1), ln_emb_g, ln_emb_b)
    L = S + N_META

    rows = S // GRID_W
    pos_1d = jnp.arange(L, dtype=jnp.int32)
    row = jnp.concatenate([jnp.full((N_META,), -1, jnp.int32), jnp.repeat(jnp.arange(rows, dtype=jnp.int32), GRID_W)])
    col = jnp.concatenate([jnp.arange(N_META, dtype=jnp.int32), jnp.tile(jnp.arange(GRID_W, dtype=jnp.int32), rows)])

    for i in range(DEPTH):
        mix = _mixer(h, pos_1d, row, col, w_in[i], g_q_a[i], w_q_b[i], g_kv_a[i], w_kv_b[i],
                     g_q_gqa[i], g_k_gqa[i], g_o_mla[i], g_o_gqa[i], w_o[i])
        h = _layernorm(DEEPNORM_ALPHA * h + mix, ln1_g[i], ln1_b[i])
        moe = lax.map(lambda hs: _moe_sequence(hs, w_router[i], b_router[i], w_gate_up[i],
                                               b_gate_up[i], w_down[i], b_down[i]), h)
        h = _layernorm(DEEPNORM_ALPHA * h + moe, ln2_g[i], ln2_b[i])

    return h[:, N_META:]
```

```python
import functools

import jax
import jax.numpy as jnp
import numpy as np
from jax import lax
from jax.experimental import pallas as pl
from jax.experimental.pallas import tpu as pltpu

D_MODEL = 1024
N_META = 16
GRID_W = 64
ROPE_THETA = 10000.0
MLA_HEADS = 4
Q_LORA = 256
KV_LORA = 128
NOPE_DIM = 128
ROPE_DIM = 64
V_DIM = 128
QK_DIM = NOPE_DIM + ROPE_DIM
GQA_HEADS = 4
GQA_KV_HEADS = 2
GQA_DIM = 128
N_EXPERTS = 32
TOP_K = 4
D_FF = D_MODEL
SWIGLU_LIMIT = 7.0
SWIGLU_ALPHA = 1.702
RMS_EPS = 1e-6
LN_EPS = 1e-5
DEPTH = 1
DEEPNORM_ALPHA = (2.0 * DEPTH) ** 0.25

LANES = 128
META_PAD = 128
MLA_K = 2 * LANES
NEG_BIG = -1e30

ROW_TILE = 512
Q_TILE_MLA = 512
Q_TILE_GQA = 256
POS_TILE = 512
MOE_TILE = 512
DISPATCH_TILE = 512
COMBINE_TILE = 256
VMEM_LIMIT = 56 * 1024 * 1024

F32 = jnp.float32
BF16 = jnp.bfloat16
I32 = jnp.int32


def _layernorm(x, g, b):
    mu = jnp.mean(x, axis=-1, keepdims=True)
    xc = x - mu
    var = jnp.mean(xc * xc, axis=-1, keepdims=True)
    return xc * lax.rsqrt(var + LN_EPS) * g + b


def _rmsnorm(x, g):
    return x * lax.rsqrt(jnp.mean(x * x, axis=-1, keepdims=True) + RMS_EPS) * g


def _swap_halves64(x):
    w = x.shape[-1]
    lane = lax.broadcasted_iota(I32, x.shape, x.ndim - 1)
    return jnp.where((lane % 64) < 32, pltpu.roll(x, w - 32, x.ndim - 1), pltpu.roll(x, 32, x.ndim - 1))


def _rotate(x, cos, sin_signed):
    return x * cos + _swap_halves64(x) * sin_signed


def _proj_body(x_ref, lng_ref, lnb_ref, win_ref, gqa_ref, wqb_ref, gkva_ref, wkvb_ref, gqg_ref, gkg_ref,
               c1_ref, s1_ref, ca_ref, sa_ref,
               qm_ref, km_ref, vm_ref, qg_ref, kg_ref, vg_ref):
    x = x_ref[0]
    h0 = _layernorm(x, lng_ref[...], lnb_ref[...])
    z = jnp.dot(h0.astype(BF16), win_ref[...], preferred_element_type=F32)
    q_a = z[:, 0:256]
    kv_a = z[:, 256:384]
    kpe2 = z[:, 384:512]
    q_g = z[:, 512:1024]
    k_g = z[:, 1024:1280]
    v_g = z[:, 1280:1536]

    c1 = c1_ref[...]
    s1 = s1_ref[...]
    ca = ca_ref[...]
    sa = sa_ref[...]

    q = jnp.dot(_rmsnorm(q_a, gqa_ref[...]).astype(BF16), wqb_ref[...], preferred_element_type=F32)
    kv = jnp.dot(_rmsnorm(kv_a, gkva_ref[...]).astype(BF16), wkvb_ref[...], preferred_element_type=F32)
    krot = _rotate(kpe2, c1, s1)
    lane = lax.broadcasted_iota(I32, krot.shape, 1)
    scale_a = QK_DIM ** -0.5
    for c in range(MLA_HEADS // 2):
        qr = _rotate(q[:, 512 + LANES * c:512 + LANES * (c + 1)], c1, s1)
        for hh in range(2):
            h = 2 * c + hh
            slot = jnp.where((lane // 64) == hh, qr, 0.0)
            qm_ref[0, h] = (jnp.concatenate([q[:, LANES * h:LANES * (h + 1)], slot], axis=1) * scale_a).astype(BF16)
    for h in range(MLA_HEADS):
        km_ref[0, h] = jnp.concatenate([kv[:, LANES * h:LANES * (h + 1)], krot], axis=1).astype(BF16)
        vm_ref[0, h] = kv[:, 512 + LANES * h:512 + LANES * (h + 1)].astype(BF16)

    scale_b = GQA_DIM ** -0.5
    gqg = gqg_ref[...]
    gkg = gkg_ref[...]
    for h in range(GQA_HEADS):
        xh = _rmsnorm(q_g[:, LANES * h:LANES * (h + 1)], gqg)
        qg_ref[0, h] = (_rotate(xh, ca, sa) * scale_b).astype(BF16)
    for j in range(GQA_KV_HEADS):
        xh = _rmsnorm(k_g[:, LANES * j:LANES * (j + 1)], gkg)
        kg_ref[0, j] = _rotate(xh, ca, sa).astype(BF16)
        vg_ref[0, j] = v_g[:, LANES * j:LANES * (j + 1)].astype(BF16)


def _project(x3, tabs, w, tile):
    b, s, d = x3.shape
    nst = s // tile
    full = lambda shape: pl.BlockSpec(shape, lambda bi, si: (0,) * len(shape))
    tab = pl.BlockSpec((tile, LANES), lambda bi, si: (si, 0))
    hm = lambda nh, dd: pl.BlockSpec((1, nh, tile, dd), lambda bi, si: (bi, 0, si, 0))
    out_shape = [
        jax.ShapeDtypeStruct((b, MLA_HEADS, s, MLA_K), BF16),
        jax.ShapeDtypeStruct((b, MLA_HEADS, s, MLA_K), BF16),
        jax.ShapeDtypeStruct((b, MLA_HEADS, s, V_DIM), BF16),
        jax.ShapeDtypeStruct((b, GQA_HEADS, s, GQA_DIM), BF16),
        jax.ShapeDtypeStruct((b, GQA_KV_HEADS, s, GQA_DIM), BF16),
        jax.ShapeDtypeStruct((b, GQA_KV_HEADS, s, GQA_DIM), BF16),
    ]
    return pl.pallas_call(
        _proj_body,
        grid=(b, nst),
        in_specs=[
            pl.BlockSpec((1, tile, d), lambda bi, si: (bi, si, 0)),
            full((1, d)), full((1, d)),
            full(w["w_in"].shape), full((1, Q_LORA)), full(w["w_qb"].shape),
            full((1, KV_LORA)), full(w["w_kvb"].shape), full((1, GQA_DIM)), full((1, GQA_DIM)),
            tab, tab, tab, tab,
        ],
        out_specs=[hm(MLA_HEADS, MLA_K), hm(MLA_HEADS, MLA_K), hm(MLA_HEADS, V_DIM),
                   hm(GQA_HEADS, GQA_DIM), hm(GQA_KV_HEADS, GQA_DIM), hm(GQA_KV_HEADS, GQA_DIM)],
        out_shape=out_shape,
        compiler_params=pltpu.CompilerParams(dimension_semantics=("parallel", "parallel"),
                                             vmem_limit_bytes=VMEM_LIMIT),
        name="proj",
    )(x3, w["ln_emb_g"], w["ln_emb_b"], w["w_in"], w["g_q_a"], w["w_qb"], w["g_kv_a"], w["w_kvb"],
      w["g_q_gqa"], w["g_k_gqa"], *tabs)


def _softmax_pv(q, k, v, km, vm):
    nt = (((1,), (1,)), ((), ()))
    s = lax.dot_general(q, k, nt, preferred_element_type=F32)
    sm = lax.dot_general(q, km, nt, preferred_element_type=F32)
    col = lax.broadcasted_iota(I32, sm.shape, 1)
    sm = jnp.where(col < N_META, sm, NEG_BIG)
    m = jnp.maximum(jnp.max(s, axis=1, keepdims=True), jnp.max(sm, axis=1, keepdims=True))
    p = jnp.exp(s - m)
    pm = jnp.exp(sm - m)
    l = jnp.sum(p, axis=1, keepdims=True) + jnp.sum(pm, axis=1, keepdims=True)
    o = jnp.dot(p.astype(BF16), v, preferred_element_type=F32)
    o = o + jnp.dot(pm.astype(BF16), vm, preferred_element_type=F32)
    return o / l


def _mla_attn_body(q_ref, k_ref, v_ref, km_ref, vm_ref, o_ref):
    o_ref[...] = _softmax_pv(q_ref[0, 0], k_ref[0, 0], v_ref[0, 0], km_ref[0], vm_ref[0]).astype(o_ref.dtype)


def _gqa_attn_body(q_ref, k_ref, v_ref, km_ref, vm_ref, o_ref):
    tq = q_ref.shape[2]
    q = q_ref[0].reshape(2 * tq, GQA_DIM)
    o = _softmax_pv(q, k_ref[0, 0], v_ref[0, 0], km_ref[0], vm_ref[0])
    o_ref[:, 0:GQA_DIM] = o[:tq].astype(o_ref.dtype)
    o_ref[:, GQA_DIM:2 * GQA_DIM] = o[tq:].astype(o_ref.dtype)


def _mla_attention(qm, km, vm, kmeta, vmeta):
    b, h, s, dk = qm.shape
    tq = Q_TILE_MLA
    nq = s // tq
    return pl.pallas_call(
        _mla_attn_body,
        grid=(b, h, nq),
        in_specs=[
            pl.BlockSpec((1, 1, tq, dk), lambda bi, hi, qi: (bi, hi, qi, 0)),
            pl.BlockSpec((1, 1, s, dk), lambda bi, hi, qi: (bi, hi, 0, 0)),
            pl.BlockSpec((1, 1, s, V_DIM), lambda bi, hi, qi: (bi, hi, 0, 0)),
            pl.BlockSpec((1, META_PAD, dk), lambda bi, hi, qi: (hi, 0, 0)),
            pl.BlockSpec((1, META_PAD, V_DIM), lambda bi, hi, qi: (hi, 0, 0)),
        ],
        out_specs=pl.BlockSpec((tq, V_DIM), lambda bi, hi, qi: (bi * nq + qi, hi)),
        out_shape=jax.ShapeDtypeStruct((b * s, h * V_DIM), BF16),
        compiler_params=pltpu.CompilerParams(dimension_semantics=("parallel", "parallel", "parallel"),
                                             vmem_limit_bytes=VMEM_LIMIT),
        name="mla_attn",
    )(qm, km, vm, kmeta, vmeta)


def _gqa_attention(qg, kg, vg, kmeta, vmeta):
    b, h, s, d = qg.shape
    hk = kg.shape[1]
    tq = Q_TILE_GQA
    nq = s // tq
    return pl.pallas_call(
        _gqa_attn_body,
        grid=(b, hk, nq),
        in_specs=[
            pl.BlockSpec((1, 2, tq, d), lambda bi, ji, qi: (bi, ji, qi, 0)),
            pl.BlockSpec((1, 1, s, d), lambda bi, ji, qi: (bi, ji, 0, 0)),
            pl.BlockSpec((1, 1, s, d), lambda bi, ji, qi: (bi, ji, 0, 0)),
            pl.BlockSpec((1, META_PAD, d), lambda bi, ji, qi: (ji, 0, 0)),
            pl.BlockSpec((1, META_PAD, d), lambda bi, ji, qi: (ji, 0, 0)),
        ],
        out_specs=pl.BlockSpec((tq, 2 * d), lambda bi, ji, qi: (bi * nq + qi, ji)),
        out_shape=jax.ShapeDtypeStruct((b * s, h * d), BF16),
        compiler_params=pltpu.CompilerParams(dimension_semantics=("parallel", "parallel", "parallel"),
                                             vmem_limit_bytes=VMEM_LIMIT),
        name="gqa_attn",
    )(qg, kg, vg, kmeta, vmeta)


def _merge_body(om_ref, og_ref, x_ref, lng_ref, lnb_ref, gom_ref, gog_ref, wo_ref, l1g_ref, l1b_ref,
                wr_ref, br_ref, h1_ref, topi_ref, route_ref):
    t = x_ref.shape[0]
    h0 = _layernorm(x_ref[...], lng_ref[...], lnb_ref[...])
    nm = _rmsnorm(om_ref[...].astype(F32), gom_ref[...]).astype(BF16)
    ng = _rmsnorm(og_ref[...].astype(F32), gog_ref[...]).astype(BF16)
    half = nm.shape[1]
    mix = jnp.dot(nm, wo_ref[0:half, :], preferred_element_type=F32)
    mix = mix + jnp.dot(ng, wo_ref[half:2 * half, :], preferred_element_type=F32)
    h1 = _layernorm(DEEPNORM_ALPHA * h0 + mix, l1g_ref[...], l1b_ref[...])
    h1_ref[...] = h1

    logits = jnp.dot(h1, wr_ref[...], precision=lax.Precision.HIGHEST, preferred_element_type=F32) + br_ref[...]
    cur = logits.T[0:N_EXPERTS, :]
    eidx = lax.broadcasted_iota(I32, cur.shape, 0)
    vals, idxs = [], []
    for _ in range(TOP_K):
        m = jnp.max(cur, axis=0, keepdims=True)
        i = jnp.min(jnp.where(cur == m, eidx, N_EXPERTS), axis=0, keepdims=True)
        vals.append(m)
        idxs.append(i)
        cur = jnp.where(eidx == i, -jnp.inf, cur)
    ex = [jnp.exp(v - vals[0]) for v in vals]
    den = ex[0] + ex[1] + ex[2] + ex[3]
    sub = lax.broadcasted_iota(I32, (8, t), 0)
    ti = jnp.zeros((8, t), I32)
    gt = jnp.zeros((8, t), F32)
    for k in range(TOP_K):
        ti = jnp.where(sub == k, idxs[k], ti)
        gt = jnp.where(sub == k, ex[k] / den, gt)
    topi_ref[...] = ti[0:TOP_K, :]
    route_ref[...] = jnp.concatenate([gt, jnp.zeros((LANES - 8, t), F32)], axis=0).T


def _merge(o_mla, o_gqa, x2, w):
    n, d = x2.shape
    tile = ROW_TILE
    half = o_mla.shape[1]
    full = lambda shape: pl.BlockSpec(shape, lambda i: (0,) * len(shape))
    row = lambda width: pl.BlockSpec((tile, width), lambda i: (i, 0))
    return pl.pallas_call(
        _merge_body,
        grid=(n // tile,),
        in_specs=[row(half), row(half), row(d), full((1, d)), full((1, d)), full((1, half)), full((1, half)),
                  full((d, d)), full((1, d)), full((1, d)), full((d, LANES)), full((1, LANES))],
        out_specs=[row(d), pl.BlockSpec((TOP_K, tile), lambda i: (0, i)), row(LANES)],
        out_shape=[jax.ShapeDtypeStruct((n, d), F32), jax.ShapeDtypeStruct((TOP_K, n), I32),
                   jax.ShapeDtypeStruct((n, LANES), F32)],
        compiler_params=pltpu.CompilerParams(dimension_semantics=("parallel",), vmem_limit_bytes=VMEM_LIMIT),
        name="merge",
    )(o_mla, o_gqa, x2, w["ln_emb_g"], w["ln_emb_b"], w["g_o_mla"], w["g_o_gqa"], w["w_o"],
      w["ln1_g"], w["ln1_b"], w["w_router"], w["b_router"])


def _positions_body(topi_ref, pos_ref, texp_ref, nvalid_ref, ecnt_ref, eoff_ref, cnt_sc, carry_sc, off_sc, *, ntp):
    p = pl.program_id(0)
    j = pl.program_id(1)
    tl = topi_ref.shape[1]
    topi = topi_ref[...]
    eidx = lax.broadcasted_iota(I32, (N_EXPERTS, tl), 0)
    ohs = [eidx == topi[k:k + 1, :] for k in range(TOP_K)]
    onehot = ohs[0].astype(F32) + ohs[1].astype(F32) + ohs[2].astype(F32) + ohs[3].astype(F32)
    tile_cnt = jnp.broadcast_to(jnp.sum(onehot, axis=1, keepdims=True), (N_EXPERTS, LANES))

    @pl.when(jnp.logical_and(p == 0, j == 0))
    def _():
        cnt_sc[...] = jnp.zeros_like(cnt_sc)

    @pl.when(p == 0)
    def _():
        cnt_sc[...] += tile_cnt

    @pl.when(jnp.logical_and(p == 1, j == 0))
    def _():
        cnt = cnt_sc[...]
        pc = jnp.floor((cnt + (MOE_TILE - 1)) * (1.0 / MOE_TILE)) * MOE_TILE
        r = lax.broadcasted_iota(I32, (N_EXPERTS, N_EXPERTS), 0)
        c = lax.broadcasted_iota(I32, (N_EXPERTS, N_EXPERTS), 1)
        lower = (c < r).astype(F32)
        off = jnp.dot(lower, pc, precision=lax.Precision.HIGHEST, preferred_element_type=F32)
        off_sc[...] = off
        carry_sc[...] = jnp.zeros_like(carry_sc)
        cumend = off + pc
        tstart = lax.broadcasted_iota(I32, (N_EXPERTS, ntp), 1).astype(F32) * MOE_TILE
        te = jnp.sum((jnp.broadcast_to(cumend[:, 0:1], (N_EXPERTS, ntp)) <= tstart).astype(I32), axis=0, keepdims=True)
        texp_ref[...] = jnp.minimum(te, N_EXPERTS - 1)
        nvalid_ref[...] = (cumend[N_EXPERTS - 1:N_EXPERTS, :] * (1.0 / MOE_TILE)).astype(I32)
        diag = lax.broadcasted_iota(I32, (N_EXPERTS, LANES), 0) == lax.broadcasted_iota(I32, (N_EXPERTS, LANES), 1)
        ecnt_ref[...] = jnp.sum(jnp.where(diag, cnt, 0.0), axis=0, keepdims=True).astype(I32)
        eoff_ref[...] = jnp.sum(jnp.where(diag, off, 0.0), axis=0, keepdims=True).astype(I32)

    @pl.when(p == 1)
    def _():
        r = lax.broadcasted_iota(I32, (tl, tl), 0)
        c = lax.broadcasted_iota(I32, (tl, tl), 1)
        upper = (r < c).astype(BF16)
        before = jnp.dot(onehot.astype(BF16), upper, preferred_element_type=F32)
        base = before + (off_sc[:, 0:1] + carry_sc[:, 0:1])
        sub = lax.broadcasted_iota(I32, (8, tl), 0)
        out = jnp.zeros((8, tl), F32)
        for k in range(TOP_K):
            pk = jnp.sum(jnp.where(ohs[k], base, 0.0), axis=0, keepdims=True)
            out = jnp.where(sub == k, pk, out)
        pos_ref[...] = out[0:TOP_K, :].astype(I32)
        carry_sc[...] += tile_cnt


def _positions(topi, ntp):
    n = topi.shape[1]
    tl = POS_TILE
    const = lambda shape: pl.BlockSpec(shape, lambda p, j: (0, 0))
    return pl.pallas_call(
        functools.partial(_positions_body, ntp=ntp),
        grid=(2, n // tl),
        in_specs=[pl.BlockSpec((TOP_K, tl), lambda p, j: (0, j))],
        out_specs=[pl.BlockSpec((TOP_K, tl), lambda p, j: (0, j * p)),
                   const((1, ntp)), const((1, LANES)), const((1, LANES)), const((1, LANES))],
        out_shape=[jax.ShapeDtypeStruct((TOP_K, n), I32), jax.ShapeDtypeStruct((1, ntp), I32),
                   jax.ShapeDtypeStruct((1, LANES), I32), jax.ShapeDtypeStruct((1, LANES), I32),
                   jax.ShapeDtypeStruct((1, LANES), I32)],
        scratch_shapes=[pltpu.VMEM((N_EXPERTS, LANES), F32)] * 3,
        compiler_params=pltpu.CompilerParams(dimension_semantics=("arbitrary", "arbitrary")),
        name="positions",
    )(topi)


def _dispatch_body(ecnt_ref, eoff_ref, pos_ref, h_ref, xs_ref, zero_ref, sem, zsem):
    i = pl.program_id(0)
    t = h_ref.shape[0]

    def row_copy(tok, dst):
        return pltpu.make_async_copy(h_ref.at[pl.ds(tok, 1)], xs_ref.at[pl.ds(dst, 1)], sem)

    def issue(tok, carry):
        for k in range(TOP_K):
            row_copy(tok, pos_ref[k, tok]).start()
        return carry

    lax.fori_loop(0, t, issue, 0)

    @pl.when(i == pl.num_programs(0) - 1)
    def _():
        zero_ref[...] = jnp.zeros_like(zero_ref)

        def zero_copy(dst):
            return pltpu.make_async_copy(zero_ref.at[pl.ds(0, 1)], xs_ref.at[pl.ds(dst, 1)], zsem)

        def per_expert(e, carry):
            cnt = ecnt_ref[e]
            start = eoff_ref[e] + cnt
            npad = lax.rem(MOE_TILE - lax.rem(cnt, MOE_TILE), MOE_TILE)

            def zissue(r, c):
                zero_copy(start + r).start()
                return c

            def zwait(r, c):
                zero_copy(start + r).wait()
                return c

            lax.fori_loop(0, npad, zissue, 0)
            lax.fori_loop(0, npad, zwait, 0)
            return carry

        lax.fori_loop(0, N_EXPERTS, per_expert, 0)

    for _ in range(TOP_K):
        pltpu.make_async_copy(h_ref, xs_ref.at[pl.ds(0, t)], sem).wait()


def _dispatch(h1, pos, ecnt, eoff, rows_pad):
    n, d = h1.shape
    t = DISPATCH_TILE
    grid_spec = pltpu.PrefetchScalarGridSpec(
        num_scalar_prefetch=2,
        grid=(n // t,),
        in_specs=[pl.BlockSpec((TOP_K, t), lambda i, c, o: (0, i), memory_space=pltpu.SMEM),
                  pl.BlockSpec((t, d), lambda i, c, o: (i, 0))],
        out_specs=pl.BlockSpec(memory_space=pl.ANY),
        scratch_shapes=[pltpu.VMEM((8, d), F32), pltpu.SemaphoreType.DMA, pltpu.SemaphoreType.DMA],
    )
    return pl.pallas_call(
        _dispatch_body,
        grid_spec=grid_spec,
        out_shape=jax.ShapeDtypeStruct((rows_pad, d), F32),
        compiler_params=pltpu.CompilerParams(dimension_semantics=("arbitrary",), has_side_effects=True),
        name="dispatch",
    )(ecnt, eoff, pos, h1)


def _wprep_body(w_ref, wg_ref, wl_ref):
    blk = 2 * LANES
    r = lax.broadcasted_iota(I32, (blk, blk), 0)
    c = lax.broadcasted_iota(I32, (blk, blk), 1)
    sel = (r == jnp.where(c < LANES, 2 * c, 2 * (c - LANES) + 1)).astype(BF16)
    nblk = w_ref.shape[2] // blk
    for b in range(nblk):
        wb = w_ref[0, :, blk * b:blk * (b + 1)].astype(BF16)
        y = jnp.dot(wb, sel, preferred_element_type=F32).astype(BF16)
        wg_ref[0, :, LANES * b:LANES * (b + 1)] = y[:, 0:LANES]
        wl_ref[0, :, LANES * b:LANES * (b + 1)] = y[:, LANES:blk]


def _prep_gate_up(w_gate_up):
    e, d, f2 = w_gate_up.shape
    cb = 1024
    out = jax.ShapeDtypeStruct((e, d, f2 // 2), BF16)
    return pl.pallas_call(
        _wprep_body,
        grid=(e, f2 // cb),
        in_specs=[pl.BlockSpec((1, d, cb), lambda ei, ci: (ei, 0, ci))],
        out_specs=[pl.BlockSpec((1, d, cb // 2), lambda ei, ci: (ei, 0, ci))] * 2,
        out_shape=[out, out],
        compiler_params=pltpu.CompilerParams(dimension_semantics=("parallel", "parallel"),
                                             vmem_limit_bytes=VMEM_LIMIT),
        name="wprep",
    )(w_gate_up)


def _ffn_body(texp_ref, nvalid_ref, x_ref, wg_ref, wl_ref, wd_ref, bg_ref, bl_ref, bd_ref, y_ref):
    @pl.when(pl.program_id(0) < nvalid_ref[0])
    def _():
        x = x_ref[...].astype(BF16)
        hg = jnp.dot(x, wg_ref[0], preferred_element_type=F32) + bg_ref[0]
        hl = jnp.dot(x, wl_ref[0], preferred_element_type=F32) + bl_ref[0]
        g = jnp.minimum(hg, SWIGLU_LIMIT)
        lin = jnp.clip(hl, -SWIGLU_LIMIT, SWIGLU_LIMIT)
        act = g * (1.0 / (1.0 + jnp.exp(-SWIGLU_ALPHA * g))) * (lin + 1.0)
        y_ref[...] = jnp.dot(act.astype(BF16), wd_ref[0], preferred_element_type=F32) + bd_ref[0]


def _grouped_ffn(xs, texp, nvalid, wg, wl, wd, bg, bl, bd, ntiles):
    rows, d = xs.shape
    tm = MOE_TILE
    f = wg.shape[2]
    xmap = lambda i, te, nv: (jnp.minimum(i, nv[0] - 1), 0)
    wmap = lambda i, te, nv: (te[i], 0, 0)
    grid_spec = pltpu.PrefetchScalarGridSpec(
        num_scalar_prefetch=2,
        grid=(ntiles,),
        in_specs=[pl.BlockSpec((tm, d), xmap),
                  pl.BlockSpec((1, d, f), wmap), pl.BlockSpec((1, d, f), wmap), pl.BlockSpec((1, f, d), wmap),
                  pl.BlockSpec((1, 1, f), wmap), pl.BlockSpec((1, 1, f), wmap), pl.BlockSpec((1, 1, d), wmap)],
        out_specs=pl.BlockSpec((tm, d), xmap),
    )
    return pl.pallas_call(
        _ffn_body,
        grid_spec=grid_spec,
        out_shape=jax.ShapeDtypeStruct((rows, d), F32),
        compiler_params=pltpu.CompilerParams(dimension_semantics=("arbitrary",), vmem_limit_bytes=VMEM_LIMIT),
        name="ffn",
    )(texp, nvalid, xs, wg, wl, wd, bg, bl, bd)


def _combine_body(pos_ref, route_ref, h_ref, l2g_ref, l2b_ref, ys_ref, o_ref, buf_ref, sem):
    t = h_ref.shape[0]

    def issue(tok, carry):
        for k in range(TOP_K):
            pltpu.make_async_copy(ys_ref.at[pl.ds(pos_ref[k, tok], 1)], buf_ref.at[k, pl.ds(tok, 1)], sem).start()
        return carry

    lax.fori_loop(0, t, issue, 0)
    for k in range(TOP_K):
        pltpu.make_async_copy(ys_ref.at[pl.ds(0, t)], buf_ref.at[k], sem).wait()

    gates = route_ref[...]
    moe = gates[:, 0:1] * buf_ref[0]
    for k in range(1, TOP_K):
        moe = moe + gates[:, k:k + 1] * buf_ref[k]
    o_ref[...] = _layernorm(DEEPNORM_ALPHA * h_ref[...] + moe, l2g_ref[...], l2b_ref[...])


def _combine(pos, route, h1, ys, w):
    n, d = h1.shape
    t = COMBINE_TILE
    full = lambda shape: pl.BlockSpec(shape, lambda i: (0,) * len(shape))
    return pl.pallas_call(
        _combine_body,
        grid=(n // t,),
        in_specs=[pl.BlockSpec((TOP_K, t), lambda i: (0, i), memory_space=pltpu.SMEM),
                  pl.BlockSpec((t, LANES), lambda i: (i, 0)),
                  pl.BlockSpec((t, d), lambda i: (i, 0)),
                  full((1, d)), full((1, d)),
                  pl.BlockSpec(memory_space=pl.ANY)],
        out_specs=pl.BlockSpec((t, d), lambda i: (i, 0)),
        out_shape=jax.ShapeDtypeStruct((n, d), F32),
        scratch_shapes=[pltpu.VMEM((TOP_K, t, d), F32), pltpu.SemaphoreType.DMA],
        compiler_params=pltpu.CompilerParams(dimension_semantics=("arbitrary",), vmem_limit_bytes=VMEM_LIMIT),
        name="combine",
    )(pos, route, h1, w["ln2_g"], w["ln2_b"], ys)


def _rope_tables(pos_1d, row, col):
    inv = ROPE_THETA ** (-jnp.arange(0, ROPE_DIM, 2, dtype=F32) / ROPE_DIM)

    def cs(p):
        ang = p.astype(F32)[:, None] * inv[None, :]
        return jnp.cos(ang), jnp.sin(ang)

    c1, s1 = cs(pos_1d)
    cr, sr = cs(row)
    cc, sc = cs(col)
    return (jnp.concatenate([c1, c1, c1, c1], axis=1), jnp.concatenate([-s1, s1, -s1, s1], axis=1),
            jnp.concatenate([cr, cr, cc, cc], axis=1), jnp.concatenate([-sr, sr, -sc, sc], axis=1))


def _prep_weights(ln_emb_g, ln_emb_b, w_in, g_q_a, w_q_b, g_kv_a, w_kv_b, g_q_gqa, g_k_gqa, g_o_mla, g_o_gqa, w_o,
                  ln1_g, ln1_b, w_router, b_router, ln2_g, ln2_b):
    r2 = lambda v: v.reshape(1, -1).astype(F32)
    o = np.cumsum([0, Q_LORA, KV_LORA, ROPE_DIM, GQA_HEADS * GQA_DIM, GQA_KV_HEADS * GQA_DIM, GQA_KV_HEADS * GQA_DIM])
    wi = w_in[0]
    seg = [wi[:, o[i]:o[i + 1]] for i in range(6)]
    w_in_p = jnp.concatenate([seg[0], seg[1], seg[2], seg[2], seg[3], seg[4], seg[5]], axis=1).astype(BF16)
    wq = w_q_b[0].reshape(Q_LORA, MLA_HEADS, QK_DIM)
    w_qb = jnp.concatenate([wq[:, :, :NOPE_DIM].reshape(Q_LORA, -1), wq[:, :, NOPE_DIM:].reshape(Q_LORA, -1)],
                           axis=1).astype(BF16)
    wk = w_kv_b[0].reshape(KV_LORA, MLA_HEADS, NOPE_DIM + V_DIM)
    w_kvb = jnp.concatenate([wk[:, :, :NOPE_DIM].reshape(KV_LORA, -1), wk[:, :, NOPE_DIM:].reshape(KV_LORA, -1)],
                            axis=1).astype(BF16)
    w_router_p = jnp.pad(w_router[0].astype(F32), ((0, 0), (0, LANES - N_EXPERTS)))
    b_router_p = jnp.pad(b_router[0].astype(F32), (0, LANES - N_EXPERTS), constant_values=NEG_BIG).reshape(1, LANES)
    return dict(
        ln_emb_g=r2(ln_emb_g), ln_emb_b=r2(ln_emb_b), w_in=w_in_p, g_q_a=r2(g_q_a[0]), w_qb=w_qb,
        g_kv_a=r2(g_kv_a[0]), w_kvb=w_kvb, g_q_gqa=r2(g_q_gqa[0]), g_k_gqa=r2(g_k_gqa[0]),
        g_o_mla=r2(g_o_mla[0]), g_o_gqa=r2(g_o_gqa[0]), w_o=w_o[0].astype(BF16), ln1_g=r2(ln1_g[0]),
        ln1_b=r2(ln1_b[0]), w_router=w_router_p, b_router=b_router_p, ln2_g=r2(ln2_g[0]), ln2_b=r2(ln2_b[0]))


def kernel(x, meta_tokens, ln_emb_g, ln_emb_b, w_in, g_q_a, w_q_b, g_kv_a, w_kv_b, g_q_gqa, g_k_gqa, g_o_mla, g_o_gqa,
           w_o, ln1_g, ln1_b, w_router, b_router, w_gate_up, b_gate_up, w_down, b_down, ln2_g, ln2_b):
    b, s, d = x.shape
    n = b * s
    w = _prep_weights(ln_emb_g, ln_emb_b, w_in, g_q_a, w_q_b, g_kv_a, w_kv_b, g_q_gqa, g_k_gqa, g_o_mla, g_o_gqa,
                      w_o, ln1_g, ln1_b, w_router, b_router, ln2_g, ln2_b)

    tok = jnp.arange(s, dtype=I32)
    tabs_real = _rope_tables(tok + N_META, tok // GRID_W, tok % GRID_W)
    mt = jnp.arange(N_META, dtype=I32)
    tabs_meta = _rope_tables(mt, jnp.full((N_META,), -1, I32), mt)

    qm, km, vm, qg, kg, vg = _project(x, tabs_real, w, ROW_TILE)
    meta = meta_tokens.astype(x.dtype).reshape(1, N_META, d)
    _, km_m, vm_m, _, kg_m, vg_m = _project(meta, tabs_meta, w, N_META)
    padm = lambda a: jnp.pad(a[0], ((0, 0), (0, META_PAD - N_META), (0, 0)))

    o_mla = _mla_attention(qm, km, vm, padm(km_m), padm(vm_m))
    o_gqa = _gqa_attention(qg, kg, vg, padm(kg_m), padm(vg_m))

    h1, topi, route = _merge(o_mla, o_gqa, x.reshape(n, d), w)

    ntiles = (n * TOP_K) // MOE_TILE + N_EXPERTS
    ntp = -(-ntiles // LANES) * LANES
    pos, texp, nvalid, ecnt, eoff = _positions(topi, ntp)
    xs = _dispatch(h1, pos, ecnt.reshape(-1), eoff.reshape(-1), ntiles * MOE_TILE)

    wg, wl = _prep_gate_up(w_gate_up[0])
    bgu = b_gate_up[0].astype(F32)
    ys = _grouped_ffn(xs, texp.reshape(-1), nvalid.reshape(-1), wg, wl, w_down[0].astype(BF16),
                      bgu[:, 0::2].reshape(N_EXPERTS, 1, D_FF), bgu[:, 1::2].reshape(N_EXPERTS, 1, D_FF),
                      b_down[0].astype(F32).reshape(N_EXPERTS, 1, d), ntiles)

    out = _combine(pos, route, h1, ys, w)
    return out.reshape(b, s, d)
```

```python
import functools

import jax
import jax.numpy as jnp
import numpy as np
from jax import lax
from jax.experimental import pallas as pl
from jax.experimental.pallas import tpu as pltpu

D_MODEL = 1024
N_META = 16
GRID_W = 64
ROPE_THETA = 10000.0
MLA_HEADS = 4
Q_LORA = 256
KV_LORA = 128
NOPE_DIM = 128
ROPE_DIM = 64
V_DIM = 128
QK_DIM = NOPE_DIM + ROPE_DIM
GQA_HEADS = 4
GQA_KV_HEADS = 2
GQA_DIM = 128
N_EXPERTS = 32
TOP_K = 4
D_FF = D_MODEL
SWIGLU_LIMIT = 7.0
SWIGLU_ALPHA = 1.702
RMS_EPS = 1e-6
LN_EPS = 1e-5
DEPTH = 1
DEEPNORM_ALPHA = (2.0 * DEPTH) ** 0.25

LANES = 128
META_PAD = 128
MLA_K = 2 * LANES
NEG_BIG = -1e30
LOG2E = 1.4426950408889634
V_EXT = 2 * LANES

ROW_TILE = 512
Q_TILE_MLA = 1024
Q_TILE_GQA = 512
Q_SUB = 256
POS_TILE = 512
MOE_TILE = 512
DISPATCH_TILE = 512
COMBINE_TILE = 256
VMEM_LIMIT = 56 * 1024 * 1024

F32 = jnp.float32
BF16 = jnp.bfloat16
I32 = jnp.int32


def _layernorm(x, g, b):
    mu = jnp.mean(x, axis=-1, keepdims=True)
    xc = x - mu
    var = jnp.mean(xc * xc, axis=-1, keepdims=True)
    return xc * lax.rsqrt(var + LN_EPS) * g + b


def _rmsnorm(x, g):
    return x * lax.rsqrt(jnp.mean(x * x, axis=-1, keepdims=True) + RMS_EPS) * g


def _swap_halves64(x):
    w = x.shape[-1]
    lane = lax.broadcasted_iota(I32, x.shape, x.ndim - 1)
    return jnp.where((lane % 64) < 32, pltpu.roll(x, w - 32, x.ndim - 1), pltpu.roll(x, 32, x.ndim - 1))


def _rotate(x, cos, sin_signed):
    return x * cos + _swap_halves64(x) * sin_signed


def _proj_body(x_ref, lng_ref, lnb_ref, win_ref, gqa_ref, wqb_ref, gkva_ref, wkvb_ref, gqg_ref, gkg_ref,
               c1_ref, s1_ref, ca_ref, sa_ref,
               qm_ref, km_ref, vm_ref, qg_ref, kg_ref, vg_ref):
    x = x_ref[0]
    h0 = _layernorm(x, lng_ref[...], lnb_ref[...])
    z = jnp.dot(h0.astype(BF16), win_ref[...], preferred_element_type=F32)
    q_a = z[:, 0:256]
    kv_a = z[:, 256:384]
    kpe2 = z[:, 384:512]
    q_g = z[:, 512:1024]
    k_g = z[:, 1024:1280]
    v_g = z[:, 1280:1536]

    c1 = c1_ref[...]
    s1 = s1_ref[...]
    ca = ca_ref[...]
    sa = sa_ref[...]

    q = jnp.dot(_rmsnorm(q_a, gqa_ref[...]).astype(BF16), wqb_ref[...], preferred_element_type=F32)
    kv = jnp.dot(_rmsnorm(kv_a, gkva_ref[...]).astype(BF16), wkvb_ref[...], preferred_element_type=F32)
    krot = _rotate(kpe2, c1, s1)
    lane = lax.broadcasted_iota(I32, krot.shape, 1)
    scale_a = QK_DIM ** -0.5 * LOG2E
    ones = jnp.ones((x.shape[0], LANES), F32)
    for c in range(MLA_HEADS // 2):
        qr = _rotate(q[:, 512 + LANES * c:512 + LANES * (c + 1)], c1, s1)
        for hh in range(2):
            h = 2 * c + hh
            slot = jnp.where((lane // 64) == hh, qr, 0.0)
            qm_ref[0, h] = (jnp.concatenate([q[:, LANES * h:LANES * (h + 1)], slot], axis=1) * scale_a).astype(BF16)
    for h in range(MLA_HEADS):
        km_ref[0, h] = jnp.concatenate([kv[:, LANES * h:LANES * (h + 1)], krot], axis=1).T.astype(BF16)
        vm_ref[0, h] = jnp.concatenate([kv[:, 512 + LANES * h:512 + LANES * (h + 1)], ones], axis=1).astype(BF16)

    scale_b = GQA_DIM ** -0.5 * LOG2E
    gqg = gqg_ref[...]
    gkg = gkg_ref[...]
    for h in range(GQA_HEADS):
        xh = _rmsnorm(q_g[:, LANES * h:LANES * (h + 1)], gqg)
        qg_ref[0, h] = (_rotate(xh, ca, sa) * scale_b).astype(BF16)
    for j in range(GQA_KV_HEADS):
        xh = _rmsnorm(k_g[:, LANES * j:LANES * (j + 1)], gkg)
        kg_ref[0, j] = _rotate(xh, ca, sa).T.astype(BF16)
        vg_ref[0, j] = jnp.concatenate([v_g[:, LANES * j:LANES * (j + 1)], ones], axis=1).astype(BF16)


def _project(x3, tabs, w, tile):
    b, s, d = x3.shape
    nst = s // tile
    full = lambda shape: pl.BlockSpec(shape, lambda bi, si: (0,) * len(shape))
    tab = pl.BlockSpec((tile, LANES), lambda bi, si: (si, 0))
    hm = lambda nh, dd: pl.BlockSpec((1, nh, tile, dd), lambda bi, si: (bi, 0, si, 0))
    hmt = lambda nh, dd: pl.BlockSpec((1, nh, dd, tile), lambda bi, si: (bi, 0, 0, si))
    out_shape = [
        jax.ShapeDtypeStruct((b, MLA_HEADS, s, MLA_K), BF16),
        jax.ShapeDtypeStruct((b, MLA_HEADS, MLA_K, s), BF16),
        jax.ShapeDtypeStruct((b, MLA_HEADS, s, V_EXT), BF16),
        jax.ShapeDtypeStruct((b, GQA_HEADS, s, GQA_DIM), BF16),
        jax.ShapeDtypeStruct((b, GQA_KV_HEADS, GQA_DIM, s), BF16),
        jax.ShapeDtypeStruct((b, GQA_KV_HEADS, s, V_EXT), BF16),
    ]
    return pl.pallas_call(
        _proj_body,
        grid=(b, nst),
        in_specs=[
            pl.BlockSpec((1, tile, d), lambda bi, si: (bi, si, 0)),
            full((1, d)), full((1, d)),
            full(w["w_in"].shape), full((1, Q_LORA)), full(w["w_qb"].shape),
            full((1, KV_LORA)), full(w["w_kvb"].shape), full((1, GQA_DIM)), full((1, GQA_DIM)),
            tab, tab, tab, tab,
        ],
        out_specs=[hm(MLA_HEADS, MLA_K), hmt(MLA_HEADS, MLA_K), hm(MLA_HEADS, V_EXT),
                   hm(GQA_HEADS, GQA_DIM), hmt(GQA_KV_HEADS, GQA_DIM), hm(GQA_KV_HEADS, V_EXT)],
        out_shape=out_shape,
        compiler_params=pltpu.CompilerParams(dimension_semantics=("parallel", "parallel"),
                                             vmem_limit_bytes=VMEM_LIMIT),
        name="proj",
    )(x3, w["ln_emb_g"], w["ln_emb_b"], w["w_in"], w["g_q_a"], w["w_qb"], w["g_kv_a"], w["w_kvb"],
      w["g_q_gqa"], w["g_k_gqa"], *tabs)


def _softmax_pv(q, kt, v, kmt, vm):
    s = jnp.dot(q, kt, preferred_element_type=F32)
    sm = jnp.dot(q, kmt, preferred_element_type=F32)
    col = lax.broadcasted_iota(I32, sm.shape, 1)
    sm = jnp.where(col < N_META, sm, NEG_BIG)
    m = jnp.maximum(jnp.max(s, axis=1, keepdims=True), jnp.max(sm, axis=1, keepdims=True))
    p = jnp.exp2(s - m).astype(BF16)
    pm = jnp.exp2(sm - m).astype(BF16)
    acc = jnp.dot(p, v, preferred_element_type=F32) + jnp.dot(pm, vm, preferred_element_type=F32)
    return acc[:, 0:V_DIM] / acc[:, V_DIM:V_EXT]


def _mla_attn_body(q_ref, k_ref, v_ref, km_ref, vm_ref, o_ref):
    for i in range(q_ref.shape[2] // Q_SUB):
        rows = pl.ds(i * Q_SUB, Q_SUB)
        o = _softmax_pv(q_ref[0, 0, rows, :], k_ref[0, 0], v_ref[0, 0], km_ref[0, 0], vm_ref[0, 0])
        o_ref[rows, :] = o.astype(o_ref.dtype)


def _gqa_attn_body(q_ref, k_ref, v_ref, km_ref, vm_ref, o_ref):
    for g in range(2):
        for i in range(q_ref.shape[2] // Q_SUB):
            rows = pl.ds(i * Q_SUB, Q_SUB)
            o = _softmax_pv(q_ref[0, g, rows, :], k_ref[0, 0], v_ref[0, 0], km_ref[0, 0], vm_ref[0, 0])
            o_ref[rows, GQA_DIM * g:GQA_DIM * (g + 1)] = o.astype(o_ref.dtype)


def _mla_attention(qm, kmt, vm, kmeta_t, vmeta):
    b, h, s, dk = qm.shape
    tq = Q_TILE_MLA
    nq = s // tq
    return pl.pallas_call(
        _mla_attn_body,
        grid=(b, h, nq),
        in_specs=[
            pl.BlockSpec((1, 1, tq, dk), lambda bi, hi, qi: (bi, hi, qi, 0)),
            pl.BlockSpec((1, 1, dk, s), lambda bi, hi, qi: (bi, hi, 0, 0)),
            pl.BlockSpec((1, 1, s, V_EXT), lambda bi, hi, qi: (bi, hi, 0, 0)),
            pl.BlockSpec((1, 1, dk, META_PAD), lambda bi, hi, qi: (0, hi, 0, 0)),
            pl.BlockSpec((1, 1, META_PAD, V_EXT), lambda bi, hi, qi: (0, hi, 0, 0)),
        ],
        out_specs=pl.BlockSpec((tq, V_DIM), lambda bi, hi, qi: (bi * nq + qi, hi)),
        out_shape=jax.ShapeDtypeStruct((b * s, h * V_DIM), BF16),
        compiler_params=pltpu.CompilerParams(dimension_semantics=("parallel", "parallel", "parallel"),
                                             vmem_limit_bytes=VMEM_LIMIT),
        name="mla_attn",
    )(qm, kmt, vm, kmeta_t, vmeta)


def _gqa_attention(qg, kgt, vg, kmeta_t, vmeta):
    b, h, s, d = qg.shape
    hk = kgt.shape[1]
    tq = Q_TILE_GQA
    nq = s // tq
    return pl.pallas_call(
        _gqa_attn_body,
        grid=(b, hk, nq),
        in_specs=[
            pl.BlockSpec((1, 2, tq, d), lambda bi, ji, qi: (bi, ji, qi, 0)),
            pl.BlockSpec((1, 1, d, s), lambda bi, ji, qi: (bi, ji, 0, 0)),
            pl.BlockSpec((1, 1, s, V_EXT), lambda bi, ji, qi: (bi, ji, 0, 0)),
            pl.BlockSpec((1, 1, d, META_PAD), lambda bi, ji, qi: (0, ji, 0, 0)),
            pl.BlockSpec((1, 1, META_PAD, V_EXT), lambda bi, ji, qi: (0, ji, 0, 0)),
        ],
        out_specs=pl.BlockSpec((tq, 2 * d), lambda bi, ji, qi: (bi * nq + qi, ji)),
        out_shape=jax.ShapeDtypeStruct((b * s, h * d), BF16),
        compiler_params=pltpu.CompilerParams(dimension_semantics=("parallel", "parallel", "parallel"),
                                             vmem_limit_bytes=VMEM_LIMIT),
        name="gqa_attn",
    )(qg, kgt, vg, kmeta_t, vmeta)


def _merge_body(om_ref, og_ref, x_ref, lng_ref, lnb_ref, gom_ref, gog_ref, wo_ref, l1g_ref, l1b_ref,
                wr_ref, br_ref, h1_ref, topi_ref, route_ref):
    t = x_ref.shape[0]
    h0 = _layernorm(x_ref[...], lng_ref[...], lnb_ref[...])
    nm = _rmsnorm(om_ref[...].astype(F32), gom_ref[...]).astype(BF16)
    ng = _rmsnorm(og_ref[...].astype(F32), gog_ref[...]).astype(BF16)
    half = nm.shape[1]
    mix = jnp.dot(nm, wo_ref[0:half, :], preferred_element_type=F32)
    mix = mix + jnp.dot(ng, wo_ref[half:2 * half, :], preferred_element_type=F32)
    h1 = _layernorm(DEEPNORM_ALPHA * h0 + mix, l1g_ref[...], l1b_ref[...])
    h1_ref[...] = h1

    logits = jnp.dot(h1, wr_ref[...], precision=lax.Precision.HIGHEST, preferred_element_type=F32) + br_ref[...]
    cur = logits.T[0:N_EXPERTS, :]
    eidx = lax.broadcasted_iota(I32, cur.shape, 0)
    vals, idxs = [], []
    for _ in range(TOP_K):
        m = jnp.max(cur, axis=0, keepdims=True)
        i = jnp.min(jnp.where(cur == m, eidx, N_EXPERTS), axis=0, keepdims=True)
        vals.append(m)
        idxs.append(i)
        cur = jnp.where(eidx == i, -jnp.inf, cur)
    ex = [jnp.exp(v - vals[0]) for v in vals]
    den = ex[0] + ex[1] + ex[2] + ex[3]
    sub = lax.broadcasted_iota(I32, (8, t), 0)
    ti = jnp.zeros((8, t), I32)
    gt = jnp.zeros((8, t), F32)
    for k in range(TOP_K):
        ti = jnp.where(sub == k, idxs[k], ti)
        gt = jnp.where(sub == k, ex[k] / den, gt)
    topi_ref[...] = ti[0:TOP_K, :]
    route_ref[...] = jnp.concatenate([gt, jnp.zeros((LANES - 8, t), F32)], axis=0).T


def _merge(o_mla, o_gqa, x2, w):
    n, d = x2.shape
    tile = ROW_TILE
    half = o_mla.shape[1]
    full = lambda shape: pl.BlockSpec(shape, lambda i: (0,) * len(shape))
    row = lambda width: pl.BlockSpec((tile, width), lambda i: (i, 0))
    return pl.pallas_call(
        _merge_body,
        grid=(n // tile,),
        in_specs=[row(half), row(half), row(d), full((1, d)), full((1, d)), full((1, half)), full((1, half)),
                  full((d, d)), full((1, d)), full((1, d)), full((d, LANES)), full((1, LANES))],
        out_specs=[row(d), pl.BlockSpec((TOP_K, tile), lambda i: (0, i)), row(LANES)],
        out_shape=[jax.ShapeDtypeStruct((n, d), F32), jax.ShapeDtypeStruct((TOP_K, n), I32),
                   jax.ShapeDtypeStruct((n, LANES), F32)],
        compiler_params=pltpu.CompilerParams(dimension_semantics=("parallel",), vmem_limit_bytes=VMEM_LIMIT),
        name="merge",
    )(o_mla, o_gqa, x2, w["ln_emb_g"], w["ln_emb_b"], w["g_o_mla"], w["g_o_gqa"], w["w_o"],
      w["ln1_g"], w["ln1_b"], w["w_router"], w["b_router"])


def _positions_body(topi_ref, pos_ref, texp_ref, nvalid_ref, ecnt_ref, eoff_ref, cnt_sc, carry_sc, off_sc, *, ntp):
    p = pl.program_id(0)
    j = pl.program_id(1)
    tl = topi_ref.shape[1]
    topi = topi_ref[...]
    eidx = lax.broadcasted_iota(I32, (N_EXPERTS, tl), 0)
    ohs = [eidx == topi[k:k + 1, :] for k in range(TOP_K)]
    onehot = ohs[0].astype(F32) + ohs[1].astype(F32) + ohs[2].astype(F32) + ohs[3].astype(F32)
    tile_cnt = jnp.broadcast_to(jnp.sum(onehot, axis=1, keepdims=True), (N_EXPERTS, LANES))

    @pl.when(jnp.logical_and(p == 0, j == 0))
    def _():
        cnt_sc[...] = jnp.zeros_like(cnt_sc)

    @pl.when(p == 0)
    def _():
        cnt_sc[...] += tile_cnt

    @pl.when(jnp.logical_and(p == 1, j == 0))
    def _():
        cnt = cnt_sc[...]
        pc = jnp.floor((cnt + (MOE_TILE - 1)) * (1.0 / MOE_TILE)) * MOE_TILE
        r = lax.broadcasted_iota(I32, (N_EXPERTS, N_EXPERTS), 0)
        c = lax.broadcasted_iota(I32, (N_EXPERTS, N_EXPERTS), 1)
        lower = (c < r).astype(F32)
        off = jnp.dot(lower, pc, precision=lax.Precision.HIGHEST, preferred_element_type=F32)
        off_sc[...] = off
        carry_sc[...] = jnp.zeros_like(carry_sc)
        cumend = off + pc
        tstart = lax.broadcasted_iota(I32, (N_EXPERTS, ntp), 1).astype(F32) * MOE_TILE
        te = jnp.sum((jnp.broadcast_to(cumend[:, 0:1], (N_EXPERTS, ntp)) <= tstart).astype(I32), axis=0, keepdims=True)
        texp_ref[...] = jnp.minimum(te, N_EXPERTS - 1)
        nvalid_ref[...] = (cumend[N_EXPERTS - 1:N_EXPERTS, :] * (1.0 / MOE_TILE)).astype(I32)
        diag = lax.broadcasted_iota(I32, (N_EXPERTS, LANES), 0) == lax.broadcasted_iota(I32, (N_EXPERTS, LANES), 1)
        ecnt_ref[...] = jnp.sum(jnp.where(diag, cnt, 0.0), axis=0, keepdims=True).astype(I32)
        eoff_ref[...] = jnp.sum(jnp.where(diag, off, 0.0), axis=0, keepdims=True).astype(I32)

    @pl.when(p == 1)
    def _():
        r = lax.broadcasted_iota(I32, (tl, tl), 0)
        c = lax.broadcasted_iota(I32, (tl, tl), 1)
        upper = (r < c).astype(BF16)
        before = jnp.dot(onehot.astype(BF16), upper, preferred_element_type=F32)
        base = before + (off_sc[:, 0:1] + carry_sc[:, 0:1])
        sub = lax.broadcasted_iota(I32, (8, tl), 0)
        out = jnp.zeros((8, tl), F32)
        for k in range(TOP_K):
            pk = jnp.sum(jnp.where(ohs[k], base, 0.0), axis=0, keepdims=True)
            out = jnp.where(sub == k, pk, out)
        pos_ref[...] = out[0:TOP_K, :].astype(I32)
        carry_sc[...] += tile_cnt


def _positions(topi, ntp):
    n = topi.shape[1]
    tl = POS_TILE
    const = lambda shape: pl.BlockSpec(shape, lambda p, j: (0, 0))
    return pl.pallas_call(
        functools.partial(_positions_body, ntp=ntp),
        grid=(2, n // tl),
        in_specs=[pl.BlockSpec((TOP_K, tl), lambda p, j: (0, j))],
        out_specs=[pl.BlockSpec((TOP_K, tl), lambda p, j: (0, j * p)),
                   const((1, ntp)), const((1, LANES)), const((1, LANES)), const((1, LANES))],
        out_shape=[jax.ShapeDtypeStruct((TOP_K, n), I32), jax.ShapeDtypeStruct((1, ntp), I32),
                   jax.ShapeDtypeStruct((1, LANES), I32), jax.ShapeDtypeStruct((1, LANES), I32),
                   jax.ShapeDtypeStruct((1, LANES), I32)],
        scratch_shapes=[pltpu.VMEM((N_EXPERTS, LANES), F32)] * 3,
        compiler_params=pltpu.CompilerParams(dimension_semantics=("arbitrary", "arbitrary")),
        name="positions",
    )(topi)


def _dispatch_body(ecnt_ref, eoff_ref, pos_ref, h_ref, xs_ref, zero_ref, sem, zsem):
    i = pl.program_id(0)
    t = h_ref.shape[0]

    def row_copy(tok, dst):
        return pltpu.make_async_copy(h_ref.at[pl.ds(tok, 1)], xs_ref.at[pl.ds(dst, 1)], sem)

    def issue(tok, carry):
        for k in range(TOP_K):
            row_copy(tok, pos_ref[k, tok]).start()
        return carry

    lax.fori_loop(0, t, issue, 0)

    @pl.when(i == pl.num_programs(0) - 1)
    def _():
        zero_ref[...] = jnp.zeros_like(zero_ref)

        def zero_copy(dst):
            return pltpu.make_async_copy(zero_ref.at[pl.ds(0, 1)], xs_ref.at[pl.ds(dst, 1)], zsem)

        def per_expert(e, carry):
            cnt = ecnt_ref[e]
            start = eoff_ref[e] + cnt
            npad = lax.rem(MOE_TILE - lax.rem(cnt, MOE_TILE), MOE_TILE)

            def zissue(r, c):
                zero_copy(start + r).start()
                return c

            def zwait(r, c):
                zero_copy(start + r).wait()
                return c

            lax.fori_loop(0, npad, zissue, 0)
            lax.fori_loop(0, npad, zwait, 0)
            return carry

        lax.fori_loop(0, N_EXPERTS, per_expert, 0)

    for _ in range(TOP_K):
        pltpu.make_async_copy(h_ref, xs_ref.at[pl.ds(0, t)], sem).wait()


def _dispatch(h1, pos, ecnt, eoff, rows_pad):
    n, d = h1.shape
    t = DISPATCH_TILE
    grid_spec = pltpu.PrefetchScalarGridSpec(
        num_scalar_prefetch=2,
        grid=(n // t,),
        in_specs=[pl.BlockSpec((TOP_K, t), lambda i, c, o: (0, i), memory_space=pltpu.SMEM),
                  pl.BlockSpec((t, d), lambda i, c, o: (i, 0))],
        out_specs=pl.BlockSpec(memory_space=pl.ANY),
        scratch_shapes=[pltpu.VMEM((8, d), F32), pltpu.SemaphoreType.DMA, pltpu.SemaphoreType.DMA],
    )
    return pl.pallas_call(
        _dispatch_body,
        grid_spec=grid_spec,
        out_shape=jax.ShapeDtypeStruct((rows_pad, d), F32),
        compiler_params=pltpu.CompilerParams(dimension_semantics=("arbitrary",), has_side_effects=True),
        name="dispatch",
    )(ecnt, eoff, pos, h1)


def _wprep_body(w_ref, wg_ref, wl_ref):
    blk = 2 * LANES
    r = lax.broadcasted_iota(I32, (blk, blk), 0)
    c = lax.broadcasted_iota(I32, (blk, blk), 1)
    sel = (r == jnp.where(c < LANES, 2 * c, 2 * (c - LANES) + 1)).astype(BF16)
    nblk = w_ref.shape[2] // blk
    for b in range(nblk):
        wb = w_ref[0, :, blk * b:blk * (b + 1)].astype(BF16)
        y = jnp.dot(wb, sel, preferred_element_type=F32).astype(BF16)
        wg_ref[0, :, LANES * b:LANES * (b + 1)] = y[:, 0:LANES]
        wl_ref[0, :, LANES * b:LANES * (b + 1)] = y[:, LANES:blk]


def _prep_gate_up(w_gate_up):
    e, d, f2 = w_gate_up.shape
    cb = 1024
    out = jax.ShapeDtypeStruct((e, d, f2 // 2), BF16)
    return pl.pallas_call(
        _wprep_body,
        grid=(e, f2 // cb),
        in_specs=[pl.BlockSpec((1, d, cb), lambda ei, ci: (ei, 0, ci))],
        out_specs=[pl.BlockSpec((1, d, cb // 2), lambda ei, ci: (ei, 0, ci))] * 2,
        out_shape=[out, out],
        compiler_params=pltpu.CompilerParams(dimension_semantics=("parallel", "parallel"),
                                             vmem_limit_bytes=VMEM_LIMIT),
        name="wprep",
    )(w_gate_up)


def _ffn_body(texp_ref, nvalid_ref, x_ref, wg_ref, wl_ref, wd_ref, bg_ref, bl_ref, bd_ref, y_ref):
    @pl.when(pl.program_id(0) < nvalid_ref[0])
    def _():
        x = x_ref[...].astype(BF16)
        hg = jnp.dot(x, wg_ref[0], preferred_element_type=F32) + bg_ref[0]
        hl = jnp.dot(x, wl_ref[0], preferred_element_type=F32) + bl_ref[0]
        g = jnp.minimum(hg, SWIGLU_LIMIT)
        lin = jnp.clip(hl, -SWIGLU_LIMIT, SWIGLU_LIMIT)
        act = g * (1.0 / (1.0 + jnp.exp(-SWIGLU_ALPHA * g))) * (lin + 1.0)
        y_ref[...] = jnp.dot(act.astype(BF16), wd_ref[0], preferred_element_type=F32) + bd_ref[0]


def _grouped_ffn(xs, texp, nvalid, wg, wl, wd, bg, bl, bd, ntiles):
    rows, d = xs.shape
    tm = MOE_TILE
    f = wg.shape[2]
    xmap = lambda i, te, nv: (jnp.minimum(i, nv[0] - 1), 0)
    wmap = lambda i, te, nv: (te[i], 0, 0)
    grid_spec = pltpu.PrefetchScalarGridSpec(
        num_scalar_prefetch=2,
        grid=(ntiles,),
        in_specs=[pl.BlockSpec((tm, d), xmap),
                  pl.BlockSpec((1, d, f), wmap), pl.BlockSpec((1, d, f), wmap), pl.BlockSpec((1, f, d), wmap),
                  pl.BlockSpec((1, 1, f), wmap), pl.BlockSpec((1, 1, f), wmap), pl.BlockSpec((1, 1, d), wmap)],
        out_specs=pl.BlockSpec((tm, d), xmap),
    )
    return pl.pallas_call(
        _ffn_body,
        grid_spec=grid_spec,
        out_shape=jax.ShapeDtypeStruct((rows, d), F32),
        compiler_params=pltpu.CompilerParams(dimension_semantics=("arbitrary",), vmem_limit_bytes=VMEM_LIMIT),
        name="ffn",
    )(texp, nvalid, xs, wg, wl, wd, bg, bl, bd)


def _combine_body(pos_ref, route_ref, h_ref, l2g_ref, l2b_ref, ys_ref, o_ref, buf_ref, sem):
    t = h_ref.shape[0]

    def issue(tok, carry):
        for k in range(TOP_K):
            pltpu.make_async_copy(ys_ref.at[pl.ds(pos_ref[k, tok], 1)], buf_ref.at[k, pl.ds(tok, 1)], sem).start()
        return carry

    lax.fori_loop(0, t, issue, 0)
    for k in range(TOP_K):
        pltpu.make_async_copy(ys_ref.at[pl.ds(0, t)], buf_ref.at[k], sem).wait()

    gates = route_ref[...]
    moe = gates[:, 0:1] * buf_ref[0]
    for k in range(1, TOP_K):
        moe = moe + gates[:, k:k + 1] * buf_ref[k]
    o_ref[...] = _layernorm(DEEPNORM_ALPHA * h_ref[...] + moe, l2g_ref[...], l2b_ref[...])


def _combine(pos, route, h1, ys, w):
    n, d = h1.shape
    t = COMBINE_TILE
    full = lambda shape: pl.BlockSpec(shape, lambda i: (0,) * len(shape))
    return pl.pallas_call(
        _combine_body,
        grid=(n // t,),
        in_specs=[pl.BlockSpec((TOP_K, t), lambda i: (0, i), memory_space=pltpu.SMEM),
                  pl.BlockSpec((t, LANES), lambda i: (i, 0)),
                  pl.BlockSpec((t, d), lambda i: (i, 0)),
                  full((1, d)), full((1, d)),
                  pl.BlockSpec(memory_space=pl.ANY)],
        out_specs=pl.BlockSpec((t, d), lambda i: (i, 0)),
        out_shape=jax.ShapeDtypeStruct((n, d), F32),
        scratch_shapes=[pltpu.VMEM((TOP_K, t, d), F32), pltpu.SemaphoreType.DMA],
        compiler_params=pltpu.CompilerParams(dimension_semantics=("arbitrary",), vmem_limit_bytes=VMEM_LIMIT),
        name="combine",
    )(pos, route, h1, w["ln2_g"], w["ln2_b"], ys)


def _rope_tables(pos_1d, row, col):
    inv = ROPE_THETA ** (-jnp.arange(0, ROPE_DIM, 2, dtype=F32) / ROPE_DIM)

    def cs(p):
        ang = p.astype(F32)[:, None] * inv[None, :]
        return jnp.cos(ang), jnp.sin(ang)

    c1, s1 = cs(pos_1d)
    cr, sr = cs(row)
    cc, sc = cs(col)
    return (jnp.concatenate([c1, c1, c1, c1], axis=1), jnp.concatenate([-s1, s1, -s1, s1], axis=1),
            jnp.concatenate([cr, cr, cc, cc], axis=1), jnp.concatenate([-sr, sr, -sc, sc], axis=1))


def _prep_weights(ln_emb_g, ln_emb_b, w_in, g_q_a, w_q_b, g_kv_a, w_kv_b, g_q_gqa, g_k_gqa, g_o_mla, g_o_gqa, w_o,
                  ln1_g, ln1_b, w_router, b_router, ln2_g, ln2_b):
    r2 = lambda v: v.reshape(1, -1).astype(F32)
    o = np.cumsum([0, Q_LORA, KV_LORA, ROPE_DIM, GQA_HEADS * GQA_DIM, GQA_KV_HEADS * GQA_DIM, GQA_KV_HEADS * GQA_DIM])
    wi = w_in[0]
    seg = [wi[:, o[i]:o[i + 1]] for i in range(6)]
    w_in_p = jnp.concatenate([seg[0], seg[1], seg[2], seg[2], seg[3], seg[4], seg[5]], axis=1).astype(BF16)
    wq = w_q_b[0].reshape(Q_LORA, MLA_HEADS, QK_DIM)
    w_qb = jnp.concatenate([wq[:, :, :NOPE_DIM].reshape(Q_LORA, -1), wq[:, :, NOPE_DIM:].reshape(Q_LORA, -1)],
                           axis=1).astype(BF16)
    wk = w_kv_b[0].reshape(KV_LORA, MLA_HEADS, NOPE_DIM + V_DIM)
    w_kvb = jnp.concatenate([wk[:, :, :NOPE_DIM].reshape(KV_LORA, -1), wk[:, :, NOPE_DIM:].reshape(KV_LORA, -1)],
                            axis=1).astype(BF16)
    w_router_p = jnp.pad(w_router[0].astype(F32), ((0, 0), (0, LANES - N_EXPERTS)))
    b_router_p = jnp.pad(b_router[0].astype(F32), (0, LANES - N_EXPERTS), constant_values=NEG_BIG).reshape(1, LANES)
    return dict(
        ln_emb_g=r2(ln_emb_g), ln_emb_b=r2(ln_emb_b), w_in=w_in_p, g_q_a=r2(g_q_a[0]), w_qb=w_qb,
        g_kv_a=r2(g_kv_a[0]), w_kvb=w_kvb, g_q_gqa=r2(g_q_gqa[0]), g_k_gqa=r2(g_k_gqa[0]),
        g_o_mla=r2(g_o_mla[0]), g_o_gqa=r2(g_o_gqa[0]), w_o=w_o[0].astype(BF16), ln1_g=r2(ln1_g[0]),
        ln1_b=r2(ln1_b[0]), w_router=w_router_p, b_router=b_router_p, ln2_g=r2(ln2_g[0]), ln2_b=r2(ln2_b[0]))


def kernel(x, meta_tokens, ln_emb_g, ln_emb_b, w_in, g_q_a, w_q_b, g_kv_a, w_kv_b, g_q_gqa, g_k_gqa, g_o_mla, g_o_gqa,
           w_o, ln1_g, ln1_b, w_router, b_router, w_gate_up, b_gate_up, w_down, b_down, ln2_g, ln2_b):
    b, s, d = x.shape
    n = b * s
    w = _prep_weights(ln_emb_g, ln_emb_b, w_in, g_q_a, w_q_b, g_kv_a, w_kv_b, g_q_gqa, g_k_gqa, g_o_mla, g_o_gqa,
                      w_o, ln1_g, ln1_b, w_router, b_router, ln2_g, ln2_b)

    tok = jnp.arange(s, dtype=I32)
    tabs_real = _rope_tables(tok + N_META, tok // GRID_W, tok % GRID_W)
    mt = jnp.arange(META_PAD, dtype=I32)
    tabs_meta = _rope_tables(mt, jnp.full((META_PAD,), -1, I32), mt)

    qm, km, vm, qg, kg, vg = _project(x, tabs_real, w, ROW_TILE)
    meta = jnp.pad(meta_tokens.astype(x.dtype), ((0, META_PAD - N_META), (0, 0))).reshape(1, META_PAD, d)
    _, km_m, vm_m, _, kg_m, vg_m = _project(meta, tabs_meta, w, META_PAD)

    o_mla = _mla_attention(qm, km, vm, km_m, vm_m)
    o_gqa = _gqa_attention(qg, kg, vg, kg_m, vg_m)

    h1, topi, route = _merge(o_mla, o_gqa, x.reshape(n, d), w)

    ntiles = (n * TOP_K) // MOE_TILE + N_EXPERTS
    ntp = -(-ntiles // LANES) * LANES
    pos, texp, nvalid, ecnt, eoff = _positions(topi, ntp)
    xs = _dispatch(h1, pos, ecnt.reshape(-1), eoff.reshape(-1), ntiles * MOE_TILE)

    wg, wl = _prep_gate_up(w_gate_up[0])
    bgu = b_gate_up[0].astype(F32)
    ys = _grouped_ffn(xs, texp.reshape(-1), nvalid.reshape(-1), wg, wl, w_down[0].astype(BF16),
                      bgu[:, 0::2].reshape(N_EXPERTS, 1, D_FF), bgu[:, 1::2].reshape(N_EXPERTS, 1, D_FF),
                      b_down[0].astype(F32).reshape(N_EXPERTS, 1, d), ntiles)

    out = _combine(pos, route, h1, ys, w)
    return out.reshape(b, s, d)
```

```python
import functools

import jax
import jax.numpy as jnp
import numpy as np
from jax import lax
from jax.experimental import pallas as pl
from jax.experimental.pallas import tpu as pltpu

D_MODEL = 1024
N_META = 16
GRID_W = 64
ROPE_THETA = 10000.0
MLA_HEADS = 4
Q_LORA = 256
KV_LORA = 128
NOPE_DIM = 128
ROPE_DIM = 64
V_DIM = 128
QK_DIM = NOPE_DIM + ROPE_DIM
GQA_HEADS = 4
GQA_KV_HEADS = 2
GQA_DIM = 128
N_EXPERTS = 32
TOP_K = 4
D_FF = D_MODEL
SWIGLU_LIMIT = 7.0
SWIGLU_ALPHA = 1.702
RMS_EPS = 1e-6
LN_EPS = 1e-5
DEPTH = 1
DEEPNORM_ALPHA = (2.0 * DEPTH) ** 0.25

LANES = 128
SUBLANES = 8
SLAB = SUBLANES * SUBLANES
META_PAD = 128
MLA_K = 2 * LANES
NEG_BIG = -1e30
LOG2E = 1.4426950408889634
V_EXT = 2 * LANES

ROW_TILE = 512
Q_TILE_MLA = 1024
Q_TILE_GQA = 512
Q_SUB = 256
TOKEN_TILE = 512
MOE_TILE = 512
VMEM_LIMIT = 56 * 1024 * 1024

F32 = jnp.float32
BF16 = jnp.bfloat16
I32 = jnp.int32


def _layernorm(x, g, b):
    mu = jnp.mean(x, axis=-1, keepdims=True)
    xc = x - mu
    var = jnp.mean(xc * xc, axis=-1, keepdims=True)
    return xc * lax.rsqrt(var + LN_EPS) * g + b


def _rmsnorm(x, g):
    return x * lax.rsqrt(jnp.mean(x * x, axis=-1, keepdims=True) + RMS_EPS) * g


def _swap_halves64(x):
    w = x.shape[-1]
    lane = lax.broadcasted_iota(I32, x.shape, x.ndim - 1)
    return jnp.where((lane % 64) < 32, pltpu.roll(x, w - 32, x.ndim - 1), pltpu.roll(x, 32, x.ndim - 1))


def _rotate(x, cos, sin_signed):
    return x * cos + _swap_halves64(x) * sin_signed


def _proj_body(x_ref, lng_ref, lnb_ref, win_ref, gqa_ref, wqb_ref, gkva_ref, wkvb_ref, gqg_ref, gkg_ref,
               c1_ref, s1_ref, ca_ref, sa_ref,
               qm_ref, km_ref, vm_ref, qg_ref, kg_ref, vg_ref):
    x = x_ref[0]
    h0 = _layernorm(x, lng_ref[...], lnb_ref[...])
    z = jnp.dot(h0.astype(BF16), win_ref[...], preferred_element_type=F32)
    q_a = z[:, 0:256]
    kv_a = z[:, 256:384]
    kpe2 = z[:, 384:512]
    q_g = z[:, 512:1024]
    k_g = z[:, 1024:1280]
    v_g = z[:, 1280:1536]

    c1 = c1_ref[...]
    s1 = s1_ref[...]
    ca = ca_ref[...]
    sa = sa_ref[...]

    q = jnp.dot(_rmsnorm(q_a, gqa_ref[...]).astype(BF16), wqb_ref[...], preferred_element_type=F32)
    kv = jnp.dot(_rmsnorm(kv_a, gkva_ref[...]).astype(BF16), wkvb_ref[...], preferred_element_type=F32)
    krot = _rotate(kpe2, c1, s1)
    lane = lax.broadcasted_iota(I32, krot.shape, 1)
    scale_a = QK_DIM ** -0.5 * LOG2E
    ones = jnp.ones((x.shape[0], LANES), F32)
    for c in range(MLA_HEADS // 2):
        qr = _rotate(q[:, 512 + LANES * c:512 + LANES * (c + 1)], c1, s1)
        for hh in range(2):
            h = 2 * c + hh
            slot = jnp.where((lane // 64) == hh, qr, 0.0)
            qm_ref[0, h] = (jnp.concatenate([q[:, LANES * h:LANES * (h + 1)], slot], axis=1) * scale_a).astype(BF16)
    for h in range(MLA_HEADS):
        km_ref[0, h] = jnp.concatenate([kv[:, LANES * h:LANES * (h + 1)], krot], axis=1).T.astype(BF16)
        vm_ref[0, h] = jnp.concatenate([kv[:, 512 + LANES * h:512 + LANES * (h + 1)], ones], axis=1).astype(BF16)

    scale_b = GQA_DIM ** -0.5 * LOG2E
    gqg = gqg_ref[...]
    gkg = gkg_ref[...]
    for h in range(GQA_HEADS):
        xh = _rmsnorm(q_g[:, LANES * h:LANES * (h + 1)], gqg)
        qg_ref[0, h] = (_rotate(xh, ca, sa) * scale_b).astype(BF16)
    for j in range(GQA_KV_HEADS):
        xh = _rmsnorm(k_g[:, LANES * j:LANES * (j + 1)], gkg)
        kg_ref[0, j] = _rotate(xh, ca, sa).T.astype(BF16)
        vg_ref[0, j] = jnp.concatenate([v_g[:, LANES * j:LANES * (j + 1)], ones], axis=1).astype(BF16)


def _project(x3, tabs, w, tile):
    b, s, d = x3.shape
    nst = s // tile
    full = lambda shape: pl.BlockSpec(shape, lambda bi, si: (0,) * len(shape))
    tab = pl.BlockSpec((tile, LANES), lambda bi, si: (si, 0))
    hm = lambda nh, dd: pl.BlockSpec((1, nh, tile, dd), lambda bi, si: (bi, 0, si, 0))
    hmt = lambda nh, dd: pl.BlockSpec((1, nh, dd, tile), lambda bi, si: (bi, 0, 0, si))
    out_shape = [
        jax.ShapeDtypeStruct((b, MLA_HEADS, s, MLA_K), BF16),
        jax.ShapeDtypeStruct((b, MLA_HEADS, MLA_K, s), BF16),
        jax.ShapeDtypeStruct((b, MLA_HEADS, s, V_EXT), BF16),
        jax.ShapeDtypeStruct((b, GQA_HEADS, s, GQA_DIM), BF16),
        jax.ShapeDtypeStruct((b, GQA_KV_HEADS, GQA_DIM, s), BF16),
        jax.ShapeDtypeStruct((b, GQA_KV_HEADS, s, V_EXT), BF16),
    ]
    return pl.pallas_call(
        _proj_body,
        grid=(b, nst),
        in_specs=[
            pl.BlockSpec((1, tile, d), lambda bi, si: (bi, si, 0)),
            full((1, d)), full((1, d)),
            full(w["w_in"].shape), full((1, Q_LORA)), full(w["w_qb"].shape),
            full((1, KV_LORA)), full(w["w_kvb"].shape), full((1, GQA_DIM)), full((1, GQA_DIM)),
            tab, tab, tab, tab,
        ],
        out_specs=[hm(MLA_HEADS, MLA_K), hmt(MLA_HEADS, MLA_K), hm(MLA_HEADS, V_EXT),
                   hm(GQA_HEADS, GQA_DIM), hmt(GQA_KV_HEADS, GQA_DIM), hm(GQA_KV_HEADS, V_EXT)],
        out_shape=out_shape,
        compiler_params=pltpu.CompilerParams(dimension_semantics=("parallel", "parallel"),
                                             vmem_limit_bytes=VMEM_LIMIT),
        name="proj",
    )(x3, w["ln_emb_g"], w["ln_emb_b"], w["w_in"], w["g_q_a"], w["w_qb"], w["g_kv_a"], w["w_kvb"],
      w["g_q_gqa"], w["g_k_gqa"], *tabs)


def _softmax_pv(q, kt, v, kmt, vm):
    s = jnp.dot(q, kt, preferred_element_type=F32)
    sm = jnp.dot(q, kmt, preferred_element_type=F32)
    col = lax.broadcasted_iota(I32, sm.shape, 1)
    sm = jnp.where(col < N_META, sm, NEG_BIG)
    m = jnp.maximum(jnp.max(s, axis=1, keepdims=True), jnp.max(sm, axis=1, keepdims=True))
    p = jnp.exp2(s - m).astype(BF16)
    pm = jnp.exp2(sm - m).astype(BF16)
    acc = jnp.dot(p, v, preferred_element_type=F32) + jnp.dot(pm, vm, preferred_element_type=F32)
    return acc[:, 0:V_DIM] / acc[:, V_DIM:V_EXT]


def _mla_attn_body(q_ref, k_ref, v_ref, km_ref, vm_ref, o_ref):
    for i in range(q_ref.shape[2] // Q_SUB):
        rows = pl.ds(i * Q_SUB, Q_SUB)
        o = _softmax_pv(q_ref[0, 0, rows, :], k_ref[0, 0], v_ref[0, 0], km_ref[0, 0], vm_ref[0, 0])
        o_ref[rows, :] = o.astype(o_ref.dtype)


def _gqa_attn_body(q_ref, k_ref, v_ref, km_ref, vm_ref, o_ref):
    for g in range(2):
        for i in range(q_ref.shape[2] // Q_SUB):
            rows = pl.ds(i * Q_SUB, Q_SUB)
            o = _softmax_pv(q_ref[0, g, rows, :], k_ref[0, 0], v_ref[0, 0], km_ref[0, 0], vm_ref[0, 0])
            o_ref[rows, GQA_DIM * g:GQA_DIM * (g + 1)] = o.astype(o_ref.dtype)


def _mla_attention(qm, kmt, vm, kmeta_t, vmeta):
    b, h, s, dk = qm.shape
    tq = Q_TILE_MLA
    nq = s // tq
    return pl.pallas_call(
        _mla_attn_body,
        grid=(b, h, nq),
        in_specs=[
            pl.BlockSpec((1, 1, tq, dk), lambda bi, hi, qi: (bi, hi, qi, 0)),
            pl.BlockSpec((1, 1, dk, s), lambda bi, hi, qi: (bi, hi, 0, 0)),
            pl.BlockSpec((1, 1, s, V_EXT), lambda bi, hi, qi: (bi, hi, 0, 0)),
            pl.BlockSpec((1, 1, dk, META_PAD), lambda bi, hi, qi: (0, hi, 0, 0)),
            pl.BlockSpec((1, 1, META_PAD, V_EXT), lambda bi, hi, qi: (0, hi, 0, 0)),
        ],
        out_specs=pl.BlockSpec((tq, V_DIM), lambda bi, hi, qi: (bi * nq + qi, hi)),
        out_shape=jax.ShapeDtypeStruct((b * s, h * V_DIM), BF16),
        compiler_params=pltpu.CompilerParams(dimension_semantics=("parallel", "parallel", "parallel"),
                                             vmem_limit_bytes=VMEM_LIMIT),
        name="mla_attn",
    )(qm, kmt, vm, kmeta_t, vmeta)


def _gqa_attention(qg, kgt, vg, kmeta_t, vmeta):
    b, h, s, d = qg.shape
    hk = kgt.shape[1]
    tq = Q_TILE_GQA
    nq = s // tq
    return pl.pallas_call(
        _gqa_attn_body,
        grid=(b, hk, nq),
        in_specs=[
            pl.BlockSpec((1, 2, tq, d), lambda bi, ji, qi: (bi, ji, qi, 0)),
            pl.BlockSpec((1, 1, d, s), lambda bi, ji, qi: (bi, ji, 0, 0)),
            pl.BlockSpec((1, 1, s, V_EXT), lambda bi, ji, qi: (bi, ji, 0, 0)),
            pl.BlockSpec((1, 1, d, META_PAD), lambda bi, ji, qi: (0, ji, 0, 0)),
            pl.BlockSpec((1, 1, META_PAD, V_EXT), lambda bi, ji, qi: (0, ji, 0, 0)),
        ],
        out_specs=pl.BlockSpec((tq, 2 * d), lambda bi, ji, qi: (bi * nq + qi, ji)),
        out_shape=jax.ShapeDtypeStruct((b * s, h * d), BF16),
        compiler_params=pltpu.CompilerParams(dimension_semantics=("parallel", "parallel", "parallel"),
                                             vmem_limit_bytes=VMEM_LIMIT),
        name="gqa_attn",
    )(qg, kgt, vg, kmeta_t, vmeta)


def _to_tiles(ref, x):
    r = x.shape[0]
    for s in range(x.shape[1] // LANES):
        ref[:, s] = x[:, LANES * s:LANES * (s + 1)].reshape(r // SUBLANES, SUBLANES, LANES)


def _from_tiles(ref):
    r = ref.shape[0] * SUBLANES
    return jnp.concatenate([ref[:, s].reshape(r, LANES) for s in range(ref.shape[1])], axis=1)


def _merge_body(om_ref, og_ref, x_ref, lng_ref, lnb_ref, gom_ref, gog_ref, wo_ref, l1g_ref, l1b_ref,
                wr_ref, br_ref, h1_ref, topi_ref, gate_ref):
    t = x_ref.shape[0]
    h0 = _layernorm(x_ref[...], lng_ref[...], lnb_ref[...])
    nm = _rmsnorm(om_ref[...].astype(F32), gom_ref[...]).astype(BF16)
    ng = _rmsnorm(og_ref[...].astype(F32), gog_ref[...]).astype(BF16)
    half = nm.shape[1]
    mix = jnp.dot(nm, wo_ref[0:half, :], preferred_element_type=F32)
    mix = mix + jnp.dot(ng, wo_ref[half:2 * half, :], preferred_element_type=F32)
    h1 = _layernorm(DEEPNORM_ALPHA * h0 + mix, l1g_ref[...], l1b_ref[...])
    _to_tiles(h1_ref, h1)

    logits = jnp.dot(h1, wr_ref[...], precision=lax.Precision.HIGHEST, preferred_element_type=F32) + br_ref[...]
    cur = logits.T[0:N_EXPERTS, :]
    eidx = lax.broadcasted_iota(I32, cur.shape, 0)
    vals, idxs = [], []
    for _ in range(TOP_K):
        m = jnp.max(cur, axis=0, keepdims=True)
        i = jnp.min(jnp.where(cur == m, eidx, N_EXPERTS), axis=0, keepdims=True)
        vals.append(m)
        idxs.append(i)
        cur = jnp.where(eidx == i, -jnp.inf, cur)
    ex = [jnp.exp(v - vals[0]) for v in vals]
    den = ex[0] + ex[1] + ex[2] + ex[3]
    sub = lax.broadcasted_iota(I32, (8, t), 0)
    ti = jnp.zeros((8, t), I32)
    gt = jnp.zeros((8, t), F32)
    for k in range(TOP_K):
        ti = jnp.where(sub == k, idxs[k], ti)
        gt = jnp.where(sub == k, ex[k] / den, gt)
    topi_ref[...] = ti[0:TOP_K, :]
    gate_ref[...] = gt[0:TOP_K, :]


def _merge(o_mla, o_gqa, x2, w):
    n, d = x2.shape
    tile = ROW_TILE
    half = o_mla.shape[1]
    full = lambda shape: pl.BlockSpec(shape, lambda i: (0,) * len(shape))
    row = lambda width: pl.BlockSpec((tile, width), lambda i: (i, 0))
    lane_major = pl.BlockSpec((TOP_K, tile), lambda i: (0, i))
    return pl.pallas_call(
        _merge_body,
        grid=(n // tile,),
        in_specs=[row(half), row(half), row(d), full((1, d)), full((1, d)), full((1, half)), full((1, half)),
                  full((d, d)), full((1, d)), full((1, d)), full((d, LANES)), full((1, LANES))],
        out_specs=[pl.BlockSpec((tile // SUBLANES, d // LANES, SUBLANES, LANES), lambda i: (i, 0, 0, 0)),
                   lane_major, lane_major],
        out_shape=[jax.ShapeDtypeStruct((n // SUBLANES, d // LANES, SUBLANES, LANES), F32),
                   jax.ShapeDtypeStruct((TOP_K, n), I32), jax.ShapeDtypeStruct((TOP_K, n), F32)],
        compiler_params=pltpu.CompilerParams(dimension_semantics=("parallel",), vmem_limit_bytes=VMEM_LIMIT),
        name="merge",
    )(o_mla, o_gqa, x2, w["ln_emb_g"], w["ln_emb_b"], w["g_o_mla"], w["g_o_gqa"], w["w_o"],
      w["ln1_g"], w["ln1_b"], w["w_router"], w["b_router"])


def _lanes_from_sublanes(col):
    diag = lax.broadcasted_iota(I32, col.shape, 0) == lax.broadcasted_iota(I32, col.shape, 1)
    return jnp.sum(jnp.where(diag, col, 0.0), axis=0, keepdims=True)


def _positions_body(topi_ref, lpb_ref, gstart_ref, nslab_ref, lstart_ref, texp_ref, nvalid_ref, ecnt_ref, eoff_ref,
                    cnt_sc, carry_sc, off_sc, *, ntp):
    p = pl.program_id(0)
    j = pl.program_id(1)
    tl = topi_ref.shape[1]
    topi = topi_ref[...]
    eidx = lax.broadcasted_iota(I32, (N_EXPERTS, tl), 0)
    ohs = [eidx == topi[k:k + 1, :] for k in range(TOP_K)]
    onehot = ohs[0].astype(F32) + ohs[1].astype(F32) + ohs[2].astype(F32) + ohs[3].astype(F32)
    cnt = jnp.sum(onehot, axis=1, keepdims=True)
    run = jnp.floor((cnt + (SUBLANES - 1)) * (1.0 / SUBLANES)) * SUBLANES
    tile_run = jnp.broadcast_to(run, (N_EXPERTS, LANES))
    r = lax.broadcasted_iota(I32, (N_EXPERTS, N_EXPERTS), 0)
    c = lax.broadcasted_iota(I32, (N_EXPERTS, N_EXPERTS), 1)
    lower = (c < r).astype(F32)

    @pl.when(jnp.logical_and(p == 0, j == 0))
    def _():
        cnt_sc[...] = jnp.zeros_like(cnt_sc)

    @pl.when(p == 0)
    def _():
        cnt_sc[...] += tile_run

    @pl.when(jnp.logical_and(p == 1, j == 0))
    def _():
        tot = cnt_sc[...]
        pc = jnp.floor((tot + (MOE_TILE - 1)) * (1.0 / MOE_TILE)) * MOE_TILE
        off = jnp.dot(lower, pc, precision=lax.Precision.HIGHEST, preferred_element_type=F32)
        off_sc[...] = off
        carry_sc[...] = jnp.zeros_like(carry_sc)
        cumend = off + pc
        tstart = lax.broadcasted_iota(I32, (N_EXPERTS, ntp), 1).astype(F32) * MOE_TILE
        te = jnp.sum((jnp.broadcast_to(cumend[:, 0:1], (N_EXPERTS, ntp)) <= tstart).astype(I32), axis=0, keepdims=True)
        texp_ref[...] = jnp.minimum(te, N_EXPERTS - 1)
        nvalid_ref[...] = (cumend[N_EXPERTS - 1:N_EXPERTS, :] * (1.0 / MOE_TILE)).astype(I32)
        ecnt_ref[...] = _lanes_from_sublanes(tot).astype(I32)
        eoff_ref[...] = _lanes_from_sublanes(off).astype(I32)

    @pl.when(p == 1)
    def _():
        rr = lax.broadcasted_iota(I32, (tl, tl), 0)
        cc = lax.broadcasted_iota(I32, (tl, tl), 1)
        upper = (rr < cc).astype(BF16)
        before = jnp.dot(onehot.astype(BF16), upper, preferred_element_type=F32)
        loff = jnp.dot(lower, tile_run, precision=lax.Precision.HIGHEST, preferred_element_type=F32)
        base = before + loff[:, 0:1]
        sub = lax.broadcasted_iota(I32, (SUBLANES, tl), 0)
        out = jnp.zeros((SUBLANES, tl), F32)
        for k in range(TOP_K):
            pk = jnp.sum(jnp.where(ohs[k], base, 0.0), axis=0, keepdims=True)
            out = jnp.where(sub == k, pk, out)
        lp = out[0:TOP_K, :].astype(I32)
        lpb_ref[...] = (lp >> 3) * SLAB + (lp & (SUBLANES - 1))
        inv = 1.0 / SUBLANES
        gstart_ref[0] = (_lanes_from_sublanes(off_sc[...] + carry_sc[...]) * inv).astype(I32)
        nslab_ref[0] = (_lanes_from_sublanes(tile_run) * inv).astype(I32)
        lstart_ref[0] = (_lanes_from_sublanes(loff) * inv).astype(I32)
        carry_sc[...] += tile_run


def _positions(topi, ntp):
    n = topi.shape[1]
    tl = TOKEN_TILE
    const = lambda shape: pl.BlockSpec(shape, lambda p, j: (0, 0))
    per_tile = pl.BlockSpec((1, 1, LANES), lambda p, j: (j * p, 0, 0))
    tab = jax.ShapeDtypeStruct((n // tl, 1, LANES), I32)
    return pl.pallas_call(
        functools.partial(_positions_body, ntp=ntp),
        grid=(2, n // tl),
        in_specs=[pl.BlockSpec((TOP_K, tl), lambda p, j: (0, j))],
        out_specs=[pl.BlockSpec((TOP_K, tl), lambda p, j: (0, j * p)), per_tile, per_tile, per_tile,
                   const((1, ntp)), const((1, LANES)), const((1, LANES)), const((1, LANES))],
        out_shape=[jax.ShapeDtypeStruct((TOP_K, n), I32), tab, tab, tab, jax.ShapeDtypeStruct((1, ntp), I32),
                   jax.ShapeDtypeStruct((1, LANES), I32), jax.ShapeDtypeStruct((1, LANES), I32),
                   jax.ShapeDtypeStruct((1, LANES), I32)],
        scratch_shapes=[pltpu.VMEM((N_EXPERTS, LANES), F32)] * 3,
        compiler_params=pltpu.CompilerParams(dimension_semantics=("arbitrary", "arbitrary")),
        name="positions",
    )(topi)


STAGE_ROWS = TOKEN_TILE * TOP_K + N_EXPERTS * SUBLANES
STAGE_FLAT = STAGE_ROWS * SUBLANES


def _load_row(ref, flat_start):
    return ref[pl.ds(flat_start, SUBLANES, stride=SUBLANES), :]


def _store_row(ref, flat_start, v):
    ref[pl.ds(flat_start, SUBLANES, stride=SUBLANES), :] = v


def _dispatch_body(ecnt_ref, eoff_ref, lpb_ref, gstart_ref, nslab_ref, lstart_ref, h_ref, xs_ref,
                   stg_ref, zero_ref, issued_ref, sem, zsem):
    g = pl.program_id(0)
    ng = pl.num_programs(0)
    slot = lax.rem(g, 2)
    sbase = slot * STAGE_FLAT

    def slab_wait(sl):
        pltpu.make_async_copy(stg_ref.at[pl.ds(0, SLAB)], xs_ref.at[pl.ds(0, SLAB)], sem.at[sl]).wait()

    def drain(sl):
        lax.fori_loop(0, issued_ref[sl], lambda i, c: (slab_wait(sl), c)[1], 0)

    @pl.when(g >= 2)
    def _():
        drain(slot)

    def zero_last(e, c):
        last = jnp.maximum(lstart_ref[0, 0, e] + nslab_ref[0, 0, e] - 1, 0)
        stg_ref[pl.ds(sbase + last * SLAB, SLAB), :] = jnp.zeros((SLAB, LANES), F32)
        return c

    lax.fori_loop(0, N_EXPERTS, zero_last, 0)

    def move(i, c):
        for u in range(SUBLANES):
            tok = i * SUBLANES + u
            v = _load_row(h_ref, i * SLAB + u)
            for k in range(TOP_K):
                _store_row(stg_ref, sbase + lpb_ref[k, tok], v)
        return c

    lax.fori_loop(0, TOKEN_TILE // SUBLANES, move, 0)

    def send_runs(e, total):
        n = nslab_ref[0, 0, e]
        src = sbase + lstart_ref[0, 0, e] * SLAB
        dst = gstart_ref[0, 0, e] * SLAB

        def send(s, c):
            pltpu.make_async_copy(stg_ref.at[pl.ds(src + s * SLAB, SLAB)], xs_ref.at[pl.ds(dst + s * SLAB, SLAB)],
                                  sem.at[slot]).start()
            return c

        lax.fori_loop(0, n, send, 0)
        return total + n

    issued_ref[slot] = lax.fori_loop(0, N_EXPERTS, send_runs, 0)

    @pl.when(g == ng - 1)
    def _():
        zero_ref[...] = jnp.zeros_like(zero_ref)

        def zero_copy(dst_slab):
            return pltpu.make_async_copy(zero_ref, xs_ref.at[pl.ds(dst_slab * SLAB, SLAB)], zsem)

        def pad_segment(e, c):
            rows = ecnt_ref[e]
            first = (eoff_ref[e] + rows) >> 3
            npad = lax.rem(MOE_TILE - lax.rem(rows, MOE_TILE), MOE_TILE) >> 3
            lax.fori_loop(0, npad, lambda s, cc: (zero_copy(first + s).start(), cc)[1], 0)
            lax.fori_loop(0, npad, lambda s, cc: (zero_copy(first + s).wait(), cc)[1], 0)
            return c

        lax.fori_loop(0, N_EXPERTS, pad_segment, 0)

        @pl.when(ng > 1)
        def _():
            drain(1 - slot)

        drain(slot)


def _dispatch(h1_flat, lpb, gstart, nslab, lstart, ecnt, eoff, rows_pad):
    n8 = h1_flat.shape[0]
    t = TOKEN_TILE
    smem = lambda shape, imap: pl.BlockSpec(shape, imap, memory_space=pltpu.SMEM)
    per_tile = smem((1, 1, LANES), lambda i, c, o: (i, 0, 0))
    grid_spec = pltpu.PrefetchScalarGridSpec(
        num_scalar_prefetch=2,
        grid=(n8 // (t * SUBLANES),),
        in_specs=[smem((TOP_K, t), lambda i, c, o: (0, i)), per_tile, per_tile, per_tile,
                  pl.BlockSpec((t * SUBLANES, LANES), lambda i, c, o: (i, 0))],
        out_specs=pl.BlockSpec(memory_space=pl.ANY),
        scratch_shapes=[pltpu.VMEM((2 * STAGE_FLAT, LANES), F32), pltpu.VMEM((SLAB, LANES), F32),
                        pltpu.SMEM((2,), I32), pltpu.SemaphoreType.DMA((2,)), pltpu.SemaphoreType.DMA],
    )
    return pl.pallas_call(
        _dispatch_body,
        grid_spec=grid_spec,
        out_shape=jax.ShapeDtypeStruct((rows_pad * SUBLANES, LANES), F32),
        compiler_params=pltpu.CompilerParams(dimension_semantics=("arbitrary",), vmem_limit_bytes=VMEM_LIMIT),
        name="dispatch",
    )(ecnt, eoff, lpb, gstart, nslab, lstart, h1_flat)


def _wprep_body(w_ref, wg_ref, wl_ref):
    blk = 2 * LANES
    r = lax.broadcasted_iota(I32, (blk, blk), 0)
    c = lax.broadcasted_iota(I32, (blk, blk), 1)
    sel = (r == jnp.where(c < LANES, 2 * c, 2 * (c - LANES) + 1)).astype(BF16)
    nblk = w_ref.shape[2] // blk
    for b in range(nblk):
        wb = w_ref[0, :, blk * b:blk * (b + 1)].astype(BF16)
        y = jnp.dot(wb, sel, preferred_element_type=F32).astype(BF16)
        wg_ref[0, :, LANES * b:LANES * (b + 1)] = y[:, 0:LANES]
        wl_ref[0, :, LANES * b:LANES * (b + 1)] = y[:, LANES:blk]


def _prep_gate_up(w_gate_up):
    e, d, f2 = w_gate_up.shape
    cb = 1024
    out = jax.ShapeDtypeStruct((e, d, f2 // 2), BF16)
    return pl.pallas_call(
        _wprep_body,
        grid=(e, f2 // cb),
        in_specs=[pl.BlockSpec((1, d, cb), lambda ei, ci: (ei, 0, ci))],
        out_specs=[pl.BlockSpec((1, d, cb // 2), lambda ei, ci: (ei, 0, ci))] * 2,
        out_shape=[out, out],
        compiler_params=pltpu.CompilerParams(dimension_semantics=("parallel", "parallel"),
                                             vmem_limit_bytes=VMEM_LIMIT),
        name="wprep",
    )(w_gate_up)


def _ffn_body(texp_ref, nvalid_ref, x_ref, wg_ref, wl_ref, wd_ref, bg_ref, bl_ref, bd_ref, y_ref):
    @pl.when(pl.program_id(0) < nvalid_ref[0])
    def _():
        x = _from_tiles(x_ref).astype(BF16)
        hg = jnp.dot(x, wg_ref[0], preferred_element_type=F32) + bg_ref[0]
        hl = jnp.dot(x, wl_ref[0], preferred_element_type=F32) + bl_ref[0]
        g = jnp.minimum(hg, SWIGLU_LIMIT)
        lin = jnp.clip(hl, -SWIGLU_LIMIT, SWIGLU_LIMIT)
        act = g * (1.0 / (1.0 + jnp.exp(-SWIGLU_ALPHA * g))) * (lin + 1.0)
        _to_tiles(y_ref, jnp.dot(act.astype(BF16), wd_ref[0], preferred_element_type=F32) + bd_ref[0])


def _grouped_ffn(xs_tiles, texp, nvalid, wg, wl, wd, bg, bl, bd, ntiles):
    d = wg.shape[1]
    tm = MOE_TILE
    f = wg.shape[2]
    xmap = lambda i, te, nv: (jnp.minimum(i, nv[0] - 1), 0, 0, 0)
    wmap = lambda i, te, nv: (te[i], 0, 0)
    rows_blk = pl.BlockSpec((tm // SUBLANES, d // LANES, SUBLANES, LANES), xmap)
    grid_spec = pltpu.PrefetchScalarGridSpec(
        num_scalar_prefetch=2,
        grid=(ntiles,),
        in_specs=[rows_blk,
                  pl.BlockSpec((1, d, f), wmap), pl.BlockSpec((1, d, f), wmap), pl.BlockSpec((1, f, d), wmap),
                  pl.BlockSpec((1, 1, f), wmap), pl.BlockSpec((1, 1, f), wmap), pl.BlockSpec((1, 1, d), wmap)],
        out_specs=rows_blk,
    )
    return pl.pallas_call(
        _ffn_body,
        grid_spec=grid_spec,
        out_shape=jax.ShapeDtypeStruct(xs_tiles.shape, F32),
        compiler_params=pltpu.CompilerParams(dimension_semantics=("arbitrary",), vmem_limit_bytes=VMEM_LIMIT),
        name="ffn",
    )(texp, nvalid, xs_tiles, wg, wl, wd, bg, bl, bd)


def _combine_body(lpb_ref, gate_ref, gstart_ref, nslab_ref, lstart_ref, gstart2_ref, nslab2_ref, lstart2_ref,
                  h_ref, l2g_ref, l2b_ref, ys_ref, o_ref, stg_ref, moe_ref, sem):
    g = pl.program_id(0)
    ng = pl.num_programs(0)
    slot = lax.rem(g, 2)
    sbase = slot * STAGE_FLAT

    def fetch_runs(gs_ref, ns_ref, ls_ref, sl):
        def per_expert(e, c):
            src = gs_ref[0, 0, e] * SLAB
            dst = sl * STAGE_FLAT + ls_ref[0, 0, e] * SLAB

            def fetch(s, cc):
                pltpu.make_async_copy(ys_ref.at[pl.ds(src + s * SLAB, SLAB)], stg_ref.at[pl.ds(dst + s * SLAB, SLAB)],
                                      sem.at[sl]).start()
                return cc

            lax.fori_loop(0, ns_ref[0, 0, e], fetch, 0)
            return c

        lax.fori_loop(0, N_EXPERTS, per_expert, 0)

    @pl.when(g == 0)
    def _():
        fetch_runs(gstart_ref, nslab_ref, lstart_ref, 0)

    @pl.when(g + 1 < ng)
    def _():
        fetch_runs(gstart2_ref, nslab2_ref, lstart2_ref, 1 - slot)

    total = lax.fori_loop(0, N_EXPERTS, lambda e, c: c + nslab_ref[0, 0, e], 0)

    def slab_wait(s, c):
        pltpu.make_async_copy(ys_ref.at[pl.ds(0, SLAB)], stg_ref.at[pl.ds(0, SLAB)], sem.at[slot]).wait()
        return c

    lax.fori_loop(0, total, slab_wait, 0)

    def gather(i, c):
        for u in range(SUBLANES):
            tok = i * SUBLANES + u
            acc = gate_ref[0, tok] * _load_row(stg_ref, sbase + lpb_ref[0, tok])
            for k in range(1, TOP_K):
                acc = acc + gate_ref[k, tok] * _load_row(stg_ref, sbase + lpb_ref[k, tok])
            _store_row(moe_ref, i * SLAB + u, acc)
        return c

    lax.fori_loop(0, TOKEN_TILE // SUBLANES, gather, 0)

    moe = jnp.concatenate(
        [jnp.concatenate([moe_ref[pl.ds(i * SLAB + s * SUBLANES, SUBLANES), :] for i in range(TOKEN_TILE // SUBLANES)],
                         axis=0) for s in range(SUBLANES)], axis=1)
    o_ref[...] = _layernorm(DEEPNORM_ALPHA * _from_tiles(h_ref) + moe, l2g_ref[...], l2b_ref[...])


def _combine(lpb, gates, gstart, nslab, lstart, h1_tiles, ys_flat, w):
    d = h1_tiles.shape[1] * LANES
    n = h1_tiles.shape[0] * SUBLANES
    t = TOKEN_TILE
    ng = n // t
    full = lambda shape: pl.BlockSpec(shape, lambda i: (0,) * len(shape))
    smem = lambda shape, imap: pl.BlockSpec(shape, imap, memory_space=pltpu.SMEM)
    this_tile = smem((1, 1, LANES), lambda i: (i, 0, 0))
    next_tile = smem((1, 1, LANES), lambda i: (jnp.minimum(i + 1, ng - 1), 0, 0))
    return pl.pallas_call(
        _combine_body,
        grid=(ng,),
        in_specs=[smem((TOP_K, t), lambda i: (0, i)), smem((TOP_K, t), lambda i: (0, i)),
                  this_tile, this_tile, this_tile, next_tile, next_tile, next_tile,
                  pl.BlockSpec((t // SUBLANES, d // LANES, SUBLANES, LANES), lambda i: (i, 0, 0, 0)),
                  full((1, d)), full((1, d)),
                  pl.BlockSpec(memory_space=pl.ANY)],
        out_specs=pl.BlockSpec((t, d), lambda i: (i, 0)),
        out_shape=jax.ShapeDtypeStruct((n, d), F32),
        scratch_shapes=[pltpu.VMEM((2 * STAGE_FLAT, LANES), F32), pltpu.VMEM((t * SUBLANES, LANES), F32),
                        pltpu.SemaphoreType.DMA((2,))],
        compiler_params=pltpu.CompilerParams(dimension_semantics=("arbitrary",), vmem_limit_bytes=VMEM_LIMIT),
        name="combine",
    )(lpb, gates, gstart, nslab, lstart, gstart, nslab, lstart, h1_tiles, w["ln2_g"], w["ln2_b"], ys_flat)


def _rope_tables(pos_1d, row, col):
    inv = ROPE_THETA ** (-jnp.arange(0, ROPE_DIM, 2, dtype=F32) / ROPE_DIM)

    def cs(p):
        ang = p.astype(F32)[:, None] * inv[None, :]
        return jnp.cos(ang), jnp.sin(ang)

    c1, s1 = cs(pos_1d)
    cr, sr = cs(row)
    cc, sc = cs(col)
    return (jnp.concatenate([c1, c1, c1, c1], axis=1), jnp.concatenate([-s1, s1, -s1, s1], axis=1),
            jnp.concatenate([cr, cr, cc, cc], axis=1), jnp.concatenate([-sr, sr, -sc, sc], axis=1))


def _prep_weights(ln_emb_g, ln_emb_b, w_in, g_q_a, w_q_b, g_kv_a, w_kv_b, g_q_gqa, g_k_gqa, g_o_mla, g_o_gqa, w_o,
                  ln1_g, ln1_b, w_router, b_router, ln2_g, ln2_b):
    r2 = lambda v: v.reshape(1, -1).astype(F32)
    o = np.cumsum([0, Q_LORA, KV_LORA, ROPE_DIM, GQA_HEADS * GQA_DIM, GQA_KV_HEADS * GQA_DIM, GQA_KV_HEADS * GQA_DIM])
    wi = w_in[0]
    seg = [wi[:, o[i]:o[i + 1]] for i in range(6)]
    w_in_p = jnp.concatenate([seg[0], seg[1], seg[2], seg[2], seg[3], seg[4], seg[5]], axis=1).astype(BF16)
    wq = w_q_b[0].reshape(Q_LORA, MLA_HEADS, QK_DIM)
    w_qb = jnp.concatenate([wq[:, :, :NOPE_DIM].reshape(Q_LORA, -1), wq[:, :, NOPE_DIM:].reshape(Q_LORA, -1)],
                           axis=1).astype(BF16)
    wk = w_kv_b[0].reshape(KV_LORA, MLA_HEADS, NOPE_DIM + V_DIM)
    w_kvb = jnp.concatenate([wk[:, :, :NOPE_DIM].reshape(KV_LORA, -1), wk[:, :, NOPE_DIM:].reshape(KV_LORA, -1)],
                            axis=1).astype(BF16)
    w_router_p = jnp.pad(w_router[0].astype(F32), ((0, 0), (0, LANES - N_EXPERTS)))
    b_router_p = jnp.pad(b_router[0].astype(F32), (0, LANES - N_EXPERTS), constant_values=NEG_BIG).reshape(1, LANES)
    return dict(
        ln_emb_g=r2(ln_emb_g), ln_emb_b=r2(ln_emb_b), w_in=w_in_p, g_q_a=r2(g_q_a[0]), w_qb=w_qb,
        g_kv_a=r2(g_kv_a[0]), w_kvb=w_kvb, g_q_gqa=r2(g_q_gqa[0]), g_k_gqa=r2(g_k_gqa[0]),
        g_o_mla=r2(g_o_mla[0]), g_o_gqa=r2(g_o_gqa[0]), w_o=w_o[0].astype(BF16), ln1_g=r2(ln1_g[0]),
        ln1_b=r2(ln1_b[0]), w_router=w_router_p, b_router=b_router_p, ln2_g=r2(ln2_g[0]), ln2_b=r2(ln2_b[0]))


def kernel(x, meta_tokens, ln_emb_g, ln_emb_b, w_in, g_q_a, w_q_b, g_kv_a, w_kv_b, g_q_gqa, g_k_gqa, g_o_mla, g_o_gqa,
           w_o, ln1_g, ln1_b, w_router, b_router, w_gate_up, b_gate_up, w_down, b_down, ln2_g, ln2_b):
    b, s, d = x.shape
    n = b * s
    w = _prep_weights(ln_emb_g, ln_emb_b, w_in, g_q_a, w_q_b, g_kv_a, w_kv_b, g_q_gqa, g_k_gqa, g_o_mla, g_o_gqa,
                      w_o, ln1_g, ln1_b, w_router, b_router, ln2_g, ln2_b)

    tok = jnp.arange(s, dtype=I32)
    tabs_real = _rope_tables(tok + N_META, tok // GRID_W, tok % GRID_W)
    mt = jnp.arange(META_PAD, dtype=I32)
    tabs_meta = _rope_tables(mt, jnp.full((META_PAD,), -1, I32), mt)

    qm, km, vm, qg, kg, vg = _project(x, tabs_real, w, ROW_TILE)
    meta = jnp.pad(meta_tokens.astype(x.dtype), ((0, META_PAD - N_META), (0, 0))).reshape(1, META_PAD, d)
    _, km_m, vm_m, _, kg_m, vg_m = _project(meta, tabs_meta, w, META_PAD)

    o_mla = _mla_attention(qm, km, vm, km_m, vm_m)
    o_gqa = _gqa_attention(qg, kg, vg, kg_m, vg_m)

    h1_tiles, topi, gates = _merge(o_mla, o_gqa, x.reshape(n, d), w)

    run_rows = n * TOP_K + (n // TOKEN_TILE) * N_EXPERTS * (SUBLANES - 1)
    ntiles = -(-run_rows // MOE_TILE) + N_EXPERTS
    ntp = -(-ntiles // LANES) * LANES
    rows_pad = ntiles * MOE_TILE
    lpb, gstart, nslab, lstart, texp, nvalid, ecnt, eoff = _positions(topi, ntp)
    flat = lambda a: a.reshape(-1, LANES)
    xs_flat = _dispatch(flat(h1_tiles), lpb, gstart, nslab, lstart, ecnt.reshape(-1), eoff.reshape(-1), rows_pad)

    wg, wl = _prep_gate_up(w_gate_up[0])
    bgu = b_gate_up[0].astype(F32)
    tiles = lambda a: a.reshape(-1, d // LANES, SUBLANES, LANES)
    ys_tiles = _grouped_ffn(tiles(xs_flat), texp.reshape(-1), nvalid.reshape(-1), wg, wl, w_down[0].astype(BF16),
                            bgu[:, 0::2].reshape(N_EXPERTS, 1, D_FF), bgu[:, 1::2].reshape(N_EXPERTS, 1, D_FF),
                            b_down[0].astype(F32).reshape(N_EXPERTS, 1, d), ntiles)

    out = _combine(lpb, gates, gstart, nslab, lstart, h1_tiles, flat(ys_tiles), w)
    return out.reshape(b, s, d)
```

```python
import functools

import jax
import jax.numpy as jnp
import numpy as np
from jax import lax
from jax.experimental import pallas as pl
from jax.experimental.pallas import tpu as pltpu

D_MODEL = 1024
N_META = 16
GRID_W = 64
ROPE_THETA = 10000.0
MLA_HEADS = 4
Q_LORA = 256
KV_LORA = 128
NOPE_DIM = 128
ROPE_DIM = 64
V_DIM = 128
QK_DIM = NOPE_DIM + ROPE_DIM
GQA_HEADS = 4
GQA_KV_HEADS = 2
GQA_DIM = 128
N_EXPERTS = 32
TOP_K = 4
D_FF = D_MODEL
SWIGLU_LIMIT = 7.0
SWIGLU_ALPHA = 1.702
RMS_EPS = 1e-6
LN_EPS = 1e-5
DEPTH = 1
DEEPNORM_ALPHA = (2.0 * DEPTH) ** 0.25

LANES = 128
SUBLANES = 8
SLAB = SUBLANES * SUBLANES
META_PAD = 128
MLA_K = 2 * LANES
NEG_BIG = -1e30
LOG2E = 1.4426950408889634
V_EXT = 2 * LANES

ROW_TILE = 512
Q_TILE_MLA = 1024
Q_TILE_GQA = 512
Q_SUB = 256
TOKEN_TILE = 512
MOE_TILE = 512
VMEM_LIMIT = 56 * 1024 * 1024

F32 = jnp.float32
BF16 = jnp.bfloat16
I32 = jnp.int32


def _layernorm(x, g, b):
    mu = jnp.mean(x, axis=-1, keepdims=True)
    xc = x - mu
    var = jnp.mean(xc * xc, axis=-1, keepdims=True)
    return xc * lax.rsqrt(var + LN_EPS) * g + b


def _rmsnorm(x, g):
    return x * lax.rsqrt(jnp.mean(x * x, axis=-1, keepdims=True) + RMS_EPS) * g


def _swap_halves64(v):
    shp = v.shape
    return v.reshape(shp[:-1] + (shp[-1] // 64, 2, 32))[..., ::-1, :].reshape(shp)


PROJ_SUB = 256


def _proj_chain(x, lng, lnb, win_ref, gqa, wqb_ref, gkva, wkvb_ref, gqg, gqgs, gkg, gkgs, c1, s1, ca, sa, put):
    r = x.shape[0]
    h0 = _layernorm(x, lng, lnb)
    z = jnp.dot(h0.astype(BF16), win_ref[...], preferred_element_type=F32)
    q_a = z[:, 0:256]
    kv_a = z[:, 256:384]
    kpe2 = z[:, 384:512]
    kpe2s = z[:, 512:640]
    q_g = z[:, 640:1152]
    q_gs = z[:, 1152:1664]
    k_g = z[:, 1664:1920]
    k_gs = z[:, 1920:2176]
    v_g = z[:, 2176:2432]

    q = jnp.dot(_rmsnorm(q_a, gqa).astype(BF16), wqb_ref[...], preferred_element_type=F32)
    kv = jnp.dot(_rmsnorm(kv_a, gkva).astype(BF16), wkvb_ref[...], preferred_element_type=F32)
    krot = kpe2 * c1 + kpe2s * s1
    lane = lax.broadcasted_iota(I32, krot.shape, 1)
    scale_a = QK_DIM ** -0.5 * LOG2E
    ones = jnp.ones((r, LANES), F32)
    for c in range(MLA_HEADS // 2):
        lo = LANES * c
        qr = q[:, 512 + lo:512 + lo + LANES] * c1 + q[:, 768 + lo:768 + lo + LANES] * s1
        for hh in range(2):
            h = 2 * c + hh
            slot = jnp.where((lane // 64) == hh, qr, 0.0)
            put("qm", h, (jnp.concatenate([q[:, LANES * h:LANES * (h + 1)], slot], axis=1) * scale_a).astype(BF16))
    for h in range(MLA_HEADS):
        put("km", h, jnp.concatenate([kv[:, LANES * h:LANES * (h + 1)], krot], axis=1).T.astype(BF16))
        put("vm", h, jnp.concatenate([kv[:, 512 + LANES * h:512 + LANES * (h + 1)], ones], axis=1).astype(BF16))

    scale_b = GQA_DIM ** -0.5 * LOG2E
    cq, sq = ca * gqg, sa * gqgs
    ck, sk = ca * gkg, sa * gkgs

    def norm_rot(xh, xs, cg, sg, scale):
        inv = lax.rsqrt(jnp.mean(xh * xh, axis=-1, keepdims=True) + RMS_EPS)
        return (xh * cg + xs * sg) * (inv * scale)

    for h in range(GQA_HEADS):
        sl = slice(LANES * h, LANES * (h + 1))
        put("qg", h, norm_rot(q_g[:, sl], q_gs[:, sl], cq, sq, scale_b).astype(BF16))
    for j in range(GQA_KV_HEADS):
        sl = slice(LANES * j, LANES * (j + 1))
        put("kg", j, norm_rot(k_g[:, sl], k_gs[:, sl], ck, sk, 1.0).T.astype(BF16))
        put("vg", j, jnp.concatenate([v_g[:, sl], ones], axis=1).astype(BF16))


def _proj_body(x_ref, lng_ref, lnb_ref, win_ref, gqa_ref, wqb_ref, gkva_ref, wkvb_ref, gqg_ref, gqgs_ref, gkg_ref,
               gkgs_ref, c1_ref, s1_ref, ca_ref, sa_ref,
               qm_ref, km_ref, vm_ref, qg_ref, kg_ref, vg_ref):
    outs = dict(qm=qm_ref, km=km_ref, vm=vm_ref, qg=qg_ref, kg=kg_ref, vg=vg_ref)
    tile = x_ref.shape[1]
    sub = min(PROJ_SUB, tile)
    for i in range(tile // sub):
        rows = pl.ds(i * sub, sub)

        def put(name, h, val):
            if name in ("km", "kg"):
                outs[name][0, h, :, rows] = val
            else:
                outs[name][0, h, rows, :] = val

        _proj_chain(x_ref[0, rows, :], lng_ref[...], lnb_ref[...], win_ref, gqa_ref[...], wqb_ref, gkva_ref[...],
                    wkvb_ref, gqg_ref[...], gqgs_ref[...], gkg_ref[...], gkgs_ref[...],
                    c1_ref[rows, :], s1_ref[rows, :], ca_ref[rows, :], sa_ref[rows, :], put)


def _project(x3, tabs, w, tile):
    b, s, d = x3.shape
    nst = s // tile
    full = lambda shape: pl.BlockSpec(shape, lambda bi, si: (0,) * len(shape))
    tab = pl.BlockSpec((tile, LANES), lambda bi, si: (si, 0))
    hm = lambda nh, dd: pl.BlockSpec((1, nh, tile, dd), lambda bi, si: (bi, 0, si, 0))
    hmt = lambda nh, dd: pl.BlockSpec((1, nh, dd, tile), lambda bi, si: (bi, 0, 0, si))
    out_shape = [
        jax.ShapeDtypeStruct((b, MLA_HEADS, s, MLA_K), BF16),
        jax.ShapeDtypeStruct((b, MLA_HEADS, MLA_K, s), BF16),
        jax.ShapeDtypeStruct((b, MLA_HEADS, s, V_EXT), BF16),
        jax.ShapeDtypeStruct((b, GQA_HEADS, s, GQA_DIM), BF16),
        jax.ShapeDtypeStruct((b, GQA_KV_HEADS, GQA_DIM, s), BF16),
        jax.ShapeDtypeStruct((b, GQA_KV_HEADS, s, V_EXT), BF16),
    ]
    return pl.pallas_call(
        _proj_body,
        grid=(b, nst),
        in_specs=[
            pl.BlockSpec((1, tile, d), lambda bi, si: (bi, si, 0)),
            full((1, d)), full((1, d)),
            full(w["w_in"].shape), full((1, Q_LORA)), full(w["w_qb"].shape),
            full((1, KV_LORA)), full(w["w_kvb"].shape),
            full((1, GQA_DIM)), full((1, GQA_DIM)), full((1, GQA_DIM)), full((1, GQA_DIM)),
            tab, tab, tab, tab,
        ],
        out_specs=[hm(MLA_HEADS, MLA_K), hmt(MLA_HEADS, MLA_K), hm(MLA_HEADS, V_EXT),
                   hm(GQA_HEADS, GQA_DIM), hmt(GQA_KV_HEADS, GQA_DIM), hm(GQA_KV_HEADS, V_EXT)],
        out_shape=out_shape,
        compiler_params=pltpu.CompilerParams(dimension_semantics=("parallel", "parallel"),
                                             vmem_limit_bytes=VMEM_LIMIT),
        name="proj",
    )(x3, w["ln_emb_g"], w["ln_emb_b"], w["w_in"], w["g_q_a"], w["w_qb"], w["g_kv_a"], w["w_kvb"],
      w["g_q_gqa"], w["g_q_gqa_sw"], w["g_k_gqa"], w["g_k_gqa_sw"], *tabs)


def _softmax_pv(q, kt, v, kmt, vm):
    s = jnp.dot(q, kt, preferred_element_type=F32)
    sm = jnp.dot(q, kmt, preferred_element_type=F32)
    col = lax.broadcasted_iota(I32, sm.shape, 1)
    sm = jnp.where(col < N_META, sm, NEG_BIG)
    m = jnp.maximum(jnp.max(s, axis=1, keepdims=True), jnp.max(sm, axis=1, keepdims=True))
    p = jnp.exp2(s - m).astype(BF16)
    pm = jnp.exp2(sm - m).astype(BF16)
    acc = jnp.dot(p, v, preferred_element_type=F32) + jnp.dot(pm, vm, preferred_element_type=F32)
    return acc[:, 0:V_DIM] / acc[:, V_DIM:V_EXT]


def _mla_attn_body(q_ref, k_ref, v_ref, km_ref, vm_ref, o_ref):
    for i in range(q_ref.shape[2] // Q_SUB):
        rows = pl.ds(i * Q_SUB, Q_SUB)
        o = _softmax_pv(q_ref[0, 0, rows, :], k_ref[0, 0], v_ref[0, 0], km_ref[0, 0], vm_ref[0, 0])
        o_ref[rows, :] = o.astype(o_ref.dtype)


def _gqa_attn_body(q_ref, k_ref, v_ref, km_ref, vm_ref, o_ref):
    for g in range(2):
        for i in range(q_ref.shape[2] // Q_SUB):
            rows = pl.ds(i * Q_SUB, Q_SUB)
            o = _softmax_pv(q_ref[0, g, rows, :], k_ref[0, 0], v_ref[0, 0], km_ref[0, 0], vm_ref[0, 0])
            o_ref[rows, GQA_DIM * g:GQA_DIM * (g + 1)] = o.astype(o_ref.dtype)


def _mla_attention(qm, kmt, vm, kmeta_t, vmeta):
    b, h, s, dk = qm.shape
    tq = Q_TILE_MLA
    nq = s // tq
    return pl.pallas_call(
        _mla_attn_body,
        grid=(b, h, nq),
        in_specs=[
            pl.BlockSpec((1, 1, tq, dk), lambda bi, hi, qi: (bi, hi, qi, 0)),
            pl.BlockSpec((1, 1, dk, s), lambda bi, hi, qi: (bi, hi, 0, 0)),
            pl.BlockSpec((1, 1, s, V_EXT), lambda bi, hi, qi: (bi, hi, 0, 0)),
            pl.BlockSpec((1, 1, dk, META_PAD), lambda bi, hi, qi: (0, hi, 0, 0)),
            pl.BlockSpec((1, 1, META_PAD, V_EXT), lambda bi, hi, qi: (0, hi, 0, 0)),
        ],
        out_specs=pl.BlockSpec((tq, V_DIM), lambda bi, hi, qi: (bi * nq + qi, hi)),
        out_shape=jax.ShapeDtypeStruct((b * s, h * V_DIM), BF16),
        compiler_params=pltpu.CompilerParams(dimension_semantics=("parallel", "parallel", "parallel"),
                                             vmem_limit_bytes=VMEM_LIMIT),
        name="mla_attn",
    )(qm, kmt, vm, kmeta_t, vmeta)


def _gqa_attention(qg, kgt, vg, kmeta_t, vmeta):
    b, h, s, d = qg.shape
    hk = kgt.shape[1]
    tq = Q_TILE_GQA
    nq = s // tq
    return pl.pallas_call(
        _gqa_attn_body,
        grid=(b, hk, nq),
        in_specs=[
            pl.BlockSpec((1, 2, tq, d), lambda bi, ji, qi: (bi, ji, qi, 0)),
            pl.BlockSpec((1, 1, d, s), lambda bi, ji, qi: (bi, ji, 0, 0)),
            pl.BlockSpec((1, 1, s, V_EXT), lambda bi, ji, qi: (bi, ji, 0, 0)),
            pl.BlockSpec((1, 1, d, META_PAD), lambda bi, ji, qi: (0, ji, 0, 0)),
            pl.BlockSpec((1, 1, META_PAD, V_EXT), lambda bi, ji, qi: (0, ji, 0, 0)),
        ],
        out_specs=pl.BlockSpec((tq, 2 * d), lambda bi, ji, qi: (bi * nq + qi, ji)),
        out_shape=jax.ShapeDtypeStruct((b * s, h * d), BF16),
        compiler_params=pltpu.CompilerParams(dimension_semantics=("parallel", "parallel", "parallel"),
                                             vmem_limit_bytes=VMEM_LIMIT),
        name="gqa_attn",
    )(qg, kgt, vg, kmeta_t, vmeta)


def _to_tiles(ref, x):
    r = x.shape[0]
    for s in range(x.shape[1] // LANES):
        ref[:, s] = x[:, LANES * s:LANES * (s + 1)].reshape(r // SUBLANES, SUBLANES, LANES)


def _from_tiles(ref):
    r = ref.shape[0] * SUBLANES
    return jnp.concatenate([ref[:, s].reshape(r, LANES) for s in range(ref.shape[1])], axis=1)


MERGE_SUB = 256


def _merge_body(om_ref, og_ref, x_ref, lng_ref, lnb_ref, gom_ref, gog_ref, wo_ref, l1g_ref, l1b_ref,
                wr_ref, br_ref, h1_ref, topi_ref, gate_ref):
    for i in range(x_ref.shape[0] // MERGE_SUB):
        rows = pl.ds(i * MERGE_SUB, MERGE_SUB)
        h1, ti, gt = _merge_chain(om_ref[rows, :], og_ref[rows, :], x_ref[rows, :], lng_ref[...], lnb_ref[...],
                                  gom_ref[...], gog_ref[...], wo_ref, l1g_ref[...], l1b_ref[...], wr_ref, br_ref[...])
        _to_tiles(h1_ref.at[pl.ds(i * (MERGE_SUB // SUBLANES), MERGE_SUB // SUBLANES)], h1)
        topi_ref[:, rows] = ti
        gate_ref[:, rows] = gt


def _merge_chain(om, og, x, lng, lnb, gom, gog, wo_ref, l1g, l1b, wr_ref, br):
    t = x.shape[0]
    h0 = _layernorm(x, lng, lnb)
    nm = _rmsnorm(om.astype(F32), gom).astype(BF16)
    ng = _rmsnorm(og.astype(F32), gog).astype(BF16)
    half = nm.shape[1]
    mix = jnp.dot(nm, wo_ref[0:half, :], preferred_element_type=F32)
    mix = mix + jnp.dot(ng, wo_ref[half:2 * half, :], preferred_element_type=F32)
    h1 = _layernorm(DEEPNORM_ALPHA * h0 + mix, l1g, l1b)

    hi = h1.astype(BF16)
    lo = (h1 - hi.astype(F32)).astype(BF16)
    acc = jnp.dot(hi, wr_ref[...], preferred_element_type=F32) + jnp.dot(lo, wr_ref[...], preferred_element_type=F32)
    logits = acc[:, 0:LANES] + acc[:, LANES:2 * LANES] + br
    cur = logits.T[0:N_EXPERTS, :]
    eidx = lax.broadcasted_iota(I32, cur.shape, 0)
    vals, idxs = [], []
    for _ in range(TOP_K):
        m = jnp.max(cur, axis=0, keepdims=True)
        i = jnp.min(jnp.where(cur == m, eidx, N_EXPERTS), axis=0, keepdims=True)
        vals.append(m)
        idxs.append(i)
        cur = jnp.where(eidx == i, -jnp.inf, cur)
    ex = [jnp.exp(v - vals[0]) for v in vals]
    den = ex[0] + ex[1] + ex[2] + ex[3]
    sub = lax.broadcasted_iota(I32, (8, t), 0)
    ti = jnp.zeros((8, t), I32)
    gt = jnp.zeros((8, t), F32)
    for k in range(TOP_K):
        ti = jnp.where(sub == k, idxs[k], ti)
        gt = jnp.where(sub == k, ex[k] / den, gt)
    return h1, ti[0:TOP_K, :], gt[0:TOP_K, :]


def _merge(o_mla, o_gqa, x2, w):
    n, d = x2.shape
    tile = ROW_TILE
    half = o_mla.shape[1]
    full = lambda shape: pl.BlockSpec(shape, lambda i: (0,) * len(shape))
    row = lambda width: pl.BlockSpec((tile, width), lambda i: (i, 0))
    lane_major = pl.BlockSpec((TOP_K, tile), lambda i: (0, i))
    return pl.pallas_call(
        _merge_body,
        grid=(n // tile,),
        in_specs=[row(half), row(half), row(d), full((1, d)), full((1, d)), full((1, half)), full((1, half)),
                  full((d, d)), full((1, d)), full((1, d)), full((d, 2 * LANES)), full((1, LANES))],
        out_specs=[pl.BlockSpec((tile // SUBLANES, d // LANES, SUBLANES, LANES), lambda i: (i, 0, 0, 0)),
                   lane_major, lane_major],
        out_shape=[jax.ShapeDtypeStruct((n // SUBLANES, d // LANES, SUBLANES, LANES), F32),
                   jax.ShapeDtypeStruct((TOP_K, n), I32), jax.ShapeDtypeStruct((TOP_K, n), F32)],
        compiler_params=pltpu.CompilerParams(dimension_semantics=("parallel",), vmem_limit_bytes=VMEM_LIMIT),
        name="merge",
    )(o_mla, o_gqa, x2, w["ln_emb_g"], w["ln_emb_b"], w["g_o_mla"], w["g_o_gqa"], w["w_o"],
      w["ln1_g"], w["ln1_b"], w["w_router"], w["b_router"])


def _lanes_from_sublanes(col):
    diag = lax.broadcasted_iota(I32, col.shape, 0) == lax.broadcasted_iota(I32, col.shape, 1)
    return jnp.sum(jnp.where(diag, col, 0.0), axis=0, keepdims=True)


def _positions_body(topi_ref, lpb_ref, gstart_ref, nslab_ref, lstart_ref, texp_ref, nvalid_ref, ecnt_ref, eoff_ref,
                    cnt_sc, carry_sc, off_sc, *, ntp):
    p = pl.program_id(0)
    j = pl.program_id(1)
    tl = topi_ref.shape[1]
    topi = topi_ref[...]
    eidx = lax.broadcasted_iota(I32, (N_EXPERTS, tl), 0)
    ohs = [eidx == topi[k:k + 1, :] for k in range(TOP_K)]
    onehot = ohs[0].astype(F32) + ohs[1].astype(F32) + ohs[2].astype(F32) + ohs[3].astype(F32)
    cnt = jnp.sum(onehot, axis=1, keepdims=True)
    run = jnp.floor((cnt + (SUBLANES - 1)) * (1.0 / SUBLANES)) * SUBLANES
    tile_run = jnp.broadcast_to(run, (N_EXPERTS, LANES))
    r = lax.broadcasted_iota(I32, (N_EXPERTS, N_EXPERTS), 0)
    c = lax.broadcasted_iota(I32, (N_EXPERTS, N_EXPERTS), 1)
    lower = (c < r).astype(F32)

    @pl.when(jnp.logical_and(p == 0, j == 0))
    def _():
        cnt_sc[...] = jnp.zeros_like(cnt_sc)

    @pl.when(p == 0)
    def _():
        cnt_sc[...] += tile_run

    @pl.when(jnp.logical_and(p == 1, j == 0))
    def _():
        tot = cnt_sc[...]
        pc = jnp.floor((tot + (MOE_TILE - 1)) * (1.0 / MOE_TILE)) * MOE_TILE
        off = jnp.dot(lower, pc, precision=lax.Precision.HIGHEST, preferred_element_type=F32)
        off_sc[...] = off
        carry_sc[...] = jnp.zeros_like(carry_sc)
        cumend = off + pc
        tstart = lax.broadcasted_iota(I32, (N_EXPERTS, ntp), 1).astype(F32) * MOE_TILE
        te = jnp.sum((jnp.broadcast_to(cumend[:, 0:1], (N_EXPERTS, ntp)) <= tstart).astype(I32), axis=0, keepdims=True)
        texp_ref[...] = jnp.minimum(te, N_EXPERTS - 1)
        nvalid_ref[...] = (cumend[N_EXPERTS - 1:N_EXPERTS, :] * (1.0 / MOE_TILE)).astype(I32)
        ecnt_ref[...] = _lanes_from_sublanes(tot).astype(I32)
        eoff_ref[...] = _lanes_from_sublanes(off).astype(I32)

    @pl.when(p == 1)
    def _():
        rr = lax.broadcasted_iota(I32, (tl, tl), 0)
        cc = lax.broadcasted_iota(I32, (tl, tl), 1)
        upper = (rr < cc).astype(BF16)
        before = jnp.dot(onehot.astype(BF16), upper, preferred_element_type=F32)
        loff = jnp.dot(lower, tile_run, precision=lax.Precision.HIGHEST, preferred_element_type=F32)
        base = before + loff[:, 0:1]
        sub = lax.broadcasted_iota(I32, (SUBLANES, tl), 0)
        out = jnp.zeros((SUBLANES, tl), F32)
        for k in range(TOP_K):
            pk = jnp.sum(jnp.where(ohs[k], base, 0.0), axis=0, keepdims=True)
            out = jnp.where(sub == k, pk, out)
        lp = out[0:TOP_K, :].astype(I32)
        lpb_ref[...] = (lp >> 3) * SLAB + (lp & (SUBLANES - 1))
        inv = 1.0 / SUBLANES
        gstart_ref[0] = (_lanes_from_sublanes(off_sc[...] + carry_sc[...]) * inv).astype(I32)
        nslab_ref[0] = (_lanes_from_sublanes(tile_run) * inv).astype(I32)
        lstart_ref[0] = (_lanes_from_sublanes(loff) * inv).astype(I32)
        carry_sc[...] += tile_run


def _positions(topi, ntp):
    n = topi.shape[1]
    tl = TOKEN_TILE
    const = lambda shape: pl.BlockSpec(shape, lambda p, j: (0, 0))
    per_tile = pl.BlockSpec((1, 1, LANES), lambda p, j: (j * p, 0, 0))
    tab = jax.ShapeDtypeStruct((n // tl, 1, LANES), I32)
    return pl.pallas_call(
        functools.partial(_positions_body, ntp=ntp),
        grid=(2, n // tl),
        in_specs=[pl.BlockSpec((TOP_K, tl), lambda p, j: (0, j))],
        out_specs=[pl.BlockSpec((TOP_K, tl), lambda p, j: (0, j * p)), per_tile, per_tile, per_tile,
                   const((1, ntp)), const((1, LANES)), const((1, LANES)), const((1, LANES))],
        out_shape=[jax.ShapeDtypeStruct((TOP_K, n), I32), tab, tab, tab, jax.ShapeDtypeStruct((1, ntp), I32),
                   jax.ShapeDtypeStruct((1, LANES), I32), jax.ShapeDtypeStruct((1, LANES), I32),
                   jax.ShapeDtypeStruct((1, LANES), I32)],
        scratch_shapes=[pltpu.VMEM((N_EXPERTS, LANES), F32)] * 3,
        compiler_params=pltpu.CompilerParams(dimension_semantics=("arbitrary", "arbitrary")),
        name="positions",
    )(topi)


STAGE_ROWS = TOKEN_TILE * TOP_K + N_EXPERTS * SUBLANES
STAGE_FLAT = STAGE_ROWS * SUBLANES


def _load_row(ref, flat_start):
    return ref[pl.ds(flat_start, SUBLANES, stride=SUBLANES), :]


def _store_row(ref, flat_start, v):
    ref[pl.ds(flat_start, SUBLANES, stride=SUBLANES), :] = v


def _dispatch_body(ecnt_ref, eoff_ref, lpb_ref, gstart_ref, nslab_ref, lstart_ref, h_ref, xs_ref,
                   stg_ref, zero_ref, issued_ref, sem, zsem):
    g = pl.program_id(0)
    ng = pl.num_programs(0)
    slot = lax.rem(g, 2)
    sbase = slot * STAGE_FLAT

    def slab_wait(sl):
        pltpu.make_async_copy(stg_ref.at[pl.ds(0, SLAB)], xs_ref.at[pl.ds(0, SLAB)], sem.at[sl]).wait()

    def drain(sl):
        lax.fori_loop(0, issued_ref[sl], lambda i, c: (slab_wait(sl), c)[1], 0)

    @pl.when(g >= 2)
    def _():
        drain(slot)

    def zero_last(e, c):
        last = jnp.maximum(lstart_ref[0, 0, e] + nslab_ref[0, 0, e] - 1, 0)
        stg_ref[pl.ds(sbase + last * SLAB, SLAB), :] = jnp.zeros((SLAB, LANES), F32)
        return c

    lax.fori_loop(0, N_EXPERTS, zero_last, 0)

    def move(i, c):
        for u in range(SUBLANES):
            tok = i * SUBLANES + u
            v = _load_row(h_ref, i * SLAB + u)
            for k in range(TOP_K):
                _store_row(stg_ref, sbase + lpb_ref[k, tok], v)
        return c

    lax.fori_loop(0, TOKEN_TILE // SUBLANES, move, 0)

    def send_runs(e, total):
        n = nslab_ref[0, 0, e]
        src = sbase + lstart_ref[0, 0, e] * SLAB
        dst = gstart_ref[0, 0, e] * SLAB

        def send(s, c):
            pltpu.make_async_copy(stg_ref.at[pl.ds(src + s * SLAB, SLAB)], xs_ref.at[pl.ds(dst + s * SLAB, SLAB)],
                                  sem.at[slot]).start()
            return c

        lax.fori_loop(0, n, send, 0)
        return total + n

    issued_ref[slot] = lax.fori_loop(0, N_EXPERTS, send_runs, 0)

    @pl.when(g == ng - 1)
    def _():
        zero_ref[...] = jnp.zeros_like(zero_ref)

        def zero_copy(dst_slab):
            return pltpu.make_async_copy(zero_ref, xs_ref.at[pl.ds(dst_slab * SLAB, SLAB)], zsem)

        def pad_segment(e, c):
            rows = ecnt_ref[e]
            first = (eoff_ref[e] + rows) >> 3
            npad = lax.rem(MOE_TILE - lax.rem(rows, MOE_TILE), MOE_TILE) >> 3
            lax.fori_loop(0, npad, lambda s, cc: (zero_copy(first + s).start(), cc)[1], 0)
            lax.fori_loop(0, npad, lambda s, cc: (zero_copy(first + s).wait(), cc)[1], 0)
            return c

        lax.fori_loop(0, N_EXPERTS, pad_segment, 0)

        @pl.when(ng > 1)
        def _():
            drain(1 - slot)

        drain(slot)


def _dispatch(h1_flat, lpb, gstart, nslab, lstart, ecnt, eoff, rows_pad):
    n8 = h1_flat.shape[0]
    t = TOKEN_TILE
    smem = lambda shape, imap: pl.BlockSpec(shape, imap, memory_space=pltpu.SMEM)
    per_tile = smem((1, 1, LANES), lambda i, c, o: (i, 0, 0))
    grid_spec = pltpu.PrefetchScalarGridSpec(
        num_scalar_prefetch=2,
        grid=(n8 // (t * SUBLANES),),
        in_specs=[smem((TOP_K, t), lambda i, c, o: (0, i)), per_tile, per_tile, per_tile,
                  pl.BlockSpec((t * SUBLANES, LANES), lambda i, c, o: (i, 0))],
        out_specs=pl.BlockSpec(memory_space=pl.ANY),
        scratch_shapes=[pltpu.VMEM((2 * STAGE_FLAT, LANES), F32), pltpu.VMEM((SLAB, LANES), F32),
                        pltpu.SMEM((2,), I32), pltpu.SemaphoreType.DMA((2,)), pltpu.SemaphoreType.DMA],
    )
    return pl.pallas_call(
        _dispatch_body,
        grid_spec=grid_spec,
        out_shape=jax.ShapeDtypeStruct((rows_pad * SUBLANES, LANES), F32),
        compiler_params=pltpu.CompilerParams(dimension_semantics=("arbitrary",), vmem_limit_bytes=VMEM_LIMIT),
        name="dispatch",
    )(ecnt, eoff, lpb, gstart, nslab, lstart, h1_flat)


def _wprep_body(w_ref, wg_ref, wl_ref):
    blk = 2 * LANES
    r = lax.broadcasted_iota(I32, (blk, blk), 0)
    c = lax.broadcasted_iota(I32, (blk, blk), 1)
    sel = (r == jnp.where(c < LANES, 2 * c, 2 * (c - LANES) + 1)).astype(BF16)
    nblk = w_ref.shape[2] // blk
    for b in range(nblk):
        wb = w_ref[0, :, blk * b:blk * (b + 1)].astype(BF16)
        y = jnp.dot(wb, sel, preferred_element_type=F32).astype(BF16)
        wg_ref[0, :, LANES * b:LANES * (b + 1)] = y[:, 0:LANES]
        wl_ref[0, :, LANES * b:LANES * (b + 1)] = y[:, LANES:blk]


def _prep_gate_up(w_gate_up):
    e, d, f2 = w_gate_up.shape
    cb = 1024
    out = jax.ShapeDtypeStruct((e, d, f2 // 2), BF16)
    return pl.pallas_call(
        _wprep_body,
        grid=(e, f2 // cb),
        in_specs=[pl.BlockSpec((1, d, cb), lambda ei, ci: (ei, 0, ci))],
        out_specs=[pl.BlockSpec((1, d, cb // 2), lambda ei, ci: (ei, 0, ci))] * 2,
        out_shape=[out, out],
        compiler_params=pltpu.CompilerParams(dimension_semantics=("parallel", "parallel"),
                                             vmem_limit_bytes=VMEM_LIMIT),
        name="wprep",
    )(w_gate_up)


def _ffn_body(texp_ref, nvalid_ref, x_ref, wg_ref, wl_ref, wd_ref, bg_ref, bl_ref, bd_ref, y_ref):
    @pl.when(pl.program_id(0) < nvalid_ref[0])
    def _():
        x = _from_tiles(x_ref).astype(BF16)
        hg = jnp.dot(x, wg_ref[0], preferred_element_type=F32) + bg_ref[0]
        hl = jnp.dot(x, wl_ref[0], preferred_element_type=F32) + bl_ref[0]
        g = jnp.minimum(hg, SWIGLU_LIMIT)
        lin = jnp.clip(hl, -SWIGLU_LIMIT, SWIGLU_LIMIT)
        act = g * (1.0 / (1.0 + jnp.exp(-SWIGLU_ALPHA * g))) * (lin + 1.0)
        _to_tiles(y_ref, jnp.dot(act.astype(BF16), wd_ref[0], preferred_element_type=F32) + bd_ref[0])


def _grouped_ffn(xs_tiles, texp, nvalid, wg, wl, wd, bg, bl, bd, ntiles):
    d = wg.shape[1]
    tm = MOE_TILE
    f = wg.shape[2]
    xmap = lambda i, te, nv: (jnp.minimum(i, nv[0] - 1), 0, 0, 0)
    wmap = lambda i, te, nv: (te[i], 0, 0)
    rows_blk = pl.BlockSpec((tm // SUBLANES, d // LANES, SUBLANES, LANES), xmap)
    grid_spec = pltpu.PrefetchScalarGridSpec(
        num_scalar_prefetch=2,
        grid=(ntiles,),
        in_specs=[rows_blk,
                  pl.BlockSpec((1, d, f), wmap), pl.BlockSpec((1, d, f), wmap), pl.BlockSpec((1, f, d), wmap),
                  pl.BlockSpec((1, 1, f), wmap), pl.BlockSpec((1, 1, f), wmap), pl.BlockSpec((1, 1, d), wmap)],
        out_specs=rows_blk,
    )
    return pl.pallas_call(
        _ffn_body,
        grid_spec=grid_spec,
        out_shape=jax.ShapeDtypeStruct(xs_tiles.shape, F32),
        compiler_params=pltpu.CompilerParams(dimension_semantics=("arbitrary",), vmem_limit_bytes=VMEM_LIMIT),
        name="ffn",
    )(texp, nvalid, xs_tiles, wg, wl, wd, bg, bl, bd)


def _combine_body(lpb_ref, gate_ref, gstart_ref, nslab_ref, lstart_ref, gstart2_ref, nslab2_ref, lstart2_ref,
                  h_ref, l2g_ref, l2b_ref, ys_ref, o_ref, stg_ref, moe_ref, sem):
    g = pl.program_id(0)
    ng = pl.num_programs(0)
    slot = lax.rem(g, 2)
    sbase = slot * STAGE_FLAT

    def fetch_runs(gs_ref, ns_ref, ls_ref, sl):
        def per_expert(e, c):
            src = gs_ref[0, 0, e] * SLAB
            dst = sl * STAGE_FLAT + ls_ref[0, 0, e] * SLAB

            def fetch(s, cc):
                pltpu.make_async_copy(ys_ref.at[pl.ds(src + s * SLAB, SLAB)], stg_ref.at[pl.ds(dst + s * SLAB, SLAB)],
                                      sem.at[sl]).start()
                return cc

            lax.fori_loop(0, ns_ref[0, 0, e], fetch, 0)
            return c

        lax.fori_loop(0, N_EXPERTS, per_expert, 0)

    @pl.when(g == 0)
    def _():
        fetch_runs(gstart_ref, nslab_ref, lstart_ref, 0)

    @pl.when(g + 1 < ng)
    def _():
        fetch_runs(gstart2_ref, nslab2_ref, lstart2_ref, 1 - slot)

    total = lax.fori_loop(0, N_EXPERTS, lambda e, c: c + nslab_ref[0, 0, e], 0)

    def slab_wait(s, c):
        pltpu.make_async_copy(ys_ref.at[pl.ds(0, SLAB)], stg_ref.at[pl.ds(0, SLAB)], sem.at[slot]).wait()
        return c

    lax.fori_loop(0, total, slab_wait, 0)

    def gather(i, c):
        for u in range(SUBLANES):
            tok = i * SUBLANES + u
            acc = gate_ref[0, tok] * _load_row(stg_ref, sbase + lpb_ref[0, tok])
            for k in range(1, TOP_K):
                acc = acc + gate_ref[k, tok] * _load_row(stg_ref, sbase + lpb_ref[k, tok])
            _store_row(moe_ref, i * SLAB + u, acc)
        return c

    lax.fori_loop(0, TOKEN_TILE // SUBLANES, gather, 0)

    moe = jnp.concatenate(
        [jnp.concatenate([moe_ref[pl.ds(i * SLAB + s * SUBLANES, SUBLANES), :] for i in range(TOKEN_TILE // SUBLANES)],
                         axis=0) for s in range(SUBLANES)], axis=1)
    o_ref[...] = _layernorm(DEEPNORM_ALPHA * _from_tiles(h_ref) + moe, l2g_ref[...], l2b_ref[...])


def _combine(lpb, gates, gstart, nslab, lstart, h1_tiles, ys_flat, w):
    d = h1_tiles.shape[1] * LANES
    n = h1_tiles.shape[0] * SUBLANES
    t = TOKEN_TILE
    ng = n // t
    full = lambda shape: pl.BlockSpec(shape, lambda i: (0,) * len(shape))
    smem = lambda shape, imap: pl.BlockSpec(shape, imap, memory_space=pltpu.SMEM)
    this_tile = smem((1, 1, LANES), lambda i: (i, 0, 0))
    next_tile = smem((1, 1, LANES), lambda i: (jnp.minimum(i + 1, ng - 1), 0, 0))
    return pl.pallas_call(
        _combine_body,
        grid=(ng,),
        in_specs=[smem((TOP_K, t), lambda i: (0, i)), smem((TOP_K, t), lambda i: (0, i)),
                  this_tile, this_tile, this_tile, next_tile, next_tile, next_tile,
                  pl.BlockSpec((t // SUBLANES, d // LANES, SUBLANES, LANES), lambda i: (i, 0, 0, 0)),
                  full((1, d)), full((1, d)),
                  pl.BlockSpec(memory_space=pl.ANY)],
        out_specs=pl.BlockSpec((t, d), lambda i: (i, 0)),
        out_shape=jax.ShapeDtypeStruct((n, d), F32),
        scratch_shapes=[pltpu.VMEM((2 * STAGE_FLAT, LANES), F32), pltpu.VMEM((t * SUBLANES, LANES), F32),
                        pltpu.SemaphoreType.DMA((2,))],
        compiler_params=pltpu.CompilerParams(dimension_semantics=("arbitrary",), vmem_limit_bytes=VMEM_LIMIT),
        name="combine",
    )(lpb, gates, gstart, nslab, lstart, gstart, nslab, lstart, h1_tiles, w["ln2_g"], w["ln2_b"], ys_flat)


def _rope_tables(pos_1d, row, col):
    inv = ROPE_THETA ** (-jnp.arange(0, ROPE_DIM, 2, dtype=F32) / ROPE_DIM)

    def cs(p):
        ang = p.astype(F32)[:, None] * inv[None, :]
        return jnp.cos(ang), jnp.sin(ang)

    c1, s1 = cs(pos_1d)
    cr, sr = cs(row)
    cc, sc = cs(col)
    return (jnp.concatenate([c1, c1, c1, c1], axis=1), jnp.concatenate([-s1, s1, -s1, s1], axis=1),
            jnp.concatenate([cr, cr, cc, cc], axis=1), jnp.concatenate([-sr, sr, -sc, sc], axis=1))


def _prep_weights(ln_emb_g, ln_emb_b, w_in, g_q_a, w_q_b, g_kv_a, w_kv_b, g_q_gqa, g_k_gqa, g_o_mla, g_o_gqa, w_o,
                  ln1_g, ln1_b, w_router, b_router, ln2_g, ln2_b):
    r2 = lambda v: v.reshape(1, -1).astype(F32)
    o = np.cumsum([0, Q_LORA, KV_LORA, ROPE_DIM, GQA_HEADS * GQA_DIM, GQA_KV_HEADS * GQA_DIM, GQA_KV_HEADS * GQA_DIM])
    wi = w_in[0]
    seg = [wi[:, o[i]:o[i + 1]] for i in range(6)]
    kpe2 = jnp.concatenate([seg[2], seg[2]], axis=1)
    w_in_p = jnp.concatenate([seg[0], seg[1], kpe2, _swap_halves64(kpe2), seg[3], _swap_halves64(seg[3]),
                              seg[4], _swap_halves64(seg[4]), seg[5]], axis=1).astype(BF16)
    wq = w_q_b[0].reshape(Q_LORA, MLA_HEADS, QK_DIM)
    wq_rope = wq[:, :, NOPE_DIM:].reshape(Q_LORA, -1)
    w_qb = jnp.concatenate([wq[:, :, :NOPE_DIM].reshape(Q_LORA, -1), wq_rope, _swap_halves64(wq_rope)],
                           axis=1).astype(BF16)
    wk = w_kv_b[0].reshape(KV_LORA, MLA_HEADS, NOPE_DIM + V_DIM)
    w_kvb = jnp.concatenate([wk[:, :, :NOPE_DIM].reshape(KV_LORA, -1), wk[:, :, NOPE_DIM:].reshape(KV_LORA, -1)],
                            axis=1).astype(BF16)
    wr = jnp.pad(w_router[0].astype(F32), ((0, 0), (0, LANES - N_EXPERTS)))
    wr_hi = wr.astype(BF16)
    w_router_p = jnp.concatenate([wr_hi, (wr - wr_hi.astype(F32)).astype(BF16)], axis=1)
    b_router_p = jnp.pad(b_router[0].astype(F32), (0, LANES - N_EXPERTS), constant_values=NEG_BIG).reshape(1, LANES)
    return dict(
        ln_emb_g=r2(ln_emb_g), ln_emb_b=r2(ln_emb_b), w_in=w_in_p, g_q_a=r2(g_q_a[0]), w_qb=w_qb,
        g_kv_a=r2(g_kv_a[0]), w_kvb=w_kvb, g_q_gqa=r2(g_q_gqa[0]), g_k_gqa=r2(g_k_gqa[0]),
        g_q_gqa_sw=_swap_halves64(r2(g_q_gqa[0])), g_k_gqa_sw=_swap_halves64(r2(g_k_gqa[0])),
        g_o_mla=r2(g_o_mla[0]), g_o_gqa=r2(g_o_gqa[0]), w_o=w_o[0].astype(BF16), ln1_g=r2(ln1_g[0]),
        ln1_b=r2(ln1_b[0]), w_router=w_router_p, b_router=b_router_p, ln2_g=r2(ln2_g[0]), ln2_b=r2(ln2_b[0]))


def kernel(x, meta_tokens, ln_emb_g, ln_emb_b, w_in, g_q_a, w_q_b, g_kv_a, w_kv_b, g_q_gqa, g_k_gqa, g_o_mla, g_o_gqa,
           w_o, ln1_g, ln1_b, w_router, b_router, w_gate_up, b_gate_up, w_down, b_down, ln2_g, ln2_b):
    b, s, d = x.shape
    n = b * s
    w = _prep_weights(ln_emb_g, ln_emb_b, w_in, g_q_a, w_q_b, g_kv_a, w_kv_b, g_q_gqa, g_k_gqa, g_o_mla, g_o_gqa,
                      w_o, ln1_g, ln1_b, w_router, b_router, ln2_g, ln2_b)

    tok = jnp.arange(s, dtype=I32)
    tabs_real = _rope_tables(tok + N_META, tok // GRID_W, tok % GRID_W)
    mt = jnp.arange(META_PAD, dtype=I32)
    tabs_meta = _rope_tables(mt, jnp.full((META_PAD,), -1, I32), mt)

    qm, km, vm, qg, kg, vg = _project(x, tabs_real, w, ROW_TILE)
    meta = jnp.pad(meta_tokens.astype(x.dtype), ((0, META_PAD - N_META), (0, 0))).reshape(1, META_PAD, d)
    _, km_m, vm_m, _, kg_m, vg_m = _project(meta, tabs_meta, w, META_PAD)

    o_mla = _mla_attention(qm, km, vm, km_m, vm_m)
    o_gqa = _gqa_attention(qg, kg, vg, kg_m, vg_m)

    h1_tiles, topi, gates = _merge(o_mla, o_gqa, x.reshape(n, d), w)

    run_rows = n * TOP_K + (n // TOKEN_TILE) * N_EXPERTS * (SUBLANES - 1)
    ntiles = -(-run_rows // MOE_TILE) + N_EXPERTS
    ntp = -(-ntiles // LANES) * LANES
    rows_pad = ntiles * MOE_TILE
    lpb, gstart, nslab, lstart, texp, nvalid, ecnt, eoff = _positions(topi, ntp)
    flat = lambda a: a.reshape(-1, LANES)
    xs_flat = _dispatch(flat(h1_tiles), lpb, gstart, nslab, lstart, ecnt.reshape(-1), eoff.reshape(-1), rows_pad)

    wg, wl = _prep_gate_up(w_gate_up[0])
    bgu = b_gate_up[0].astype(F32)
    tiles = lambda a: a.reshape(-1, d // LANES, SUBLANES, LANES)
    ys_tiles = _grouped_ffn(tiles(xs_flat), texp.reshape(-1), nvalid.reshape(-1), wg, wl, w_down[0].astype(BF16),
                            bgu[:, 0::2].reshape(N_EXPERTS, 1, D_FF), bgu[:, 1::2].reshape(N_EXPERTS, 1, D_FF),
                            b_down[0].astype(F32).reshape(N_EXPERTS, 1, d), ntiles)

    out = _combine(lpb, gates, gstart, nslab, lstart, h1_tiles, flat(ys_tiles), w)
    return out.reshape(b, s, d)
```

```python
import functools

import jax
import jax.numpy as jnp
import numpy as np
from jax import lax
from jax.experimental import pallas as pl
from jax.experimental.pallas import tpu as pltpu

D_MODEL = 1024
N_META = 16
GRID_W = 64
ROPE_THETA = 10000.0
MLA_HEADS = 4
Q_LORA = 256
KV_LORA = 128
NOPE_DIM = 128
ROPE_DIM = 64
V_DIM = 128
QK_DIM = NOPE_DIM + ROPE_DIM
GQA_HEADS = 4
GQA_KV_HEADS = 2
GQA_DIM = 128
N_EXPERTS = 32
TOP_K = 4
D_FF = D_MODEL
SWIGLU_LIMIT = 7.0
SWIGLU_ALPHA = 1.702
RMS_EPS = 1e-6
LN_EPS = 1e-5
DEPTH = 1
DEEPNORM_ALPHA = (2.0 * DEPTH) ** 0.25

LANES = 128
SUBLANES = 8
SLAB = SUBLANES * SUBLANES
META_PAD = 128
MLA_K = 2 * LANES
NEG_BIG = -1e30
LOG2E = 1.4426950408889634
V_EXT = 2 * LANES

ROW_TILE = 512
Q_TILE_MLA = 1024
Q_TILE_GQA = 512
Q_SUB = 256
TOKEN_TILE = 512
MOE_TILE = 512
VMEM_LIMIT = 56 * 1024 * 1024

F32 = jnp.float32
BF16 = jnp.bfloat16
I32 = jnp.int32


def _layernorm(x, g, b):
    mu = jnp.mean(x, axis=-1, keepdims=True)
    xc = x - mu
    var = jnp.mean(xc * xc, axis=-1, keepdims=True)
    return xc * lax.rsqrt(var + LN_EPS) * g + b


def _rmsnorm(x, g):
    return x * lax.rsqrt(jnp.mean(x * x, axis=-1, keepdims=True) + RMS_EPS) * g


def _swap_halves64(v):
    shp = v.shape
    return v.reshape(shp[:-1] + (shp[-1] // 64, 2, 32))[..., ::-1, :].reshape(shp)


PROJ_SUB = 256


def _proj_chain(x, lng, lnb, win_ref, gqa, wqb_ref, gkva, wkvb_ref, gqg, gqgs, gkg, gkgs, c1, s1, ca, sa, put):
    r = x.shape[0]
    h0 = _layernorm(x, lng, lnb)
    z = jnp.dot(h0.astype(BF16), win_ref[...], preferred_element_type=F32)
    q_a = z[:, 0:256]
    kv_a = z[:, 256:384]
    kpe2 = z[:, 384:512]
    kpe2s = z[:, 512:640]
    q_g = z[:, 640:1152]
    q_gs = z[:, 1152:1664]
    k_g = z[:, 1664:1920]
    k_gs = z[:, 1920:2176]
    v_g = z[:, 2176:2432]

    q = jnp.dot(_rmsnorm(q_a, gqa).astype(BF16), wqb_ref[...], preferred_element_type=F32)
    kv = jnp.dot(_rmsnorm(kv_a, gkva).astype(BF16), wkvb_ref[...], preferred_element_type=F32)
    krot = kpe2 * c1 + kpe2s * s1
    lane = lax.broadcasted_iota(I32, krot.shape, 1)
    scale_a = QK_DIM ** -0.5 * LOG2E
    ones = jnp.ones((r, LANES), F32)
    for c in range(MLA_HEADS // 2):
        lo = LANES * c
        qr = q[:, 512 + lo:512 + lo + LANES] * c1 + q[:, 768 + lo:768 + lo + LANES] * s1
        for hh in range(2):
            h = 2 * c + hh
            slot = jnp.where((lane // 64) == hh, qr, 0.0)
            put("qm", h, (jnp.concatenate([q[:, LANES * h:LANES * (h + 1)], slot], axis=1) * scale_a).astype(BF16))
    for h in range(MLA_HEADS):
        put("km", h, jnp.concatenate([kv[:, LANES * h:LANES * (h + 1)], krot], axis=1).T.astype(BF16))
        put("vm", h, jnp.concatenate([kv[:, 512 + LANES * h:512 + LANES * (h + 1)], ones], axis=1).astype(BF16))

    scale_b = GQA_DIM ** -0.5 * LOG2E
    cq, sq = ca * gqg, sa * gqgs
    ck, sk = ca * gkg, sa * gkgs

    def norm_rot(xh, xs, cg, sg, scale):
        inv = lax.rsqrt(jnp.mean(xh * xh, axis=-1, keepdims=True) + RMS_EPS)
        return (xh * cg + xs * sg) * (inv * scale)

    for h in range(GQA_HEADS):
        sl = slice(LANES * h, LANES * (h + 1))
        put("qg", h, norm_rot(q_g[:, sl], q_gs[:, sl], cq, sq, scale_b).astype(BF16))
    for j in range(GQA_KV_HEADS):
        sl = slice(LANES * j, LANES * (j + 1))
        put("kg", j, norm_rot(k_g[:, sl], k_gs[:, sl], ck, sk, 1.0).T.astype(BF16))
        put("vg", j, jnp.concatenate([v_g[:, sl], ones], axis=1).astype(BF16))


def _proj_body(x_ref, lng_ref, lnb_ref, win_ref, gqa_ref, wqb_ref, gkva_ref, wkvb_ref, gqg_ref, gqgs_ref, gkg_ref,
               gkgs_ref, c1_ref, s1_ref, ca_ref, sa_ref,
               qm_ref, km_ref, vm_ref, qg_ref, kg_ref, vg_ref):
    outs = dict(qm=qm_ref, km=km_ref, vm=vm_ref, qg=qg_ref, kg=kg_ref, vg=vg_ref)
    tile = x_ref.shape[1]
    sub = min(PROJ_SUB, tile)
    for i in range(tile // sub):
        rows = pl.ds(i * sub, sub)

        def put(name, h, val):
            if name in ("km", "kg"):
                outs[name][0, h, :, rows] = val
            else:
                outs[name][0, h, rows, :] = val

        _proj_chain(x_ref[0, rows, :], lng_ref[...], lnb_ref[...], win_ref, gqa_ref[...], wqb_ref, gkva_ref[...],
                    wkvb_ref, gqg_ref[...], gqgs_ref[...], gkg_ref[...], gkgs_ref[...],
                    c1_ref[rows, :], s1_ref[rows, :], ca_ref[rows, :], sa_ref[rows, :], put)


def _project(x3, tabs, w, tile):
    b, s, d = x3.shape
    nst = s // tile
    full = lambda shape: pl.BlockSpec(shape, lambda bi, si: (0,) * len(shape))
    tab = pl.BlockSpec((tile, LANES), lambda bi, si: (si, 0))
    hm = lambda nh, dd: pl.BlockSpec((1, nh, tile, dd), lambda bi, si: (bi, 0, si, 0))
    hmt = lambda nh, dd: pl.BlockSpec((1, nh, dd, tile), lambda bi, si: (bi, 0, 0, si))
    out_shape = [
        jax.ShapeDtypeStruct((b, MLA_HEADS, s, MLA_K), BF16),
        jax.ShapeDtypeStruct((b, MLA_HEADS, MLA_K, s), BF16),
        jax.ShapeDtypeStruct((b, MLA_HEADS, s, V_EXT), BF16),
        jax.ShapeDtypeStruct((b, GQA_HEADS, s, GQA_DIM), BF16),
        jax.ShapeDtypeStruct((b, GQA_KV_HEADS, GQA_DIM, s), BF16),
        jax.ShapeDtypeStruct((b, GQA_KV_HEADS, s, V_EXT), BF16),
    ]
    return pl.pallas_call(
        _proj_body,
        grid=(b, nst),
        in_specs=[
            pl.BlockSpec((1, tile, d), lambda bi, si: (bi, si, 0)),
            full((1, d)), full((1, d)),
            full(w["w_in"].shape), full((1, Q_LORA)), full(w["w_qb"].shape),
            full((1, KV_LORA)), full(w["w_kvb"].shape),
            full((1, GQA_DIM)), full((1, GQA_DIM)), full((1, GQA_DIM)), full((1, GQA_DIM)),
            tab, tab, tab, tab,
        ],
        out_specs=[hm(MLA_HEADS, MLA_K), hmt(MLA_HEADS, MLA_K), hm(MLA_HEADS, V_EXT),
                   hm(GQA_HEADS, GQA_DIM), hmt(GQA_KV_HEADS, GQA_DIM), hm(GQA_KV_HEADS, V_EXT)],
        out_shape=out_shape,
        compiler_params=pltpu.CompilerParams(dimension_semantics=("parallel", "parallel"),
                                             vmem_limit_bytes=VMEM_LIMIT),
        name="proj",
    )(x3, w["ln_emb_g"], w["ln_emb_b"], w["w_in"], w["g_q_a"], w["w_qb"], w["g_kv_a"], w["w_kvb"],
      w["g_q_gqa"], w["g_q_gqa_sw"], w["g_k_gqa"], w["g_k_gqa_sw"], *tabs)


def _softmax_pv(q, kt, v, kmt, vm):
    s = jnp.dot(q, kt, preferred_element_type=F32)
    sm = jnp.dot(q, kmt, preferred_element_type=F32)
    col = lax.broadcasted_iota(I32, sm.shape, 1)
    sm = jnp.where(col < N_META, sm, NEG_BIG)
    m = jnp.maximum(jnp.max(s, axis=1, keepdims=True), jnp.max(sm, axis=1, keepdims=True))
    p = jnp.exp2(s - m).astype(BF16)
    pm = jnp.exp2(sm - m).astype(BF16)
    acc = jnp.dot(p, v, preferred_element_type=F32) + jnp.dot(pm, vm, preferred_element_type=F32)
    return acc[:, 0:V_DIM] / acc[:, V_DIM:V_EXT]


def _mla_attn_body(q_ref, k_ref, v_ref, km_ref, vm_ref, o_ref):
    for i in range(q_ref.shape[2] // Q_SUB):
        rows = pl.ds(i * Q_SUB, Q_SUB)
        o = _softmax_pv(q_ref[0, 0, rows, :], k_ref[0, 0], v_ref[0, 0], km_ref[0, 0], vm_ref[0, 0])
        o_ref[rows, :] = o.astype(o_ref.dtype)


def _gqa_attn_body(q_ref, k_ref, v_ref, km_ref, vm_ref, o_ref):
    for g in range(2):
        for i in range(q_ref.shape[2] // Q_SUB):
            rows = pl.ds(i * Q_SUB, Q_SUB)
            o = _softmax_pv(q_ref[0, g, rows, :], k_ref[0, 0], v_ref[0, 0], km_ref[0, 0], vm_ref[0, 0])
            o_ref[rows, GQA_DIM * g:GQA_DIM * (g + 1)] = o.astype(o_ref.dtype)


def _mla_attention(qm, kmt, vm, kmeta_t, vmeta):
    b, h, s, dk = qm.shape
    tq = Q_TILE_MLA
    nq = s // tq
    return pl.pallas_call(
        _mla_attn_body,
        grid=(b, h, nq),
        in_specs=[
            pl.BlockSpec((1, 1, tq, dk), lambda bi, hi, qi: (bi, hi, qi, 0)),
            pl.BlockSpec((1, 1, dk, s), lambda bi, hi, qi: (bi, hi, 0, 0)),
            pl.BlockSpec((1, 1, s, V_EXT), lambda bi, hi, qi: (bi, hi, 0, 0)),
            pl.BlockSpec((1, 1, dk, META_PAD), lambda bi, hi, qi: (0, hi, 0, 0)),
            pl.BlockSpec((1, 1, META_PAD, V_EXT), lambda bi, hi, qi: (0, hi, 0, 0)),
        ],
        out_specs=pl.BlockSpec((tq, V_DIM), lambda bi, hi, qi: (bi * nq + qi, hi)),
        out_shape=jax.ShapeDtypeStruct((b * s, h * V_DIM), BF16),
        compiler_params=pltpu.CompilerParams(dimension_semantics=("parallel", "parallel", "parallel"),
                                             vmem_limit_bytes=VMEM_LIMIT),
        name="mla_attn",
    )(qm, kmt, vm, kmeta_t, vmeta)


def _gqa_attention(qg, kgt, vg, kmeta_t, vmeta):
    b, h, s, d = qg.shape
    hk = kgt.shape[1]
    tq = Q_TILE_GQA
    nq = s // tq
    return pl.pallas_call(
        _gqa_attn_body,
        grid=(b, hk, nq),
        in_specs=[
            pl.BlockSpec((1, 2, tq, d), lambda bi, ji, qi: (bi, ji, qi, 0)),
            pl.BlockSpec((1, 1, d, s), lambda bi, ji, qi: (bi, ji, 0, 0)),
            pl.BlockSpec((1, 1, s, V_EXT), lambda bi, ji, qi: (bi, ji, 0, 0)),
            pl.BlockSpec((1, 1, d, META_PAD), lambda bi, ji, qi: (0, ji, 0, 0)),
            pl.BlockSpec((1, 1, META_PAD, V_EXT), lambda bi, ji, qi: (0, ji, 0, 0)),
        ],
        out_specs=pl.BlockSpec((tq, 2 * d), lambda bi, ji, qi: (bi * nq + qi, ji)),
        out_shape=jax.ShapeDtypeStruct((b * s, h * d), BF16),
        compiler_params=pltpu.CompilerParams(dimension_semantics=("parallel", "parallel", "parallel"),
                                             vmem_limit_bytes=VMEM_LIMIT),
        name="gqa_attn",
    )(qg, kgt, vg, kmeta_t, vmeta)


def _to_tiles(ref, x):
    r = x.shape[0]
    for s in range(x.shape[1] // LANES):
        ref[:, s] = x[:, LANES * s:LANES * (s + 1)].reshape(r // SUBLANES, SUBLANES, LANES)


def _from_tiles(ref):
    r = ref.shape[0] * SUBLANES
    return jnp.concatenate([ref[:, s].reshape(r, LANES) for s in range(ref.shape[1])], axis=1)


MERGE_SUB = 256


def _merge_body(om_ref, og_ref, x_ref, lng_ref, lnb_ref, gom_ref, gog_ref, wo_ref, l1g_ref, l1b_ref,
                wr_ref, br_ref, h1_ref, topi_ref, gate_ref):
    for i in range(x_ref.shape[0] // MERGE_SUB):
        rows = pl.ds(i * MERGE_SUB, MERGE_SUB)
        h1, ti, gt = _merge_chain(om_ref[rows, :], og_ref[rows, :], x_ref[rows, :], lng_ref[...], lnb_ref[...],
                                  gom_ref[...], gog_ref[...], wo_ref, l1g_ref[...], l1b_ref[...], wr_ref, br_ref[...])
        _to_tiles(h1_ref.at[pl.ds(i * (MERGE_SUB // SUBLANES), MERGE_SUB // SUBLANES)], h1)
        topi_ref[:, rows] = ti
        for k in range(TOP_K):
            gate_ref[0, :, pl.ds(k * TOKEN_TILE + i * MERGE_SUB, MERGE_SUB)] = gt[k:k + 1, :]


def _merge_chain(om, og, x, lng, lnb, gom, gog, wo_ref, l1g, l1b, wr_ref, br):
    t = x.shape[0]
    h0 = _layernorm(x, lng, lnb)
    nm = _rmsnorm(om.astype(F32), gom).astype(BF16)
    ng = _rmsnorm(og.astype(F32), gog).astype(BF16)
    half = nm.shape[1]
    mix = jnp.dot(nm, wo_ref[0:half, :], preferred_element_type=F32)
    mix = mix + jnp.dot(ng, wo_ref[half:2 * half, :], preferred_element_type=F32)
    h1 = _layernorm(DEEPNORM_ALPHA * h0 + mix, l1g, l1b)

    hi = h1.astype(BF16)
    lo = (h1 - hi.astype(F32)).astype(BF16)
    acc = jnp.dot(hi, wr_ref[...], preferred_element_type=F32) + jnp.dot(lo, wr_ref[...], preferred_element_type=F32)
    logits = acc[:, 0:LANES] + acc[:, LANES:2 * LANES] + br
    cur = logits.T[0:N_EXPERTS, :]
    eidx = lax.broadcasted_iota(I32, cur.shape, 0)
    vals, idxs = [], []
    for _ in range(TOP_K):
        m = jnp.max(cur, axis=0, keepdims=True)
        i = jnp.min(jnp.where(cur == m, eidx, N_EXPERTS), axis=0, keepdims=True)
        vals.append(m)
        idxs.append(i)
        cur = jnp.where(eidx == i, -jnp.inf, cur)
    ex = [jnp.exp(v - vals[0]) for v in vals]
    den = ex[0] + ex[1] + ex[2] + ex[3]
    sub = lax.broadcasted_iota(I32, (8, t), 0)
    ti = jnp.zeros((8, t), I32)
    gt = jnp.zeros((8, t), F32)
    for k in range(TOP_K):
        ti = jnp.where(sub == k, idxs[k], ti)
        gt = jnp.where(sub == k, ex[k] / den, gt)
    return h1, ti[0:TOP_K, :], gt[0:TOP_K, :]


def _merge(o_mla, o_gqa, x2, w):
    n, d = x2.shape
    tile = TOKEN_TILE
    half = o_mla.shape[1]
    full = lambda shape: pl.BlockSpec(shape, lambda i: (0,) * len(shape))
    row = lambda width: pl.BlockSpec((tile, width), lambda i: (i, 0))
    return pl.pallas_call(
        _merge_body,
        grid=(n // tile,),
        in_specs=[row(half), row(half), row(d), full((1, d)), full((1, d)), full((1, half)), full((1, half)),
                  full((d, d)), full((1, d)), full((1, d)), full((d, 2 * LANES)), full((1, LANES))],
        out_specs=[pl.BlockSpec((tile // SUBLANES, d // LANES, SUBLANES, LANES), lambda i: (i, 0, 0, 0)),
                   pl.BlockSpec((TOP_K, tile), lambda i: (0, i)),
                   pl.BlockSpec((1, 1, TOP_K * tile), lambda i: (i, 0, 0))],
        out_shape=[jax.ShapeDtypeStruct((n // SUBLANES, d // LANES, SUBLANES, LANES), F32),
                   jax.ShapeDtypeStruct((TOP_K, n), I32), jax.ShapeDtypeStruct((n // tile, 1, TOP_K * tile), F32)],
        compiler_params=pltpu.CompilerParams(dimension_semantics=("parallel",), vmem_limit_bytes=VMEM_LIMIT),
        name="merge",
    )(o_mla, o_gqa, x2, w["ln_emb_g"], w["ln_emb_b"], w["g_o_mla"], w["g_o_gqa"], w["w_o"],
      w["ln1_g"], w["ln1_b"], w["w_router"], w["b_router"])


def _lanes_from_sublanes(col):
    diag = lax.broadcasted_iota(I32, col.shape, 0) == lax.broadcasted_iota(I32, col.shape, 1)
    return jnp.sum(jnp.where(diag, col, 0.0), axis=0, keepdims=True)


def _positions_body(topi_ref, lpb_ref, gstart_ref, nslab_ref, lstart_ref, texp_ref, nvalid_ref, ecnt_ref, eoff_ref,
                    cnt_sc, carry_sc, off_sc, *, ntp):
    p = pl.program_id(0)
    j = pl.program_id(1)
    tl = topi_ref.shape[1]
    topi = topi_ref[...]
    eidx = lax.broadcasted_iota(I32, (N_EXPERTS, tl), 0)
    ohs = [eidx == topi[k:k + 1, :] for k in range(TOP_K)]
    onehot = ohs[0].astype(F32) + ohs[1].astype(F32) + ohs[2].astype(F32) + ohs[3].astype(F32)
    cnt = jnp.sum(onehot, axis=1, keepdims=True)
    run = jnp.floor((cnt + (SUBLANES - 1)) * (1.0 / SUBLANES)) * SUBLANES
    tile_run = jnp.broadcast_to(run, (N_EXPERTS, LANES))
    r = lax.broadcasted_iota(I32, (N_EXPERTS, N_EXPERTS), 0)
    c = lax.broadcasted_iota(I32, (N_EXPERTS, N_EXPERTS), 1)
    lower = (c < r).astype(F32)

    @pl.when(jnp.logical_and(p == 0, j == 0))
    def _():
        cnt_sc[...] = jnp.zeros_like(cnt_sc)

    @pl.when(p == 0)
    def _():
        cnt_sc[...] += tile_run

    @pl.when(jnp.logical_and(p == 1, j == 0))
    def _():
        tot = cnt_sc[...]
        pc = jnp.floor((tot + (MOE_TILE - 1)) * (1.0 / MOE_TILE)) * MOE_TILE
        off = jnp.dot(lower, pc, precision=lax.Precision.HIGHEST, preferred_element_type=F32)
        off_sc[...] = off
        carry_sc[...] = jnp.zeros_like(carry_sc)
        cumend = off + pc
        tstart = lax.broadcasted_iota(I32, (N_EXPERTS, ntp), 1).astype(F32) * MOE_TILE
        te = jnp.sum((jnp.broadcast_to(cumend[:, 0:1], (N_EXPERTS, ntp)) <= tstart).astype(I32), axis=0, keepdims=True)
        texp_ref[...] = jnp.minimum(te, N_EXPERTS - 1)
        nvalid_ref[...] = (cumend[N_EXPERTS - 1:N_EXPERTS, :] * (1.0 / MOE_TILE)).astype(I32)
        ecnt_ref[...] = _lanes_from_sublanes(tot).astype(I32)
        eoff_ref[...] = _lanes_from_sublanes(off).astype(I32)

    @pl.when(p == 1)
    def _():
        rr = lax.broadcasted_iota(I32, (tl, tl), 0)
        cc = lax.broadcasted_iota(I32, (tl, tl), 1)
        upper = (rr < cc).astype(BF16)
        before = jnp.dot(onehot.astype(BF16), upper, preferred_element_type=F32)
        loff = jnp.dot(lower, tile_run, precision=lax.Precision.HIGHEST, preferred_element_type=F32)
        base = before + loff[:, 0:1]
        sub = lax.broadcasted_iota(I32, (SUBLANES, tl), 0)
        out = jnp.zeros((SUBLANES, tl), F32)
        for k in range(TOP_K):
            pk = jnp.sum(jnp.where(ohs[k], base, 0.0), axis=0, keepdims=True)
            out = jnp.where(sub == k, pk, out)
        lp = out[0:TOP_K, :].astype(I32)
        lpb = (lp >> 3) * SLAB + (lp & (SUBLANES - 1))
        lpb_ref[0] = jnp.concatenate([lpb[k:k + 1, :] for k in range(TOP_K)], axis=1)
        inv = 1.0 / SUBLANES
        gstart_ref[0] = (_lanes_from_sublanes(off_sc[...] + carry_sc[...]) * inv).astype(I32)
        nslab_ref[0] = (_lanes_from_sublanes(tile_run) * inv).astype(I32)
        lstart_ref[0] = (_lanes_from_sublanes(loff) * inv).astype(I32)
        carry_sc[...] += tile_run


def _positions(topi, ntp):
    n = topi.shape[1]
    tl = TOKEN_TILE
    const = lambda shape: pl.BlockSpec(shape, lambda p, j: (0, 0))
    per_tile = pl.BlockSpec((1, 1, LANES), lambda p, j: (j * p, 0, 0))
    tab = jax.ShapeDtypeStruct((n // tl, 1, LANES), I32)
    return pl.pallas_call(
        functools.partial(_positions_body, ntp=ntp),
        grid=(2, n // tl),
        in_specs=[pl.BlockSpec((TOP_K, tl), lambda p, j: (0, j))],
        out_specs=[pl.BlockSpec((1, 1, TOP_K * tl), lambda p, j: (j * p, 0, 0)), per_tile, per_tile, per_tile,
                   const((1, ntp)), const((1, LANES)), const((1, LANES)), const((1, LANES))],
        out_shape=[jax.ShapeDtypeStruct((n // tl, 1, TOP_K * tl), I32), tab, tab, tab,
                   jax.ShapeDtypeStruct((1, ntp), I32),
                   jax.ShapeDtypeStruct((1, LANES), I32), jax.ShapeDtypeStruct((1, LANES), I32),
                   jax.ShapeDtypeStruct((1, LANES), I32)],
        scratch_shapes=[pltpu.VMEM((N_EXPERTS, LANES), F32)] * 3,
        compiler_params=pltpu.CompilerParams(dimension_semantics=("arbitrary", "arbitrary")),
        name="positions",
    )(topi)


STAGE_ROWS = TOKEN_TILE * TOP_K + N_EXPERTS * SUBLANES
STAGE_FLAT = STAGE_ROWS * SUBLANES


def _load_row(ref, flat_start):
    return ref[pl.ds(flat_start, SUBLANES, stride=SUBLANES), :]


def _store_row(ref, flat_start, v):
    ref[pl.ds(flat_start, SUBLANES, stride=SUBLANES), :] = v


CHUNK_SLABS = 4


def _copy_run(src_ref, src0, dst_ref, dst0, nslab, sem):
    big = CHUNK_SLABS * SLAB
    nbig = nslab >> 2
    nsmall = nslab & (CHUNK_SLABS - 1)

    def start_big(s, c):
        pltpu.make_async_copy(src_ref.at[pl.ds(src0 + s * big, big)], dst_ref.at[pl.ds(dst0 + s * big, big)], sem).start()
        return c

    def start_small(s, c):
        o = nbig * big + s * SLAB
        pltpu.make_async_copy(src_ref.at[pl.ds(src0 + o, SLAB)], dst_ref.at[pl.ds(dst0 + o, SLAB)], sem).start()
        return c

    lax.fori_loop(0, nbig, start_big, 0)
    lax.fori_loop(0, nsmall, start_small, 0)


def _wait_runs(src_ref, dst_ref, nbig, nsmall, sem):
    big = CHUNK_SLABS * SLAB

    def wait_big(s, c):
        pltpu.make_async_copy(src_ref.at[pl.ds(0, big)], dst_ref.at[pl.ds(0, big)], sem).wait()
        return c

    def wait_small(s, c):
        pltpu.make_async_copy(src_ref.at[pl.ds(0, SLAB)], dst_ref.at[pl.ds(0, SLAB)], sem).wait()
        return c

    lax.fori_loop(0, nbig, wait_big, 0)
    lax.fori_loop(0, nsmall, wait_small, 0)


def _chunk_counts(nslab_ref, j):
    def add(e, c):
        n = nslab_ref[j, 0, e]
        return c[0] + (n >> 2), c[1] + (n & (CHUNK_SLABS - 1))

    return lax.fori_loop(0, N_EXPERTS, add, (jnp.int32(0), jnp.int32(0)))


def _dispatch_body(ecnt_ref, eoff_ref, lpb0_ref, lpb1_ref, gstart_ref, nslab_ref, lstart_ref, h0_ref, h1_ref, xs_ref,
                   stg0_ref, stg1_ref, zero_ref, issued_ref, sem, zsem):
    g = pl.program_id(0)
    ng = pl.num_programs(0)
    halves = ((lpb0_ref, h0_ref, stg0_ref), (lpb1_ref, h1_ref, stg1_ref))

    def drain(j):
        _wait_runs(halves[j][2], xs_ref, issued_ref[2 * j], issued_ref[2 * j + 1], sem.at[j])

    for j in range(2):
        lpb_ref, h_ref, stg_ref = halves[j]

        @pl.when(g >= 1)
        def _():
            drain(j)

        def zero_last(e, c):
            last = jnp.maximum(lstart_ref[j, 0, e] + nslab_ref[j, 0, e] - 1, 0)
            stg_ref[pl.ds(last * SLAB, SLAB), :] = jnp.zeros((SLAB, LANES), F32)
            return c

        lax.fori_loop(0, N_EXPERTS, zero_last, 0)

        def move(i, c):
            for u in range(SUBLANES):
                v = _load_row(h_ref, i * SLAB + u)
                for k in range(TOP_K):
                    _store_row(stg_ref, lpb_ref[0, 0, k * TOKEN_TILE + i * SUBLANES + u], v)
            return c

        lax.fori_loop(0, TOKEN_TILE // SUBLANES, move, 0)

        def send_run(e, c):
            _copy_run(stg_ref, lstart_ref[j, 0, e] * SLAB, xs_ref, gstart_ref[j, 0, e] * SLAB,
                      nslab_ref[j, 0, e], sem.at[j])
            return c

        lax.fori_loop(0, N_EXPERTS, send_run, 0)
        nbig, nsmall = _chunk_counts(nslab_ref, j)
        issued_ref[2 * j] = nbig
        issued_ref[2 * j + 1] = nsmall

    @pl.when(g == ng - 1)
    def _():
        zero_ref[...] = jnp.zeros_like(zero_ref)

        def zero_copy(dst_slab):
            return pltpu.make_async_copy(zero_ref, xs_ref.at[pl.ds(dst_slab * SLAB, SLAB)], zsem)

        def pad_segment(e, c):
            rows = ecnt_ref[e]
            first = (eoff_ref[e] + rows) >> 3
            npad = lax.rem(MOE_TILE - lax.rem(rows, MOE_TILE), MOE_TILE) >> 3
            lax.fori_loop(0, npad, lambda s, cc: (zero_copy(first + s).start(), cc)[1], 0)
            lax.fori_loop(0, npad, lambda s, cc: (zero_copy(first + s).wait(), cc)[1], 0)
            return c

        lax.fori_loop(0, N_EXPERTS, pad_segment, 0)
        drain(0)
        drain(1)


def _dispatch(h1_flat, lpb, gstart, nslab, lstart, ecnt, eoff, rows_pad):
    n8 = h1_flat.shape[0]
    t = 2 * TOKEN_TILE
    smem = lambda shape, imap: pl.BlockSpec(shape, imap, memory_space=pltpu.SMEM)
    per_tile = smem((2, 1, LANES), lambda i, c, o: (i, 0, 0))
    vec = lambda j: smem((1, 1, TOP_K * TOKEN_TILE), lambda i, c, o: (2 * i + j, 0, 0))
    rows = lambda j: pl.BlockSpec((TOKEN_TILE * SUBLANES, LANES), lambda i, c, o: (2 * i + j, 0))
    grid_spec = pltpu.PrefetchScalarGridSpec(
        num_scalar_prefetch=2,
        grid=(n8 // (t * SUBLANES),),
        in_specs=[vec(0), vec(1), per_tile, per_tile, per_tile, rows(0), rows(1)],
        out_specs=pl.BlockSpec(memory_space=pl.ANY),
        scratch_shapes=[pltpu.VMEM((STAGE_FLAT, LANES), F32), pltpu.VMEM((STAGE_FLAT, LANES), F32),
                        pltpu.VMEM((SLAB, LANES), F32),
                        pltpu.SMEM((4,), I32), pltpu.SemaphoreType.DMA((2,)), pltpu.SemaphoreType.DMA],
    )
    return pl.pallas_call(
        _dispatch_body,
        grid_spec=grid_spec,
        out_shape=jax.ShapeDtypeStruct((rows_pad * SUBLANES, LANES), F32),
        compiler_params=pltpu.CompilerParams(dimension_semantics=("arbitrary",), vmem_limit_bytes=VMEM_LIMIT),
        name="dispatch",
    )(ecnt, eoff, lpb, lpb, gstart, nslab, lstart, h1_flat, h1_flat)


def _wprep_body(w_ref, wg_ref, wl_ref):
    blk = 2 * LANES
    r = lax.broadcasted_iota(I32, (blk, blk), 0)
    c = lax.broadcasted_iota(I32, (blk, blk), 1)
    sel = (r == jnp.where(c < LANES, 2 * c, 2 * (c - LANES) + 1)).astype(BF16)
    nblk = w_ref.shape[2] // blk
    for b in range(nblk):
        wb = w_ref[0, :, blk * b:blk * (b + 1)].astype(BF16)
        y = jnp.dot(wb, sel, preferred_element_type=F32).astype(BF16)
        wg_ref[0, :, LANES * b:LANES * (b + 1)] = y[:, 0:LANES]
        wl_ref[0, :, LANES * b:LANES * (b + 1)] = y[:, LANES:blk]


def _prep_gate_up(w_gate_up):
    e, d, f2 = w_gate_up.shape
    cb = 1024
    out = jax.ShapeDtypeStruct((e, d, f2 // 2), BF16)
    return pl.pallas_call(
        _wprep_body,
        grid=(e, f2 // cb),
        in_specs=[pl.BlockSpec((1, d, cb), lambda ei, ci: (ei, 0, ci))],
        out_specs=[pl.BlockSpec((1, d, cb // 2), lambda ei, ci: (ei, 0, ci))] * 2,
        out_shape=[out, out],
        compiler_params=pltpu.CompilerParams(dimension_semantics=("parallel", "parallel"),
                                             vmem_limit_bytes=VMEM_LIMIT),
        name="wprep",
    )(w_gate_up)


def _ffn_body(texp_ref, nvalid_ref, x_ref, wg_ref, wl_ref, wd_ref, bg_ref, bl_ref, bd_ref, y_ref):
    @pl.when(pl.program_id(0) < nvalid_ref[0])
    def _():
        x = _from_tiles(x_ref).astype(BF16)
        hg = jnp.dot(x, wg_ref[0], preferred_element_type=F32) + bg_ref[0]
        hl = jnp.dot(x, wl_ref[0], preferred_element_type=F32) + bl_ref[0]
        g = jnp.minimum(hg, SWIGLU_LIMIT)
        lin = jnp.clip(hl, -SWIGLU_LIMIT, SWIGLU_LIMIT)
        act = g * (1.0 / (1.0 + jnp.exp(-SWIGLU_ALPHA * g))) * (lin + 1.0)
        _to_tiles(y_ref, jnp.dot(act.astype(BF16), wd_ref[0], preferred_element_type=F32) + bd_ref[0])


def _grouped_ffn(xs_tiles, texp, nvalid, wg, wl, wd, bg, bl, bd, ntiles):
    d = wg.shape[1]
    tm = MOE_TILE
    f = wg.shape[2]
    xmap = lambda i, te, nv: (jnp.minimum(i, nv[0] - 1), 0, 0, 0)
    wmap = lambda i, te, nv: (te[i], 0, 0)
    rows_blk = pl.BlockSpec((tm // SUBLANES, d // LANES, SUBLANES, LANES), xmap)
    grid_spec = pltpu.PrefetchScalarGridSpec(
        num_scalar_prefetch=2,
        grid=(ntiles,),
        in_specs=[rows_blk,
                  pl.BlockSpec((1, d, f), wmap), pl.BlockSpec((1, d, f), wmap), pl.BlockSpec((1, f, d), wmap),
                  pl.BlockSpec((1, 1, f), wmap), pl.BlockSpec((1, 1, f), wmap), pl.BlockSpec((1, 1, d), wmap)],
        out_specs=rows_blk,
    )
    return pl.pallas_call(
        _ffn_body,
        grid_spec=grid_spec,
        out_shape=jax.ShapeDtypeStruct(xs_tiles.shape, F32),
        compiler_params=pltpu.CompilerParams(dimension_semantics=("arbitrary",), vmem_limit_bytes=VMEM_LIMIT),
        name="ffn",
    )(texp, nvalid, xs_tiles, wg, wl, wd, bg, bl, bd)


def _combine_body(lpb0_ref, lpb1_ref, gate0_ref, gate1_ref, gstart_ref, nslab_ref, lstart_ref,
                  gstart2_ref, nslab2_ref, lstart2_ref, h_ref, l2g_ref, l2b_ref, ys_ref, o_ref,
                  stg0_ref, stg1_ref, moe_ref, sem):
    g = pl.program_id(0)
    ng = pl.num_programs(0)
    halves = ((lpb0_ref, gate0_ref, stg0_ref), (lpb1_ref, gate1_ref, stg1_ref))

    def fetch(gs_ref, ns_ref, ls_ref, jt, half):
        def per_expert(e, c):
            _copy_run(ys_ref, gs_ref[jt, 0, e] * SLAB, halves[half][2], ls_ref[jt, 0, e] * SLAB,
                      ns_ref[jt, 0, e], sem.at[half])
            return c

        lax.fori_loop(0, N_EXPERTS, per_expert, 0)

    def wait_tile(j):
        nbig, nsmall = _chunk_counts(nslab_ref, j)
        _wait_runs(ys_ref, halves[j][2], nbig, nsmall, sem.at[j])

    def gather(j):
        lpb_ref, gate_ref, stg_ref = halves[j]

        def body(i, c):
            for u in range(SUBLANES):
                tok = i * SUBLANES + u
                acc = gate_ref[0, 0, tok] * _load_row(stg_ref, lpb_ref[0, 0, tok])
                for k in range(1, TOP_K):
                    acc = acc + (gate_ref[0, 0, k * TOKEN_TILE + tok]
                                 * _load_row(stg_ref, lpb_ref[0, 0, k * TOKEN_TILE + tok]))
                _store_row(moe_ref, i * SLAB + u, acc)
            return c

        lax.fori_loop(0, TOKEN_TILE // SUBLANES, body, 0)

    def finish(j):
        nblk = TOKEN_TILE // SUBLANES
        moe = jnp.concatenate(
            [jnp.concatenate([moe_ref[pl.ds(i * SLAB + s * SUBLANES, SUBLANES), :] for i in range(nblk)], axis=0)
             for s in range(SUBLANES)], axis=1)
        h1 = _from_tiles(h_ref.at[pl.ds(j * nblk, nblk)])
        o_ref[pl.ds(j * TOKEN_TILE, TOKEN_TILE), :] = _layernorm(DEEPNORM_ALPHA * h1 + moe, l2g_ref[...], l2b_ref[...])

    @pl.when(g == 0)
    def _():
        fetch(gstart_ref, nslab_ref, lstart_ref, 0, 0)

    fetch(gstart_ref, nslab_ref, lstart_ref, 1, 1)
    wait_tile(0)
    gather(0)

    @pl.when(g + 1 < ng)
    def _():
        fetch(gstart2_ref, nslab2_ref, lstart2_ref, 0, 0)

    finish(0)
    wait_tile(1)
    gather(1)
    finish(1)


def _combine(lpb, gates, gstart, nslab, lstart, h1_tiles, ys_flat, w):
    d = h1_tiles.shape[1] * LANES
    n = h1_tiles.shape[0] * SUBLANES
    t = 2 * TOKEN_TILE
    ng = n // t
    full = lambda shape: pl.BlockSpec(shape, lambda i: (0,) * len(shape))
    smem = lambda shape, imap: pl.BlockSpec(shape, imap, memory_space=pltpu.SMEM)
    vec = lambda j: smem((1, 1, TOP_K * TOKEN_TILE), lambda i: (2 * i + j, 0, 0))
    this_step = smem((2, 1, LANES), lambda i: (i, 0, 0))
    next_step = smem((2, 1, LANES), lambda i: (jnp.minimum(i + 1, ng - 1), 0, 0))
    return pl.pallas_call(
        _combine_body,
        grid=(ng,),
        in_specs=[vec(0), vec(1), vec(0), vec(1), this_step, this_step, this_step, next_step, next_step, next_step,
                  pl.BlockSpec((t // SUBLANES, d // LANES, SUBLANES, LANES), lambda i: (i, 0, 0, 0)),
                  full((1, d)), full((1, d)),
                  pl.BlockSpec(memory_space=pl.ANY)],
        out_specs=pl.BlockSpec((t, d), lambda i: (i, 0)),
        out_shape=jax.ShapeDtypeStruct((n, d), F32),
        scratch_shapes=[pltpu.VMEM((STAGE_FLAT, LANES), F32), pltpu.VMEM((STAGE_FLAT, LANES), F32),
                        pltpu.VMEM((TOKEN_TILE * SUBLANES, LANES), F32), pltpu.SemaphoreType.DMA((2,))],
        compiler_params=pltpu.CompilerParams(dimension_semantics=("arbitrary",), vmem_limit_bytes=VMEM_LIMIT),
        name="combine",
    )(lpb, lpb, gates, gates, gstart, nslab, lstart, gstart, nslab, lstart, h1_tiles, w["ln2_g"], w["ln2_b"],
      ys_flat)


def _rope_tables(pos_1d, row, col):
    inv = ROPE_THETA ** (-jnp.arange(0, ROPE_DIM, 2, dtype=F32) / ROPE_DIM)

    def cs(p):
        ang = p.astype(F32)[:, None] * inv[None, :]
        return jnp.cos(ang), jnp.sin(ang)

    c1, s1 = cs(pos_1d)
    cr, sr = cs(row)
    cc, sc = cs(col)
    return (jnp.concatenate([c1, c1, c1, c1], axis=1), jnp.concatenate([-s1, s1, -s1, s1], axis=1),
            jnp.concatenate([cr, cr, cc, cc], axis=1), jnp.concatenate([-sr, sr, -sc, sc], axis=1))


def _prep_weights(ln_emb_g, ln_emb_b, w_in, g_q_a, w_q_b, g_kv_a, w_kv_b, g_q_gqa, g_k_gqa, g_o_mla, g_o_gqa, w_o,
                  ln1_g, ln1_b, w_router, b_router, ln2_g, ln2_b):
    r2 = lambda v: v.reshape(1, -1).astype(F32)
    o = np.cumsum([0, Q_LORA, KV_LORA, ROPE_DIM, GQA_HEADS * GQA_DIM, GQA_KV_HEADS * GQA_DIM, GQA_KV_HEADS * GQA_DIM])
    wi = w_in[0]
    seg = [wi[:, o[i]:o[i + 1]] for i in range(6)]
    kpe2 = jnp.concatenate([seg[2], seg[2]], axis=1)
    w_in_p = jnp.concatenate([seg[0], seg[1], kpe2, _swap_halves64(kpe2), seg[3], _swap_halves64(seg[3]),
                              seg[4], _swap_halves64(seg[4]), seg[5]], axis=1).astype(BF16)
    wq = w_q_b[0].reshape(Q_LORA, MLA_HEADS, QK_DIM)
    wq_rope = wq[:, :, NOPE_DIM:].reshape(Q_LORA, -1)
    w_qb = jnp.concatenate([wq[:, :, :NOPE_DIM].reshape(Q_LORA, -1), wq_rope, _swap_halves64(wq_rope)],
                           axis=1).astype(BF16)
    wk = w_kv_b[0].reshape(KV_LORA, MLA_HEADS, NOPE_DIM + V_DIM)
    w_kvb = jnp.concatenate([wk[:, :, :NOPE_DIM].reshape(KV_LORA, -1), wk[:, :, NOPE_DIM:].reshape(KV_LORA, -1)],
                            axis=1).astype(BF16)
    wr = jnp.pad(w_router[0].astype(F32), ((0, 0), (0, LANES - N_EXPERTS)))
    wr_hi = wr.astype(BF16)
    w_router_p = jnp.concatenate([wr_hi, (wr - wr_hi.astype(F32)).astype(BF16)], axis=1)
    b_router_p = jnp.pad(b_router[0].astype(F32), (0, LANES - N_EXPERTS), constant_values=NEG_BIG).reshape(1, LANES)
    return dict(
        ln_emb_g=r2(ln_emb_g), ln_emb_b=r2(ln_emb_b), w_in=w_in_p, g_q_a=r2(g_q_a[0]), w_qb=w_qb,
        g_kv_a=r2(g_kv_a[0]), w_kvb=w_kvb, g_q_gqa=r2(g_q_gqa[0]), g_k_gqa=r2(g_k_gqa[0]),
        g_q_gqa_sw=_swap_halves64(r2(g_q_gqa[0])), g_k_gqa_sw=_swap_halves64(r2(g_k_gqa[0])),
        g_o_mla=r2(g_o_mla[0]), g_o_gqa=r2(g_o_gqa[0]), w_o=w_o[0].astype(BF16), ln1_g=r2(ln1_g[0]),
        ln1_b=r2(ln1_b[0]), w_router=w_router_p, b_router=b_router_p, ln2_g=r2(ln2_g[0]), ln2_b=r2(ln2_b[0]))


def kernel(x, meta_tokens, ln_emb_g, ln_emb_b, w_in, g_q_a, w_q_b, g_kv_a, w_kv_b, g_q_gqa, g_k_gqa, g_o_mla, g_o_gqa,
           w_o, ln1_g, ln1_b, w_router, b_router, w_gate_up, b_gate_up, w_down, b_down, ln2_g, ln2_b):
    b, s, d = x.shape
    n = b * s
    w = _prep_weights(ln_emb_g, ln_emb_b, w_in, g_q_a, w_q_b, g_kv_a, w_kv_b, g_q_gqa, g_k_gqa, g_o_mla, g_o_gqa,
                      w_o, ln1_g, ln1_b, w_router, b_router, ln2_g, ln2_b)

    tok = jnp.arange(s, dtype=I32)
    tabs_real = _rope_tables(tok + N_META, tok // GRID_W, tok % GRID_W)
    mt = jnp.arange(META_PAD, dtype=I32)
    tabs_meta = _rope_tables(mt, jnp.full((META_PAD,), -1, I32), mt)

    qm, km, vm, qg, kg, vg = _project(x, tabs_real, w, ROW_TILE)
    meta = jnp.pad(meta_tokens.astype(x.dtype), ((0, META_PAD - N_META), (0, 0))).reshape(1, META_PAD, d)
    _, km_m, vm_m, _, kg_m, vg_m = _project(meta, tabs_meta, w, META_PAD)

    o_mla = _mla_attention(qm, km, vm, km_m, vm_m)
    o_gqa = _gqa_attention(qg, kg, vg, kg_m, vg_m)

    h1_tiles, topi, gates = _merge(o_mla, o_gqa, x.reshape(n, d), w)

    run_rows = n * TOP_K + (n // TOKEN_TILE) * N_EXPERTS * (SUBLANES - 1)
    ntiles = -(-run_rows // MOE_TILE) + N_EXPERTS
    ntp = -(-ntiles // LANES) * LANES
    rows_pad = ntiles * MOE_TILE
    lpb, gstart, nslab, lstart, texp, nvalid, ecnt, eoff = _positions(topi, ntp)
    flat = lambda a: a.reshape(-1, LANES)
    xs_flat = _dispatch(flat(h1_tiles), lpb, gstart, nslab, lstart, ecnt.reshape(-1), eoff.reshape(-1), rows_pad)

    wg, wl = _prep_gate_up(w_gate_up[0])
    bgu = b_gate_up[0].astype(F32)
    tiles = lambda a: a.reshape(-1, d // LANES, SUBLANES, LANES)
    ys_tiles = _grouped_ffn(tiles(xs_flat), texp.reshape(-1), nvalid.reshape(-1), wg, wl, w_down[0].astype(BF16),
                            bgu[:, 0::2].reshape(N_EXPERTS, 1, D_FF), bgu[:, 1::2].reshape(N_EXPERTS, 1, D_FF),
                            b_down[0].astype(F32).reshape(N_EXPERTS, 1, d), ntiles)

    out = _combine(lpb, gates, gstart, nslab, lstart, h1_tiles, flat(ys_tiles), w)
    return out.reshape(b, s, d)
```

```python
import functools

import jax
import jax.numpy as jnp
import numpy as np
from jax import lax
from jax.experimental import pallas as pl
from jax.experimental.pallas import tpu as pltpu

D_MODEL = 1024
N_META = 16
GRID_W = 64
ROPE_THETA = 10000.0
MLA_HEADS = 4
Q_LORA = 256
KV_LORA = 128
NOPE_DIM = 128
ROPE_DIM = 64
V_DIM = 128
QK_DIM = NOPE_DIM + ROPE_DIM
GQA_HEADS = 4
GQA_KV_HEADS = 2
GQA_DIM = 128
N_EXPERTS = 32
TOP_K = 4
D_FF = D_MODEL
SWIGLU_LIMIT = 7.0
SWIGLU_ALPHA = 1.702
RMS_EPS = 1e-6
LN_EPS = 1e-5
DEPTH = 1
DEEPNORM_ALPHA = (2.0 * DEPTH) ** 0.25

LANES = 128
SUBLANES = 8
SLAB = SUBLANES * SUBLANES
META_PAD = 128
MLA_K = 2 * LANES
NEG_BIG = -1e30
LOG2E = 1.4426950408889634
V_EXT = 2 * LANES

ROW_TILE = 512
Q_TILE_MLA = 1024
Q_TILE_GQA = 512
Q_SUB_MLA = 128
Q_SUB_GQA = 512
TOKEN_TILE = 512
MOE_TILE = 512
VMEM_LIMIT = 56 * 1024 * 1024

F32 = jnp.float32
BF16 = jnp.bfloat16
I32 = jnp.int32


def _layernorm(x, g, b):
    mu = jnp.mean(x, axis=-1, keepdims=True)
    xc = x - mu
    var = jnp.mean(xc * xc, axis=-1, keepdims=True)
    return xc * lax.rsqrt(var + LN_EPS) * g + b


def _rmsnorm(x, g):
    return x * lax.rsqrt(jnp.mean(x * x, axis=-1, keepdims=True) + RMS_EPS) * g


def _swap_halves64(v):
    shp = v.shape
    return v.reshape(shp[:-1] + (shp[-1] // 64, 2, 32))[..., ::-1, :].reshape(shp)


PROJ_SUB = 256


def _proj_chain(x, lng, lnb, win_ref, gqa, wqb_ref, gkva, wkvb_ref, gqg, gqgs, gkg, gkgs, c1, s1, ca, sa, put):
    r = x.shape[0]
    h0 = _layernorm(x, lng, lnb)
    z = jnp.dot(h0.astype(BF16), win_ref[...], preferred_element_type=F32)
    q_a = z[:, 0:256]
    kv_a = z[:, 256:384]
    kpe2 = z[:, 384:512]
    kpe2s = z[:, 512:640]
    q_g = z[:, 640:1152]
    q_gs = z[:, 1152:1664]
    k_g = z[:, 1664:1920]
    k_gs = z[:, 1920:2176]
    v_g = z[:, 2176:2432]

    q = jnp.dot(_rmsnorm(q_a, gqa).astype(BF16), wqb_ref[...], preferred_element_type=F32)
    kv = jnp.dot(_rmsnorm(kv_a, gkva).astype(BF16), wkvb_ref[...], preferred_element_type=F32)
    krot = kpe2 * c1 + kpe2s * s1
    lane = lax.broadcasted_iota(I32, krot.shape, 1)
    scale_a = QK_DIM ** -0.5 * LOG2E
    ones = jnp.ones((r, LANES), F32)
    for c in range(MLA_HEADS // 2):
        lo = LANES * c
        qr = q[:, 512 + lo:512 + lo + LANES] * c1 + q[:, 768 + lo:768 + lo + LANES] * s1
        for hh in range(2):
            h = 2 * c + hh
            slot = jnp.where((lane // 64) == hh, qr, 0.0)
            put("qm", h, (jnp.concatenate([q[:, LANES * h:LANES * (h + 1)], slot], axis=1) * scale_a).astype(BF16))
    for h in range(MLA_HEADS):
        put("km", h, jnp.concatenate([kv[:, LANES * h:LANES * (h + 1)], krot], axis=1).T.astype(BF16))
        put("vm", h, jnp.concatenate([kv[:, 512 + LANES * h:512 + LANES * (h + 1)], ones], axis=1).astype(BF16))

    scale_b = GQA_DIM ** -0.5 * LOG2E
    cq, sq = ca * gqg, sa * gqgs
    ck, sk = ca * gkg, sa * gkgs

    def norm_rot(xh, xs, cg, sg, scale):
        inv = lax.rsqrt(jnp.mean(xh * xh, axis=-1, keepdims=True) + RMS_EPS)
        return (xh * cg + xs * sg) * (inv * scale)

    for h in range(GQA_HEADS):
        sl = slice(LANES * h, LANES * (h + 1))
        put("qg", h, norm_rot(q_g[:, sl], q_gs[:, sl], cq, sq, scale_b).astype(BF16))
    for j in range(GQA_KV_HEADS):
        sl = slice(LANES * j, LANES * (j + 1))
        put("kg", j, norm_rot(k_g[:, sl], k_gs[:, sl], ck, sk, 1.0).T.astype(BF16))
        put("vg", j, jnp.concatenate([v_g[:, sl], ones], axis=1).astype(BF16))


def _proj_body(x_ref, lng_ref, lnb_ref, win_ref, gqa_ref, wqb_ref, gkva_ref, wkvb_ref, gqg_ref, gqgs_ref, gkg_ref,
               gkgs_ref, c1_ref, s1_ref, ca_ref, sa_ref,
               qm_ref, km_ref, vm_ref, qg_ref, kg_ref, vg_ref):
    outs = dict(qm=qm_ref, km=km_ref, vm=vm_ref, qg=qg_ref, kg=kg_ref, vg=vg_ref)
    tile = x_ref.shape[1]
    sub = min(PROJ_SUB, tile)
    for i in range(tile // sub):
        rows = pl.ds(i * sub, sub)

        def put(name, h, val):
            if name in ("km", "kg"):
                outs[name][0, h, :, rows] = val
            else:
                outs[name][0, h, rows, :] = val

        _proj_chain(x_ref[0, rows, :], lng_ref[...], lnb_ref[...], win_ref, gqa_ref[...], wqb_ref, gkva_ref[...],
                    wkvb_ref, gqg_ref[...], gqgs_ref[...], gkg_ref[...], gkgs_ref[...],
                    c1_ref[rows, :], s1_ref[rows, :], ca_ref[rows, :], sa_ref[rows, :], put)


def _project(x3, tabs, w, tile):
    b, s, d = x3.shape
    nst = s // tile
    full = lambda shape: pl.BlockSpec(shape, lambda bi, si: (0,) * len(shape))
    tab = pl.BlockSpec((tile, LANES), lambda bi, si: (si, 0))
    hm = lambda nh, dd: pl.BlockSpec((1, nh, tile, dd), lambda bi, si: (bi, 0, si, 0))
    hmt = lambda nh, dd: pl.BlockSpec((1, nh, dd, tile), lambda bi, si: (bi, 0, 0, si))
    out_shape = [
        jax.ShapeDtypeStruct((b, MLA_HEADS, s, MLA_K), BF16),
        jax.ShapeDtypeStruct((b, MLA_HEADS, MLA_K, s), BF16),
        jax.ShapeDtypeStruct((b, MLA_HEADS, s, V_EXT), BF16),
        jax.ShapeDtypeStruct((b, GQA_HEADS, s, GQA_DIM), BF16),
        jax.ShapeDtypeStruct((b, GQA_KV_HEADS, GQA_DIM, s), BF16),
        jax.ShapeDtypeStruct((b, GQA_KV_HEADS, s, V_EXT), BF16),
    ]
    return pl.pallas_call(
        _proj_body,
        grid=(b, nst),
        in_specs=[
            pl.BlockSpec((1, tile, d), lambda bi, si: (bi, si, 0)),
            full((1, d)), full((1, d)),
            full(w["w_in"].shape), full((1, Q_LORA)), full(w["w_qb"].shape),
            full((1, KV_LORA)), full(w["w_kvb"].shape),
            full((1, GQA_DIM)), full((1, GQA_DIM)), full((1, GQA_DIM)), full((1, GQA_DIM)),
            tab, tab, tab, tab,
        ],
        out_specs=[hm(MLA_HEADS, MLA_K), hmt(MLA_HEADS, MLA_K), hm(MLA_HEADS, V_EXT),
                   hm(GQA_HEADS, GQA_DIM), hmt(GQA_KV_HEADS, GQA_DIM), hm(GQA_KV_HEADS, V_EXT)],
        out_shape=out_shape,
        compiler_params=pltpu.CompilerParams(dimension_semantics=("parallel", "parallel"),
                                             vmem_limit_bytes=VMEM_LIMIT),
        name="proj",
    )(x3, w["ln_emb_g"], w["ln_emb_b"], w["w_in"], w["g_q_a"], w["w_qb"], w["g_kv_a"], w["w_kvb"],
      w["g_q_gqa"], w["g_q_gqa_sw"], w["g_k_gqa"], w["g_k_gqa_sw"], *tabs)


def _softmax_pv(q, kt, v, kmt, vm):
    s = jnp.dot(q, kt, preferred_element_type=F32)
    sm = jnp.dot(q, kmt, preferred_element_type=F32)
    col = lax.broadcasted_iota(I32, sm.shape, 1)
    sm = jnp.where(col < N_META, sm, NEG_BIG)
    m = jnp.maximum(jnp.max(s, axis=1, keepdims=True), jnp.max(sm, axis=1, keepdims=True))
    p = jnp.exp2(s - m).astype(BF16)
    pm = jnp.exp2(sm - m).astype(BF16)
    acc = jnp.dot(p, v, preferred_element_type=F32) + jnp.dot(pm, vm, preferred_element_type=F32)
    return acc[:, 0:V_DIM] / acc[:, V_DIM:V_EXT]


def _mla_attn_body(q_ref, k_ref, v_ref, km_ref, vm_ref, o_ref):
    for i in range(q_ref.shape[2] // Q_SUB_MLA):
        rows = pl.ds(i * Q_SUB_MLA, Q_SUB_MLA)
        o = _softmax_pv(q_ref[0, 0, rows, :], k_ref[0, 0], v_ref[0, 0], km_ref[0, 0], vm_ref[0, 0])
        o_ref[rows, :] = o.astype(o_ref.dtype)


def _gqa_attn_body(q_ref, k_ref, v_ref, km_ref, vm_ref, o_ref):
    for g in range(2):
        for i in range(q_ref.shape[2] // Q_SUB_GQA):
            rows = pl.ds(i * Q_SUB_GQA, Q_SUB_GQA)
            o = _softmax_pv(q_ref[0, g, rows, :], k_ref[0, 0], v_ref[0, 0], km_ref[0, 0], vm_ref[0, 0])
            o_ref[rows, GQA_DIM * g:GQA_DIM * (g + 1)] = o.astype(o_ref.dtype)


def _mla_attention(qm, kmt, vm, kmeta_t, vmeta):
    b, h, s, dk = qm.shape
    tq = Q_TILE_MLA
    nq = s // tq
    return pl.pallas_call(
        _mla_attn_body,
        grid=(b, h, nq),
        in_specs=[
            pl.BlockSpec((1, 1, tq, dk), lambda bi, hi, qi: (bi, hi, qi, 0)),
            pl.BlockSpec((1, 1, dk, s), lambda bi, hi, qi: (bi, hi, 0, 0)),
            pl.BlockSpec((1, 1, s, V_EXT), lambda bi, hi, qi: (bi, hi, 0, 0)),
            pl.BlockSpec((1, 1, dk, META_PAD), lambda bi, hi, qi: (0, hi, 0, 0)),
            pl.BlockSpec((1, 1, META_PAD, V_EXT), lambda bi, hi, qi: (0, hi, 0, 0)),
        ],
        out_specs=pl.BlockSpec((tq, V_DIM), lambda bi, hi, qi: (bi * nq + qi, hi)),
        out_shape=jax.ShapeDtypeStruct((b * s, h * V_DIM), BF16),
        compiler_params=pltpu.CompilerParams(dimension_semantics=("parallel", "parallel", "parallel"),
                                             vmem_limit_bytes=VMEM_LIMIT),
        name="mla_attn",
    )(qm, kmt, vm, kmeta_t, vmeta)


def _gqa_attention(qg, kgt, vg, kmeta_t, vmeta):
    b, h, s, d = qg.shape
    hk = kgt.shape[1]
    tq = Q_TILE_GQA
    nq = s // tq
    return pl.pallas_call(
        _gqa_attn_body,
        grid=(b, hk, nq),
        in_specs=[
            pl.BlockSpec((1, 2, tq, d), lambda bi, ji, qi: (bi, ji, qi, 0)),
            pl.BlockSpec((1, 1, d, s), lambda bi, ji, qi: (bi, ji, 0, 0)),
            pl.BlockSpec((1, 1, s, V_EXT), lambda bi, ji, qi: (bi, ji, 0, 0)),
            pl.BlockSpec((1, 1, d, META_PAD), lambda bi, ji, qi: (0, ji, 0, 0)),
            pl.BlockSpec((1, 1, META_PAD, V_EXT), lambda bi, ji, qi: (0, ji, 0, 0)),
        ],
        out_specs=pl.BlockSpec((tq, 2 * d), lambda bi, ji, qi: (bi * nq + qi, ji)),
        out_shape=jax.ShapeDtypeStruct((b * s, h * d), BF16),
        compiler_params=pltpu.CompilerParams(dimension_semantics=("parallel", "parallel", "parallel"),
                                             vmem_limit_bytes=VMEM_LIMIT),
        name="gqa_attn",
    )(qg, kgt, vg, kmeta_t, vmeta)


def _to_tiles(ref, x):
    r = x.shape[0]
    for s in range(x.shape[1] // LANES):
        ref[:, s] = x[:, LANES * s:LANES * (s + 1)].reshape(r // SUBLANES, SUBLANES, LANES)


def _from_tiles(ref):
    r = ref.shape[0] * SUBLANES
    return jnp.concatenate([ref[:, s].reshape(r, LANES) for s in range(ref.shape[1])], axis=1)


MERGE_SUB = 256


def _merge_body(om_ref, og_ref, x_ref, lng_ref, lnb_ref, gom_ref, gog_ref, wo_ref, l1g_ref, l1b_ref,
                wr_ref, br_ref, h1_ref, topi_ref, gate_ref):
    for i in range(x_ref.shape[0] // MERGE_SUB):
        rows = pl.ds(i * MERGE_SUB, MERGE_SUB)
        h1, ti, gt = _merge_chain(om_ref[rows, :], og_ref[rows, :], x_ref[rows, :], lng_ref[...], lnb_ref[...],
                                  gom_ref[...], gog_ref[...], wo_ref, l1g_ref[...], l1b_ref[...], wr_ref, br_ref[...])
        _to_tiles(h1_ref.at[pl.ds(i * (MERGE_SUB // SUBLANES), MERGE_SUB // SUBLANES)], h1)
        topi_ref[:, rows] = ti
        for k in range(TOP_K):
            gate_ref[0, :, pl.ds(k * TOKEN_TILE + i * MERGE_SUB, MERGE_SUB)] = gt[k:k + 1, :]


def _merge_chain(om, og, x, lng, lnb, gom, gog, wo_ref, l1g, l1b, wr_ref, br):
    t = x.shape[0]
    h0 = _layernorm(x, lng, lnb)
    nm = _rmsnorm(om.astype(F32), gom).astype(BF16)
    ng = _rmsnorm(og.astype(F32), gog).astype(BF16)
    half = nm.shape[1]
    mix = jnp.dot(nm, wo_ref[0:half, :], preferred_element_type=F32)
    mix = mix + jnp.dot(ng, wo_ref[half:2 * half, :], preferred_element_type=F32)
    h1 = _layernorm(DEEPNORM_ALPHA * h0 + mix, l1g, l1b)

    hi = h1.astype(BF16)
    lo = (h1 - hi.astype(F32)).astype(BF16)
    acc = jnp.dot(hi, wr_ref[...], preferred_element_type=F32) + jnp.dot(lo, wr_ref[...], preferred_element_type=F32)
    logits = acc[:, 0:LANES] + acc[:, LANES:2 * LANES] + br
    cur = logits.T[0:N_EXPERTS, :]
    eidx = lax.broadcasted_iota(I32, cur.shape, 0)
    vals, idxs = [], []
    for _ in range(TOP_K):
        m = jnp.max(cur, axis=0, keepdims=True)
        i = jnp.min(jnp.where(cur == m, eidx, N_EXPERTS), axis=0, keepdims=True)
        vals.append(m)
        idxs.append(i)
        cur = jnp.where(eidx == i, -jnp.inf, cur)
    ex = [jnp.exp(v - vals[0]) for v in vals]
    den = ex[0] + ex[1] + ex[2] + ex[3]
    sub = lax.broadcasted_iota(I32, (8, t), 0)
    ti = jnp.zeros((8, t), I32)
    gt = jnp.zeros((8, t), F32)
    for k in range(TOP_K):
        ti = jnp.where(sub == k, idxs[k], ti)
        gt = jnp.where(sub == k, ex[k] / den, gt)
    return h1, ti[0:TOP_K, :], gt[0:TOP_K, :]


def _merge(o_mla, o_gqa, x2, w):
    n, d = x2.shape
    tile = TOKEN_TILE
    half = o_mla.shape[1]
    full = lambda shape: pl.BlockSpec(shape, lambda i: (0,) * len(shape))
    row = lambda width: pl.BlockSpec((tile, width), lambda i: (i, 0))
    return pl.pallas_call(
        _merge_body,
        grid=(n // tile,),
        in_specs=[row(half), row(half), row(d), full((1, d)), full((1, d)), full((1, half)), full((1, half)),
                  full((d, d)), full((1, d)), full((1, d)), full((d, 2 * LANES)), full((1, LANES))],
        out_specs=[pl.BlockSpec((tile // SUBLANES, d // LANES, SUBLANES, LANES), lambda i: (i, 0, 0, 0)),
                   pl.BlockSpec((TOP_K, tile), lambda i: (0, i)),
                   pl.BlockSpec((1, 1, TOP_K * tile), lambda i: (i, 0, 0))],
        out_shape=[jax.ShapeDtypeStruct((n // SUBLANES, d // LANES, SUBLANES, LANES), F32),
                   jax.ShapeDtypeStruct((TOP_K, n), I32), jax.ShapeDtypeStruct((n // tile, 1, TOP_K * tile), F32)],
        compiler_params=pltpu.CompilerParams(dimension_semantics=("parallel",), vmem_limit_bytes=VMEM_LIMIT),
        name="merge",
    )(o_mla, o_gqa, x2, w["ln_emb_g"], w["ln_emb_b"], w["g_o_mla"], w["g_o_gqa"], w["w_o"],
      w["ln1_g"], w["ln1_b"], w["w_router"], w["b_router"])


def _lanes_from_sublanes(col):
    diag = lax.broadcasted_iota(I32, col.shape, 0) == lax.broadcasted_iota(I32, col.shape, 1)
    return jnp.sum(jnp.where(diag, col, 0.0), axis=0, keepdims=True)


def _positions_body(topi_ref, lpb_ref, gstart_ref, nslab_ref, lstart_ref, texp_ref, nvalid_ref, ecnt_ref, eoff_ref,
                    cnt_sc, carry_sc, off_sc, *, ntp):
    p = pl.program_id(0)
    j = pl.program_id(1)
    tl = topi_ref.shape[1]
    topi = topi_ref[...]
    eidx = lax.broadcasted_iota(I32, (N_EXPERTS, tl), 0)
    ohs = [eidx == topi[k:k + 1, :] for k in range(TOP_K)]
    onehot = ohs[0].astype(F32) + ohs[1].astype(F32) + ohs[2].astype(F32) + ohs[3].astype(F32)
    cnt = jnp.sum(onehot, axis=1, keepdims=True)
    run = jnp.floor((cnt + (SUBLANES - 1)) * (1.0 / SUBLANES)) * SUBLANES
    tile_run = jnp.broadcast_to(run, (N_EXPERTS, LANES))
    r = lax.broadcasted_iota(I32, (N_EXPERTS, N_EXPERTS), 0)
    c = lax.broadcasted_iota(I32, (N_EXPERTS, N_EXPERTS), 1)
    lower = (c < r).astype(F32)

    @pl.when(jnp.logical_and(p == 0, j == 0))
    def _():
        cnt_sc[...] = jnp.zeros_like(cnt_sc)

    @pl.when(p == 0)
    def _():
        cnt_sc[...] += tile_run

    @pl.when(jnp.logical_and(p == 1, j == 0))
    def _():
        tot = cnt_sc[...]
        pc = jnp.floor((tot + (MOE_TILE - 1)) * (1.0 / MOE_TILE)) * MOE_TILE
        off = jnp.dot(lower, pc, precision=lax.Precision.HIGHEST, preferred_element_type=F32)
        off_sc[...] = off
        carry_sc[...] = jnp.zeros_like(carry_sc)
        cumend = off + pc
        tstart = lax.broadcasted_iota(I32, (N_EXPERTS, ntp), 1).astype(F32) * MOE_TILE
        te = jnp.sum((jnp.broadcast_to(cumend[:, 0:1], (N_EXPERTS, ntp)) <= tstart).astype(I32), axis=0, keepdims=True)
        texp_ref[...] = jnp.minimum(te, N_EXPERTS - 1)
        nvalid_ref[...] = (cumend[N_EXPERTS - 1:N_EXPERTS, :] * (1.0 / MOE_TILE)).astype(I32)
        ecnt_ref[...] = _lanes_from_sublanes(tot).astype(I32)
        eoff_ref[...] = _lanes_from_sublanes(off).astype(I32)

    @pl.when(p == 1)
    def _():
        rr = lax.broadcasted_iota(I32, (tl, tl), 0)
        cc = lax.broadcasted_iota(I32, (tl, tl), 1)
        upper = (rr < cc).astype(BF16)
        before = jnp.dot(onehot.astype(BF16), upper, preferred_element_type=F32)
        loff = jnp.dot(lower, tile_run, precision=lax.Precision.HIGHEST, preferred_element_type=F32)
        base = before + loff[:, 0:1]
        sub = lax.broadcasted_iota(I32, (SUBLANES, tl), 0)
        out = jnp.zeros((SUBLANES, tl), F32)
        for k in range(TOP_K):
            pk = jnp.sum(jnp.where(ohs[k], base, 0.0), axis=0, keepdims=True)
            out = jnp.where(sub == k, pk, out)
        lp = out[0:TOP_K, :].astype(I32)
        lpb = (lp >> 3) * SLAB + (lp & (SUBLANES - 1))
        lpb_ref[0] = jnp.concatenate([lpb[k:k + 1, :] for k in range(TOP_K)], axis=1)
        inv = 1.0 / SUBLANES
        gstart_ref[0] = (_lanes_from_sublanes(off_sc[...] + carry_sc[...]) * inv).astype(I32)
        nslab_ref[0] = (_lanes_from_sublanes(tile_run) * inv).astype(I32)
        lstart_ref[0] = (_lanes_from_sublanes(loff) * inv).astype(I32)
        carry_sc[...] += tile_run


def _positions(topi, ntp):
    n = topi.shape[1]
    tl = TOKEN_TILE
    const = lambda shape: pl.BlockSpec(shape, lambda p, j: (0, 0))
    per_tile = pl.BlockSpec((1, 1, LANES), lambda p, j: (j * p, 0, 0))
    tab = jax.ShapeDtypeStruct((n // tl, 1, LANES), I32)
    return pl.pallas_call(
        functools.partial(_positions_body, ntp=ntp),
        grid=(2, n // tl),
        in_specs=[pl.BlockSpec((TOP_K, tl), lambda p, j: (0, j))],
        out_specs=[pl.BlockSpec((1, 1, TOP_K * tl), lambda p, j: (j * p, 0, 0)), per_tile, per_tile, per_tile,
                   const((1, ntp)), const((1, LANES)), const((1, LANES)), const((1, LANES))],
        out_shape=[jax.ShapeDtypeStruct((n // tl, 1, TOP_K * tl), I32), tab, tab, tab,
                   jax.ShapeDtypeStruct((1, ntp), I32),
                   jax.ShapeDtypeStruct((1, LANES), I32), jax.ShapeDtypeStruct((1, LANES), I32),
                   jax.ShapeDtypeStruct((1, LANES), I32)],
        scratch_shapes=[pltpu.VMEM((N_EXPERTS, LANES), F32)] * 3,
        compiler_params=pltpu.CompilerParams(dimension_semantics=("arbitrary", "arbitrary")),
        name="positions",
    )(topi)


STAGE_ROWS = TOKEN_TILE * TOP_K + N_EXPERTS * SUBLANES
STAGE_FLAT = STAGE_ROWS * SUBLANES


def _load_row(ref, flat_start):
    return ref[pl.ds(flat_start, SUBLANES, stride=SUBLANES), :]


def _store_row(ref, flat_start, v):
    ref[pl.ds(flat_start, SUBLANES, stride=SUBLANES), :] = v


CHUNK_SLABS = 4


def _copy_run(src_ref, src0, dst_ref, dst0, nslab, sem):
    big = CHUNK_SLABS * SLAB
    nbig = nslab >> 2
    nsmall = nslab & (CHUNK_SLABS - 1)

    def start_big(s, c):
        pltpu.make_async_copy(src_ref.at[pl.ds(src0 + s * big, big)], dst_ref.at[pl.ds(dst0 + s * big, big)], sem).start()
        return c

    def start_small(s, c):
        o = nbig * big + s * SLAB
        pltpu.make_async_copy(src_ref.at[pl.ds(src0 + o, SLAB)], dst_ref.at[pl.ds(dst0 + o, SLAB)], sem).start()
        return c

    lax.fori_loop(0, nbig, start_big, 0)
    lax.fori_loop(0, nsmall, start_small, 0)


def _wait_runs(src_ref, dst_ref, nbig, nsmall, sem):
    big = CHUNK_SLABS * SLAB

    def wait_big(s, c):
        pltpu.make_async_copy(src_ref.at[pl.ds(0, big)], dst_ref.at[pl.ds(0, big)], sem).wait()
        return c

    def wait_small(s, c):
        pltpu.make_async_copy(src_ref.at[pl.ds(0, SLAB)], dst_ref.at[pl.ds(0, SLAB)], sem).wait()
        return c

    lax.fori_loop(0, nbig, wait_big, 0)
    lax.fori_loop(0, nsmall, wait_small, 0)


def _chunk_counts(nslab_ref, j):
    def add(e, c):
        n = nslab_ref[j, 0, e]
        return c[0] + (n >> 2), c[1] + (n & (CHUNK_SLABS - 1))

    return lax.fori_loop(0, N_EXPERTS, add, (jnp.int32(0), jnp.int32(0)))


def _dispatch_body(ecnt_ref, eoff_ref, lpb0_ref, lpb1_ref, gstart_ref, nslab_ref, lstart_ref, h0_ref, h1_ref, xs_ref,
                   stg0_ref, stg1_ref, zero_ref, issued_ref, sem, zsem):
    g = pl.program_id(0)
    ng = pl.num_programs(0)
    halves = ((lpb0_ref, h0_ref, stg0_ref), (lpb1_ref, h1_ref, stg1_ref))

    def drain(j):
        _wait_runs(halves[j][2], xs_ref, issued_ref[2 * j], issued_ref[2 * j + 1], sem.at[j])

    for j in range(2):
        lpb_ref, h_ref, stg_ref = halves[j]

        @pl.when(g >= 1)
        def _():
            drain(j)

        def zero_last(e, c):
            last = jnp.maximum(lstart_ref[j, 0, e] + nslab_ref[j, 0, e] - 1, 0)
            stg_ref[pl.ds(last * SLAB, SLAB), :] = jnp.zeros((SLAB, LANES), F32)
            return c

        lax.fori_loop(0, N_EXPERTS, zero_last, 0)

        def move(i, c):
            for u in range(SUBLANES):
                v = _load_row(h_ref, i * SLAB + u)
                for k in range(TOP_K):
                    _store_row(stg_ref, lpb_ref[0, 0, k * TOKEN_TILE + i * SUBLANES + u], v)
            return c

        lax.fori_loop(0, TOKEN_TILE // SUBLANES, move, 0)

        def send_run(e, c):
            _copy_run(stg_ref, lstart_ref[j, 0, e] * SLAB, xs_ref, gstart_ref[j, 0, e] * SLAB,
                      nslab_ref[j, 0, e], sem.at[j])
            return c

        lax.fori_loop(0, N_EXPERTS, send_run, 0)
        nbig, nsmall = _chunk_counts(nslab_ref, j)
        issued_ref[2 * j] = nbig
        issued_ref[2 * j + 1] = nsmall

    @pl.when(g == ng - 1)
    def _():
        zero_ref[...] = jnp.zeros_like(zero_ref)

        def zero_copy(dst_slab):
            return pltpu.make_async_copy(zero_ref, xs_ref.at[pl.ds(dst_slab * SLAB, SLAB)], zsem)

        def pad_segment(e, c):
            rows = ecnt_ref[e]
            first = (eoff_ref[e] + rows) >> 3
            npad = lax.rem(MOE_TILE - lax.rem(rows, MOE_TILE), MOE_TILE) >> 3
            lax.fori_loop(0, npad, lambda s, cc: (zero_copy(first + s).start(), cc)[1], 0)
            lax.fori_loop(0, npad, lambda s, cc: (zero_copy(first + s).wait(), cc)[1], 0)
            return c

        lax.fori_loop(0, N_EXPERTS, pad_segment, 0)
        drain(0)
        drain(1)


def _dispatch(h1_flat, lpb, gstart, nslab, lstart, ecnt, eoff, rows_pad):
    n8 = h1_flat.shape[0]
    t = 2 * TOKEN_TILE
    smem = lambda shape, imap: pl.BlockSpec(shape, imap, memory_space=pltpu.SMEM)
    per_tile = smem((2, 1, LANES), lambda i, c, o: (i, 0, 0))
    vec = lambda j: smem((1, 1, TOP_K * TOKEN_TILE), lambda i, c, o: (2 * i + j, 0, 0))
    rows = lambda j: pl.BlockSpec((TOKEN_TILE * SUBLANES, LANES), lambda i, c, o: (2 * i + j, 0))
    grid_spec = pltpu.PrefetchScalarGridSpec(
        num_scalar_prefetch=2,
        grid=(n8 // (t * SUBLANES),),
        in_specs=[vec(0), vec(1), per_tile, per_tile, per_tile, rows(0), rows(1)],
        out_specs=pl.BlockSpec(memory_space=pl.ANY),
        scratch_shapes=[pltpu.VMEM((STAGE_FLAT, LANES), F32), pltpu.VMEM((STAGE_FLAT, LANES), F32),
                        pltpu.VMEM((SLAB, LANES), F32),
                        pltpu.SMEM((4,), I32), pltpu.SemaphoreType.DMA((2,)), pltpu.SemaphoreType.DMA],
    )
    return pl.pallas_call(
        _dispatch_body,
        grid_spec=grid_spec,
        out_shape=jax.ShapeDtypeStruct((rows_pad * SUBLANES, LANES), F32),
        compiler_params=pltpu.CompilerParams(dimension_semantics=("arbitrary",), vmem_limit_bytes=VMEM_LIMIT),
        name="dispatch",
    )(ecnt, eoff, lpb, lpb, gstart, nslab, lstart, h1_flat, h1_flat)


def _ffn_body(texp_ref, nvalid_ref, x_ref, wgu_ref, wd_ref, bg_ref, bl_ref, bd_ref, y_ref, wg_sc, wl_sc, wd_sc):
    i = pl.program_id(0)
    valid = i < nvalid_ref[0]
    first_of_expert = jnp.logical_or(i == 0, texp_ref[i] != texp_ref[jnp.maximum(i - 1, 0)])

    @pl.when(jnp.logical_and(valid, first_of_expert))
    def _():
        blk = 2 * LANES
        r = lax.broadcasted_iota(I32, (blk, blk), 0)
        c = lax.broadcasted_iota(I32, (blk, blk), 1)
        sel = (r == jnp.where(c < LANES, 2 * c, 2 * (c - LANES) + 1)).astype(BF16)
        for b in range(wgu_ref.shape[2] // blk):
            wb = wgu_ref[0, :, blk * b:blk * (b + 1)].astype(BF16)
            y = jnp.dot(wb, sel, preferred_element_type=F32).astype(BF16)
            wg_sc[:, LANES * b:LANES * (b + 1)] = y[:, 0:LANES]
            wl_sc[:, LANES * b:LANES * (b + 1)] = y[:, LANES:blk]
        wd_sc[...] = wd_ref[0].astype(BF16)

    @pl.when(valid)
    def _():
        x = _from_tiles(x_ref).astype(BF16)
        hg = jnp.dot(x, wg_sc[...], preferred_element_type=F32) + bg_ref[0]
        hl = jnp.dot(x, wl_sc[...], preferred_element_type=F32) + bl_ref[0]
        g = jnp.minimum(hg, SWIGLU_LIMIT)
        lin = jnp.clip(hl, -SWIGLU_LIMIT, SWIGLU_LIMIT)
        act = g * (1.0 / (1.0 + jnp.exp(-SWIGLU_ALPHA * g))) * (lin + 1.0)
        _to_tiles(y_ref, jnp.dot(act.astype(BF16), wd_sc[...], preferred_element_type=F32) + bd_ref[0])


def _grouped_ffn(xs_tiles, texp, nvalid, wgu, wd, bg, bl, bd, ntiles):
    d = wgu.shape[1]
    tm = MOE_TILE
    f = wd.shape[1]
    xmap = lambda i, te, nv: (jnp.minimum(i, nv[0] - 1), 0, 0, 0)
    wmap = lambda i, te, nv: (te[i], 0, 0)
    rows_blk = pl.BlockSpec((tm // SUBLANES, d // LANES, SUBLANES, LANES), xmap)
    grid_spec = pltpu.PrefetchScalarGridSpec(
        num_scalar_prefetch=2,
        grid=(ntiles,),
        in_specs=[rows_blk,
                  pl.BlockSpec((1, d, 2 * f), wmap), pl.BlockSpec((1, f, d), wmap),
                  pl.BlockSpec((1, 1, f), wmap), pl.BlockSpec((1, 1, f), wmap), pl.BlockSpec((1, 1, d), wmap)],
        out_specs=rows_blk,
        scratch_shapes=[pltpu.VMEM((d, f), BF16), pltpu.VMEM((d, f), BF16), pltpu.VMEM((f, d), BF16)],
    )
    return pl.pallas_call(
        _ffn_body,
        grid_spec=grid_spec,
        out_shape=jax.ShapeDtypeStruct(xs_tiles.shape, F32),
        compiler_params=pltpu.CompilerParams(dimension_semantics=("arbitrary",), vmem_limit_bytes=VMEM_LIMIT),
        name="ffn",
    )(texp, nvalid, xs_tiles, wgu, wd, bg, bl, bd)


def _combine_body(lpb0_ref, lpb1_ref, gate0_ref, gate1_ref, gstart_ref, nslab_ref, lstart_ref,
                  gstart2_ref, nslab2_ref, lstart2_ref, h_ref, l2g_ref, l2b_ref, ys_ref, o_ref,
                  stg0_ref, stg1_ref, moe_ref, sem):
    g = pl.program_id(0)
    ng = pl.num_programs(0)
    halves = ((lpb0_ref, gate0_ref, stg0_ref), (lpb1_ref, gate1_ref, stg1_ref))

    def fetch(gs_ref, ns_ref, ls_ref, jt, half):
        def per_expert(e, c):
            _copy_run(ys_ref, gs_ref[jt, 0, e] * SLAB, halves[half][2], ls_ref[jt, 0, e] * SLAB,
                      ns_ref[jt, 0, e], sem.at[half])
            return c

        lax.fori_loop(0, N_EXPERTS, per_expert, 0)

    def wait_tile(j):
        nbig, nsmall = _chunk_counts(nslab_ref, j)
        _wait_runs(ys_ref, halves[j][2], nbig, nsmall, sem.at[j])

    def gather(j):
        lpb_ref, gate_ref, stg_ref = halves[j]

        def body(i, c):
            for u in range(SUBLANES):
                tok = i * SUBLANES + u
                acc = gate_ref[0, 0, tok] * _load_row(stg_ref, lpb_ref[0, 0, tok])
                for k in range(1, TOP_K):
                    acc = acc + (gate_ref[0, 0, k * TOKEN_TILE + tok]
                                 * _load_row(stg_ref, lpb_ref[0, 0, k * TOKEN_TILE + tok]))
                _store_row(moe_ref, i * SLAB + u, acc)
            return c

        lax.fori_loop(0, TOKEN_TILE // SUBLANES, body, 0)

    def finish(j):
        nblk = TOKEN_TILE // SUBLANES
        moe = jnp.concatenate(
            [jnp.concatenate([moe_ref[pl.ds(i * SLAB + s * SUBLANES, SUBLANES), :] for i in range(nblk)], axis=0)
             for s in range(SUBLANES)], axis=1)
        h1 = _from_tiles(h_ref.at[pl.ds(j * nblk, nblk)])
        o_ref[pl.ds(j * TOKEN_TILE, TOKEN_TILE), :] = _layernorm(DEEPNORM_ALPHA * h1 + moe, l2g_ref[...], l2b_ref[...])

    @pl.when(g == 0)
    def _():
        fetch(gstart_ref, nslab_ref, lstart_ref, 0, 0)

    fetch(gstart_ref, nslab_ref, lstart_ref, 1, 1)
    wait_tile(0)
    gather(0)

    @pl.when(g + 1 < ng)
    def _():
        fetch(gstart2_ref, nslab2_ref, lstart2_ref, 0, 0)

    finish(0)
    wait_tile(1)
    gather(1)
    finish(1)


def _combine(lpb, gates, gstart, nslab, lstart, h1_tiles, ys_flat, w):
    d = h1_tiles.shape[1] * LANES
    n = h1_tiles.shape[0] * SUBLANES
    t = 2 * TOKEN_TILE
    ng = n // t
    full = lambda shape: pl.BlockSpec(shape, lambda i: (0,) * len(shape))
    smem = lambda shape, imap: pl.BlockSpec(shape, imap, memory_space=pltpu.SMEM)
    vec = lambda j: smem((1, 1, TOP_K * TOKEN_TILE), lambda i: (2 * i + j, 0, 0))
    this_step = smem((2, 1, LANES), lambda i: (i, 0, 0))
    next_step = smem((2, 1, LANES), lambda i: (jnp.minimum(i + 1, ng - 1), 0, 0))
    return pl.pallas_call(
        _combine_body,
        grid=(ng,),
        in_specs=[vec(0), vec(1), vec(0), vec(1), this_step, this_step, this_step, next_step, next_step, next_step,
                  pl.BlockSpec((t // SUBLANES, d // LANES, SUBLANES, LANES), lambda i: (i, 0, 0, 0)),
                  full((1, d)), full((1, d)),
                  pl.BlockSpec(memory_space=pl.ANY)],
        out_specs=pl.BlockSpec((t, d), lambda i: (i, 0)),
        out_shape=jax.ShapeDtypeStruct((n, d), F32),
        scratch_shapes=[pltpu.VMEM((STAGE_FLAT, LANES), F32), pltpu.VMEM((STAGE_FLAT, LANES), F32),
                        pltpu.VMEM((TOKEN_TILE * SUBLANES, LANES), F32), pltpu.SemaphoreType.DMA((2,))],
        compiler_params=pltpu.CompilerParams(dimension_semantics=("arbitrary",), vmem_limit_bytes=VMEM_LIMIT),
        name="combine",
    )(lpb, lpb, gates, gates, gstart, nslab, lstart, gstart, nslab, lstart, h1_tiles, w["ln2_g"], w["ln2_b"],
      ys_flat)


def _rope_tables(pos_1d, row, col):
    inv = ROPE_THETA ** (-jnp.arange(0, ROPE_DIM, 2, dtype=F32) / ROPE_DIM)

    def cs(p):
        ang = p.astype(F32)[:, None] * inv[None, :]
        return jnp.cos(ang), jnp.sin(ang)

    c1, s1 = cs(pos_1d)
    cr, sr = cs(row)
    cc, sc = cs(col)
    return (jnp.concatenate([c1, c1, c1, c1], axis=1), jnp.concatenate([-s1, s1, -s1, s1], axis=1),
            jnp.concatenate([cr, cr, cc, cc], axis=1), jnp.concatenate([-sr, sr, -sc, sc], axis=1))


def _prep_weights(ln_emb_g, ln_emb_b, w_in, g_q_a, w_q_b, g_kv_a, w_kv_b, g_q_gqa, g_k_gqa, g_o_mla, g_o_gqa, w_o,
                  ln1_g, ln1_b, w_router, b_router, ln2_g, ln2_b):
    r2 = lambda v: v.reshape(1, -1).astype(F32)
    o = np.cumsum([0, Q_LORA, KV_LORA, ROPE_DIM, GQA_HEADS * GQA_DIM, GQA_KV_HEADS * GQA_DIM, GQA_KV_HEADS * GQA_DIM])
    wi = w_in[0]
    seg = [wi[:, o[i]:o[i + 1]] for i in range(6)]
    kpe2 = jnp.concatenate([seg[2], seg[2]], axis=1)
    w_in_p = jnp.concatenate([seg[0], seg[1], kpe2, _swap_halves64(kpe2), seg[3], _swap_halves64(seg[3]),
                              seg[4], _swap_halves64(seg[4]), seg[5]], axis=1).astype(BF16)
    wq = w_q_b[0].reshape(Q_LORA, MLA_HEADS, QK_DIM)
    wq_rope = wq[:, :, NOPE_DIM:].reshape(Q_LORA, -1)
    w_qb = jnp.concatenate([wq[:, :, :NOPE_DIM].reshape(Q_LORA, -1), wq_rope, _swap_halves64(wq_rope)],
                           axis=1).astype(BF16)
    wk = w_kv_b[0].reshape(KV_LORA, MLA_HEADS, NOPE_DIM + V_DIM)
    w_kvb = jnp.concatenate([wk[:, :, :NOPE_DIM].reshape(KV_LORA, -1), wk[:, :, NOPE_DIM:].reshape(KV_LORA, -1)],
                            axis=1).astype(BF16)
    wr = jnp.pad(w_router[0].astype(F32), ((0, 0), (0, LANES - N_EXPERTS)))
    wr_hi = wr.astype(BF16)
    w_router_p = jnp.concatenate([wr_hi, (wr - wr_hi.astype(F32)).astype(BF16)], axis=1)
    b_router_p = jnp.pad(b_router[0].astype(F32), (0, LANES - N_EXPERTS), constant_values=NEG_BIG).reshape(1, LANES)
    return dict(
        ln_emb_g=r2(ln_emb_g), ln_emb_b=r2(ln_emb_b), w_in=w_in_p, g_q_a=r2(g_q_a[0]), w_qb=w_qb,
        g_kv_a=r2(g_kv_a[0]), w_kvb=w_kvb, g_q_gqa=r2(g_q_gqa[0]), g_k_gqa=r2(g_k_gqa[0]),
        g_q_gqa_sw=_swap_halves64(r2(g_q_gqa[0])), g_k_gqa_sw=_swap_halves64(r2(g_k_gqa[0])),
        g_o_mla=r2(g_o_mla[0]), g_o_gqa=r2(g_o_gqa[0]), w_o=w_o[0].astype(BF16), ln1_g=r2(ln1_g[0]),
        ln1_b=r2(ln1_b[0]), w_router=w_router_p, b_router=b_router_p, ln2_g=r2(ln2_g[0]), ln2_b=r2(ln2_b[0]))


def kernel(x, meta_tokens, ln_emb_g, ln_emb_b, w_in, g_q_a, w_q_b, g_kv_a, w_kv_b, g_q_gqa, g_k_gqa, g_o_mla, g_o_gqa,
           w_o, ln1_g, ln1_b, w_router, b_router, w_gate_up, b_gate_up, w_down, b_down, ln2_g, ln2_b):
    b, s, d = x.shape
    n = b * s
    w = _prep_weights(ln_emb_g, ln_emb_b, w_in, g_q_a, w_q_b, g_kv_a, w_kv_b, g_q_gqa, g_k_gqa, g_o_mla, g_o_gqa,
                      w_o, ln1_g, ln1_b, w_router, b_router, ln2_g, ln2_b)

    tok = jnp.arange(s, dtype=I32)
    tabs_real = _rope_tables(tok + N_META, tok // GRID_W, tok % GRID_W)
    mt = jnp.arange(META_PAD, dtype=I32)
    tabs_meta = _rope_tables(mt, jnp.full((META_PAD,), -1, I32), mt)

    qm, km, vm, qg, kg, vg = _project(x, tabs_real, w, ROW_TILE)
    meta = jnp.pad(meta_tokens.astype(x.dtype), ((0, META_PAD - N_META), (0, 0))).reshape(1, META_PAD, d)
    _, km_m, vm_m, _, kg_m, vg_m = _project(meta, tabs_meta, w, META_PAD)

    o_mla = _mla_attention(qm, km, vm, km_m, vm_m)
    o_gqa = _gqa_attention(qg, kg, vg, kg_m, vg_m)

    h1_tiles, topi, gates = _merge(o_mla, o_gqa, x.reshape(n, d), w)

    run_rows = n * TOP_K + (n // TOKEN_TILE) * N_EXPERTS * (SUBLANES - 1)
    ntiles = -(-run_rows // MOE_TILE) + N_EXPERTS
    ntp = -(-ntiles // LANES) * LANES
    rows_pad = ntiles * MOE_TILE
    lpb, gstart, nslab, lstart, texp, nvalid, ecnt, eoff = _positions(topi, ntp)
    flat = lambda a: a.reshape(-1, LANES)
    xs_flat = _dispatch(flat(h1_tiles), lpb, gstart, nslab, lstart, ecnt.reshape(-1), eoff.reshape(-1), rows_pad)

    bgu = b_gate_up[0].astype(F32)
    tiles = lambda a: a.reshape(-1, d // LANES, SUBLANES, LANES)
    ys_tiles = _grouped_ffn(tiles(xs_flat), texp.reshape(-1), nvalid.reshape(-1), w_gate_up[0], w_down[0],
                            bgu[:, 0::2].reshape(N_EXPERTS, 1, D_FF), bgu[:, 1::2].reshape(N_EXPERTS, 1, D_FF),
                            b_down[0].astype(F32).reshape(N_EXPERTS, 1, d), ntiles)

    out = _combine(lpb, gates, gstart, nslab, lstart, h1_tiles, flat(ys_tiles), w)
    return out.reshape(b, s, d)
```

```python
import functools

import jax
import jax.numpy as jnp
import numpy as np
from jax import lax
from jax.experimental import pallas as pl
from jax.experimental.pallas import tpu as pltpu

D_MODEL = 1024
N_META = 16
GRID_W = 64
ROPE_THETA = 10000.0
MLA_HEADS = 4
Q_LORA = 256
KV_LORA = 128
NOPE_DIM = 128
ROPE_DIM = 64
V_DIM = 128
QK_DIM = NOPE_DIM + ROPE_DIM
GQA_HEADS = 4
GQA_KV_HEADS = 2
GQA_DIM = 128
N_EXPERTS = 32
TOP_K = 4
D_FF = D_MODEL
SWIGLU_LIMIT = 7.0
SWIGLU_ALPHA = 1.702
RMS_EPS = 1e-6
LN_EPS = 1e-5
DEPTH = 1
DEEPNORM_ALPHA = (2.0 * DEPTH) ** 0.25

LANES = 128
SUBLANES = 8
SLAB = SUBLANES * SUBLANES
META_PAD = 128
MLA_K = 2 * LANES
NEG_BIG = -1e30
LOG2E = 1.4426950408889634
V_EXT = 2 * LANES

ROW_TILE = 512
Q_TILE_MLA = 1024
Q_TILE_GQA = 512
Q_SUB_MLA = 128
Q_SUB_GQA = 512
TOKEN_TILE = 512
MOE_TILE = 512
VMEM_LIMIT = 56 * 1024 * 1024

F32 = jnp.float32
BF16 = jnp.bfloat16
I32 = jnp.int32


def _layernorm(x, g, b):
    mu = jnp.mean(x, axis=-1, keepdims=True)
    xc = x - mu
    var = jnp.mean(xc * xc, axis=-1, keepdims=True)
    return xc * lax.rsqrt(var + LN_EPS) * g + b


def _rmsnorm(x, g):
    return x * lax.rsqrt(jnp.mean(x * x, axis=-1, keepdims=True) + RMS_EPS) * g


def _swap_halves64(v):
    shp = v.shape
    return v.reshape(shp[:-1] + (shp[-1] // 64, 2, 32))[..., ::-1, :].reshape(shp)


PROJ_SUB = 256


def _proj_chain(x, lng, lnb, win_ref, gqa, wqb_ref, gkva, wkvb_ref, gqg, gqgs, gkg, gkgs, c1, s1, ca, sa, put):
    r = x.shape[0]
    h0 = _layernorm(x, lng, lnb)
    z = jnp.dot(h0.astype(BF16), win_ref[...], preferred_element_type=F32)
    q_a = z[:, 0:256]
    kv_a = z[:, 256:384]
    kpe2 = z[:, 384:512]
    kpe2s = z[:, 512:640]
    q_g = z[:, 640:1152]
    q_gs = z[:, 1152:1664]
    k_g = z[:, 1664:1920]
    k_gs = z[:, 1920:2176]
    v_g = z[:, 2176:2432]

    q = jnp.dot(_rmsnorm(q_a, gqa).astype(BF16), wqb_ref[...], preferred_element_type=F32)
    kv = jnp.dot(_rmsnorm(kv_a, gkva).astype(BF16), wkvb_ref[...], preferred_element_type=F32)
    krot = kpe2 * c1 + kpe2s * s1
    lane = lax.broadcasted_iota(I32, krot.shape, 1)
    scale_a = QK_DIM ** -0.5 * LOG2E
    ones = jnp.ones((r, LANES), F32)
    for c in range(MLA_HEADS // 2):
        lo = LANES * c
        qr = q[:, 512 + lo:512 + lo + LANES] * c1 + q[:, 768 + lo:768 + lo + LANES] * s1
        for hh in range(2):
            h = 2 * c + hh
            slot = jnp.where((lane // 64) == hh, qr, 0.0)
            put("qm", h, (jnp.concatenate([q[:, LANES * h:LANES * (h + 1)], slot], axis=1) * scale_a).astype(BF16))
    for h in range(MLA_HEADS):
        put("km", h, jnp.concatenate([kv[:, LANES * h:LANES * (h + 1)], krot], axis=1).T.astype(BF16))
        put("vm", h, jnp.concatenate([kv[:, 512 + LANES * h:512 + LANES * (h + 1)], ones], axis=1).astype(BF16))

    scale_b = GQA_DIM ** -0.5 * LOG2E
    cq, sq = ca * gqg, sa * gqgs
    ck, sk = ca * gkg, sa * gkgs

    def norm_rot(xh, xs, cg, sg, scale):
        inv = lax.rsqrt(jnp.mean(xh * xh, axis=-1, keepdims=True) + RMS_EPS)
        return (xh * cg + xs * sg) * (inv * scale)

    for h in range(GQA_HEADS):
        sl = slice(LANES * h, LANES * (h + 1))
        put("qg", h, norm_rot(q_g[:, sl], q_gs[:, sl], cq, sq, scale_b).astype(BF16))
    for j in range(GQA_KV_HEADS):
        sl = slice(LANES * j, LANES * (j + 1))
        put("kg", j, norm_rot(k_g[:, sl], k_gs[:, sl], ck, sk, 1.0).T.astype(BF16))
        put("vg", j, jnp.concatenate([v_g[:, sl], ones], axis=1).astype(BF16))


def _proj_body(x_ref, lng_ref, lnb_ref, win_ref, gqa_ref, wqb_ref, gkva_ref, wkvb_ref, gqg_ref, gqgs_ref, gkg_ref,
               gkgs_ref, c1_ref, s1_ref, ca_ref, sa_ref,
               qm_ref, km_ref, vm_ref, qg_ref, kg_ref, vg_ref):
    outs = dict(qm=qm_ref, km=km_ref, vm=vm_ref, qg=qg_ref, kg=kg_ref, vg=vg_ref)
    tile = x_ref.shape[1]
    sub = min(PROJ_SUB, tile)
    for i in range(tile // sub):
        rows = pl.ds(i * sub, sub)

        def put(name, h, val):
            if name in ("km", "kg"):
                outs[name][0, h, :, rows] = val
            else:
                outs[name][0, h, rows, :] = val

        _proj_chain(x_ref[0, rows, :], lng_ref[...], lnb_ref[...], win_ref, gqa_ref[...], wqb_ref, gkva_ref[...],
                    wkvb_ref, gqg_ref[...], gqgs_ref[...], gkg_ref[...], gkgs_ref[...],
                    c1_ref[rows, :], s1_ref[rows, :], ca_ref[rows, :], sa_ref[rows, :], put)


def _project(x3, tabs, w, tile):
    b, s, d = x3.shape
    nst = s // tile
    full = lambda shape: pl.BlockSpec(shape, lambda bi, si: (0,) * len(shape))
    tab = pl.BlockSpec((tile, LANES), lambda bi, si: (si, 0))
    hm = lambda nh, dd: pl.BlockSpec((1, nh, tile, dd), lambda bi, si: (bi, 0, si, 0))
    hmt = lambda nh, dd: pl.BlockSpec((1, nh, dd, tile), lambda bi, si: (bi, 0, 0, si))
    out_shape = [
        jax.ShapeDtypeStruct((b, MLA_HEADS, s, MLA_K), BF16),
        jax.ShapeDtypeStruct((b, MLA_HEADS, MLA_K, s), BF16),
        jax.ShapeDtypeStruct((b, MLA_HEADS, s, V_EXT), BF16),
        jax.ShapeDtypeStruct((b, GQA_HEADS, s, GQA_DIM), BF16),
        jax.ShapeDtypeStruct((b, GQA_KV_HEADS, GQA_DIM, s), BF16),
        jax.ShapeDtypeStruct((b, GQA_KV_HEADS, s, V_EXT), BF16),
    ]
    return pl.pallas_call(
        _proj_body,
        grid=(b, nst),
        in_specs=[
            pl.BlockSpec((1, tile, d), lambda bi, si: (bi, si, 0)),
            full((1, d)), full((1, d)),
            full(w["w_in"].shape), full((1, Q_LORA)), full(w["w_qb"].shape),
            full((1, KV_LORA)), full(w["w_kvb"].shape),
            full((1, GQA_DIM)), full((1, GQA_DIM)), full((1, GQA_DIM)), full((1, GQA_DIM)),
            tab, tab, tab, tab,
        ],
        out_specs=[hm(MLA_HEADS, MLA_K), hmt(MLA_HEADS, MLA_K), hm(MLA_HEADS, V_EXT),
                   hm(GQA_HEADS, GQA_DIM), hmt(GQA_KV_HEADS, GQA_DIM), hm(GQA_KV_HEADS, V_EXT)],
        out_shape=out_shape,
        compiler_params=pltpu.CompilerParams(dimension_semantics=("parallel", "parallel"),
                                             vmem_limit_bytes=VMEM_LIMIT),
        name="proj",
    )(x3, w["ln_emb_g"], w["ln_emb_b"], w["w_in"], w["g_q_a"], w["w_qb"], w["g_kv_a"], w["w_kvb"],
      w["g_q_gqa"], w["g_q_gqa_sw"], w["g_k_gqa"], w["g_k_gqa_sw"], *tabs)


def _softmax_pv(q, kt, v, kmt, vm):
    s = jnp.dot(q, kt, preferred_element_type=F32)
    sm = jnp.dot(q, kmt, preferred_element_type=F32)
    col = lax.broadcasted_iota(I32, sm.shape, 1)
    sm = jnp.where(col < N_META, sm, NEG_BIG)
    m = jnp.maximum(jnp.max(s, axis=1, keepdims=True), jnp.max(sm, axis=1, keepdims=True))
    p = jnp.exp2(s - m).astype(BF16)
    pm = jnp.exp2(sm - m).astype(BF16)
    acc = jnp.dot(p, v, preferred_element_type=F32) + jnp.dot(pm, vm, preferred_element_type=F32)
    return acc[:, 0:V_DIM] / acc[:, V_DIM:V_EXT]


def _mla_attn_body(q_ref, k_ref, v_ref, km_ref, vm_ref, o_ref):
    for i in range(q_ref.shape[2] // Q_SUB_MLA):
        rows = pl.ds(i * Q_SUB_MLA, Q_SUB_MLA)
        o = _softmax_pv(q_ref[0, 0, rows, :], k_ref[0, 0], v_ref[0, 0], km_ref[0, 0], vm_ref[0, 0])
        o_ref[rows, :] = o.astype(o_ref.dtype)


def _gqa_attn_body(q_ref, k_ref, v_ref, km_ref, vm_ref, o_ref):
    for g in range(2):
        for i in range(q_ref.shape[2] // Q_SUB_GQA):
            rows = pl.ds(i * Q_SUB_GQA, Q_SUB_GQA)
            o = _softmax_pv(q_ref[0, g, rows, :], k_ref[0, 0], v_ref[0, 0], km_ref[0, 0], vm_ref[0, 0])
            o_ref[rows, GQA_DIM * g:GQA_DIM * (g + 1)] = o.astype(o_ref.dtype)


def _mla_attention(qm, kmt, vm, kmeta_t, vmeta):
    b, h, s, dk = qm.shape
    tq = Q_TILE_MLA
    nq = s // tq
    return pl.pallas_call(
        _mla_attn_body,
        grid=(b, h, nq),
        in_specs=[
            pl.BlockSpec((1, 1, tq, dk), lambda bi, hi, qi: (bi, hi, qi, 0)),
            pl.BlockSpec((1, 1, dk, s), lambda bi, hi, qi: (bi, hi, 0, 0)),
            pl.BlockSpec((1, 1, s, V_EXT), lambda bi, hi, qi: (bi, hi, 0, 0)),
            pl.BlockSpec((1, 1, dk, META_PAD), lambda bi, hi, qi: (0, hi, 0, 0)),
            pl.BlockSpec((1, 1, META_PAD, V_EXT), lambda bi, hi, qi: (0, hi, 0, 0)),
        ],
        out_specs=pl.BlockSpec((tq, V_DIM), lambda bi, hi, qi: (bi * nq + qi, hi)),
        out_shape=jax.ShapeDtypeStruct((b * s, h * V_DIM), BF16),
        compiler_params=pltpu.CompilerParams(dimension_semantics=("parallel", "parallel", "parallel"),
                                             vmem_limit_bytes=VMEM_LIMIT),
        name="mla_attn",
    )(qm, kmt, vm, kmeta_t, vmeta)


def _gqa_attention(qg, kgt, vg, kmeta_t, vmeta):
    b, h, s, d = qg.shape
    hk = kgt.shape[1]
    tq = Q_TILE_GQA
    nq = s // tq
    return pl.pallas_call(
        _gqa_attn_body,
        grid=(b, hk, nq),
        in_specs=[
            pl.BlockSpec((1, 2, tq, d), lambda bi, ji, qi: (bi, ji, qi, 0)),
            pl.BlockSpec((1, 1, d, s), lambda bi, ji, qi: (bi, ji, 0, 0)),
            pl.BlockSpec((1, 1, s, V_EXT), lambda bi, ji, qi: (bi, ji, 0, 0)),
            pl.BlockSpec((1, 1, d, META_PAD), lambda bi, ji, qi: (0, ji, 0, 0)),
            pl.BlockSpec((1, 1, META_PAD, V_EXT), lambda bi, ji, qi: (0, ji, 0, 0)),
        ],
        out_specs=pl.BlockSpec((tq, 2 * d), lambda bi, ji, qi: (bi * nq + qi, ji)),
        out_shape=jax.ShapeDtypeStruct((b * s, h * d), BF16),
        compiler_params=pltpu.CompilerParams(dimension_semantics=("parallel", "parallel", "parallel"),
                                             vmem_limit_bytes=VMEM_LIMIT),
        name="gqa_attn",
    )(qg, kgt, vg, kmeta_t, vmeta)


def _to_tiles(ref, x):
    r = x.shape[0]
    for s in range(x.shape[1] // LANES):
        ref[:, s] = x[:, LANES * s:LANES * (s + 1)].reshape(r // SUBLANES, SUBLANES, LANES)


def _from_tiles(ref):
    r = ref.shape[0] * SUBLANES
    return jnp.concatenate([ref[:, s].reshape(r, LANES) for s in range(ref.shape[1])], axis=1)


MERGE_SUB = 256


def _merge_body(om_ref, og_ref, x_ref, lng_ref, lnb_ref, gom_ref, gog_ref, wo_ref, l1g_ref, l1b_ref,
                wr_ref, br_ref, h1_ref, topi_ref, gate_ref):
    for i in range(x_ref.shape[0] // MERGE_SUB):
        rows = pl.ds(i * MERGE_SUB, MERGE_SUB)
        h1, ti, gt = _merge_chain(om_ref[rows, :], og_ref[rows, :], x_ref[rows, :], lng_ref[...], lnb_ref[...],
                                  gom_ref[...], gog_ref[...], wo_ref, l1g_ref[...], l1b_ref[...], wr_ref, br_ref[...])
        _to_tiles(h1_ref.at[pl.ds(i * (MERGE_SUB // SUBLANES), MERGE_SUB // SUBLANES)], h1)
        topi_ref[:, rows] = ti
        for k in range(TOP_K):
            gate_ref[0, :, pl.ds(k * TOKEN_TILE + i * MERGE_SUB, MERGE_SUB)] = gt[k:k + 1, :]


def _merge_chain(om, og, x, lng, lnb, gom, gog, wo_ref, l1g, l1b, wr_ref, br):
    t = x.shape[0]
    h0 = _layernorm(x, lng, lnb)
    nm = _rmsnorm(om.astype(F32), gom).astype(BF16)
    ng = _rmsnorm(og.astype(F32), gog).astype(BF16)
    half = nm.shape[1]
    mix = jnp.dot(nm, wo_ref[0:half, :], preferred_element_type=F32)
    mix = mix + jnp.dot(ng, wo_ref[half:2 * half, :], preferred_element_type=F32)
    h1 = _layernorm(DEEPNORM_ALPHA * h0 + mix, l1g, l1b)

    hi = h1.astype(BF16)
    lo = (h1 - hi.astype(F32)).astype(BF16)
    acc = jnp.dot(hi, wr_ref[...], preferred_element_type=F32) + jnp.dot(lo, wr_ref[...], preferred_element_type=F32)
    logits = acc[:, 0:LANES] + acc[:, LANES:2 * LANES] + br
    cur = logits.T[0:N_EXPERTS, :]
    eidx = lax.broadcasted_iota(I32, cur.shape, 0)
    vals, idxs = [], []
    for _ in range(TOP_K):
        m = jnp.max(cur, axis=0, keepdims=True)
        i = jnp.min(jnp.where(cur == m, eidx, N_EXPERTS), axis=0, keepdims=True)
        vals.append(m)
        idxs.append(i)
        cur = jnp.where(eidx == i, -jnp.inf, cur)
    ex = [jnp.exp(v - vals[0]) for v in vals]
    den = ex[0] + ex[1] + ex[2] + ex[3]
    sub = lax.broadcasted_iota(I32, (8, t), 0)
    ti = jnp.zeros((8, t), I32)
    gt = jnp.zeros((8, t), F32)
    for k in range(TOP_K):
        ti = jnp.where(sub == k, idxs[k], ti)
        gt = jnp.where(sub == k, ex[k] / den, gt)
    return h1, ti[0:TOP_K, :], gt[0:TOP_K, :]


def _merge(o_mla, o_gqa, x2, w):
    n, d = x2.shape
    tile = TOKEN_TILE
    half = o_mla.shape[1]
    full = lambda shape: pl.BlockSpec(shape, lambda i: (0,) * len(shape))
    row = lambda width: pl.BlockSpec((tile, width), lambda i: (i, 0))
    return pl.pallas_call(
        _merge_body,
        grid=(n // tile,),
        in_specs=[row(half), row(half), row(d), full((1, d)), full((1, d)), full((1, half)), full((1, half)),
                  full((d, d)), full((1, d)), full((1, d)), full((d, 2 * LANES)), full((1, LANES))],
        out_specs=[pl.BlockSpec((tile // SUBLANES, d // LANES, SUBLANES, LANES), lambda i: (i, 0, 0, 0)),
                   pl.BlockSpec((TOP_K, tile), lambda i: (0, i)),
                   pl.BlockSpec((1, 1, TOP_K * tile), lambda i: (i, 0, 0))],
        out_shape=[jax.ShapeDtypeStruct((n // SUBLANES, d // LANES, SUBLANES, LANES), F32),
                   jax.ShapeDtypeStruct((TOP_K, n), I32), jax.ShapeDtypeStruct((n // tile, 1, TOP_K * tile), F32)],
        compiler_params=pltpu.CompilerParams(dimension_semantics=("parallel",), vmem_limit_bytes=VMEM_LIMIT),
        name="merge",
    )(o_mla, o_gqa, x2, w["ln_emb_g"], w["ln_emb_b"], w["g_o_mla"], w["g_o_gqa"], w["w_o"],
      w["ln1_g"], w["ln1_b"], w["w_router"], w["b_router"])


def _lanes_from_sublanes(col):
    diag = lax.broadcasted_iota(I32, col.shape, 0) == lax.broadcasted_iota(I32, col.shape, 1)
    return jnp.sum(jnp.where(diag, col, 0.0), axis=0, keepdims=True)


def _positions_body(topi_ref, lpb_ref, gstart_ref, nslab_ref, lstart_ref, texp_ref, nvalid_ref, ecnt_ref, eoff_ref,
                    cnt_sc, carry_sc, off_sc, *, ntp):
    p = pl.program_id(0)
    j = pl.program_id(1)
    tl = topi_ref.shape[1]
    topi = topi_ref[...]
    eidx = lax.broadcasted_iota(I32, (N_EXPERTS, tl), 0)
    ohs = [eidx == topi[k:k + 1, :] for k in range(TOP_K)]
    onehot = ohs[0].astype(F32) + ohs[1].astype(F32) + ohs[2].astype(F32) + ohs[3].astype(F32)
    cnt = jnp.sum(onehot, axis=1, keepdims=True)
    run = jnp.floor((cnt + (SUBLANES - 1)) * (1.0 / SUBLANES)) * SUBLANES
    tile_run = jnp.broadcast_to(run, (N_EXPERTS, LANES))
    r = lax.broadcasted_iota(I32, (N_EXPERTS, N_EXPERTS), 0)
    c = lax.broadcasted_iota(I32, (N_EXPERTS, N_EXPERTS), 1)
    lower = (c < r).astype(F32)

    @pl.when(jnp.logical_and(p == 0, j == 0))
    def _():
        cnt_sc[...] = jnp.zeros_like(cnt_sc)

    @pl.when(p == 0)
    def _():
        cnt_sc[...] += tile_run

    @pl.when(jnp.logical_and(p == 1, j == 0))
    def _():
        tot = cnt_sc[...]
        pc = jnp.floor((tot + (MOE_TILE - 1)) * (1.0 / MOE_TILE)) * MOE_TILE
        off = jnp.dot(lower, pc, precision=lax.Precision.HIGHEST, preferred_element_type=F32)
        off_sc[...] = off
        carry_sc[...] = jnp.zeros_like(carry_sc)
        cumend = off + pc
        tstart = lax.broadcasted_iota(I32, (N_EXPERTS, ntp), 1).astype(F32) * MOE_TILE
        te = jnp.sum((jnp.broadcast_to(cumend[:, 0:1], (N_EXPERTS, ntp)) <= tstart).astype(I32), axis=0, keepdims=True)
        texp_ref[...] = jnp.minimum(te, N_EXPERTS - 1)
        nvalid_ref[...] = (cumend[N_EXPERTS - 1:N_EXPERTS, :] * (1.0 / MOE_TILE)).astype(I32)
        ecnt_ref[...] = _lanes_from_sublanes(tot).astype(I32)
        eoff_ref[...] = _lanes_from_sublanes(off).astype(I32)

    @pl.when(p == 1)
    def _():
        rr = lax.broadcasted_iota(I32, (tl, tl), 0)
        cc = lax.broadcasted_iota(I32, (tl, tl), 1)
        upper = (rr < cc).astype(BF16)
        before = jnp.dot(onehot.astype(BF16), upper, preferred_element_type=F32)
        loff = jnp.dot(lower, tile_run, precision=lax.Precision.HIGHEST, preferred_element_type=F32)
        base = before + loff[:, 0:1]
        sub = lax.broadcasted_iota(I32, (SUBLANES, tl), 0)
        out = jnp.zeros((SUBLANES, tl), F32)
        for k in range(TOP_K):
            pk = jnp.sum(jnp.where(ohs[k], base, 0.0), axis=0, keepdims=True)
            out = jnp.where(sub == k, pk, out)
        lp = out[0:TOP_K, :].astype(I32)
        lpb = (lp >> 3) * SLAB + (lp & (SUBLANES - 1))
        lpb_ref[0] = jnp.concatenate([lpb[k:k + 1, :] for k in range(TOP_K)], axis=1)
        inv = 1.0 / SUBLANES
        gstart_ref[0] = (_lanes_from_sublanes(off_sc[...] + carry_sc[...]) * inv).astype(I32)
        nslab_ref[0] = (_lanes_from_sublanes(tile_run) * inv).astype(I32)
        lstart_ref[0] = (_lanes_from_sublanes(loff) * inv).astype(I32)
        carry_sc[...] += tile_run


def _positions(topi, ntp):
    n = topi.shape[1]
    tl = TOKEN_TILE
    const = lambda shape: pl.BlockSpec(shape, lambda p, j: (0, 0))
    per_tile = pl.BlockSpec((1, 1, LANES), lambda p, j: (j * p, 0, 0))
    tab = jax.ShapeDtypeStruct((n // tl, 1, LANES), I32)
    return pl.pallas_call(
        functools.partial(_positions_body, ntp=ntp),
        grid=(2, n // tl),
        in_specs=[pl.BlockSpec((TOP_K, tl), lambda p, j: (0, j))],
        out_specs=[pl.BlockSpec((1, 1, TOP_K * tl), lambda p, j: (j * p, 0, 0)), per_tile, per_tile, per_tile,
                   const((1, ntp)), const((1, LANES)), const((1, LANES)), const((1, LANES))],
        out_shape=[jax.ShapeDtypeStruct((n // tl, 1, TOP_K * tl), I32), tab, tab, tab,
                   jax.ShapeDtypeStruct((1, ntp), I32),
                   jax.ShapeDtypeStruct((1, LANES), I32), jax.ShapeDtypeStruct((1, LANES), I32),
                   jax.ShapeDtypeStruct((1, LANES), I32)],
        scratch_shapes=[pltpu.VMEM((N_EXPERTS, LANES), F32)] * 3,
        compiler_params=pltpu.CompilerParams(dimension_semantics=("arbitrary", "arbitrary")),
        name="positions",
    )(topi)


STAGE_ROWS = TOKEN_TILE * TOP_K + N_EXPERTS * SUBLANES
STAGE_FLAT = STAGE_ROWS * SUBLANES


def _load_row(ref, flat_start):
    return ref[pl.ds(flat_start, SUBLANES, stride=SUBLANES), :]


def _store_row(ref, flat_start, v):
    ref[pl.ds(flat_start, SUBLANES, stride=SUBLANES), :] = v


CHUNK_SLABS = 4


def _copy_run(src_ref, src0, dst_ref, dst0, nslab, sem):
    big = CHUNK_SLABS * SLAB
    nbig = nslab >> 2
    nsmall = nslab & (CHUNK_SLABS - 1)

    def start_big(s, c):
        pltpu.make_async_copy(src_ref.at[pl.ds(src0 + s * big, big)], dst_ref.at[pl.ds(dst0 + s * big, big)], sem).start()
        return c

    def start_small(s, c):
        o = nbig * big + s * SLAB
        pltpu.make_async_copy(src_ref.at[pl.ds(src0 + o, SLAB)], dst_ref.at[pl.ds(dst0 + o, SLAB)], sem).start()
        return c

    lax.fori_loop(0, nbig, start_big, 0)
    lax.fori_loop(0, nsmall, start_small, 0)


def _wait_runs(src_ref, dst_ref, nbig, nsmall, sem):
    big = CHUNK_SLABS * SLAB

    def wait_big(s, c):
        pltpu.make_async_copy(src_ref.at[pl.ds(0, big)], dst_ref.at[pl.ds(0, big)], sem).wait()
        return c

    def wait_small(s, c):
        pltpu.make_async_copy(src_ref.at[pl.ds(0, SLAB)], dst_ref.at[pl.ds(0, SLAB)], sem).wait()
        return c

    lax.fori_loop(0, nbig, wait_big, 0)
    lax.fori_loop(0, nsmall, wait_small, 0)


def _chunk_counts(nslab_ref, j):
    def add(e, c):
        n = nslab_ref[j, 0, e]
        return c[0] + (n >> 2), c[1] + (n & (CHUNK_SLABS - 1))

    return lax.fori_loop(0, N_EXPERTS, add, (jnp.int32(0), jnp.int32(0)))


def _dispatch_body(ecnt_ref, eoff_ref, lpb0_ref, lpb1_ref, gstart_ref, nslab_ref, lstart_ref, h0_ref, h1_ref, xs_ref,
                   stg0_ref, stg1_ref, zero_ref, issued_ref, sem, zsem):
    g = pl.program_id(0)
    ng = pl.num_programs(0)
    halves = ((lpb0_ref, h0_ref, stg0_ref), (lpb1_ref, h1_ref, stg1_ref))

    def drain(j):
        _wait_runs(halves[j][2], xs_ref, issued_ref[2 * j], issued_ref[2 * j + 1], sem.at[j])

    for j in range(2):
        lpb_ref, h_ref, stg_ref = halves[j]

        @pl.when(g >= 1)
        def _():
            drain(j)

        def zero_last(e, c):
            last = jnp.maximum(lstart_ref[j, 0, e] + nslab_ref[j, 0, e] - 1, 0)
            stg_ref[pl.ds(last * SLAB, SLAB), :] = jnp.zeros((SLAB, LANES), F32)
            return c

        lax.fori_loop(0, N_EXPERTS, zero_last, 0)

        def move(i, c):
            for u in range(SUBLANES):
                v = _load_row(h_ref, i * SLAB + u)
                for k in range(TOP_K):
                    _store_row(stg_ref, lpb_ref[0, 0, k * TOKEN_TILE + i * SUBLANES + u], v)
            return c

        lax.fori_loop(0, TOKEN_TILE // SUBLANES, move, 0)

        def send_run(e, c):
            _copy_run(stg_ref, lstart_ref[j, 0, e] * SLAB, xs_ref, gstart_ref[j, 0, e] * SLAB,
                      nslab_ref[j, 0, e], sem.at[j])
            return c

        lax.fori_loop(0, N_EXPERTS, send_run, 0)
        nbig, nsmall = _chunk_counts(nslab_ref, j)
        issued_ref[2 * j] = nbig
        issued_ref[2 * j + 1] = nsmall

    @pl.when(g == ng - 1)
    def _():
        zero_ref[...] = jnp.zeros_like(zero_ref)

        def zero_copy(dst_slab):
            return pltpu.make_async_copy(zero_ref, xs_ref.at[pl.ds(dst_slab * SLAB, SLAB)], zsem)

        def pad_segment(e, c):
            rows = ecnt_ref[e]
            first = (eoff_ref[e] + rows) >> 3
            npad = lax.rem(MOE_TILE - lax.rem(rows, MOE_TILE), MOE_TILE) >> 3
            lax.fori_loop(0, npad, lambda s, cc: (zero_copy(first + s).start(), cc)[1], 0)
            lax.fori_loop(0, npad, lambda s, cc: (zero_copy(first + s).wait(), cc)[1], 0)
            return c

        lax.fori_loop(0, N_EXPERTS, pad_segment, 0)
        drain(0)
        drain(1)


def _dispatch(h1_flat, lpb, gstart, nslab, lstart, ecnt, eoff, rows_pad):
    n8 = h1_flat.shape[0]
    t = 2 * TOKEN_TILE
    smem = lambda shape, imap: pl.BlockSpec(shape, imap, memory_space=pltpu.SMEM)
    per_tile = smem((2, 1, LANES), lambda i, c, o: (i, 0, 0))
    vec = lambda j: smem((1, 1, TOP_K * TOKEN_TILE), lambda i, c, o: (2 * i + j, 0, 0))
    rows = lambda j: pl.BlockSpec((TOKEN_TILE * SUBLANES, LANES), lambda i, c, o: (2 * i + j, 0))
    grid_spec = pltpu.PrefetchScalarGridSpec(
        num_scalar_prefetch=2,
        grid=(n8 // (t * SUBLANES),),
        in_specs=[vec(0), vec(1), per_tile, per_tile, per_tile, rows(0), rows(1)],
        out_specs=pl.BlockSpec(memory_space=pl.ANY),
        scratch_shapes=[pltpu.VMEM((STAGE_FLAT, LANES), F32), pltpu.VMEM((STAGE_FLAT, LANES), F32),
                        pltpu.VMEM((SLAB, LANES), F32),
                        pltpu.SMEM((4,), I32), pltpu.SemaphoreType.DMA((2,)), pltpu.SemaphoreType.DMA],
    )
    return pl.pallas_call(
        _dispatch_body,
        grid_spec=grid_spec,
        out_shape=jax.ShapeDtypeStruct((rows_pad * SUBLANES, LANES), F32),
        compiler_params=pltpu.CompilerParams(dimension_semantics=("arbitrary",), vmem_limit_bytes=VMEM_LIMIT),
        name="dispatch",
    )(ecnt, eoff, lpb, lpb, gstart, nslab, lstart, h1_flat, h1_flat)


def _ffn_body(texp_ref, nvalid_ref, ecnt_ref, x_ref, wgu_hbm, wd_hbm, bg_ref, bl_ref, bd_ref, y_ref,
              wgu_buf, wd_buf, wg_sc, wl_sc, wd_sc, seg_ref, sem):
    i = pl.program_id(0)
    valid = i < nvalid_ref[0]
    e = texp_ref[i]
    first_of_expert = jnp.logical_or(i == 0, e != texp_ref[jnp.maximum(i - 1, 0)])

    def weight_copies(expert, slot):
        return (pltpu.make_async_copy(wgu_hbm.at[expert], wgu_buf.at[slot], sem.at[slot]),
                pltpu.make_async_copy(wd_hbm.at[expert], wd_buf.at[slot], sem.at[slot]))

    def prepare(slot):
        blk = 2 * LANES
        r = lax.broadcasted_iota(I32, (blk, blk), 0)
        c = lax.broadcasted_iota(I32, (blk, blk), 1)
        sel = (r == jnp.where(c < LANES, 2 * c, 2 * (c - LANES) + 1)).astype(BF16)
        for b in range(wgu_buf.shape[2] // blk):
            wb = wgu_buf[slot, :, blk * b:blk * (b + 1)].astype(BF16)
            y = jnp.dot(wb, sel, preferred_element_type=F32).astype(BF16)
            wg_sc[:, LANES * b:LANES * (b + 1)] = y[:, 0:LANES]
            wl_sc[:, LANES * b:LANES * (b + 1)] = y[:, LANES:blk]
        wd_sc[...] = wd_buf[slot].astype(BF16)

    @pl.when(jnp.logical_and(valid, first_of_expert))
    def _():
        @pl.when(i == 0)
        def _():
            seg_ref[0] = 0
            for cp in weight_copies(e, 0):
                cp.start()

        seg = seg_ref[0]
        seg_ref[0] = seg + 1
        nxt = lax.while_loop(lambda n: jnp.logical_and(n < N_EXPERTS, ecnt_ref[jnp.minimum(n, N_EXPERTS - 1)] == 0),
                             lambda n: n + 1, e + 1)
        for slot in range(2):
            @pl.when(lax.rem(seg, 2) == slot)
            def _():
                for cp in weight_copies(e, slot):
                    cp.wait()

                @pl.when(nxt < N_EXPERTS)
                def _():
                    for cp in weight_copies(nxt, 1 - slot):
                        cp.start()

                prepare(slot)

    @pl.when(valid)
    def _():
        x = _from_tiles(x_ref).astype(BF16)
        hg = jnp.dot(x, wg_sc[...], preferred_element_type=F32) + bg_ref[0]
        hl = jnp.dot(x, wl_sc[...], preferred_element_type=F32) + bl_ref[0]
        g = jnp.minimum(hg, SWIGLU_LIMIT)
        lin = jnp.clip(hl, -SWIGLU_LIMIT, SWIGLU_LIMIT)
        act = g * (1.0 / (1.0 + jnp.exp(-SWIGLU_ALPHA * g))) * (lin + 1.0)
        _to_tiles(y_ref, jnp.dot(act.astype(BF16), wd_sc[...], preferred_element_type=F32) + bd_ref[0])


def _grouped_ffn(xs_tiles, texp, nvalid, ecnt, wgu, wd, bg, bl, bd, ntiles):
    d = wgu.shape[1]
    tm = MOE_TILE
    f = wd.shape[1]
    xmap = lambda i, te, nv, ec: (jnp.minimum(i, nv[0] - 1), 0, 0, 0)
    wmap = lambda i, te, nv, ec: (te[i], 0, 0)
    rows_blk = pl.BlockSpec((tm // SUBLANES, d // LANES, SUBLANES, LANES), xmap)
    hbm = pl.BlockSpec(memory_space=pl.ANY)
    grid_spec = pltpu.PrefetchScalarGridSpec(
        num_scalar_prefetch=3,
        grid=(ntiles,),
        in_specs=[rows_blk, hbm, hbm,
                  pl.BlockSpec((1, 1, f), wmap), pl.BlockSpec((1, 1, f), wmap), pl.BlockSpec((1, 1, d), wmap)],
        out_specs=rows_blk,
        scratch_shapes=[pltpu.VMEM((2, d, 2 * f), F32), pltpu.VMEM((2, f, d), F32),
                        pltpu.VMEM((d, f), BF16), pltpu.VMEM((d, f), BF16), pltpu.VMEM((f, d), BF16),
                        pltpu.SMEM((1,), I32), pltpu.SemaphoreType.DMA((2,))],
    )
    return pl.pallas_call(
        _ffn_body,
        grid_spec=grid_spec,
        out_shape=jax.ShapeDtypeStruct(xs_tiles.shape, F32),
        compiler_params=pltpu.CompilerParams(dimension_semantics=("arbitrary",), vmem_limit_bytes=VMEM_LIMIT),
        name="ffn",
    )(texp, nvalid, ecnt, xs_tiles, wgu, wd, bg, bl, bd)


def _combine_body(lpb0_ref, lpb1_ref, gate0_ref, gate1_ref, gstart_ref, nslab_ref, lstart_ref,
                  gstart2_ref, nslab2_ref, lstart2_ref, h_ref, l2g_ref, l2b_ref, ys_ref, o_ref,
                  stg0_ref, stg1_ref, moe_ref, sem):
    g = pl.program_id(0)
    ng = pl.num_programs(0)
    halves = ((lpb0_ref, gate0_ref, stg0_ref), (lpb1_ref, gate1_ref, stg1_ref))

    def fetch(gs_ref, ns_ref, ls_ref, jt, half):
        def per_expert(e, c):
            _copy_run(ys_ref, gs_ref[jt, 0, e] * SLAB, halves[half][2], ls_ref[jt, 0, e] * SLAB,
                      ns_ref[jt, 0, e], sem.at[half])
            return c

        lax.fori_loop(0, N_EXPERTS, per_expert, 0)

    def wait_tile(j):
        nbig, nsmall = _chunk_counts(nslab_ref, j)
        _wait_runs(ys_ref, halves[j][2], nbig, nsmall, sem.at[j])

    def gather(j):
        lpb_ref, gate_ref, stg_ref = halves[j]

        def body(i, c):
            for u in range(SUBLANES):
                tok = i * SUBLANES + u
                acc = gate_ref[0, 0, tok] * _load_row(stg_ref, lpb_ref[0, 0, tok])
                for k in range(1, TOP_K):
                    acc = acc + (gate_ref[0, 0, k * TOKEN_TILE + tok]
                                 * _load_row(stg_ref, lpb_ref[0, 0, k * TOKEN_TILE + tok]))
                _store_row(moe_ref, i * SLAB + u, acc)
            return c

        lax.fori_loop(0, TOKEN_TILE // SUBLANES, body, 0)

    def finish(j):
        nblk = TOKEN_TILE // SUBLANES
        moe = jnp.concatenate(
            [jnp.concatenate([moe_ref[pl.ds(i * SLAB + s * SUBLANES, SUBLANES), :] for i in range(nblk)], axis=0)
             for s in range(SUBLANES)], axis=1)
        h1 = _from_tiles(h_ref.at[pl.ds(j * nblk, nblk)])
        o_ref[pl.ds(j * TOKEN_TILE, TOKEN_TILE), :] = _layernorm(DEEPNORM_ALPHA * h1 + moe, l2g_ref[...], l2b_ref[...])

    @pl.when(g == 0)
    def _():
        fetch(gstart_ref, nslab_ref, lstart_ref, 0, 0)

    fetch(gstart_ref, nslab_ref, lstart_ref, 1, 1)
    wait_tile(0)
    gather(0)

    @pl.when(g + 1 < ng)
    def _():
        fetch(gstart2_ref, nslab2_ref, lstart2_ref, 0, 0)

    finish(0)
    wait_tile(1)
    gather(1)
    finish(1)


def _combine(lpb, gates, gstart, nslab, lstart, h1_tiles, ys_flat, w):
    d = h1_tiles.shape[1] * LANES
    n = h1_tiles.shape[0] * SUBLANES
    t = 2 * TOKEN_TILE
    ng = n // t
    full = lambda shape: pl.BlockSpec(shape, lambda i: (0,) * len(shape))
    smem = lambda shape, imap: pl.BlockSpec(shape, imap, memory_space=pltpu.SMEM)
    vec = lambda j: smem((1, 1, TOP_K * TOKEN_TILE), lambda i: (2 * i + j, 0, 0))
    this_step = smem((2, 1, LANES), lambda i: (i, 0, 0))
    next_step = smem((2, 1, LANES), lambda i: (jnp.minimum(i + 1, ng - 1), 0, 0))
    return pl.pallas_call(
        _combine_body,
        grid=(ng,),
        in_specs=[vec(0), vec(1), vec(0), vec(1), this_step, this_step, this_step, next_step, next_step, next_step,
                  pl.BlockSpec((t // SUBLANES, d // LANES, SUBLANES, LANES), lambda i: (i, 0, 0, 0)),
                  full((1, d)), full((1, d)),
                  pl.BlockSpec(memory_space=pl.ANY)],
        out_specs=pl.BlockSpec((t, d), lambda i: (i, 0)),
        out_shape=jax.ShapeDtypeStruct((n, d), F32),
        scratch_shapes=[pltpu.VMEM((STAGE_FLAT, LANES), F32), pltpu.VMEM((STAGE_FLAT, LANES), F32),
                        pltpu.VMEM((TOKEN_TILE * SUBLANES, LANES), F32), pltpu.SemaphoreType.DMA((2,))],
        compiler_params=pltpu.CompilerParams(dimension_semantics=("arbitrary",), vmem_limit_bytes=VMEM_LIMIT),
        name="combine",
    )(lpb, lpb, gates, gates, gstart, nslab, lstart, gstart, nslab, lstart, h1_tiles, w["ln2_g"], w["ln2_b"],
      ys_flat)


def _rope_tables(pos_1d, row, col):
    inv = ROPE_THETA ** (-jnp.arange(0, ROPE_DIM, 2, dtype=F32) / ROPE_DIM)

    def cs(p):
        ang = p.astype(F32)[:, None] * inv[None, :]
        return jnp.cos(ang), jnp.sin(ang)

    c1, s1 = cs(pos_1d)
    cr, sr = cs(row)
    cc, sc = cs(col)
    return (jnp.concatenate([c1, c1, c1, c1], axis=1), jnp.concatenate([-s1, s1, -s1, s1], axis=1),
            jnp.concatenate([cr, cr, cc, cc], axis=1), jnp.concatenate([-sr, sr, -sc, sc], axis=1))


def _prep_weights(ln_emb_g, ln_emb_b, w_in, g_q_a, w_q_b, g_kv_a, w_kv_b, g_q_gqa, g_k_gqa, g_o_mla, g_o_gqa, w_o,
                  ln1_g, ln1_b, w_router, b_router, ln2_g, ln2_b):
    r2 = lambda v: v.reshape(1, -1).astype(F32)
    o = np.cumsum([0, Q_LORA, KV_LORA, ROPE_DIM, GQA_HEADS * GQA_DIM, GQA_KV_HEADS * GQA_DIM, GQA_KV_HEADS * GQA_DIM])
    wi = w_in[0]
    seg = [wi[:, o[i]:o[i + 1]] for i in range(6)]
    kpe2 = jnp.concatenate([seg[2], seg[2]], axis=1)
    w_in_p = jnp.concatenate([seg[0], seg[1], kpe2, _swap_halves64(kpe2), seg[3], _swap_halves64(seg[3]),
                              seg[4], _swap_halves64(seg[4]), seg[5]], axis=1).astype(BF16)
    wq = w_q_b[0].reshape(Q_LORA, MLA_HEADS, QK_DIM)
    wq_rope = wq[:, :, NOPE_DIM:].reshape(Q_LORA, -1)
    w_qb = jnp.concatenate([wq[:, :, :NOPE_DIM].reshape(Q_LORA, -1), wq_rope, _swap_halves64(wq_rope)],
                           axis=1).astype(BF16)
    wk = w_kv_b[0].reshape(KV_LORA, MLA_HEADS, NOPE_DIM + V_DIM)
    w_kvb = jnp.concatenate([wk[:, :, :NOPE_DIM].reshape(KV_LORA, -1), wk[:, :, NOPE_DIM:].reshape(KV_LORA, -1)],
                            axis=1).astype(BF16)
    wr = jnp.pad(w_router[0].astype(F32), ((0, 0), (0, LANES - N_EXPERTS)))
    wr_hi = wr.astype(BF16)
    w_router_p = jnp.concatenate([wr_hi, (wr - wr_hi.astype(F32)).astype(BF16)], axis=1)
    b_router_p = jnp.pad(b_router[0].astype(F32), (0, LANES - N_EXPERTS), constant_values=NEG_BIG).reshape(1, LANES)
    return dict(
        ln_emb_g=r2(ln_emb_g), ln_emb_b=r2(ln_emb_b), w_in=w_in_p, g_q_a=r2(g_q_a[0]), w_qb=w_qb,
        g_kv_a=r2(g_kv_a[0]), w_kvb=w_kvb, g_q_gqa=r2(g_q_gqa[0]), g_k_gqa=r2(g_k_gqa[0]),
        g_q_gqa_sw=_swap_halves64(r2(g_q_gqa[0])), g_k_gqa_sw=_swap_halves64(r2(g_k_gqa[0])),
        g_o_mla=r2(g_o_mla[0]), g_o_gqa=r2(g_o_gqa[0]), w_o=w_o[0].astype(BF16), ln1_g=r2(ln1_g[0]),
        ln1_b=r2(ln1_b[0]), w_router=w_router_p, b_router=b_router_p, ln2_g=r2(ln2_g[0]), ln2_b=r2(ln2_b[0]))


def kernel(x, meta_tokens, ln_emb_g, ln_emb_b, w_in, g_q_a, w_q_b, g_kv_a, w_kv_b, g_q_gqa, g_k_gqa, g_o_mla, g_o_gqa,
           w_o, ln1_g, ln1_b, w_router, b_router, w_gate_up, b_gate_up, w_down, b_down, ln2_g, ln2_b):
    b, s, d = x.shape
    n = b * s
    w = _prep_weights(ln_emb_g, ln_emb_b, w_in, g_q_a, w_q_b, g_kv_a, w_kv_b, g_q_gqa, g_k_gqa, g_o_mla, g_o_gqa,
                      w_o, ln1_g, ln1_b, w_router, b_router, ln2_g, ln2_b)

    tok = jnp.arange(s, dtype=I32)
    tabs_real = _rope_tables(tok + N_META, tok // GRID_W, tok % GRID_W)
    mt = jnp.arange(META_PAD, dtype=I32)
    tabs_meta = _rope_tables(mt, jnp.full((META_PAD,), -1, I32), mt)

    qm, km, vm, qg, kg, vg = _project(x, tabs_real, w, ROW_TILE)
    meta = jnp.pad(meta_tokens.astype(x.dtype), ((0, META_PAD - N_META), (0, 0))).reshape(1, META_PAD, d)
    _, km_m, vm_m, _, kg_m, vg_m = _project(meta, tabs_meta, w, META_PAD)

    o_mla = _mla_attention(qm, km, vm, km_m, vm_m)
    o_gqa = _gqa_attention(qg, kg, vg, kg_m, vg_m)

    h1_tiles, topi, gates = _merge(o_mla, o_gqa, x.reshape(n, d), w)

    run_rows = n * TOP_K + (n // TOKEN_TILE) * N_EXPERTS * (SUBLANES - 1)
    ntiles = -(-run_rows // MOE_TILE) + N_EXPERTS
    ntp = -(-ntiles // LANES) * LANES
    rows_pad = ntiles * MOE_TILE
    lpb, gstart, nslab, lstart, texp, nvalid, ecnt, eoff = _positions(topi, ntp)
    flat = lambda a: a.reshape(-1, LANES)
    xs_flat = _dispatch(flat(h1_tiles), lpb, gstart, nslab, lstart, ecnt.reshape(-1), eoff.reshape(-1), rows_pad)

    bgu = b_gate_up[0].astype(F32)
    tiles = lambda a: a.reshape(-1, d // LANES, SUBLANES, LANES)
    ys_tiles = _grouped_ffn(tiles(xs_flat), texp.reshape(-1), nvalid.reshape(-1), ecnt.reshape(-1),
                            w_gate_up[0], w_down[0],
                            bgu[:, 0::2].reshape(N_EXPERTS, 1, D_FF), bgu[:, 1::2].reshape(N_EXPERTS, 1, D_FF),
                            b_down[0].astype(F32).reshape(N_EXPERTS, 1, d), ntiles)

    out = _combine(lpb, gates, gstart, nslab, lstart, h1_tiles, flat(ys_tiles), w)
    return out.reshape(b, s, d)
```

```python
import functools

import jax
import jax.numpy as jnp
import numpy as np
from jax import lax
from jax.experimental import pallas as pl
from jax.experimental.pallas import tpu as pltpu

D_MODEL = 1024
N_META = 16
GRID_W = 64
ROPE_THETA = 10000.0
MLA_HEADS = 4
Q_LORA = 256
KV_LORA = 128
NOPE_DIM = 128
ROPE_DIM = 64
V_DIM = 128
QK_DIM = NOPE_DIM + ROPE_DIM
GQA_HEADS = 4
GQA_KV_HEADS = 2
GQA_DIM = 128
N_EXPERTS = 32
TOP_K = 4
D_FF = D_MODEL
SWIGLU_LIMIT = 7.0
SWIGLU_ALPHA = 1.702
RMS_EPS = 1e-6
LN_EPS = 1e-5
DEPTH = 1
DEEPNORM_ALPHA = (2.0 * DEPTH) ** 0.25

LANES = 128
SUBLANES = 8
SLAB = SUBLANES * SUBLANES
PACK_CHUNKS = D_MODEL // 2 // LANES
PSLAB = SUBLANES * PACK_CHUNKS
META_PAD = 128
MLA_K = 2 * LANES
NEG_BIG = -1e30
LOG2E = 1.4426950408889634
V_EXT = 2 * LANES

ROW_TILE = 512
Q_TILE_MLA = 1024
Q_TILE_GQA = 512
Q_SUB_MLA = 128
Q_SUB_GQA = 512
TOKEN_TILE = 512
MOE_TILE = 512
VMEM_LIMIT = 56 * 1024 * 1024

F32 = jnp.float32
BF16 = jnp.bfloat16
I32 = jnp.int32


def _layernorm(x, g, b):
    mu = jnp.mean(x, axis=-1, keepdims=True)
    xc = x - mu
    var = jnp.mean(xc * xc, axis=-1, keepdims=True)
    return xc * lax.rsqrt(var + LN_EPS) * g + b


def _rmsnorm(x, g):
    return x * lax.rsqrt(jnp.mean(x * x, axis=-1, keepdims=True) + RMS_EPS) * g


def _swap_halves64(v):
    shp = v.shape
    return v.reshape(shp[:-1] + (shp[-1] // 64, 2, 32))[..., ::-1, :].reshape(shp)


PROJ_SUB = 256


def _proj_chain(x, lng, lnb, win_ref, gqa, wqb_ref, gkva, wkvb_ref, gqg, gqgs, gkg, gkgs, c1, s1, ca, sa, put):
    r = x.shape[0]
    h0 = _layernorm(x, lng, lnb)
    z = jnp.dot(h0.astype(BF16), win_ref[...], preferred_element_type=F32)
    q_a = z[:, 0:256]
    kv_a = z[:, 256:384]
    kpe2 = z[:, 384:512]
    kpe2s = z[:, 512:640]
    q_g = z[:, 640:1152]
    q_gs = z[:, 1152:1664]
    k_g = z[:, 1664:1920]
    k_gs = z[:, 1920:2176]
    v_g = z[:, 2176:2432]

    q = jnp.dot(_rmsnorm(q_a, gqa).astype(BF16), wqb_ref[...], preferred_element_type=F32)
    kv = jnp.dot(_rmsnorm(kv_a, gkva).astype(BF16), wkvb_ref[...], preferred_element_type=F32)
    krot = kpe2 * c1 + kpe2s * s1
    lane = lax.broadcasted_iota(I32, krot.shape, 1)
    scale_a = QK_DIM ** -0.5 * LOG2E
    ones = jnp.ones((r, LANES), F32)
    for c in range(MLA_HEADS // 2):
        lo = LANES * c
        qr = q[:, 512 + lo:512 + lo + LANES] * c1 + q[:, 768 + lo:768 + lo + LANES] * s1
        for hh in range(2):
            h = 2 * c + hh
            slot = jnp.where((lane // 64) == hh, qr, 0.0)
            put("qm", h, (jnp.concatenate([q[:, LANES * h:LANES * (h + 1)], slot], axis=1) * scale_a).astype(BF16))
    for h in range(MLA_HEADS):
        put("km", h, jnp.concatenate([kv[:, LANES * h:LANES * (h + 1)], krot], axis=1).T.astype(BF16))
        put("vm", h, jnp.concatenate([kv[:, 512 + LANES * h:512 + LANES * (h + 1)], ones], axis=1).astype(BF16))

    scale_b = GQA_DIM ** -0.5 * LOG2E
    cq, sq = ca * gqg, sa * gqgs
    ck, sk = ca * gkg, sa * gkgs

    def norm_rot(xh, xs, cg, sg, scale):
        inv = lax.rsqrt(jnp.mean(xh * xh, axis=-1, keepdims=True) + RMS_EPS)
        return (xh * cg + xs * sg) * (inv * scale)

    for h in range(GQA_HEADS):
        sl = slice(LANES * h, LANES * (h + 1))
        put("qg", h, norm_rot(q_g[:, sl], q_gs[:, sl], cq, sq, scale_b).astype(BF16))
    for j in range(GQA_KV_HEADS):
        sl = slice(LANES * j, LANES * (j + 1))
        put("kg", j, norm_rot(k_g[:, sl], k_gs[:, sl], ck, sk, 1.0).T.astype(BF16))
        put("vg", j, jnp.concatenate([v_g[:, sl], ones], axis=1).astype(BF16))


def _proj_body(x_ref, lng_ref, lnb_ref, win_ref, gqa_ref, wqb_ref, gkva_ref, wkvb_ref, gqg_ref, gqgs_ref, gkg_ref,
               gkgs_ref, c1_ref, s1_ref, ca_ref, sa_ref,
               qm_ref, km_ref, vm_ref, qg_ref, kg_ref, vg_ref):
    outs = dict(qm=qm_ref, km=km_ref, vm=vm_ref, qg=qg_ref, kg=kg_ref, vg=vg_ref)
    tile = x_ref.shape[1]
    sub = min(PROJ_SUB, tile)
    for i in range(tile // sub):
        rows = pl.ds(i * sub, sub)

        def put(name, h, val):
            if name in ("km", "kg"):
                outs[name][0, h, :, rows] = val
            else:
                outs[name][0, h, rows, :] = val

        _proj_chain(x_ref[0, rows, :], lng_ref[...], lnb_ref[...], win_ref, gqa_ref[...], wqb_ref, gkva_ref[...],
                    wkvb_ref, gqg_ref[...], gqgs_ref[...], gkg_ref[...], gkgs_ref[...],
                    c1_ref[rows, :], s1_ref[rows, :], ca_ref[rows, :], sa_ref[rows, :], put)


def _project(x3, tabs, w, tile):
    b, s, d = x3.shape
    nst = s // tile
    full = lambda shape: pl.BlockSpec(shape, lambda bi, si: (0,) * len(shape))
    tab = pl.BlockSpec((tile, LANES), lambda bi, si: (si, 0))
    hm = lambda nh, dd: pl.BlockSpec((1, nh, tile, dd), lambda bi, si: (bi, 0, si, 0))
    hmt = lambda nh, dd: pl.BlockSpec((1, nh, dd, tile), lambda bi, si: (bi, 0, 0, si))
    out_shape = [
        jax.ShapeDtypeStruct((b, MLA_HEADS, s, MLA_K), BF16),
        jax.ShapeDtypeStruct((b, MLA_HEADS, MLA_K, s), BF16),
        jax.ShapeDtypeStruct((b, MLA_HEADS, s, V_EXT), BF16),
        jax.ShapeDtypeStruct((b, GQA_HEADS, s, GQA_DIM), BF16),
        jax.ShapeDtypeStruct((b, GQA_KV_HEADS, GQA_DIM, s), BF16),
        jax.ShapeDtypeStruct((b, GQA_KV_HEADS, s, V_EXT), BF16),
    ]
    return pl.pallas_call(
        _proj_body,
        grid=(b, nst),
        in_specs=[
            pl.BlockSpec((1, tile, d), lambda bi, si: (bi, si, 0)),
            full((1, d)), full((1, d)),
            full(w["w_in"].shape), full((1, Q_LORA)), full(w["w_qb"].shape),
            full((1, KV_LORA)), full(w["w_kvb"].shape),
            full((1, GQA_DIM)), full((1, GQA_DIM)), full((1, GQA_DIM)), full((1, GQA_DIM)),
            tab, tab, tab, tab,
        ],
        out_specs=[hm(MLA_HEADS, MLA_K), hmt(MLA_HEADS, MLA_K), hm(MLA_HEADS, V_EXT),
                   hm(GQA_HEADS, GQA_DIM), hmt(GQA_KV_HEADS, GQA_DIM), hm(GQA_KV_HEADS, V_EXT)],
        out_shape=out_shape,
        compiler_params=pltpu.CompilerParams(dimension_semantics=("parallel", "parallel"),
                                             vmem_limit_bytes=VMEM_LIMIT),
        name="proj",
    )(x3, w["ln_emb_g"], w["ln_emb_b"], w["w_in"], w["g_q_a"], w["w_qb"], w["g_kv_a"], w["w_kvb"],
      w["g_q_gqa"], w["g_q_gqa_sw"], w["g_k_gqa"], w["g_k_gqa_sw"], *tabs)


def _softmax_pv(q, kt, v, kmt, vm):
    s = jnp.dot(q, kt, preferred_element_type=F32)
    sm = jnp.dot(q, kmt, preferred_element_type=F32)
    col = lax.broadcasted_iota(I32, sm.shape, 1)
    sm = jnp.where(col < N_META, sm, NEG_BIG)
    m = jnp.maximum(jnp.max(s, axis=1, keepdims=True), jnp.max(sm, axis=1, keepdims=True))
    p = jnp.exp2(s - m).astype(BF16)
    pm = jnp.exp2(sm - m).astype(BF16)
    acc = jnp.dot(p, v, preferred_element_type=F32) + jnp.dot(pm, vm, preferred_element_type=F32)
    return acc[:, 0:V_DIM] / acc[:, V_DIM:V_EXT]


def _mla_attn_body(q_ref, k_ref, v_ref, km_ref, vm_ref, o_ref):
    for i in range(q_ref.shape[2] // Q_SUB_MLA):
        rows = pl.ds(i * Q_SUB_MLA, Q_SUB_MLA)
        o = _softmax_pv(q_ref[0, 0, rows, :], k_ref[0, 0], v_ref[0, 0], km_ref[0, 0], vm_ref[0, 0])
        o_ref[rows, :] = o.astype(o_ref.dtype)


def _gqa_attn_body(q_ref, k_ref, v_ref, km_ref, vm_ref, o_ref):
    for g in range(2):
        for i in range(q_ref.shape[2] // Q_SUB_GQA):
            rows = pl.ds(i * Q_SUB_GQA, Q_SUB_GQA)
            o = _softmax_pv(q_ref[0, g, rows, :], k_ref[0, 0], v_ref[0, 0], km_ref[0, 0], vm_ref[0, 0])
            o_ref[rows, GQA_DIM * g:GQA_DIM * (g + 1)] = o.astype(o_ref.dtype)


def _mla_attention(qm, kmt, vm, kmeta_t, vmeta):
    b, h, s, dk = qm.shape
    tq = Q_TILE_MLA
    nq = s // tq
    return pl.pallas_call(
        _mla_attn_body,
        grid=(b, h, nq),
        in_specs=[
            pl.BlockSpec((1, 1, tq, dk), lambda bi, hi, qi: (bi, hi, qi, 0)),
            pl.BlockSpec((1, 1, dk, s), lambda bi, hi, qi: (bi, hi, 0, 0)),
            pl.BlockSpec((1, 1, s, V_EXT), lambda bi, hi, qi: (bi, hi, 0, 0)),
            pl.BlockSpec((1, 1, dk, META_PAD), lambda bi, hi, qi: (0, hi, 0, 0)),
            pl.BlockSpec((1, 1, META_PAD, V_EXT), lambda bi, hi, qi: (0, hi, 0, 0)),
        ],
        out_specs=pl.BlockSpec((tq, V_DIM), lambda bi, hi, qi: (bi * nq + qi, hi)),
        out_shape=jax.ShapeDtypeStruct((b * s, h * V_DIM), BF16),
        compiler_params=pltpu.CompilerParams(dimension_semantics=("parallel", "parallel", "parallel"),
                                             vmem_limit_bytes=VMEM_LIMIT),
        name="mla_attn",
    )(qm, kmt, vm, kmeta_t, vmeta)


def _gqa_attention(qg, kgt, vg, kmeta_t, vmeta):
    b, h, s, d = qg.shape
    hk = kgt.shape[1]
    tq = Q_TILE_GQA
    nq = s // tq
    return pl.pallas_call(
        _gqa_attn_body,
        grid=(b, hk, nq),
        in_specs=[
            pl.BlockSpec((1, 2, tq, d), lambda bi, ji, qi: (bi, ji, qi, 0)),
            pl.BlockSpec((1, 1, d, s), lambda bi, ji, qi: (bi, ji, 0, 0)),
            pl.BlockSpec((1, 1, s, V_EXT), lambda bi, ji, qi: (bi, ji, 0, 0)),
            pl.BlockSpec((1, 1, d, META_PAD), lambda bi, ji, qi: (0, ji, 0, 0)),
            pl.BlockSpec((1, 1, META_PAD, V_EXT), lambda bi, ji, qi: (0, ji, 0, 0)),
        ],
        out_specs=pl.BlockSpec((tq, 2 * d), lambda bi, ji, qi: (bi * nq + qi, ji)),
        out_shape=jax.ShapeDtypeStruct((b * s, h * d), BF16),
        compiler_params=pltpu.CompilerParams(dimension_semantics=("parallel", "parallel", "parallel"),
                                             vmem_limit_bytes=VMEM_LIMIT),
        name="gqa_attn",
    )(qg, kgt, vg, kmeta_t, vmeta)


def _to_tiles(ref, x):
    r = x.shape[0]
    for s in range(x.shape[1] // LANES):
        ref[:, s] = x[:, LANES * s:LANES * (s + 1)].reshape(r // SUBLANES, SUBLANES, LANES)


def _from_tiles(ref):
    r = ref.shape[0] * SUBLANES
    return jnp.concatenate([ref[:, s].reshape(r, LANES) for s in range(ref.shape[1])], axis=1)


MERGE_SUB = 256


def _pack_rows(v):
    c = v.shape[1] // 2
    lo = lax.bitcast_convert_type(v[:, 0:c].astype(BF16).astype(F32), I32)
    hi = lax.bitcast_convert_type(v[:, c:2 * c].astype(BF16).astype(F32), I32)
    return lax.shift_right_logical(lo, 16) | hi


def _unpack_words(w):
    return (lax.bitcast_convert_type(w << 16, F32), lax.bitcast_convert_type(w & jnp.int32(-65536), F32))


def _merge_body(om_ref, og_ref, x_ref, lng_ref, lnb_ref, gom_ref, gog_ref, wo_ref, l1g_ref, l1b_ref,
                wr_ref, br_ref, h1_ref, h1p_ref, topi_ref, gate_ref):
    for i in range(x_ref.shape[0] // MERGE_SUB):
        rows = pl.ds(i * MERGE_SUB, MERGE_SUB)
        h1, ti, gt = _merge_chain(om_ref[rows, :], og_ref[rows, :], x_ref[rows, :], lng_ref[...], lnb_ref[...],
                                  gom_ref[...], gog_ref[...], wo_ref, l1g_ref[...], l1b_ref[...], wr_ref, br_ref[...])
        blocks = pl.ds(i * (MERGE_SUB // SUBLANES), MERGE_SUB // SUBLANES)
        _to_tiles(h1_ref.at[blocks], h1)
        _to_tiles(h1p_ref.at[blocks], _pack_rows(h1))
        topi_ref[:, rows] = ti
        for k in range(TOP_K):
            gate_ref[0, :, pl.ds(k * TOKEN_TILE + i * MERGE_SUB, MERGE_SUB)] = gt[k:k + 1, :]


def _merge_chain(om, og, x, lng, lnb, gom, gog, wo_ref, l1g, l1b, wr_ref, br):
    t = x.shape[0]
    h0 = _layernorm(x, lng, lnb)
    nm = _rmsnorm(om.astype(F32), gom).astype(BF16)
    ng = _rmsnorm(og.astype(F32), gog).astype(BF16)
    half = nm.shape[1]
    mix = jnp.dot(nm, wo_ref[0:half, :], preferred_element_type=F32)
    mix = mix + jnp.dot(ng, wo_ref[half:2 * half, :], preferred_element_type=F32)
    h1 = _layernorm(DEEPNORM_ALPHA * h0 + mix, l1g, l1b)

    hi = h1.astype(BF16)
    lo = (h1 - hi.astype(F32)).astype(BF16)
    acc = jnp.dot(hi, wr_ref[...], preferred_element_type=F32) + jnp.dot(lo, wr_ref[...], preferred_element_type=F32)
    logits = acc[:, 0:LANES] + acc[:, LANES:2 * LANES] + br
    cur = logits.T[0:N_EXPERTS, :]
    eidx = lax.broadcasted_iota(I32, cur.shape, 0)
    vals, idxs = [], []
    for _ in range(TOP_K):
        m = jnp.max(cur, axis=0, keepdims=True)
        i = jnp.min(jnp.where(cur == m, eidx, N_EXPERTS), axis=0, keepdims=True)
        vals.append(m)
        idxs.append(i)
        cur = jnp.where(eidx == i, -jnp.inf, cur)
    ex = [jnp.exp(v - vals[0]) for v in vals]
    den = ex[0] + ex[1] + ex[2] + ex[3]
    sub = lax.broadcasted_iota(I32, (8, t), 0)
    ti = jnp.zeros((8, t), I32)
    gt = jnp.zeros((8, t), F32)
    for k in range(TOP_K):
        ti = jnp.where(sub == k, idxs[k], ti)
        gt = jnp.where(sub == k, ex[k] / den, gt)
    return h1, ti[0:TOP_K, :], gt[0:TOP_K, :]


def _merge(o_mla, o_gqa, x2, w):
    n, d = x2.shape
    tile = TOKEN_TILE
    half = o_mla.shape[1]
    full = lambda shape: pl.BlockSpec(shape, lambda i: (0,) * len(shape))
    row = lambda width: pl.BlockSpec((tile, width), lambda i: (i, 0))
    return pl.pallas_call(
        _merge_body,
        grid=(n // tile,),
        in_specs=[row(half), row(half), row(d), full((1, d)), full((1, d)), full((1, half)), full((1, half)),
                  full((d, d)), full((1, d)), full((1, d)), full((d, 2 * LANES)), full((1, LANES))],
        out_specs=[pl.BlockSpec((tile // SUBLANES, d // LANES, SUBLANES, LANES), lambda i: (i, 0, 0, 0)),
                   pl.BlockSpec((tile // SUBLANES, PACK_CHUNKS, SUBLANES, LANES), lambda i: (i, 0, 0, 0)),
                   pl.BlockSpec((TOP_K, tile), lambda i: (0, i)),
                   pl.BlockSpec((1, 1, TOP_K * tile), lambda i: (i, 0, 0))],
        out_shape=[jax.ShapeDtypeStruct((n // SUBLANES, d // LANES, SUBLANES, LANES), F32),
                   jax.ShapeDtypeStruct((n // SUBLANES, PACK_CHUNKS, SUBLANES, LANES), I32),
                   jax.ShapeDtypeStruct((TOP_K, n), I32), jax.ShapeDtypeStruct((n // tile, 1, TOP_K * tile), F32)],
        compiler_params=pltpu.CompilerParams(dimension_semantics=("parallel",), vmem_limit_bytes=VMEM_LIMIT),
        name="merge",
    )(o_mla, o_gqa, x2, w["ln_emb_g"], w["ln_emb_b"], w["g_o_mla"], w["g_o_gqa"], w["w_o"],
      w["ln1_g"], w["ln1_b"], w["w_router"], w["b_router"])


def _lanes_from_sublanes(col):
    diag = lax.broadcasted_iota(I32, col.shape, 0) == lax.broadcasted_iota(I32, col.shape, 1)
    return jnp.sum(jnp.where(diag, col, 0.0), axis=0, keepdims=True)


def _positions_body(topi_ref, lpb_ref, gstart_ref, nslab_ref, lstart_ref, texp_ref, nvalid_ref, ecnt_ref, eoff_ref,
                    cnt_sc, carry_sc, off_sc, *, ntp):
    p = pl.program_id(0)
    j = pl.program_id(1)
    tl = topi_ref.shape[1]
    topi = topi_ref[...]
    eidx = lax.broadcasted_iota(I32, (N_EXPERTS, tl), 0)
    ohs = [eidx == topi[k:k + 1, :] for k in range(TOP_K)]
    onehot = ohs[0].astype(F32) + ohs[1].astype(F32) + ohs[2].astype(F32) + ohs[3].astype(F32)
    cnt = jnp.sum(onehot, axis=1, keepdims=True)
    run = jnp.floor((cnt + (SUBLANES - 1)) * (1.0 / SUBLANES)) * SUBLANES
    tile_run = jnp.broadcast_to(run, (N_EXPERTS, LANES))
    r = lax.broadcasted_iota(I32, (N_EXPERTS, N_EXPERTS), 0)
    c = lax.broadcasted_iota(I32, (N_EXPERTS, N_EXPERTS), 1)
    lower = (c < r).astype(F32)

    @pl.when(jnp.logical_and(p == 0, j == 0))
    def _():
        cnt_sc[...] = jnp.zeros_like(cnt_sc)

    @pl.when(p == 0)
    def _():
        cnt_sc[...] += tile_run

    @pl.when(jnp.logical_and(p == 1, j == 0))
    def _():
        tot = cnt_sc[...]
        pc = jnp.floor((tot + (MOE_TILE - 1)) * (1.0 / MOE_TILE)) * MOE_TILE
        off = jnp.dot(lower, pc, precision=lax.Precision.HIGHEST, preferred_element_type=F32)
        off_sc[...] = off
        carry_sc[...] = jnp.zeros_like(carry_sc)
        cumend = off + pc
        tstart = lax.broadcasted_iota(I32, (N_EXPERTS, ntp), 1).astype(F32) * MOE_TILE
        te = jnp.sum((jnp.broadcast_to(cumend[:, 0:1], (N_EXPERTS, ntp)) <= tstart).astype(I32), axis=0, keepdims=True)
        texp_ref[...] = jnp.minimum(te, N_EXPERTS - 1)
        nvalid_ref[...] = (cumend[N_EXPERTS - 1:N_EXPERTS, :] * (1.0 / MOE_TILE)).astype(I32)
        ecnt_ref[...] = _lanes_from_sublanes(tot).astype(I32)
        eoff_ref[...] = _lanes_from_sublanes(off).astype(I32)

    @pl.when(p == 1)
    def _():
        rr = lax.broadcasted_iota(I32, (tl, tl), 0)
        cc = lax.broadcasted_iota(I32, (tl, tl), 1)
        upper = (rr < cc).astype(BF16)
        before = jnp.dot(onehot.astype(BF16), upper, preferred_element_type=F32)
        loff = jnp.dot(lower, tile_run, precision=lax.Precision.HIGHEST, preferred_element_type=F32)
        base = before + loff[:, 0:1]
        sub = lax.broadcasted_iota(I32, (SUBLANES, tl), 0)
        out = jnp.zeros((SUBLANES, tl), F32)
        for k in range(TOP_K):
            pk = jnp.sum(jnp.where(ohs[k], base, 0.0), axis=0, keepdims=True)
            out = jnp.where(sub == k, pk, out)
        lp = out[0:TOP_K, :].astype(I32)
        lpb = (lp >> 3) * PSLAB + (lp & (SUBLANES - 1))
        lpb_ref[0] = jnp.concatenate([lpb[k:k + 1, :] for k in range(TOP_K)], axis=1)
        inv = 1.0 / SUBLANES
        gstart_ref[0] = (_lanes_from_sublanes(off_sc[...] + carry_sc[...]) * inv).astype(I32)
        nslab_ref[0] = (_lanes_from_sublanes(tile_run) * inv).astype(I32)
        lstart_ref[0] = (_lanes_from_sublanes(loff) * inv).astype(I32)
        carry_sc[...] += tile_run


def _positions(topi, ntp):
    n = topi.shape[1]
    tl = TOKEN_TILE
    const = lambda shape: pl.BlockSpec(shape, lambda p, j: (0, 0))
    per_tile = pl.BlockSpec((1, 1, LANES), lambda p, j: (j * p, 0, 0))
    tab = jax.ShapeDtypeStruct((n // tl, 1, LANES), I32)
    return pl.pallas_call(
        functools.partial(_positions_body, ntp=ntp),
        grid=(2, n // tl),
        in_specs=[pl.BlockSpec((TOP_K, tl), lambda p, j: (0, j))],
        out_specs=[pl.BlockSpec((1, 1, TOP_K * tl), lambda p, j: (j * p, 0, 0)), per_tile, per_tile, per_tile,
                   const((1, ntp)), const((1, LANES)), const((1, LANES)), const((1, LANES))],
        out_shape=[jax.ShapeDtypeStruct((n // tl, 1, TOP_K * tl), I32), tab, tab, tab,
                   jax.ShapeDtypeStruct((1, ntp), I32),
                   jax.ShapeDtypeStruct((1, LANES), I32), jax.ShapeDtypeStruct((1, LANES), I32),
                   jax.ShapeDtypeStruct((1, LANES), I32)],
        scratch_shapes=[pltpu.VMEM((N_EXPERTS, LANES), F32)] * 3,
        compiler_params=pltpu.CompilerParams(dimension_semantics=("arbitrary", "arbitrary")),
        name="positions",
    )(topi)


STAGE_ROWS = TOKEN_TILE * TOP_K + N_EXPERTS * SUBLANES
STAGE_FLAT = STAGE_ROWS * PACK_CHUNKS


def _load_row(ref, flat_start, chunks=SUBLANES):
    return ref[pl.ds(flat_start, chunks, stride=SUBLANES), :]


def _store_row(ref, flat_start, v):
    ref[pl.ds(flat_start, v.shape[0], stride=SUBLANES), :] = v


CHUNK_SLABS = 4


def _copy_run(src_ref, src0, dst_ref, dst0, nslab, sem):
    big = CHUNK_SLABS * PSLAB
    nbig = nslab >> 2
    nsmall = nslab & (CHUNK_SLABS - 1)

    def start_big(s, c):
        pltpu.make_async_copy(src_ref.at[pl.ds(src0 + s * big, big)], dst_ref.at[pl.ds(dst0 + s * big, big)], sem).start()
        return c

    def start_small(s, c):
        o = nbig * big + s * PSLAB
        pltpu.make_async_copy(src_ref.at[pl.ds(src0 + o, PSLAB)], dst_ref.at[pl.ds(dst0 + o, PSLAB)], sem).start()
        return c

    lax.fori_loop(0, nbig, start_big, 0)
    lax.fori_loop(0, nsmall, start_small, 0)


def _wait_runs(src_ref, dst_ref, nbig, nsmall, sem):
    big = CHUNK_SLABS * PSLAB

    def wait_big(s, c):
        pltpu.make_async_copy(src_ref.at[pl.ds(0, big)], dst_ref.at[pl.ds(0, big)], sem).wait()
        return c

    def wait_small(s, c):
        pltpu.make_async_copy(src_ref.at[pl.ds(0, PSLAB)], dst_ref.at[pl.ds(0, PSLAB)], sem).wait()
        return c

    lax.fori_loop(0, nbig, wait_big, 0)
    lax.fori_loop(0, nsmall, wait_small, 0)


def _chunk_counts(nslab_ref, j):
    def add(e, c):
        n = nslab_ref[j, 0, e]
        return c[0] + (n >> 2), c[1] + (n & (CHUNK_SLABS - 1))

    return lax.fori_loop(0, N_EXPERTS, add, (jnp.int32(0), jnp.int32(0)))


def _dispatch_body(ecnt_ref, eoff_ref, lpb0_ref, lpb1_ref, gstart_ref, nslab_ref, lstart_ref, h0_ref, h1_ref, xs_ref,
                   stg0_ref, stg1_ref, zero_ref, issued_ref, sem, zsem):
    g = pl.program_id(0)
    ng = pl.num_programs(0)
    halves = ((lpb0_ref, h0_ref, stg0_ref), (lpb1_ref, h1_ref, stg1_ref))

    def drain(j):
        _wait_runs(halves[j][2], xs_ref, issued_ref[2 * j], issued_ref[2 * j + 1], sem.at[j])

    for j in range(2):
        lpb_ref, h_ref, stg_ref = halves[j]

        @pl.when(g >= 1)
        def _():
            drain(j)

        def zero_last(e, c):
            last = jnp.maximum(lstart_ref[j, 0, e] + nslab_ref[j, 0, e] - 1, 0)
            stg_ref[pl.ds(last * PSLAB, PSLAB), :] = jnp.zeros((PSLAB, LANES), I32)
            return c

        lax.fori_loop(0, N_EXPERTS, zero_last, 0)

        def move(i, c):
            for u in range(SUBLANES):
                v = _load_row(h_ref, i * PSLAB + u, PACK_CHUNKS)
                for k in range(TOP_K):
                    _store_row(stg_ref, lpb_ref[0, 0, k * TOKEN_TILE + i * SUBLANES + u], v)
            return c

        lax.fori_loop(0, TOKEN_TILE // SUBLANES, move, 0)

        def send_run(e, c):
            _copy_run(stg_ref, lstart_ref[j, 0, e] * PSLAB, xs_ref, gstart_ref[j, 0, e] * PSLAB,
                      nslab_ref[j, 0, e], sem.at[j])
            return c

        lax.fori_loop(0, N_EXPERTS, send_run, 0)
        nbig, nsmall = _chunk_counts(nslab_ref, j)
        issued_ref[2 * j] = nbig
        issued_ref[2 * j + 1] = nsmall

    @pl.when(g == ng - 1)
    def _():
        zero_ref[...] = jnp.zeros_like(zero_ref)

        def zero_copy(dst_slab):
            return pltpu.make_async_copy(zero_ref, xs_ref.at[pl.ds(dst_slab * PSLAB, PSLAB)], zsem)

        def pad_segment(e, c):
            rows = ecnt_ref[e]
            first = (eoff_ref[e] + rows) >> 3
            npad = lax.rem(MOE_TILE - lax.rem(rows, MOE_TILE), MOE_TILE) >> 3
            lax.fori_loop(0, npad, lambda s, cc: (zero_copy(first + s).start(), cc)[1], 0)
            lax.fori_loop(0, npad, lambda s, cc: (zero_copy(first + s).wait(), cc)[1], 0)
            return c

        lax.fori_loop(0, N_EXPERTS, pad_segment, 0)
        drain(0)
        drain(1)


def _dispatch(h1_flat, lpb, gstart, nslab, lstart, ecnt, eoff, rows_pad):
    nflat = h1_flat.shape[0]
    t = 2 * TOKEN_TILE
    smem = lambda shape, imap: pl.BlockSpec(shape, imap, memory_space=pltpu.SMEM)
    per_tile = smem((2, 1, LANES), lambda i, c, o: (i, 0, 0))
    vec = lambda j: smem((1, 1, TOP_K * TOKEN_TILE), lambda i, c, o: (2 * i + j, 0, 0))
    rows = lambda j: pl.BlockSpec((TOKEN_TILE * PACK_CHUNKS, LANES), lambda i, c, o: (2 * i + j, 0))
    grid_spec = pltpu.PrefetchScalarGridSpec(
        num_scalar_prefetch=2,
        grid=(nflat // (t * PACK_CHUNKS),),
        in_specs=[vec(0), vec(1), per_tile, per_tile, per_tile, rows(0), rows(1)],
        out_specs=pl.BlockSpec(memory_space=pl.ANY),
        scratch_shapes=[pltpu.VMEM((STAGE_FLAT, LANES), I32), pltpu.VMEM((STAGE_FLAT, LANES), I32),
                        pltpu.VMEM((PSLAB, LANES), I32),
                        pltpu.SMEM((4,), I32), pltpu.SemaphoreType.DMA((2,)), pltpu.SemaphoreType.DMA],
    )
    return pl.pallas_call(
        _dispatch_body,
        grid_spec=grid_spec,
        out_shape=jax.ShapeDtypeStruct((rows_pad * PACK_CHUNKS, LANES), I32),
        compiler_params=pltpu.CompilerParams(dimension_semantics=("arbitrary",), vmem_limit_bytes=VMEM_LIMIT),
        name="dispatch",
    )(ecnt, eoff, lpb, lpb, gstart, nslab, lstart, h1_flat, h1_flat)


def _ffn_body(texp_ref, nvalid_ref, ecnt_ref, x_ref, wgu_hbm, wd_hbm, bg_ref, bl_ref, bd_ref, y_ref,
              wgu_buf, wd_buf, wg_sc, wl_sc, wd_sc, seg_ref, sem):
    i = pl.program_id(0)
    valid = i < nvalid_ref[0]
    e = texp_ref[i]
    first_of_expert = jnp.logical_or(i == 0, e != texp_ref[jnp.maximum(i - 1, 0)])

    def weight_copies(expert, slot):
        return (pltpu.make_async_copy(wgu_hbm.at[expert], wgu_buf.at[slot], sem.at[slot]),
                pltpu.make_async_copy(wd_hbm.at[expert], wd_buf.at[slot], sem.at[slot]))

    def prepare(slot):
        blk = 2 * LANES
        r = lax.broadcasted_iota(I32, (blk, blk), 0)
        c = lax.broadcasted_iota(I32, (blk, blk), 1)
        sel = (r == jnp.where(c < LANES, 2 * c, 2 * (c - LANES) + 1)).astype(BF16)
        for b in range(wgu_buf.shape[2] // blk):
            wb = wgu_buf[slot, :, blk * b:blk * (b + 1)].astype(BF16)
            y = jnp.dot(wb, sel, preferred_element_type=F32).astype(BF16)
            wg_sc[:, LANES * b:LANES * (b + 1)] = y[:, 0:LANES]
            wl_sc[:, LANES * b:LANES * (b + 1)] = y[:, LANES:blk]
        wd_sc[...] = wd_buf[slot].astype(BF16)

    @pl.when(jnp.logical_and(valid, first_of_expert))
    def _():
        @pl.when(i == 0)
        def _():
            seg_ref[0] = 0
            for cp in weight_copies(e, 0):
                cp.start()

        seg = seg_ref[0]
        seg_ref[0] = seg + 1
        nxt = lax.while_loop(lambda n: jnp.logical_and(n < N_EXPERTS, ecnt_ref[jnp.minimum(n, N_EXPERTS - 1)] == 0),
                             lambda n: n + 1, e + 1)
        for slot in range(2):
            @pl.when(lax.rem(seg, 2) == slot)
            def _():
                for cp in weight_copies(e, slot):
                    cp.wait()

                @pl.when(nxt < N_EXPERTS)
                def _():
                    for cp in weight_copies(nxt, 1 - slot):
                        cp.start()

                prepare(slot)

    @pl.when(valid)
    def _():
        x = jnp.concatenate(_unpack_words(_from_tiles(x_ref)), axis=1).astype(BF16)
        hg = jnp.dot(x, wg_sc[...], preferred_element_type=F32) + bg_ref[0]
        hl = jnp.dot(x, wl_sc[...], preferred_element_type=F32) + bl_ref[0]
        g = jnp.minimum(hg, SWIGLU_LIMIT)
        lin = jnp.clip(hl, -SWIGLU_LIMIT, SWIGLU_LIMIT)
        act = g * (1.0 / (1.0 + jnp.exp(-SWIGLU_ALPHA * g))) * (lin + 1.0)
        y = jnp.dot(act.astype(BF16), wd_sc[...], preferred_element_type=F32) + bd_ref[0]
        _to_tiles(y_ref, _pack_rows(y))


def _grouped_ffn(xs_tiles, texp, nvalid, ecnt, wgu, wd, bg, bl, bd, ntiles):
    d = wgu.shape[1]
    tm = MOE_TILE
    f = wd.shape[1]
    xmap = lambda i, te, nv, ec: (jnp.minimum(i, nv[0] - 1), 0, 0, 0)
    wmap = lambda i, te, nv, ec: (te[i], 0, 0)
    rows_blk = pl.BlockSpec((tm // SUBLANES, PACK_CHUNKS, SUBLANES, LANES), xmap)
    hbm = pl.BlockSpec(memory_space=pl.ANY)
    grid_spec = pltpu.PrefetchScalarGridSpec(
        num_scalar_prefetch=3,
        grid=(ntiles,),
        in_specs=[rows_blk, hbm, hbm,
                  pl.BlockSpec((1, 1, f), wmap), pl.BlockSpec((1, 1, f), wmap), pl.BlockSpec((1, 1, d), wmap)],
        out_specs=rows_blk,
        scratch_shapes=[pltpu.VMEM((2, d, 2 * f), F32), pltpu.VMEM((2, f, d), F32),
                        pltpu.VMEM((d, f), BF16), pltpu.VMEM((d, f), BF16), pltpu.VMEM((f, d), BF16),
                        pltpu.SMEM((1,), I32), pltpu.SemaphoreType.DMA((2,))],
    )
    return pl.pallas_call(
        _ffn_body,
        grid_spec=grid_spec,
        out_shape=jax.ShapeDtypeStruct(xs_tiles.shape, I32),
        compiler_params=pltpu.CompilerParams(dimension_semantics=("arbitrary",), vmem_limit_bytes=VMEM_LIMIT),
        name="ffn",
    )(texp, nvalid, ecnt, xs_tiles, wgu, wd, bg, bl, bd)


def _combine_body(lpb0_ref, lpb1_ref, gate0_ref, gate1_ref, gstart_ref, nslab_ref, lstart_ref,
                  gstart2_ref, nslab2_ref, lstart2_ref, h_ref, l2g_ref, l2b_ref, ys_ref, o_ref,
                  stg0_ref, stg1_ref, moe_ref, sem):
    g = pl.program_id(0)
    ng = pl.num_programs(0)
    halves = ((lpb0_ref, gate0_ref, stg0_ref), (lpb1_ref, gate1_ref, stg1_ref))

    def fetch(gs_ref, ns_ref, ls_ref, jt, half):
        def per_expert(e, c):
            _copy_run(ys_ref, gs_ref[jt, 0, e] * PSLAB, halves[half][2], ls_ref[jt, 0, e] * PSLAB,
                      ns_ref[jt, 0, e], sem.at[half])
            return c

        lax.fori_loop(0, N_EXPERTS, per_expert, 0)

    def wait_tile(j):
        nbig, nsmall = _chunk_counts(nslab_ref, j)
        _wait_runs(ys_ref, halves[j][2], nbig, nsmall, sem.at[j])

    def gather(j):
        lpb_ref, gate_ref, stg_ref = halves[j]

        def body(i, c):
            for u in range(SUBLANES):
                tok = i * SUBLANES + u
                acc_lo = acc_hi = None
                for k in range(TOP_K):
                    gk = gate_ref[0, 0, k * TOKEN_TILE + tok]
                    lo, hi = _unpack_words(_load_row(stg_ref, lpb_ref[0, 0, k * TOKEN_TILE + tok], PACK_CHUNKS))
                    acc_lo = gk * lo if k == 0 else acc_lo + gk * lo
                    acc_hi = gk * hi if k == 0 else acc_hi + gk * hi
                _store_row(moe_ref, i * SLAB + u, acc_lo)
                _store_row(moe_ref, i * SLAB + PSLAB + u, acc_hi)
            return c

        lax.fori_loop(0, TOKEN_TILE // SUBLANES, body, 0)

    def finish(j):
        nblk = TOKEN_TILE // SUBLANES
        moe = jnp.concatenate(
            [jnp.concatenate([moe_ref[pl.ds(i * SLAB + s * SUBLANES, SUBLANES), :] for i in range(nblk)], axis=0)
             for s in range(SUBLANES)], axis=1)
        h1 = _from_tiles(h_ref.at[pl.ds(j * nblk, nblk)])
        o_ref[pl.ds(j * TOKEN_TILE, TOKEN_TILE), :] = _layernorm(DEEPNORM_ALPHA * h1 + moe, l2g_ref[...], l2b_ref[...])

    @pl.when(g == 0)
    def _():
        fetch(gstart_ref, nslab_ref, lstart_ref, 0, 0)

    fetch(gstart_ref, nslab_ref, lstart_ref, 1, 1)
    wait_tile(0)
    gather(0)

    @pl.when(g + 1 < ng)
    def _():
        fetch(gstart2_ref, nslab2_ref, lstart2_ref, 0, 0)

    finish(0)
    wait_tile(1)
    gather(1)
    finish(1)


def _combine(lpb, gates, gstart, nslab, lstart, h1_tiles, ys_flat, w):
    d = h1_tiles.shape[1] * LANES
    n = h1_tiles.shape[0] * SUBLANES
    t = 2 * TOKEN_TILE
    ng = n // t
    full = lambda shape: pl.BlockSpec(shape, lambda i: (0,) * len(shape))
    smem = lambda shape, imap: pl.BlockSpec(shape, imap, memory_space=pltpu.SMEM)
    vec = lambda j: smem((1, 1, TOP_K * TOKEN_TILE), lambda i: (2 * i + j, 0, 0))
    this_step = smem((2, 1, LANES), lambda i: (i, 0, 0))
    next_step = smem((2, 1, LANES), lambda i: (jnp.minimum(i + 1, ng - 1), 0, 0))
    return pl.pallas_call(
        _combine_body,
        grid=(ng,),
        in_specs=[vec(0), vec(1), vec(0), vec(1), this_step, this_step, this_step, next_step, next_step, next_step,
                  pl.BlockSpec((t // SUBLANES, d // LANES, SUBLANES, LANES), lambda i: (i, 0, 0, 0)),
                  full((1, d)), full((1, d)),
                  pl.BlockSpec(memory_space=pl.ANY)],
        out_specs=pl.BlockSpec((t, d), lambda i: (i, 0)),
        out_shape=jax.ShapeDtypeStruct((n, d), F32),
        scratch_shapes=[pltpu.VMEM((STAGE_FLAT, LANES), I32), pltpu.VMEM((STAGE_FLAT, LANES), I32),
                        pltpu.VMEM((TOKEN_TILE * SUBLANES, LANES), F32), pltpu.SemaphoreType.DMA((2,))],
        compiler_params=pltpu.CompilerParams(dimension_semantics=("arbitrary",), vmem_limit_bytes=VMEM_LIMIT),
        name="combine",
    )(lpb, lpb, gates, gates, gstart, nslab, lstart, gstart, nslab, lstart, h1_tiles, w["ln2_g"], w["ln2_b"],
      ys_flat)


def _rope_tables(pos_1d, row, col):
    inv = ROPE_THETA ** (-jnp.arange(0, ROPE_DIM, 2, dtype=F32) / ROPE_DIM)

    def cs(p):
        ang = p.astype(F32)[:, None] * inv[None, :]
        return jnp.cos(ang), jnp.sin(ang)

    c1, s1 = cs(pos_1d)
    cr, sr = cs(row)
    cc, sc = cs(col)
    return (jnp.concatenate([c1, c1, c1, c1], axis=1), jnp.concatenate([-s1, s1, -s1, s1], axis=1),
            jnp.concatenate([cr, cr, cc, cc], axis=1), jnp.concatenate([-sr, sr, -sc, sc], axis=1))


def _prep_weights(ln_emb_g, ln_emb_b, w_in, g_q_a, w_q_b, g_kv_a, w_kv_b, g_q_gqa, g_k_gqa, g_o_mla, g_o_gqa, w_o,
                  ln1_g, ln1_b, w_router, b_router, ln2_g, ln2_b):
    r2 = lambda v: v.reshape(1, -1).astype(F32)
    o = np.cumsum([0, Q_LORA, KV_LORA, ROPE_DIM, GQA_HEADS * GQA_DIM, GQA_KV_HEADS * GQA_DIM, GQA_KV_HEADS * GQA_DIM])
    wi = w_in[0]
    seg = [wi[:, o[i]:o[i + 1]] for i in range(6)]
    kpe2 = jnp.concatenate([seg[2], seg[2]], axis=1)
    w_in_p = jnp.concatenate([seg[0], seg[1], kpe2, _swap_halves64(kpe2), seg[3], _swap_halves64(seg[3]),
                              seg[4], _swap_halves64(seg[4]), seg[5]], axis=1).astype(BF16)
    wq = w_q_b[0].reshape(Q_LORA, MLA_HEADS, QK_DIM)
    wq_rope = wq[:, :, NOPE_DIM:].reshape(Q_LORA, -1)
    w_qb = jnp.concatenate([wq[:, :, :NOPE_DIM].reshape(Q_LORA, -1), wq_rope, _swap_halves64(wq_rope)],
                           axis=1).astype(BF16)
    wk = w_kv_b[0].reshape(KV_LORA, MLA_HEADS, NOPE_DIM + V_DIM)
    w_kvb = jnp.concatenate([wk[:, :, :NOPE_DIM].reshape(KV_LORA, -1), wk[:, :, NOPE_DIM:].reshape(KV_LORA, -1)],
                            axis=1).astype(BF16)
    wr = jnp.pad(w_router[0].astype(F32), ((0, 0), (0, LANES - N_EXPERTS)))
    wr_hi = wr.astype(BF16)
    w_router_p = jnp.concatenate([wr_hi, (wr - wr_hi.astype(F32)).astype(BF16)], axis=1)
    b_router_p = jnp.pad(b_router[0].astype(F32), (0, LANES - N_EXPERTS), constant_values=NEG_BIG).reshape(1, LANES)
    return dict(
        ln_emb_g=r2(ln_emb_g), ln_emb_b=r2(ln_emb_b), w_in=w_in_p, g_q_a=r2(g_q_a[0]), w_qb=w_qb,
        g_kv_a=r2(g_kv_a[0]), w_kvb=w_kvb, g_q_gqa=r2(g_q_gqa[0]), g_k_gqa=r2(g_k_gqa[0]),
        g_q_gqa_sw=_swap_halves64(r2(g_q_gqa[0])), g_k_gqa_sw=_swap_halves64(r2(g_k_gqa[0])),
        g_o_mla=r2(g_o_mla[0]), g_o_gqa=r2(g_o_gqa[0]), w_o=w_o[0].astype(BF16), ln1_g=r2(ln1_g[0]),
        ln1_b=r2(ln1_b[0]), w_router=w_router_p, b_router=b_router_p, ln2_g=r2(ln2_g[0]), ln2_b=r2(ln2_b[0]))


def kernel(x, meta_tokens, ln_emb_g, ln_emb_b, w_in, g_q_a, w_q_b, g_kv_a, w_kv_b, g_q_gqa, g_k_gqa, g_o_mla, g_o_gqa,
           w_o, ln1_g, ln1_b, w_router, b_router, w_gate_up, b_gate_up, w_down, b_down, ln2_g, ln2_b):
    b, s, d = x.shape
    n = b * s
    w = _prep_weights(ln_emb_g, ln_emb_b, w_in, g_q_a, w_q_b, g_kv_a, w_kv_b, g_q_gqa, g_k_gqa, g_o_mla, g_o_gqa,
                      w_o, ln1_g, ln1_b, w_router, b_router, ln2_g, ln2_b)

    tok = jnp.arange(s, dtype=I32)
    tabs_real = _rope_tables(tok + N_META, tok // GRID_W, tok % GRID_W)
    mt = jnp.arange(META_PAD, dtype=I32)
    tabs_meta = _rope_tables(mt, jnp.full((META_PAD,), -1, I32), mt)

    qm, km, vm, qg, kg, vg = _project(x, tabs_real, w, ROW_TILE)
    meta = jnp.pad(meta_tokens.astype(x.dtype), ((0, META_PAD - N_META), (0, 0))).reshape(1, META_PAD, d)
    _, km_m, vm_m, _, kg_m, vg_m = _project(meta, tabs_meta, w, META_PAD)

    o_mla = _mla_attention(qm, km, vm, km_m, vm_m)
    o_gqa = _gqa_attention(qg, kg, vg, kg_m, vg_m)

    h1_tiles, h1_packed, topi, gates = _merge(o_mla, o_gqa, x.reshape(n, d), w)

    run_rows = n * TOP_K + (n // TOKEN_TILE) * N_EXPERTS * (SUBLANES - 1)
    ntiles = -(-run_rows // MOE_TILE) + N_EXPERTS
    ntp = -(-ntiles // LANES) * LANES
    rows_pad = ntiles * MOE_TILE
    lpb, gstart, nslab, lstart, texp, nvalid, ecnt, eoff = _positions(topi, ntp)
    flat = lambda a: a.reshape(-1, LANES)
    xs_flat = _dispatch(flat(h1_packed), lpb, gstart, nslab, lstart, ecnt.reshape(-1), eoff.reshape(-1), rows_pad)

    bgu = b_gate_up[0].astype(F32)
    tiles = lambda a: a.reshape(-1, PACK_CHUNKS, SUBLANES, LANES)
    ys_tiles = _grouped_ffn(tiles(xs_flat), texp.reshape(-1), nvalid.reshape(-1), ecnt.reshape(-1),
                            w_gate_up[0], w_down[0],
                            bgu[:, 0::2].reshape(N_EXPERTS, 1, D_FF), bgu[:, 1::2].reshape(N_EXPERTS, 1, D_FF),
                            b_down[0].astype(F32).reshape(N_EXPERTS, 1, d), ntiles)

    out = _combine(lpb, gates, gstart, nslab, lstart, h1_tiles, flat(ys_tiles), w)
    return out.reshape(b, s, d)
```

```python
import functools

import jax
import jax.numpy as jnp
import numpy as np
from jax import lax
from jax.experimental import pallas as pl
from jax.experimental.pallas import tpu as pltpu

D_MODEL = 1024
N_META = 16
GRID_W = 64
ROPE_THETA = 10000.0
MLA_HEADS = 4
Q_LORA = 256
KV_LORA = 128
NOPE_DIM = 128
ROPE_DIM = 64
V_DIM = 128
QK_DIM = NOPE_DIM + ROPE_DIM
GQA_HEADS = 4
GQA_KV_HEADS = 2
GQA_DIM = 128
N_EXPERTS = 32
TOP_K = 4
D_FF = D_MODEL
SWIGLU_LIMIT = 7.0
SWIGLU_ALPHA = 1.702
RMS_EPS = 1e-6
LN_EPS = 1e-5
DEPTH = 1
DEEPNORM_ALPHA = (2.0 * DEPTH) ** 0.25

LANES = 128
SUBLANES = 8
SLAB = SUBLANES * SUBLANES
PACK_CHUNKS = D_MODEL // 2 // LANES
PSLAB = SUBLANES * PACK_CHUNKS
META_PAD = 128
MLA_K = 2 * LANES
NEG_BIG = -1e30
LOG2E = 1.4426950408889634
V_EXT = 2 * LANES

ROW_TILE = 512
Q_TILE_MLA = 1024
Q_TILE_GQA = 512
Q_SUB_MLA = 128
Q_SUB_GQA = 512
TOKEN_TILE = 1024
MOE_TILE = 512
VMEM_LIMIT = 56 * 1024 * 1024

F32 = jnp.float32
BF16 = jnp.bfloat16
I32 = jnp.int32


def _layernorm(x, g, b):
    mu = jnp.mean(x, axis=-1, keepdims=True)
    xc = x - mu
    var = jnp.mean(xc * xc, axis=-1, keepdims=True)
    return xc * lax.rsqrt(var + LN_EPS) * g + b


def _rmsnorm(x, g):
    return x * lax.rsqrt(jnp.mean(x * x, axis=-1, keepdims=True) + RMS_EPS) * g


def _swap_halves64(v):
    shp = v.shape
    return v.reshape(shp[:-1] + (shp[-1] // 64, 2, 32))[..., ::-1, :].reshape(shp)


PROJ_SUB = 256


def _proj_chain(x, lng, lnb, win_ref, gqa, wqb_ref, gkva, wkvb_ref, gqg, gqgs, gkg, gkgs, c1, s1, ca, sa, put):
    r = x.shape[0]
    h0 = _layernorm(x, lng, lnb)
    z = jnp.dot(h0.astype(BF16), win_ref[...], preferred_element_type=F32)
    q_a = z[:, 0:256]
    kv_a = z[:, 256:384]
    kpe2 = z[:, 384:512]
    kpe2s = z[:, 512:640]
    q_g = z[:, 640:1152]
    q_gs = z[:, 1152:1664]
    k_g = z[:, 1664:1920]
    k_gs = z[:, 1920:2176]
    v_g = z[:, 2176:2432]

    q = jnp.dot(_rmsnorm(q_a, gqa).astype(BF16), wqb_ref[...], preferred_element_type=F32)
    kv = jnp.dot(_rmsnorm(kv_a, gkva).astype(BF16), wkvb_ref[...], preferred_element_type=F32)
    krot = kpe2 * c1 + kpe2s * s1
    lane = lax.broadcasted_iota(I32, krot.shape, 1)
    scale_a = QK_DIM ** -0.5 * LOG2E
    ones = jnp.ones((r, LANES), F32)
    for c in range(MLA_HEADS // 2):
        lo = LANES * c
        qr = q[:, 512 + lo:512 + lo + LANES] * c1 + q[:, 768 + lo:768 + lo + LANES] * s1
        for hh in range(2):
            h = 2 * c + hh
            slot = jnp.where((lane // 64) == hh, qr, 0.0)
            put("qm", h, (jnp.concatenate([q[:, LANES * h:LANES * (h + 1)], slot], axis=1) * scale_a).astype(BF16))
    for h in range(MLA_HEADS):
        put("km", h, jnp.concatenate([kv[:, LANES * h:LANES * (h + 1)], krot], axis=1).T.astype(BF16))
        put("vm", h, jnp.concatenate([kv[:, 512 + LANES * h:512 + LANES * (h + 1)], ones], axis=1).astype(BF16))

    scale_b = GQA_DIM ** -0.5 * LOG2E
    cq, sq = ca * gqg, sa * gqgs
    ck, sk = ca * gkg, sa * gkgs

    def norm_rot(xh, xs, cg, sg, scale):
        inv = lax.rsqrt(jnp.mean(xh * xh, axis=-1, keepdims=True) + RMS_EPS)
        return (xh * cg + xs * sg) * (inv * scale)

    for h in range(GQA_HEADS):
        sl = slice(LANES * h, LANES * (h + 1))
        put("qg", h, norm_rot(q_g[:, sl], q_gs[:, sl], cq, sq, scale_b).astype(BF16))
    for j in range(GQA_KV_HEADS):
        sl = slice(LANES * j, LANES * (j + 1))
        put("kg", j, norm_rot(k_g[:, sl], k_gs[:, sl], ck, sk, 1.0).T.astype(BF16))
        put("vg", j, jnp.concatenate([v_g[:, sl], ones], axis=1).astype(BF16))


def _proj_body(x_ref, lng_ref, lnb_ref, win_ref, gqa_ref, wqb_ref, gkva_ref, wkvb_ref, gqg_ref, gqgs_ref, gkg_ref,
               gkgs_ref, c1_ref, s1_ref, ca_ref, sa_ref,
               qm_ref, km_ref, vm_ref, qg_ref, kg_ref, vg_ref):
    outs = dict(qm=qm_ref, km=km_ref, vm=vm_ref, qg=qg_ref, kg=kg_ref, vg=vg_ref)
    tile = x_ref.shape[1]
    sub = min(PROJ_SUB, tile)
    for i in range(tile // sub):
        rows = pl.ds(i * sub, sub)

        def put(name, h, val):
            if name in ("km", "kg"):
                outs[name][0, h, :, rows] = val
            else:
                outs[name][0, h, rows, :] = val

        _proj_chain(x_ref[0, rows, :], lng_ref[...], lnb_ref[...], win_ref, gqa_ref[...], wqb_ref, gkva_ref[...],
                    wkvb_ref, gqg_ref[...], gqgs_ref[...], gkg_ref[...], gkgs_ref[...],
                    c1_ref[rows, :], s1_ref[rows, :], ca_ref[rows, :], sa_ref[rows, :], put)


def _project(x3, tabs, w, tile):
    b, s, d = x3.shape
    nst = s // tile
    full = lambda shape: pl.BlockSpec(shape, lambda bi, si: (0,) * len(shape))
    tab = pl.BlockSpec((tile, LANES), lambda bi, si: (si, 0))
    hm = lambda nh, dd: pl.BlockSpec((1, nh, tile, dd), lambda bi, si: (bi, 0, si, 0))
    hmt = lambda nh, dd: pl.BlockSpec((1, nh, dd, tile), lambda bi, si: (bi, 0, 0, si))
    out_shape = [
        jax.ShapeDtypeStruct((b, MLA_HEADS, s, MLA_K), BF16),
        jax.ShapeDtypeStruct((b, MLA_HEADS, MLA_K, s), BF16),
        jax.ShapeDtypeStruct((b, MLA_HEADS, s, V_EXT), BF16),
        jax.ShapeDtypeStruct((b, GQA_HEADS, s, GQA_DIM), BF16),
        jax.ShapeDtypeStruct((b, GQA_KV_HEADS, GQA_DIM, s), BF16),
        jax.ShapeDtypeStruct((b, GQA_KV_HEADS, s, V_EXT), BF16),
    ]
    return pl.pallas_call(
        _proj_body,
        grid=(b, nst),
        in_specs=[
            pl.BlockSpec((1, tile, d), lambda bi, si: (bi, si, 0)),
            full((1, d)), full((1, d)),
            full(w["w_in"].shape), full((1, Q_LORA)), full(w["w_qb"].shape),
            full((1, KV_LORA)), full(w["w_kvb"].shape),
            full((1, GQA_DIM)), full((1, GQA_DIM)), full((1, GQA_DIM)), full((1, GQA_DIM)),
            tab, tab, tab, tab,
        ],
        out_specs=[hm(MLA_HEADS, MLA_K), hmt(MLA_HEADS, MLA_K), hm(MLA_HEADS, V_EXT),
                   hm(GQA_HEADS, GQA_DIM), hmt(GQA_KV_HEADS, GQA_DIM), hm(GQA_KV_HEADS, V_EXT)],
        out_shape=out_shape,
        compiler_params=pltpu.CompilerParams(dimension_semantics=("parallel", "parallel"),
                                             vmem_limit_bytes=VMEM_LIMIT),
        name="proj",
    )(x3, w["ln_emb_g"], w["ln_emb_b"], w["w_in"], w["g_q_a"], w["w_qb"], w["g_kv_a"], w["w_kvb"],
      w["g_q_gqa"], w["g_q_gqa_sw"], w["g_k_gqa"], w["g_k_gqa_sw"], *tabs)


def _softmax_pv(q, kt, v, kmt, vm):
    s = jnp.dot(q, kt, preferred_element_type=F32)
    sm = jnp.dot(q, kmt, preferred_element_type=F32)
    col = lax.broadcasted_iota(I32, sm.shape, 1)
    sm = jnp.where(col < N_META, sm, NEG_BIG)
    m = jnp.maximum(jnp.max(s, axis=1, keepdims=True), jnp.max(sm, axis=1, keepdims=True))
    p = jnp.exp2(s - m).astype(BF16)
    pm = jnp.exp2(sm - m).astype(BF16)
    acc = jnp.dot(p, v, preferred_element_type=F32) + jnp.dot(pm, vm, preferred_element_type=F32)
    return acc[:, 0:V_DIM] / acc[:, V_DIM:V_EXT]


def _mla_attn_body(q_ref, k_ref, v_ref, km_ref, vm_ref, o_ref):
    for i in range(q_ref.shape[2] // Q_SUB_MLA):
        rows = pl.ds(i * Q_SUB_MLA, Q_SUB_MLA)
        o = _softmax_pv(q_ref[0, 0, rows, :], k_ref[0, 0], v_ref[0, 0], km_ref[0, 0], vm_ref[0, 0])
        o_ref[rows, :] = o.astype(o_ref.dtype)


def _gqa_attn_body(q_ref, k_ref, v_ref, km_ref, vm_ref, o_ref):
    for g in range(2):
        for i in range(q_ref.shape[2] // Q_SUB_GQA):
            rows = pl.ds(i * Q_SUB_GQA, Q_SUB_GQA)
            o = _softmax_pv(q_ref[0, g, rows, :], k_ref[0, 0], v_ref[0, 0], km_ref[0, 0], vm_ref[0, 0])
            o_ref[rows, GQA_DIM * g:GQA_DIM * (g + 1)] = o.astype(o_ref.dtype)


def _mla_attention(qm, kmt, vm, kmeta_t, vmeta):
    b, h, s, dk = qm.shape
    tq = Q_TILE_MLA
    nq = s // tq
    return pl.pallas_call(
        _mla_attn_body,
        grid=(b, h, nq),
        in_specs=[
            pl.BlockSpec((1, 1, tq, dk), lambda bi, hi, qi: (bi, hi, qi, 0)),
            pl.BlockSpec((1, 1, dk, s), lambda bi, hi, qi: (bi, hi, 0, 0)),
            pl.BlockSpec((1, 1, s, V_EXT), lambda bi, hi, qi: (bi, hi, 0, 0)),
            pl.BlockSpec((1, 1, dk, META_PAD), lambda bi, hi, qi: (0, hi, 0, 0)),
            pl.BlockSpec((1, 1, META_PAD, V_EXT), lambda bi, hi, qi: (0, hi, 0, 0)),
        ],
        out_specs=pl.BlockSpec((tq, V_DIM), lambda bi, hi, qi: (bi * nq + qi, hi)),
        out_shape=jax.ShapeDtypeStruct((b * s, h * V_DIM), BF16),
        compiler_params=pltpu.CompilerParams(dimension_semantics=("parallel", "parallel", "parallel"),
                                             vmem_limit_bytes=VMEM_LIMIT),
        name="mla_attn",
    )(qm, kmt, vm, kmeta_t, vmeta)


def _gqa_attention(qg, kgt, vg, kmeta_t, vmeta):
    b, h, s, d = qg.shape
    hk = kgt.shape[1]
    tq = Q_TILE_GQA
    nq = s // tq
    return pl.pallas_call(
        _gqa_attn_body,
        grid=(b, hk, nq),
        in_specs=[
            pl.BlockSpec((1, 2, tq, d), lambda bi, ji, qi: (bi, ji, qi, 0)),
            pl.BlockSpec((1, 1, d, s), lambda bi, ji, qi: (bi, ji, 0, 0)),
            pl.BlockSpec((1, 1, s, V_EXT), lambda bi, ji, qi: (bi, ji, 0, 0)),
            pl.BlockSpec((1, 1, d, META_PAD), lambda bi, ji, qi: (0, ji, 0, 0)),
            pl.BlockSpec((1, 1, META_PAD, V_EXT), lambda bi, ji, qi: (0, ji, 0, 0)),
        ],
        out_specs=pl.BlockSpec((tq, 2 * d), lambda bi, ji, qi: (bi * nq + qi, ji)),
        out_shape=jax.ShapeDtypeStruct((b * s, h * d), BF16),
        compiler_params=pltpu.CompilerParams(dimension_semantics=("parallel", "parallel", "parallel"),
                                             vmem_limit_bytes=VMEM_LIMIT),
        name="gqa_attn",
    )(qg, kgt, vg, kmeta_t, vmeta)


def _to_tiles(ref, x):
    r = x.shape[0]
    for s in range(x.shape[1] // LANES):
        ref[:, s] = x[:, LANES * s:LANES * (s + 1)].reshape(r // SUBLANES, SUBLANES, LANES)


def _from_tiles(ref):
    r = ref.shape[0] * SUBLANES
    return jnp.concatenate([ref[:, s].reshape(r, LANES) for s in range(ref.shape[1])], axis=1)


MERGE_SUB = 256


def _pack_rows(v):
    c = v.shape[1] // 2
    lo = lax.bitcast_convert_type(v[:, 0:c].astype(BF16).astype(F32), I32)
    hi = lax.bitcast_convert_type(v[:, c:2 * c].astype(BF16).astype(F32), I32)
    return lax.shift_right_logical(lo, 16) | hi


def _unpack_words(w):
    return (lax.bitcast_convert_type(w << 16, F32), lax.bitcast_convert_type(w & jnp.int32(-65536), F32))


def _merge_body(om_ref, og_ref, x_ref, lng_ref, lnb_ref, gom_ref, gog_ref, wo_ref, l1g_ref, l1b_ref,
                wr_ref, br_ref, h1_ref, h1p_ref, topi_ref, gate_ref):
    for i in range(x_ref.shape[0] // MERGE_SUB):
        rows = pl.ds(i * MERGE_SUB, MERGE_SUB)
        h1, ti, gt = _merge_chain(om_ref[rows, :], og_ref[rows, :], x_ref[rows, :], lng_ref[...], lnb_ref[...],
                                  gom_ref[...], gog_ref[...], wo_ref, l1g_ref[...], l1b_ref[...], wr_ref, br_ref[...])
        blocks = pl.ds(i * (MERGE_SUB // SUBLANES), MERGE_SUB // SUBLANES)
        _to_tiles(h1_ref.at[blocks], h1)
        _to_tiles(h1p_ref.at[blocks], _pack_rows(h1))
        topi_ref[:, rows] = ti
        for k in range(TOP_K):
            gate_ref[0, :, pl.ds(k * TOKEN_TILE + i * MERGE_SUB, MERGE_SUB)] = gt[k:k + 1, :]


def _merge_chain(om, og, x, lng, lnb, gom, gog, wo_ref, l1g, l1b, wr_ref, br):
    t = x.shape[0]
    h0 = _layernorm(x, lng, lnb)
    nm = _rmsnorm(om.astype(F32), gom).astype(BF16)
    ng = _rmsnorm(og.astype(F32), gog).astype(BF16)
    half = nm.shape[1]
    mix = jnp.dot(nm, wo_ref[0:half, :], preferred_element_type=F32)
    mix = mix + jnp.dot(ng, wo_ref[half:2 * half, :], preferred_element_type=F32)
    h1 = _layernorm(DEEPNORM_ALPHA * h0 + mix, l1g, l1b)

    hi = h1.astype(BF16)
    lo = (h1 - hi.astype(F32)).astype(BF16)
    acc = jnp.dot(hi, wr_ref[...], preferred_element_type=F32) + jnp.dot(lo, wr_ref[...], preferred_element_type=F32)
    logits = acc[:, 0:LANES] + acc[:, LANES:2 * LANES] + br
    cur = logits.T[0:N_EXPERTS, :]
    eidx = lax.broadcasted_iota(I32, cur.shape, 0)
    vals, idxs = [], []
    for _ in range(TOP_K):
        m = jnp.max(cur, axis=0, keepdims=True)
        i = jnp.min(jnp.where(cur == m, eidx, N_EXPERTS), axis=0, keepdims=True)
        vals.append(m)
        idxs.append(i)
        cur = jnp.where(eidx == i, -jnp.inf, cur)
    ex = [jnp.exp(v - vals[0]) for v in vals]
    den = ex[0] + ex[1] + ex[2] + ex[3]
    sub = lax.broadcasted_iota(I32, (8, t), 0)
    ti = jnp.zeros((8, t), I32)
    gt = jnp.zeros((8, t), F32)
    for k in range(TOP_K):
        ti = jnp.where(sub == k, idxs[k], ti)
        gt = jnp.where(sub == k, ex[k] / den, gt)
    return h1, ti[0:TOP_K, :], gt[0:TOP_K, :]


def _merge(o_mla, o_gqa, x2, w):
    n, d = x2.shape
    tile = TOKEN_TILE
    half = o_mla.shape[1]
    full = lambda shape: pl.BlockSpec(shape, lambda i: (0,) * len(shape))
    row = lambda width: pl.BlockSpec((tile, width), lambda i: (i, 0))
    return pl.pallas_call(
        _merge_body,
        grid=(n // tile,),
        in_specs=[row(half), row(half), row(d), full((1, d)), full((1, d)), full((1, half)), full((1, half)),
                  full((d, d)), full((1, d)), full((1, d)), full((d, 2 * LANES)), full((1, LANES))],
        out_specs=[pl.BlockSpec((tile // SUBLANES, d // LANES, SUBLANES, LANES), lambda i: (i, 0, 0, 0)),
                   pl.BlockSpec((tile // SUBLANES, PACK_CHUNKS, SUBLANES, LANES), lambda i: (i, 0, 0, 0)),
                   pl.BlockSpec((TOP_K, tile), lambda i: (0, i)),
                   pl.BlockSpec((1, 1, TOP_K * tile), lambda i: (i, 0, 0))],
        out_shape=[jax.ShapeDtypeStruct((n // SUBLANES, d // LANES, SUBLANES, LANES), F32),
                   jax.ShapeDtypeStruct((n // SUBLANES, PACK_CHUNKS, SUBLANES, LANES), I32),
                   jax.ShapeDtypeStruct((TOP_K, n), I32), jax.ShapeDtypeStruct((n // tile, 1, TOP_K * tile), F32)],
        compiler_params=pltpu.CompilerParams(dimension_semantics=("parallel",), vmem_limit_bytes=VMEM_LIMIT),
        name="merge",
    )(o_mla, o_gqa, x2, w["ln_emb_g"], w["ln_emb_b"], w["g_o_mla"], w["g_o_gqa"], w["w_o"],
      w["ln1_g"], w["ln1_b"], w["w_router"], w["b_router"])


def _lanes_from_sublanes(col):
    diag = lax.broadcasted_iota(I32, col.shape, 0) == lax.broadcasted_iota(I32, col.shape, 1)
    return jnp.sum(jnp.where(diag, col, 0.0), axis=0, keepdims=True)


def _positions_body(topi_ref, upper_ref, lpb_ref, gstart_ref, nslab_ref, lstart_ref, texp_ref, nvalid_ref, ecnt_ref,
                    eoff_ref, cnt_sc, carry_sc, off_sc, *, ntp):
    p = pl.program_id(0)
    j = pl.program_id(1)
    tl = topi_ref.shape[1]
    topi = topi_ref[...]
    eidx = lax.broadcasted_iota(I32, (N_EXPERTS, tl), 0)
    ohs = [eidx == topi[k:k + 1, :] for k in range(TOP_K)]
    onehot = ohs[0].astype(F32) + ohs[1].astype(F32) + ohs[2].astype(F32) + ohs[3].astype(F32)
    cnt = jnp.sum(onehot, axis=1, keepdims=True)
    run = jnp.floor((cnt + (SUBLANES - 1)) * (1.0 / SUBLANES)) * SUBLANES
    tile_run = jnp.broadcast_to(run, (N_EXPERTS, LANES))
    r = lax.broadcasted_iota(I32, (N_EXPERTS, N_EXPERTS), 0)
    c = lax.broadcasted_iota(I32, (N_EXPERTS, N_EXPERTS), 1)
    lower = (c < r).astype(F32)

    @pl.when(jnp.logical_and(p == 0, j == 0))
    def _():
        cnt_sc[...] = jnp.zeros_like(cnt_sc)

    @pl.when(p == 0)
    def _():
        cnt_sc[...] += tile_run

    @pl.when(jnp.logical_and(p == 1, j == 0))
    def _():
        tot = cnt_sc[...]
        pc = jnp.floor((tot + (MOE_TILE - 1)) * (1.0 / MOE_TILE)) * MOE_TILE
        off = jnp.dot(lower, pc, precision=lax.Precision.HIGHEST, preferred_element_type=F32)
        off_sc[...] = off
        carry_sc[...] = jnp.zeros_like(carry_sc)
        cumend = off + pc
        tstart = lax.broadcasted_iota(I32, (N_EXPERTS, ntp), 1).astype(F32) * MOE_TILE
        te = jnp.sum((jnp.broadcast_to(cumend[:, 0:1], (N_EXPERTS, ntp)) <= tstart).astype(I32), axis=0, keepdims=True)
        texp_ref[...] = jnp.minimum(te, N_EXPERTS - 1)
        nvalid_ref[...] = (cumend[N_EXPERTS - 1:N_EXPERTS, :] * (1.0 / MOE_TILE)).astype(I32)
        ecnt_ref[...] = _lanes_from_sublanes(tot).astype(I32)
        eoff_ref[...] = _lanes_from_sublanes(off).astype(I32)

    @pl.when(p == 1)
    def _():
        before = jnp.dot(onehot.astype(BF16), upper_ref[...], preferred_element_type=F32)
        loff = jnp.dot(lower, tile_run, precision=lax.Precision.HIGHEST, preferred_element_type=F32)
        base = before + loff[:, 0:1]
        sub = lax.broadcasted_iota(I32, (SUBLANES, tl), 0)
        out = jnp.zeros((SUBLANES, tl), F32)
        for k in range(TOP_K):
            pk = jnp.sum(jnp.where(ohs[k], base, 0.0), axis=0, keepdims=True)
            out = jnp.where(sub == k, pk, out)
        lp = out[0:TOP_K, :].astype(I32)
        lpb = (lp >> 3) * PSLAB + (lp & (SUBLANES - 1))
        lpb_ref[0] = jnp.concatenate([lpb[k:k + 1, :] for k in range(TOP_K)], axis=1)
        inv = 1.0 / SUBLANES
        gstart_ref[0] = (_lanes_from_sublanes(off_sc[...] + carry_sc[...]) * inv).astype(I32)
        nslab_ref[0] = (_lanes_from_sublanes(tile_run) * inv).astype(I32)
        lstart_ref[0] = (_lanes_from_sublanes(loff) * inv).astype(I32)
        carry_sc[...] += tile_run


def _positions(topi, ntp):
    n = topi.shape[1]
    tl = TOKEN_TILE
    const = lambda shape: pl.BlockSpec(shape, lambda p, j: (0, 0))
    per_tile = pl.BlockSpec((1, 1, LANES), lambda p, j: (j * p, 0, 0))
    tab = jax.ShapeDtypeStruct((n // tl, 1, LANES), I32)
    upper = (jnp.arange(tl, dtype=I32)[:, None] < jnp.arange(tl, dtype=I32)[None, :]).astype(BF16)
    return pl.pallas_call(
        functools.partial(_positions_body, ntp=ntp),
        grid=(2, n // tl),
        in_specs=[pl.BlockSpec((TOP_K, tl), lambda p, j: (0, j)), const((tl, tl))],
        out_specs=[pl.BlockSpec((1, 1, TOP_K * tl), lambda p, j: (j * p, 0, 0)), per_tile, per_tile, per_tile,
                   const((1, ntp)), const((1, LANES)), const((1, LANES)), const((1, LANES))],
        out_shape=[jax.ShapeDtypeStruct((n // tl, 1, TOP_K * tl), I32), tab, tab, tab,
                   jax.ShapeDtypeStruct((1, ntp), I32),
                   jax.ShapeDtypeStruct((1, LANES), I32), jax.ShapeDtypeStruct((1, LANES), I32),
                   jax.ShapeDtypeStruct((1, LANES), I32)],
        scratch_shapes=[pltpu.VMEM((N_EXPERTS, LANES), F32)] * 3,
        compiler_params=pltpu.CompilerParams(dimension_semantics=("arbitrary", "arbitrary"),
                                             vmem_limit_bytes=VMEM_LIMIT),
        name="positions",
    )(topi, upper)


STAGE_ROWS = TOKEN_TILE * TOP_K + N_EXPERTS * SUBLANES
STAGE_FLAT = STAGE_ROWS * PACK_CHUNKS


def _load_row(ref, flat_start, chunks=SUBLANES):
    return ref[pl.ds(flat_start, chunks, stride=SUBLANES), :]


def _store_row(ref, flat_start, v):
    ref[pl.ds(flat_start, v.shape[0], stride=SUBLANES), :] = v


CHUNK_SLABS = 4


def _copy_run(src_ref, src0, dst_ref, dst0, nslab, sem):
    big = CHUNK_SLABS * PSLAB
    nbig = nslab >> 2
    nsmall = nslab & (CHUNK_SLABS - 1)

    def start_big(s, c):
        pltpu.make_async_copy(src_ref.at[pl.ds(src0 + s * big, big)], dst_ref.at[pl.ds(dst0 + s * big, big)], sem).start()
        return c

    def start_small(s, c):
        o = nbig * big + s * PSLAB
        pltpu.make_async_copy(src_ref.at[pl.ds(src0 + o, PSLAB)], dst_ref.at[pl.ds(dst0 + o, PSLAB)], sem).start()
        return c

    lax.fori_loop(0, nbig, start_big, 0)
    lax.fori_loop(0, nsmall, start_small, 0)


def _wait_runs(src_ref, dst_ref, nbig, nsmall, sem):
    big = CHUNK_SLABS * PSLAB

    def wait_big(s, c):
        pltpu.make_async_copy(src_ref.at[pl.ds(0, big)], dst_ref.at[pl.ds(0, big)], sem).wait()
        return c

    def wait_small(s, c):
        pltpu.make_async_copy(src_ref.at[pl.ds(0, PSLAB)], dst_ref.at[pl.ds(0, PSLAB)], sem).wait()
        return c

    lax.fori_loop(0, nbig, wait_big, 0)
    lax.fori_loop(0, nsmall, wait_small, 0)


def _chunk_counts(nslab_ref, j):
    def add(e, c):
        n = nslab_ref[j, 0, e]
        return c[0] + (n >> 2), c[1] + (n & (CHUNK_SLABS - 1))

    return lax.fori_loop(0, N_EXPERTS, add, (jnp.int32(0), jnp.int32(0)))


def _dispatch_body(ecnt_ref, eoff_ref, lpb_ref, gstart_ref, nslab_ref, lstart_ref, h_ref, xs_ref,
                   stg0_ref, stg1_ref, zero_ref, issued_ref, sem, zsem):
    g = pl.program_id(0)
    ng = pl.num_programs(0)
    stages = (stg0_ref, stg1_ref)

    def drain(j):
        _wait_runs(stages[j], xs_ref, issued_ref[2 * j], issued_ref[2 * j + 1], sem.at[j])

    def run(j):
        stg_ref = stages[j]

        @pl.when(g >= 2)
        def _():
            drain(j)

        def zero_last(e, c):
            last = jnp.maximum(lstart_ref[0, 0, e] + nslab_ref[0, 0, e] - 1, 0)
            stg_ref[pl.ds(last * PSLAB, PSLAB), :] = jnp.zeros((PSLAB, LANES), I32)
            return c

        lax.fori_loop(0, N_EXPERTS, zero_last, 0)

        def move(i, c):
            for u in range(SUBLANES):
                v = _load_row(h_ref, i * PSLAB + u, PACK_CHUNKS)
                for k in range(TOP_K):
                    _store_row(stg_ref, lpb_ref[0, 0, k * TOKEN_TILE + i * SUBLANES + u], v)
            return c

        lax.fori_loop(0, TOKEN_TILE // SUBLANES, move, 0)

        def send_run(e, c):
            _copy_run(stg_ref, lstart_ref[0, 0, e] * PSLAB, xs_ref, gstart_ref[0, 0, e] * PSLAB,
                      nslab_ref[0, 0, e], sem.at[j])
            return c

        lax.fori_loop(0, N_EXPERTS, send_run, 0)
        nbig, nsmall = _chunk_counts(nslab_ref, 0)
        issued_ref[2 * j] = nbig
        issued_ref[2 * j + 1] = nsmall

    for j in range(2):
        @pl.when(lax.rem(g, 2) == j)
        def _():
            run(j)

    @pl.when(g == ng - 1)
    def _():
        zero_ref[...] = jnp.zeros_like(zero_ref)

        def zero_copy(dst_slab):
            return pltpu.make_async_copy(zero_ref, xs_ref.at[pl.ds(dst_slab * PSLAB, PSLAB)], zsem)

        def pad_segment(e, c):
            rows = ecnt_ref[e]
            first = (eoff_ref[e] + rows) >> 3
            npad = lax.rem(MOE_TILE - lax.rem(rows, MOE_TILE), MOE_TILE) >> 3
            lax.fori_loop(0, npad, lambda s, cc: (zero_copy(first + s).start(), cc)[1], 0)
            lax.fori_loop(0, npad, lambda s, cc: (zero_copy(first + s).wait(), cc)[1], 0)
            return c

        lax.fori_loop(0, N_EXPERTS, pad_segment, 0)
        for j in range(2):
            @pl.when(jnp.logical_and(ng > 1, lax.rem(g, 2) != j))
            def _():
                drain(j)

        for j in range(2):
            @pl.when(lax.rem(g, 2) == j)
            def _():
                drain(j)


def _dispatch(h1_flat, lpb, gstart, nslab, lstart, ecnt, eoff, rows_pad):
    nflat = h1_flat.shape[0]
    t = TOKEN_TILE
    smem = lambda shape, imap: pl.BlockSpec(shape, imap, memory_space=pltpu.SMEM)
    per_tile = smem((1, 1, LANES), lambda i, c, o: (i, 0, 0))
    grid_spec = pltpu.PrefetchScalarGridSpec(
        num_scalar_prefetch=2,
        grid=(nflat // (t * PACK_CHUNKS),),
        in_specs=[smem((1, 1, TOP_K * t), lambda i, c, o: (i, 0, 0)), per_tile, per_tile, per_tile,
                  pl.BlockSpec((t * PACK_CHUNKS, LANES), lambda i, c, o: (i, 0))],
        out_specs=pl.BlockSpec(memory_space=pl.ANY),
        scratch_shapes=[pltpu.VMEM((STAGE_FLAT, LANES), I32), pltpu.VMEM((STAGE_FLAT, LANES), I32),
                        pltpu.VMEM((PSLAB, LANES), I32),
                        pltpu.SMEM((4,), I32), pltpu.SemaphoreType.DMA((2,)), pltpu.SemaphoreType.DMA],
    )
    return pl.pallas_call(
        _dispatch_body,
        grid_spec=grid_spec,
        out_shape=jax.ShapeDtypeStruct((rows_pad * PACK_CHUNKS, LANES), I32),
        compiler_params=pltpu.CompilerParams(dimension_semantics=("arbitrary",), vmem_limit_bytes=VMEM_LIMIT),
        name="dispatch",
    )(ecnt, eoff, lpb, gstart, nslab, lstart, h1_flat)


def _ffn_body(texp_ref, nvalid_ref, ecnt_ref, x_ref, wgu_hbm, wd_hbm, bg_ref, bl_ref, bd_ref, y_ref,
              wgu_buf, wd_buf, wg_sc, wl_sc, wd_sc, seg_ref, sem):
    i = pl.program_id(0)
    valid = i < nvalid_ref[0]
    e = texp_ref[i]
    first_of_expert = jnp.logical_or(i == 0, e != texp_ref[jnp.maximum(i - 1, 0)])

    def weight_copies(expert, slot):
        return (pltpu.make_async_copy(wgu_hbm.at[expert], wgu_buf.at[slot], sem.at[slot]),
                pltpu.make_async_copy(wd_hbm.at[expert], wd_buf.at[slot], sem.at[slot]))

    def prepare(slot):
        blk = 2 * LANES
        r = lax.broadcasted_iota(I32, (blk, blk), 0)
        c = lax.broadcasted_iota(I32, (blk, blk), 1)
        sel = (r == jnp.where(c < LANES, 2 * c, 2 * (c - LANES) + 1)).astype(BF16)
        for b in range(wgu_buf.shape[2] // blk):
            wb = wgu_buf[slot, :, blk * b:blk * (b + 1)].astype(BF16)
            y = jnp.dot(wb, sel, preferred_element_type=F32).astype(BF16)
            wg_sc[:, LANES * b:LANES * (b + 1)] = y[:, 0:LANES]
            wl_sc[:, LANES * b:LANES * (b + 1)] = y[:, LANES:blk]
        wd_sc[...] = wd_buf[slot].astype(BF16)

    @pl.when(jnp.logical_and(valid, first_of_expert))
    def _():
        @pl.when(i == 0)
        def _():
            seg_ref[0] = 0
            for cp in weight_copies(e, 0):
                cp.start()

        seg = seg_ref[0]
        seg_ref[0] = seg + 1
        nxt = lax.while_loop(lambda n: jnp.logical_and(n < N_EXPERTS, ecnt_ref[jnp.minimum(n, N_EXPERTS - 1)] == 0),
                             lambda n: n + 1, e + 1)
        for slot in range(2):
            @pl.when(lax.rem(seg, 2) == slot)
            def _():
                for cp in weight_copies(e, slot):
                    cp.wait()

                @pl.when(nxt < N_EXPERTS)
                def _():
                    for cp in weight_copies(nxt, 1 - slot):
                        cp.start()

                prepare(slot)

    @pl.when(valid)
    def _():
        x = jnp.concatenate(_unpack_words(_from_tiles(x_ref)), axis=1).astype(BF16)
        hg = jnp.dot(x, wg_sc[...], preferred_element_type=F32) + bg_ref[0]
        hl = jnp.dot(x, wl_sc[...], preferred_element_type=F32) + bl_ref[0]
        g = jnp.minimum(hg, SWIGLU_LIMIT)
        lin = jnp.clip(hl, -SWIGLU_LIMIT, SWIGLU_LIMIT)
        act = g * (1.0 / (1.0 + jnp.exp(-SWIGLU_ALPHA * g))) * (lin + 1.0)
        y = jnp.dot(act.astype(BF16), wd_sc[...], preferred_element_type=F32) + bd_ref[0]
        _to_tiles(y_ref, _pack_rows(y))


def _grouped_ffn(xs_tiles, texp, nvalid, ecnt, wgu, wd, bg, bl, bd, ntiles):
    d = wgu.shape[1]
    tm = MOE_TILE
    f = wd.shape[1]
    xmap = lambda i, te, nv, ec: (jnp.minimum(i, nv[0] - 1), 0, 0, 0)
    wmap = lambda i, te, nv, ec: (te[i], 0, 0)
    rows_blk = pl.BlockSpec((tm // SUBLANES, PACK_CHUNKS, SUBLANES, LANES), xmap)
    hbm = pl.BlockSpec(memory_space=pl.ANY)
    grid_spec = pltpu.PrefetchScalarGridSpec(
        num_scalar_prefetch=3,
        grid=(ntiles,),
        in_specs=[rows_blk, hbm, hbm,
                  pl.BlockSpec((1, 1, f), wmap), pl.BlockSpec((1, 1, f), wmap), pl.BlockSpec((1, 1, d), wmap)],
        out_specs=rows_blk,
        scratch_shapes=[pltpu.VMEM((2, d, 2 * f), F32), pltpu.VMEM((2, f, d), F32),
                        pltpu.VMEM((d, f), BF16), pltpu.VMEM((d, f), BF16), pltpu.VMEM((f, d), BF16),
                        pltpu.SMEM((1,), I32), pltpu.SemaphoreType.DMA((2,))],
    )
    return pl.pallas_call(
        _ffn_body,
        grid_spec=grid_spec,
        out_shape=jax.ShapeDtypeStruct(xs_tiles.shape, I32),
        compiler_params=pltpu.CompilerParams(dimension_semantics=("arbitrary",), vmem_limit_bytes=VMEM_LIMIT),
        name="ffn",
    )(texp, nvalid, ecnt, xs_tiles, wgu, wd, bg, bl, bd)


COMBINE_SUB = 512


def _combine_body(lpb_ref, gate_ref, gstart_ref, nslab_ref, lstart_ref, gstart2_ref, nslab2_ref, lstart2_ref,
                  h_ref, l2g_ref, l2b_ref, ys_ref, o_ref, stg0_ref, stg1_ref, moe_ref, sem):
    g = pl.program_id(0)
    ng = pl.num_programs(0)
    stages = (stg0_ref, stg1_ref)

    def fetch(gs_ref, ns_ref, ls_ref, j):
        def per_expert(e, c):
            _copy_run(ys_ref, gs_ref[0, 0, e] * PSLAB, stages[j], ls_ref[0, 0, e] * PSLAB, ns_ref[0, 0, e], sem.at[j])
            return c

        lax.fori_loop(0, N_EXPERTS, per_expert, 0)

    def wait_tile(j):
        nbig, nsmall = _chunk_counts(nslab_ref, 0)
        _wait_runs(ys_ref, stages[j], nbig, nsmall, sem.at[j])

    def gather(j):
        stg_ref = stages[j]

        def body(i, c):
            for u in range(SUBLANES):
                tok = i * SUBLANES + u
                acc_lo = acc_hi = None
                for k in range(TOP_K):
                    gk = gate_ref[0, 0, k * TOKEN_TILE + tok]
                    lo, hi = _unpack_words(_load_row(stg_ref, lpb_ref[0, 0, k * TOKEN_TILE + tok], PACK_CHUNKS))
                    acc_lo = gk * lo if k == 0 else acc_lo + gk * lo
                    acc_hi = gk * hi if k == 0 else acc_hi + gk * hi
                _store_row(moe_ref, i * SLAB + u, acc_lo)
                _store_row(moe_ref, i * SLAB + PSLAB + u, acc_hi)
            return c

        lax.fori_loop(0, TOKEN_TILE // SUBLANES, body, 0)

    def finish():
        nblk = COMBINE_SUB // SUBLANES
        for part in range(TOKEN_TILE // COMBINE_SUB):
            moe = jnp.concatenate(
                [jnp.concatenate([moe_ref[pl.ds((part * nblk + i) * SLAB + s * SUBLANES, SUBLANES), :]
                                  for i in range(nblk)], axis=0) for s in range(SUBLANES)], axis=1)
            h1 = _from_tiles(h_ref.at[pl.ds(part * nblk, nblk)])
            o_ref[pl.ds(part * COMBINE_SUB, COMBINE_SUB), :] = _layernorm(DEEPNORM_ALPHA * h1 + moe, l2g_ref[...],
                                                                          l2b_ref[...])

    @pl.when(g == 0)
    def _():
        fetch(gstart_ref, nslab_ref, lstart_ref, 0)

    for j in range(2):
        @pl.when(lax.rem(g, 2) == j)
        def _():
            @pl.when(g + 1 < ng)
            def _():
                fetch(gstart2_ref, nslab2_ref, lstart2_ref, 1 - j)

            wait_tile(j)
            gather(j)

    finish()


def _combine(lpb, gates, gstart, nslab, lstart, h1_tiles, ys_flat, w):
    d = h1_tiles.shape[1] * LANES
    n = h1_tiles.shape[0] * SUBLANES
    t = TOKEN_TILE
    ng = n // t
    full = lambda shape: pl.BlockSpec(shape, lambda i: (0,) * len(shape))
    smem = lambda shape, imap: pl.BlockSpec(shape, imap, memory_space=pltpu.SMEM)
    vec = smem((1, 1, TOP_K * t), lambda i: (i, 0, 0))
    this_step = smem((1, 1, LANES), lambda i: (i, 0, 0))
    next_step = smem((1, 1, LANES), lambda i: (jnp.minimum(i + 1, ng - 1), 0, 0))
    return pl.pallas_call(
        _combine_body,
        grid=(ng,),
        in_specs=[vec, vec, this_step, this_step, this_step, next_step, next_step, next_step,
                  pl.BlockSpec((t // SUBLANES, d // LANES, SUBLANES, LANES), lambda i: (i, 0, 0, 0)),
                  full((1, d)), full((1, d)),
                  pl.BlockSpec(memory_space=pl.ANY)],
        out_specs=pl.BlockSpec((t, d), lambda i: (i, 0)),
        out_shape=jax.ShapeDtypeStruct((n, d), F32),
        scratch_shapes=[pltpu.VMEM((STAGE_FLAT, LANES), I32), pltpu.VMEM((STAGE_FLAT, LANES), I32),
                        pltpu.VMEM((TOKEN_TILE * SUBLANES, LANES), F32), pltpu.SemaphoreType.DMA((2,))],
        compiler_params=pltpu.CompilerParams(dimension_semantics=("arbitrary",), vmem_limit_bytes=VMEM_LIMIT),
        name="combine",
    )(lpb, gates, gstart, nslab, lstart, gstart, nslab, lstart, h1_tiles, w["ln2_g"], w["ln2_b"], ys_flat)


def _rope_tables(pos_1d, row, col):
    inv = ROPE_THETA ** (-jnp.arange(0, ROPE_DIM, 2, dtype=F32) / ROPE_DIM)

    def cs(p):
        ang = p.astype(F32)[:, None] * inv[None, :]
        return jnp.cos(ang), jnp.sin(ang)

    c1, s1 = cs(pos_1d)
    cr, sr = cs(row)
    cc, sc = cs(col)
    return (jnp.concatenate([c1, c1, c1, c1], axis=1), jnp.concatenate([-s1, s1, -s1, s1], axis=1),
            jnp.concatenate([cr, cr, cc, cc], axis=1), jnp.concatenate([-sr, sr, -sc, sc], axis=1))


def _prep_weights(ln_emb_g, ln_emb_b, w_in, g_q_a, w_q_b, g_kv_a, w_kv_b, g_q_gqa, g_k_gqa, g_o_mla, g_o_gqa, w_o,
                  ln1_g, ln1_b, w_router, b_router, ln2_g, ln2_b):
    r2 = lambda v: v.reshape(1, -1).astype(F32)
    o = np.cumsum([0, Q_LORA, KV_LORA, ROPE_DIM, GQA_HEADS * GQA_DIM, GQA_KV_HEADS * GQA_DIM, GQA_KV_HEADS * GQA_DIM])
    wi = w_in[0]
    seg = [wi[:, o[i]:o[i + 1]] for i in range(6)]
    kpe2 = jnp.concatenate([seg[2], seg[2]], axis=1)
    w_in_p = jnp.concatenate([seg[0], seg[1], kpe2, _swap_halves64(kpe2), seg[3], _swap_halves64(seg[3]),
                              seg[4], _swap_halves64(seg[4]), seg[5]], axis=1).astype(BF16)
    wq = w_q_b[0].reshape(Q_LORA, MLA_HEADS, QK_DIM)
    wq_rope = wq[:, :, NOPE_DIM:].reshape(Q_LORA, -1)
    w_qb = jnp.concatenate([wq[:, :, :NOPE_DIM].reshape(Q_LORA, -1), wq_rope, _swap_halves64(wq_rope)],
                           axis=1).astype(BF16)
    wk = w_kv_b[0].reshape(KV_LORA, MLA_HEADS, NOPE_DIM + V_DIM)
    w_kvb = jnp.concatenate([wk[:, :, :NOPE_DIM].reshape(KV_LORA, -1), wk[:, :, NOPE_DIM:].reshape(KV_LORA, -1)],
                            axis=1).astype(BF16)
    wr = jnp.pad(w_router[0].astype(F32), ((0, 0), (0, LANES - N_EXPERTS)))
    wr_hi = wr.astype(BF16)
    w_router_p = jnp.concatenate([wr_hi, (wr - wr_hi.astype(F32)).astype(BF16)], axis=1)
    b_router_p = jnp.pad(b_router[0].astype(F32), (0, LANES - N_EXPERTS), constant_values=NEG_BIG).reshape(1, LANES)
    return dict(
        ln_emb_g=r2(ln_emb_g), ln_emb_b=r2(ln_emb_b), w_in=w_in_p, g_q_a=r2(g_q_a[0]), w_qb=w_qb,
        g_kv_a=r2(g_kv_a[0]), w_kvb=w_kvb, g_q_gqa=r2(g_q_gqa[0]), g_k_gqa=r2(g_k_gqa[0]),
        g_q_gqa_sw=_swap_halves64(r2(g_q_gqa[0])), g_k_gqa_sw=_swap_halves64(r2(g_k_gqa[0])),
        g_o_mla=r2(g_o_mla[0]), g_o_gqa=r2(g_o_gqa[0]), w_o=w_o[0].astype(BF16), ln1_g=r2(ln1_g[0]),
        ln1_b=r2(ln1_b[0]), w_router=w_router_p, b_router=b_router_p, ln2_g=r2(ln2_g[0]), ln2_b=r2(ln2_b[0]))


def kernel(x, meta_tokens, ln_emb_g, ln_emb_b, w_in, g_q_a, w_q_b, g_kv_a, w_kv_b, g_q_gqa, g_k_gqa, g_o_mla, g_o_gqa,
           w_o, ln1_g, ln1_b, w_router, b_router, w_gate_up, b_gate_up, w_down, b_down, ln2_g, ln2_b):
    b, s, d = x.shape
    n = b * s
    w = _prep_weights(ln_emb_g, ln_emb_b, w_in, g_q_a, w_q_b, g_kv_a, w_kv_b, g_q_gqa, g_k_gqa, g_o_mla, g_o_gqa,
                      w_o, ln1_g, ln1_b, w_router, b_router, ln2_g, ln2_b)

    tok = jnp.arange(s, dtype=I32)
    tabs_real = _rope_tables(tok + N_META, tok // GRID_W, tok % GRID_W)
    mt = jnp.arange(META_PAD, dtype=I32)
    tabs_meta = _rope_tables(mt, jnp.full((META_PAD,), -1, I32), mt)

    qm, km, vm, qg, kg, vg = _project(x, tabs_real, w, ROW_TILE)
    meta = jnp.pad(meta_tokens.astype(x.dtype), ((0, META_PAD - N_META), (0, 0))).reshape(1, META_PAD, d)
    _, km_m, vm_m, _, kg_m, vg_m = _project(meta, tabs_meta, w, META_PAD)

    o_mla = _mla_attention(qm, km, vm, km_m, vm_m)
    o_gqa = _gqa_attention(qg, kg, vg, kg_m, vg_m)

    h1_tiles, h1_packed, topi, gates = _merge(o_mla, o_gqa, x.reshape(n, d), w)

    run_rows = n * TOP_K + (n // TOKEN_TILE) * N_EXPERTS * (SUBLANES - 1)
    ntiles = -(-run_rows // MOE_TILE) + N_EXPERTS
    ntp = -(-ntiles // LANES) * LANES
    rows_pad = ntiles * MOE_TILE
    lpb, gstart, nslab, lstart, texp, nvalid, ecnt, eoff = _positions(topi, ntp)
    flat = lambda a: a.reshape(-1, LANES)
    xs_flat = _dispatch(flat(h1_packed), lpb, gstart, nslab, lstart, ecnt.reshape(-1), eoff.reshape(-1), rows_pad)

    bgu = b_gate_up[0].astype(F32)
    tiles = lambda a: a.reshape(-1, PACK_CHUNKS, SUBLANES, LANES)
    ys_tiles = _grouped_ffn(tiles(xs_flat), texp.reshape(-1), nvalid.reshape(-1), ecnt.reshape(-1),
                            w_gate_up[0], w_down[0],
                            bgu[:, 0::2].reshape(N_EXPERTS, 1, D_FF), bgu[:, 1::2].reshape(N_EXPERTS, 1, D_FF),
                            b_down[0].astype(F32).reshape(N_EXPERTS, 1, d), ntiles)

    out = _combine(lpb, gates, gstart, nslab, lstart, h1_tiles, flat(ys_tiles), w)
    return out.reshape(b, s, d)
```

```python
import functools

import jax
import jax.numpy as jnp
import numpy as np
from jax import lax
from jax.experimental import pallas as pl
from jax.experimental.pallas import tpu as pltpu

D_MODEL = 1024
N_META = 16
GRID_W = 64
ROPE_THETA = 10000.0
MLA_HEADS = 4
Q_LORA = 256
KV_LORA = 128
NOPE_DIM = 128
ROPE_DIM = 64
V_DIM = 128
QK_DIM = NOPE_DIM + ROPE_DIM
GQA_HEADS = 4
GQA_KV_HEADS = 2
GQA_DIM = 128
N_EXPERTS = 32
TOP_K = 4
D_FF = D_MODEL
SWIGLU_LIMIT = 7.0
SWIGLU_ALPHA = 1.702
RMS_EPS = 1e-6
LN_EPS = 1e-5
DEPTH = 1
DEEPNORM_ALPHA = (2.0 * DEPTH) ** 0.25

LANES = 128
SUBLANES = 8
SLAB = SUBLANES * SUBLANES
PACK_CHUNKS = D_MODEL // 2 // LANES
PSLAB = SUBLANES * PACK_CHUNKS
META_PAD = 128
MLA_K = 2 * LANES
NEG_BIG = -1e30
LOG2E = 1.4426950408889634
V_EXT = 2 * LANES

ROW_TILE = 512
Q_TILE_MLA = 1024
Q_TILE_GQA = 512
Q_SUB_MLA = 128
Q_SUB_GQA = 512
KEY_BLOCK = 512
TOKEN_TILE = 1024
MOE_TILE = 512
VMEM_LIMIT = 56 * 1024 * 1024

F32 = jnp.float32
BF16 = jnp.bfloat16
I32 = jnp.int32


def _layernorm(x, g, b):
    mu = jnp.mean(x, axis=-1, keepdims=True)
    xc = x - mu
    var = jnp.mean(xc * xc, axis=-1, keepdims=True)
    return xc * lax.rsqrt(var + LN_EPS) * g + b


def _rmsnorm(x, g):
    return x * lax.rsqrt(jnp.mean(x * x, axis=-1, keepdims=True) + RMS_EPS) * g


def _swap_halves64(v):
    shp = v.shape
    return v.reshape(shp[:-1] + (shp[-1] // 64, 2, 32))[..., ::-1, :].reshape(shp)


PROJ_SUB = 256


def _proj_chain(x, lng, lnb, win_ref, gqa, wqb_ref, gkva, wkvb_ref, gqg, gqgs, gkg, gkgs, c1, s1, ca, sa, put):
    r = x.shape[0]
    h0 = _layernorm(x, lng, lnb)
    z = jnp.dot(h0.astype(BF16), win_ref[...], preferred_element_type=F32)
    q_a = z[:, 0:256]
    kv_a = z[:, 256:384]
    kpe2 = z[:, 384:512]
    kpe2s = z[:, 512:640]
    q_g = z[:, 640:1152]
    q_gs = z[:, 1152:1664]
    k_g = z[:, 1664:1920]
    k_gs = z[:, 1920:2176]
    v_g = z[:, 2176:2432]

    q = jnp.dot(_rmsnorm(q_a, gqa).astype(BF16), wqb_ref[...], preferred_element_type=F32)
    kv = jnp.dot(_rmsnorm(kv_a, gkva).astype(BF16), wkvb_ref[...], preferred_element_type=F32)
    krot = kpe2 * c1 + kpe2s * s1
    lane = lax.broadcasted_iota(I32, krot.shape, 1)
    scale_a = QK_DIM ** -0.5 * LOG2E
    ones = jnp.ones((r, LANES), F32)
    for c in range(MLA_HEADS // 2):
        lo = LANES * c
        qr = q[:, 512 + lo:512 + lo + LANES] * c1 + q[:, 768 + lo:768 + lo + LANES] * s1
        for hh in range(2):
            h = 2 * c + hh
            slot = jnp.where((lane // 64) == hh, qr, 0.0)
            put("qm", h, (jnp.concatenate([q[:, LANES * h:LANES * (h + 1)], slot], axis=1) * scale_a).astype(BF16))
    for h in range(MLA_HEADS):
        put("km", h, jnp.concatenate([kv[:, LANES * h:LANES * (h + 1)], krot], axis=1).T.astype(BF16))
        put("vm", h, jnp.concatenate([kv[:, 512 + LANES * h:512 + LANES * (h + 1)], ones], axis=1).astype(BF16))

    scale_b = GQA_DIM ** -0.5 * LOG2E
    cq, sq = ca * gqg, sa * gqgs
    ck, sk = ca * gkg, sa * gkgs

    def norm_rot(xh, xs, cg, sg, scale):
        inv = lax.rsqrt(jnp.mean(xh * xh, axis=-1, keepdims=True) + RMS_EPS)
        return (xh * cg + xs * sg) * (inv * scale)

    for h in range(GQA_HEADS):
        sl = slice(LANES * h, LANES * (h + 1))
        put("qg", h, norm_rot(q_g[:, sl], q_gs[:, sl], cq, sq, scale_b).astype(BF16))
    for j in range(GQA_KV_HEADS):
        sl = slice(LANES * j, LANES * (j + 1))
        put("kg", j, norm_rot(k_g[:, sl], k_gs[:, sl], ck, sk, 1.0).T.astype(BF16))
        put("vg", j, jnp.concatenate([v_g[:, sl], ones], axis=1).astype(BF16))


def _proj_body(x_ref, lng_ref, lnb_ref, win_ref, gqa_ref, wqb_ref, gkva_ref, wkvb_ref, gqg_ref, gqgs_ref, gkg_ref,
               gkgs_ref, c1_ref, s1_ref, ca_ref, sa_ref,
               qm_ref, km_ref, vm_ref, qg_ref, kg_ref, vg_ref):
    outs = dict(qm=qm_ref, km=km_ref, vm=vm_ref, qg=qg_ref, kg=kg_ref, vg=vg_ref)
    tile = x_ref.shape[1]
    sub = min(PROJ_SUB, tile)
    for i in range(tile // sub):
        rows = pl.ds(i * sub, sub)

        def put(name, h, val):
            if name in ("km", "kg"):
                outs[name][0, h, :, rows] = val
            else:
                outs[name][0, h, rows, :] = val

        _proj_chain(x_ref[0, rows, :], lng_ref[...], lnb_ref[...], win_ref, gqa_ref[...], wqb_ref, gkva_ref[...],
                    wkvb_ref, gqg_ref[...], gqgs_ref[...], gkg_ref[...], gkgs_ref[...],
                    c1_ref[rows, :], s1_ref[rows, :], ca_ref[rows, :], sa_ref[rows, :], put)


def _project(x3, tabs, w, tile):
    b, s, d = x3.shape
    nst = s // tile
    full = lambda shape: pl.BlockSpec(shape, lambda bi, si: (0,) * len(shape))
    tab = pl.BlockSpec((tile, LANES), lambda bi, si: (si, 0))
    hm = lambda nh, dd: pl.BlockSpec((1, nh, tile, dd), lambda bi, si: (bi, 0, si, 0))
    hmt = lambda nh, dd: pl.BlockSpec((1, nh, dd, tile), lambda bi, si: (bi, 0, 0, si))
    out_shape = [
        jax.ShapeDtypeStruct((b, MLA_HEADS, s, MLA_K), BF16),
        jax.ShapeDtypeStruct((b, MLA_HEADS, MLA_K, s), BF16),
        jax.ShapeDtypeStruct((b, MLA_HEADS, s, V_EXT), BF16),
        jax.ShapeDtypeStruct((b, GQA_HEADS, s, GQA_DIM), BF16),
        jax.ShapeDtypeStruct((b, GQA_KV_HEADS, GQA_DIM, s), BF16),
        jax.ShapeDtypeStruct((b, GQA_KV_HEADS, s, V_EXT), BF16),
    ]
    return pl.pallas_call(
        _proj_body,
        grid=(b, nst),
        in_specs=[
            pl.BlockSpec((1, tile, d), lambda bi, si: (bi, si, 0)),
            full((1, d)), full((1, d)),
            full(w["w_in"].shape), full((1, Q_LORA)), full(w["w_qb"].shape),
            full((1, KV_LORA)), full(w["w_kvb"].shape),
            full((1, GQA_DIM)), full((1, GQA_DIM)), full((1, GQA_DIM)), full((1, GQA_DIM)),
            tab, tab, tab, tab,
        ],
        out_specs=[hm(MLA_HEADS, MLA_K), hmt(MLA_HEADS, MLA_K), hm(MLA_HEADS, V_EXT),
                   hm(GQA_HEADS, GQA_DIM), hmt(GQA_KV_HEADS, GQA_DIM), hm(GQA_KV_HEADS, V_EXT)],
        out_shape=out_shape,
        compiler_params=pltpu.CompilerParams(dimension_semantics=("parallel", "parallel"),
                                             vmem_limit_bytes=VMEM_LIMIT),
        name="proj",
    )(x3, w["ln_emb_g"], w["ln_emb_b"], w["w_in"], w["g_q_a"], w["w_qb"], w["g_kv_a"], w["w_kvb"],
      w["g_q_gqa"], w["g_q_gqa_sw"], w["g_k_gqa"], w["g_k_gqa_sw"], *tabs)


def _softmax_pv(q, kt, v, kmt, vm):
    sm = jnp.dot(q, kmt, preferred_element_type=F32)
    col = lax.broadcasted_iota(I32, sm.shape, 1)
    sm = jnp.where(col < N_META, sm, NEG_BIG)
    nblk = kt.shape[1] // KEY_BLOCK
    blocks = [jnp.dot(q, kt[:, KEY_BLOCK * c:KEY_BLOCK * (c + 1)], preferred_element_type=F32) for c in range(nblk)]
    mx = blocks[0]
    for c in range(1, nblk):
        mx = jnp.maximum(mx, blocks[c])
    m = jnp.maximum(jnp.max(mx, axis=1, keepdims=True), jnp.max(sm, axis=1, keepdims=True))
    acc = jnp.dot(jnp.exp2(sm - m).astype(BF16), vm, preferred_element_type=F32)
    for c in range(nblk):
        p = jnp.exp2(blocks[c] - m).astype(BF16)
        acc = acc + jnp.dot(p, v[KEY_BLOCK * c:KEY_BLOCK * (c + 1), :], preferred_element_type=F32)
    return acc[:, 0:V_DIM] / acc[:, V_DIM:V_EXT]


def _mla_attn_body(q_ref, k_ref, v_ref, km_ref, vm_ref, o_ref):
    for i in range(q_ref.shape[2] // Q_SUB_MLA):
        rows = pl.ds(i * Q_SUB_MLA, Q_SUB_MLA)
        o = _softmax_pv(q_ref[0, 0, rows, :], k_ref.at[0, 0], v_ref.at[0, 0], km_ref[0, 0], vm_ref[0, 0])
        o_ref[rows, :] = o.astype(o_ref.dtype)


def _gqa_attn_body(q_ref, k_ref, v_ref, km_ref, vm_ref, o_ref):
    for g in range(2):
        for i in range(q_ref.shape[2] // Q_SUB_GQA):
            rows = pl.ds(i * Q_SUB_GQA, Q_SUB_GQA)
            o = _softmax_pv(q_ref[0, g, rows, :], k_ref.at[0, 0], v_ref.at[0, 0], km_ref[0, 0], vm_ref[0, 0])
            o_ref[rows, GQA_DIM * g:GQA_DIM * (g + 1)] = o.astype(o_ref.dtype)


def _mla_attention(qm, kmt, vm, kmeta_t, vmeta):
    b, h, s, dk = qm.shape
    tq = Q_TILE_MLA
    nq = s // tq
    return pl.pallas_call(
        _mla_attn_body,
        grid=(b, h, nq),
        in_specs=[
            pl.BlockSpec((1, 1, tq, dk), lambda bi, hi, qi: (bi, hi, qi, 0)),
            pl.BlockSpec((1, 1, dk, s), lambda bi, hi, qi: (bi, hi, 0, 0)),
            pl.BlockSpec((1, 1, s, V_EXT), lambda bi, hi, qi: (bi, hi, 0, 0)),
            pl.BlockSpec((1, 1, dk, META_PAD), lambda bi, hi, qi: (0, hi, 0, 0)),
            pl.BlockSpec((1, 1, META_PAD, V_EXT), lambda bi, hi, qi: (0, hi, 0, 0)),
        ],
        out_specs=pl.BlockSpec((tq, V_DIM), lambda bi, hi, qi: (bi * nq + qi, hi)),
        out_shape=jax.ShapeDtypeStruct((b * s, h * V_DIM), BF16),
        compiler_params=pltpu.CompilerParams(dimension_semantics=("parallel", "parallel", "parallel"),
                                             vmem_limit_bytes=VMEM_LIMIT),
        name="mla_attn",
    )(qm, kmt, vm, kmeta_t, vmeta)


def _gqa_attention(qg, kgt, vg, kmeta_t, vmeta):
    b, h, s, d = qg.shape
    hk = kgt.shape[1]
    tq = Q_TILE_GQA
    nq = s // tq
    return pl.pallas_call(
        _gqa_attn_body,
        grid=(b, hk, nq),
        in_specs=[
            pl.BlockSpec((1, 2, tq, d), lambda bi, ji, qi: (bi, ji, qi, 0)),
            pl.BlockSpec((1, 1, d, s), lambda bi, ji, qi: (bi, ji, 0, 0)),
            pl.BlockSpec((1, 1, s, V_EXT), lambda bi, ji, qi: (bi, ji, 0, 0)),
            pl.BlockSpec((1, 1, d, META_PAD), lambda bi, ji, qi: (0, ji, 0, 0)),
            pl.BlockSpec((1, 1, META_PAD, V_EXT), lambda bi, ji, qi: (0, ji, 0, 0)),
        ],
        out_specs=pl.BlockSpec((tq, 2 * d), lambda bi, ji, qi: (bi * nq + qi, ji)),
        out_shape=jax.ShapeDtypeStruct((b * s, h * d), BF16),
        compiler_params=pltpu.CompilerParams(dimension_semantics=("parallel", "parallel", "parallel"),
                                             vmem_limit_bytes=VMEM_LIMIT),
        name="gqa_attn",
    )(qg, kgt, vg, kmeta_t, vmeta)


def _to_tiles(ref, x):
    r = x.shape[0]
    for s in range(x.shape[1] // LANES):
        ref[:, s] = x[:, LANES * s:LANES * (s + 1)].reshape(r // SUBLANES, SUBLANES, LANES)


def _from_tiles(ref):
    r = ref.shape[0] * SUBLANES
    return jnp.concatenate([ref[:, s].reshape(r, LANES) for s in range(ref.shape[1])], axis=1)


MERGE_SUB = 256


def _pack_rows(v):
    c = v.shape[1] // 2
    lo = lax.bitcast_convert_type(v[:, 0:c].astype(BF16).astype(F32), I32)
    hi = lax.bitcast_convert_type(v[:, c:2 * c].astype(BF16).astype(F32), I32)
    return lax.shift_right_logical(lo, 16) | hi


def _unpack_words(w):
    return (lax.bitcast_convert_type(w << 16, F32), lax.bitcast_convert_type(w & jnp.int32(-65536), F32))


def _merge_body(om_ref, og_ref, x_ref, lng_ref, lnb_ref, gom_ref, gog_ref, wo_ref, l1g_ref, l1b_ref,
                wr_ref, br_ref, h1_ref, h1p_ref, topi_ref, gate_ref):
    for i in range(x_ref.shape[0] // MERGE_SUB):
        rows = pl.ds(i * MERGE_SUB, MERGE_SUB)
        h1, ti, gt = _merge_chain(om_ref[rows, :], og_ref[rows, :], x_ref[rows, :], lng_ref[...], lnb_ref[...],
                                  gom_ref[...], gog_ref[...], wo_ref, l1g_ref[...], l1b_ref[...], wr_ref, br_ref[...])
        blocks = pl.ds(i * (MERGE_SUB // SUBLANES), MERGE_SUB // SUBLANES)
        _to_tiles(h1_ref.at[blocks], h1)
        _to_tiles(h1p_ref.at[blocks], _pack_rows(h1))
        topi_ref[:, rows] = ti
        for k in range(TOP_K):
            gate_ref[0, :, pl.ds(k * TOKEN_TILE + i * MERGE_SUB, MERGE_SUB)] = gt[k:k + 1, :]


def _merge_chain(om, og, x, lng, lnb, gom, gog, wo_ref, l1g, l1b, wr_ref, br):
    t = x.shape[0]
    h0 = _layernorm(x, lng, lnb)
    nm = _rmsnorm(om.astype(F32), gom).astype(BF16)
    ng = _rmsnorm(og.astype(F32), gog).astype(BF16)
    half = nm.shape[1]
    mix = jnp.dot(nm, wo_ref[0:half, :], preferred_element_type=F32)
    mix = mix + jnp.dot(ng, wo_ref[half:2 * half, :], preferred_element_type=F32)
    h1 = _layernorm(DEEPNORM_ALPHA * h0 + mix, l1g, l1b)

    hi = h1.astype(BF16)
    lo = (h1 - hi.astype(F32)).astype(BF16)
    acc = jnp.dot(hi, wr_ref[...], preferred_element_type=F32) + jnp.dot(lo, wr_ref[...], preferred_element_type=F32)
    logits = acc[:, 0:LANES] + acc[:, LANES:2 * LANES] + br
    cur = logits.T[0:N_EXPERTS, :]
    eidx = lax.broadcasted_iota(I32, cur.shape, 0)
    vals, idxs = [], []
    for _ in range(TOP_K):
        m = jnp.max(cur, axis=0, keepdims=True)
        i = jnp.min(jnp.where(cur == m, eidx, N_EXPERTS), axis=0, keepdims=True)
        vals.append(m)
        idxs.append(i)
        cur = jnp.where(eidx == i, -jnp.inf, cur)
    ex = [jnp.exp(v - vals[0]) for v in vals]
    den = ex[0] + ex[1] + ex[2] + ex[3]
    sub = lax.broadcasted_iota(I32, (8, t), 0)
    ti = jnp.zeros((8, t), I32)
    gt = jnp.zeros((8, t), F32)
    for k in range(TOP_K):
        ti = jnp.where(sub == k, idxs[k], ti)
        gt = jnp.where(sub == k, ex[k] / den, gt)
    return h1, ti[0:TOP_K, :], gt[0:TOP_K, :]


def _merge(o_mla, o_gqa, x2, w):
    n, d = x2.shape
    tile = TOKEN_TILE
    half = o_mla.shape[1]
    full = lambda shape: pl.BlockSpec(shape, lambda i: (0,) * len(shape))
    row = lambda width: pl.BlockSpec((tile, width), lambda i: (i, 0))
    return pl.pallas_call(
        _merge_body,
        grid=(n // tile,),
        in_specs=[row(half), row(half), row(d), full((1, d)), full((1, d)), full((1, half)), full((1, half)),
                  full((d, d)), full((1, d)), full((1, d)), full((d, 2 * LANES)), full((1, LANES))],
        out_specs=[pl.BlockSpec((tile // SUBLANES, d // LANES, SUBLANES, LANES), lambda i: (i, 0, 0, 0)),
                   pl.BlockSpec((tile // SUBLANES, PACK_CHUNKS, SUBLANES, LANES), lambda i: (i, 0, 0, 0)),
                   pl.BlockSpec((TOP_K, tile), lambda i: (0, i)),
                   pl.BlockSpec((1, 1, TOP_K * tile), lambda i: (i, 0, 0))],
        out_shape=[jax.ShapeDtypeStruct((n // SUBLANES, d // LANES, SUBLANES, LANES), F32),
                   jax.ShapeDtypeStruct((n // SUBLANES, PACK_CHUNKS, SUBLANES, LANES), I32),
                   jax.ShapeDtypeStruct((TOP_K, n), I32), jax.ShapeDtypeStruct((n // tile, 1, TOP_K * tile), F32)],
        compiler_params=pltpu.CompilerParams(dimension_semantics=("parallel",), vmem_limit_bytes=VMEM_LIMIT),
        name="merge",
    )(o_mla, o_gqa, x2, w["ln_emb_g"], w["ln_emb_b"], w["g_o_mla"], w["g_o_gqa"], w["w_o"],
      w["ln1_g"], w["ln1_b"], w["w_router"], w["b_router"])


def _lanes_from_sublanes(col):
    diag = lax.broadcasted_iota(I32, col.shape, 0) == lax.broadcasted_iota(I32, col.shape, 1)
    return jnp.sum(jnp.where(diag, col, 0.0), axis=0, keepdims=True)


def _positions_body(topi_ref, upper_ref, lpb_ref, gstart_ref, nslab_ref, lstart_ref, texp_ref, nvalid_ref, ecnt_ref,
                    eoff_ref, cnt_sc, carry_sc, off_sc, *, ntp):
    p = pl.program_id(0)
    j = pl.program_id(1)
    tl = topi_ref.shape[1]
    topi = topi_ref[...]
    eidx = lax.broadcasted_iota(I32, (N_EXPERTS, tl), 0)
    ohs = [eidx == topi[k:k + 1, :] for k in range(TOP_K)]
    onehot = ohs[0].astype(F32) + ohs[1].astype(F32) + ohs[2].astype(F32) + ohs[3].astype(F32)
    cnt = jnp.sum(onehot, axis=1, keepdims=True)
    run = jnp.floor((cnt + (SUBLANES - 1)) * (1.0 / SUBLANES)) * SUBLANES
    tile_run = jnp.broadcast_to(run, (N_EXPERTS, LANES))
    r = lax.broadcasted_iota(I32, (N_EXPERTS, N_EXPERTS), 0)
    c = lax.broadcasted_iota(I32, (N_EXPERTS, N_EXPERTS), 1)
    lower = (c < r).astype(F32)

    @pl.when(jnp.logical_and(p == 0, j == 0))
    def _():
        cnt_sc[...] = jnp.zeros_like(cnt_sc)

    @pl.when(p == 0)
    def _():
        cnt_sc[...] += tile_run

    @pl.when(jnp.logical_and(p == 1, j == 0))
    def _():
        tot = cnt_sc[...]
        pc = jnp.floor((tot + (MOE_TILE - 1)) * (1.0 / MOE_TILE)) * MOE_TILE
        off = jnp.dot(lower, pc, precision=lax.Precision.HIGHEST, preferred_element_type=F32)
        off_sc[...] = off
        carry_sc[...] = jnp.zeros_like(carry_sc)
        cumend = off + pc
        tstart = lax.broadcasted_iota(I32, (N_EXPERTS, ntp), 1).astype(F32) * MOE_TILE
        te = jnp.sum((jnp.broadcast_to(cumend[:, 0:1], (N_EXPERTS, ntp)) <= tstart).astype(I32), axis=0, keepdims=True)
        texp_ref[...] = jnp.minimum(te, N_EXPERTS - 1)
        nvalid_ref[...] = (cumend[N_EXPERTS - 1:N_EXPERTS, :] * (1.0 / MOE_TILE)).astype(I32)
        ecnt_ref[...] = _lanes_from_sublanes(tot).astype(I32)
        eoff_ref[...] = _lanes_from_sublanes(off).astype(I32)

    @pl.when(p == 1)
    def _():
        before = jnp.dot(onehot.astype(BF16), upper_ref[...], preferred_element_type=F32)
        loff = jnp.dot(lower, tile_run, precision=lax.Precision.HIGHEST, preferred_element_type=F32)
        base = before + loff[:, 0:1]
        sub = lax.broadcasted_iota(I32, (SUBLANES, tl), 0)
        out = jnp.zeros((SUBLANES, tl), F32)
        for k in range(TOP_K):
            pk = jnp.sum(jnp.where(ohs[k], base, 0.0), axis=0, keepdims=True)
            out = jnp.where(sub == k, pk, out)
        lp = out[0:TOP_K, :].astype(I32)
        lpb = (lp >> 3) * PSLAB + (lp & (SUBLANES - 1))
        lpb_ref[0] = jnp.concatenate([lpb[k:k + 1, :] for k in range(TOP_K)], axis=1)
        inv = 1.0 / SUBLANES
        gstart_ref[0] = (_lanes_from_sublanes(off_sc[...] + carry_sc[...]) * inv).astype(I32)
        nslab_ref[0] = (_lanes_from_sublanes(tile_run) * inv).astype(I32)
        lstart_ref[0] = (_lanes_from_sublanes(loff) * inv).astype(I32)
        carry_sc[...] += tile_run


def _positions(topi, ntp):
    n = topi.shape[1]
    tl = TOKEN_TILE
    const = lambda shape: pl.BlockSpec(shape, lambda p, j: (0, 0))
    per_tile = pl.BlockSpec((1, 1, LANES), lambda p, j: (j * p, 0, 0))
    tab = jax.ShapeDtypeStruct((n // tl, 1, LANES), I32)
    upper = (jnp.arange(tl, dtype=I32)[:, None] < jnp.arange(tl, dtype=I32)[None, :]).astype(BF16)
    return pl.pallas_call(
        functools.partial(_positions_body, ntp=ntp),
        grid=(2, n // tl),
        in_specs=[pl.BlockSpec((TOP_K, tl), lambda p, j: (0, j)), const((tl, tl))],
        out_specs=[pl.BlockSpec((1, 1, TOP_K * tl), lambda p, j: (j * p, 0, 0)), per_tile, per_tile, per_tile,
                   const((1, ntp)), const((1, LANES)), const((1, LANES)), const((1, LANES))],
        out_shape=[jax.ShapeDtypeStruct((n // tl, 1, TOP_K * tl), I32), tab, tab, tab,
                   jax.ShapeDtypeStruct((1, ntp), I32),
                   jax.ShapeDtypeStruct((1, LANES), I32), jax.ShapeDtypeStruct((1, LANES), I32),
                   jax.ShapeDtypeStruct((1, LANES), I32)],
        scratch_shapes=[pltpu.VMEM((N_EXPERTS, LANES), F32)] * 3,
        compiler_params=pltpu.CompilerParams(dimension_semantics=("arbitrary", "arbitrary"),
                                             vmem_limit_bytes=VMEM_LIMIT),
        name="positions",
    )(topi, upper)


STAGE_ROWS = TOKEN_TILE * TOP_K + N_EXPERTS * SUBLANES
STAGE_FLAT = STAGE_ROWS * PACK_CHUNKS


def _load_row(ref, flat_start, chunks=SUBLANES):
    return ref[pl.ds(flat_start, chunks, stride=SUBLANES), :]


def _store_row(ref, flat_start, v):
    ref[pl.ds(flat_start, v.shape[0], stride=SUBLANES), :] = v


CHUNK_SLABS = 4


def _copy_run(src_ref, src0, dst_ref, dst0, nslab, sem):
    big = CHUNK_SLABS * PSLAB
    nbig = nslab >> 2
    nsmall = nslab & (CHUNK_SLABS - 1)

    def start_big(s, c):
        pltpu.make_async_copy(src_ref.at[pl.ds(src0 + s * big, big)], dst_ref.at[pl.ds(dst0 + s * big, big)], sem).start()
        return c

    def start_small(s, c):
        o = nbig * big + s * PSLAB
        pltpu.make_async_copy(src_ref.at[pl.ds(src0 + o, PSLAB)], dst_ref.at[pl.ds(dst0 + o, PSLAB)], sem).start()
        return c

    lax.fori_loop(0, nbig, start_big, 0)
    lax.fori_loop(0, nsmall, start_small, 0)


def _wait_runs(src_ref, dst_ref, nbig, nsmall, sem):
    big = CHUNK_SLABS * PSLAB

    def wait_big(s, c):
        pltpu.make_async_copy(src_ref.at[pl.ds(0, big)], dst_ref.at[pl.ds(0, big)], sem).wait()
        return c

    def wait_small(s, c):
        pltpu.make_async_copy(src_ref.at[pl.ds(0, PSLAB)], dst_ref.at[pl.ds(0, PSLAB)], sem).wait()
        return c

    lax.fori_loop(0, nbig, wait_big, 0)
    lax.fori_loop(0, nsmall, wait_small, 0)


def _chunk_counts(nslab_ref, j):
    def add(e, c):
        n = nslab_ref[j, 0, e]
        return c[0] + (n >> 2), c[1] + (n & (CHUNK_SLABS - 1))

    return lax.fori_loop(0, N_EXPERTS, add, (jnp.int32(0), jnp.int32(0)))


def _dispatch_body(ecnt_ref, eoff_ref, lpb_ref, gstart_ref, nslab_ref, lstart_ref, h_ref, xs_ref,
                   stg0_ref, stg1_ref, zero_ref, issued_ref, sem, zsem):
    g = pl.program_id(0)
    ng = pl.num_programs(0)
    stages = (stg0_ref, stg1_ref)

    def drain(j):
        _wait_runs(stages[j], xs_ref, issued_ref[2 * j], issued_ref[2 * j + 1], sem.at[j])

    def run(j):
        stg_ref = stages[j]

        @pl.when(g >= 2)
        def _():
            drain(j)

        def zero_last(e, c):
            last = jnp.maximum(lstart_ref[0, 0, e] + nslab_ref[0, 0, e] - 1, 0)
            stg_ref[pl.ds(last * PSLAB, PSLAB), :] = jnp.zeros((PSLAB, LANES), I32)
            return c

        lax.fori_loop(0, N_EXPERTS, zero_last, 0)

        def move(i, c):
            for u in range(SUBLANES):
                v = _load_row(h_ref, i * PSLAB + u, PACK_CHUNKS)
                for k in range(TOP_K):
                    _store_row(stg_ref, lpb_ref[0, 0, k * TOKEN_TILE + i * SUBLANES + u], v)
            return c

        lax.fori_loop(0, TOKEN_TILE // SUBLANES, move, 0)

        def send_run(e, c):
            _copy_run(stg_ref, lstart_ref[0, 0, e] * PSLAB, xs_ref, gstart_ref[0, 0, e] * PSLAB,
                      nslab_ref[0, 0, e], sem.at[j])
            return c

        lax.fori_loop(0, N_EXPERTS, send_run, 0)
        nbig, nsmall = _chunk_counts(nslab_ref, 0)
        issued_ref[2 * j] = nbig
        issued_ref[2 * j + 1] = nsmall

    for j in range(2):
        @pl.when(lax.rem(g, 2) == j)
        def _():
            run(j)

    @pl.when(g == ng - 1)
    def _():
        zero_ref[...] = jnp.zeros_like(zero_ref)

        def zero_copy(dst_slab):
            return pltpu.make_async_copy(zero_ref, xs_ref.at[pl.ds(dst_slab * PSLAB, PSLAB)], zsem)

        def pad_segment(e, c):
            rows = ecnt_ref[e]
            first = (eoff_ref[e] + rows) >> 3
            npad = lax.rem(MOE_TILE - lax.rem(rows, MOE_TILE), MOE_TILE) >> 3
            lax.fori_loop(0, npad, lambda s, cc: (zero_copy(first + s).start(), cc)[1], 0)
            lax.fori_loop(0, npad, lambda s, cc: (zero_copy(first + s).wait(), cc)[1], 0)
            return c

        lax.fori_loop(0, N_EXPERTS, pad_segment, 0)
        for j in range(2):
            @pl.when(jnp.logical_and(ng > 1, lax.rem(g, 2) != j))
            def _():
                drain(j)

        for j in range(2):
            @pl.when(lax.rem(g, 2) == j)
            def _():
                drain(j)


def _dispatch(h1_flat, lpb, gstart, nslab, lstart, ecnt, eoff, rows_pad):
    nflat = h1_flat.shape[0]
    t = TOKEN_TILE
    smem = lambda shape, imap: pl.BlockSpec(shape, imap, memory_space=pltpu.SMEM)
    per_tile = smem((1, 1, LANES), lambda i, c, o: (i, 0, 0))
    grid_spec = pltpu.PrefetchScalarGridSpec(
        num_scalar_prefetch=2,
        grid=(nflat // (t * PACK_CHUNKS),),
        in_specs=[smem((1, 1, TOP_K * t), lambda i, c, o: (i, 0, 0)), per_tile, per_tile, per_tile,
                  pl.BlockSpec((t * PACK_CHUNKS, LANES), lambda i, c, o: (i, 0))],
        out_specs=pl.BlockSpec(memory_space=pl.ANY),
        scratch_shapes=[pltpu.VMEM((STAGE_FLAT, LANES), I32), pltpu.VMEM((STAGE_FLAT, LANES), I32),
                        pltpu.VMEM((PSLAB, LANES), I32),
                        pltpu.SMEM((4,), I32), pltpu.SemaphoreType.DMA((2,)), pltpu.SemaphoreType.DMA],
    )
    return pl.pallas_call(
        _dispatch_body,
        grid_spec=grid_spec,
        out_shape=jax.ShapeDtypeStruct((rows_pad * PACK_CHUNKS, LANES), I32),
        compiler_params=pltpu.CompilerParams(dimension_semantics=("arbitrary",), vmem_limit_bytes=VMEM_LIMIT),
        name="dispatch",
    )(ecnt, eoff, lpb, gstart, nslab, lstart, h1_flat)


def _ffn_body(texp_ref, nvalid_ref, ecnt_ref, eoff_ref, x_ref, wgu_hbm, wd_hbm, bg_ref, bl_ref, bd_ref, y_ref,
              wgu_buf, wd_buf, wg_sc, wl_sc, wd_sc, seg_ref, sem):
    i = pl.program_id(0)
    valid = i < nvalid_ref[0]
    e = texp_ref[i]
    first_of_expert = jnp.logical_or(i == 0, e != texp_ref[jnp.maximum(i - 1, 0)])

    def weight_copies(expert, slot):
        return (pltpu.make_async_copy(wgu_hbm.at[expert], wgu_buf.at[slot], sem.at[slot]),
                pltpu.make_async_copy(wd_hbm.at[expert], wd_buf.at[slot], sem.at[slot]))

    def prepare(slot):
        blk = 2 * LANES
        r = lax.broadcasted_iota(I32, (blk, blk), 0)
        c = lax.broadcasted_iota(I32, (blk, blk), 1)
        sel = (r == jnp.where(c < LANES, 2 * c, 2 * (c - LANES) + 1)).astype(BF16)
        for b in range(wgu_buf.shape[2] // blk):
            wb = wgu_buf[slot, :, blk * b:blk * (b + 1)].astype(BF16)
            y = jnp.dot(wb, sel, preferred_element_type=F32).astype(BF16)
            wg_sc[:, LANES * b:LANES * (b + 1)] = y[:, 0:LANES]
            wl_sc[:, LANES * b:LANES * (b + 1)] = y[:, LANES:blk]
        wd_sc[...] = wd_buf[slot].astype(BF16)

    @pl.when(jnp.logical_and(valid, first_of_expert))
    def _():
        @pl.when(i == 0)
        def _():
            seg_ref[0] = 0
            for cp in weight_copies(e, 0):
                cp.start()

        seg = seg_ref[0]
        seg_ref[0] = seg + 1
        nxt = lax.while_loop(lambda n: jnp.logical_and(n < N_EXPERTS, ecnt_ref[jnp.minimum(n, N_EXPERTS - 1)] == 0),
                             lambda n: n + 1, e + 1)
        for slot in range(2):
            @pl.when(lax.rem(seg, 2) == slot)
            def _():
                for cp in weight_copies(e, slot):
                    cp.wait()

                @pl.when(nxt < N_EXPERTS)
                def _():
                    for cp in weight_copies(nxt, 1 - slot):
                        cp.start()

                prepare(slot)

    def compute(nrows):
        nb = nrows // SUBLANES
        x = jnp.concatenate(_unpack_words(_from_tiles(x_ref.at[pl.ds(0, nb)])), axis=1).astype(BF16)
        hg = jnp.dot(x, wg_sc[...], preferred_element_type=F32) + bg_ref[0]
        hl = jnp.dot(x, wl_sc[...], preferred_element_type=F32) + bl_ref[0]
        g = jnp.minimum(hg, SWIGLU_LIMIT)
        lin = jnp.clip(hl, -SWIGLU_LIMIT, SWIGLU_LIMIT)
        act = g * (1.0 / (1.0 + jnp.exp(-SWIGLU_ALPHA * g))) * (lin + 1.0)
        y = jnp.dot(act.astype(BF16), wd_sc[...], preferred_element_type=F32) + bd_ref[0]
        _to_tiles(y_ref.at[pl.ds(0, nb)], _pack_rows(y))
        rest = y_ref.shape[0] - nb
        if rest:
            y_ref[pl.ds(nb, rest)] = jnp.zeros((rest,) + y_ref.shape[1:], I32)

    used = eoff_ref[e] + ecnt_ref[e] - i * MOE_TILE
    half = MOE_TILE // 2

    @pl.when(jnp.logical_and(valid, used > half))
    def _():
        compute(MOE_TILE)

    @pl.when(jnp.logical_and(valid, used <= half))
    def _():
        compute(half)


def _grouped_ffn(xs_tiles, texp, nvalid, ecnt, eoff, wgu, wd, bg, bl, bd, ntiles):
    d = wgu.shape[1]
    tm = MOE_TILE
    f = wd.shape[1]
    xmap = lambda i, te, nv, ec, eo: (jnp.minimum(i, nv[0] - 1), 0, 0, 0)
    wmap = lambda i, te, nv, ec, eo: (te[i], 0, 0)
    rows_blk = pl.BlockSpec((tm // SUBLANES, PACK_CHUNKS, SUBLANES, LANES), xmap)
    hbm = pl.BlockSpec(memory_space=pl.ANY)
    grid_spec = pltpu.PrefetchScalarGridSpec(
        num_scalar_prefetch=4,
        grid=(ntiles,),
        in_specs=[rows_blk, hbm, hbm,
                  pl.BlockSpec((1, 1, f), wmap), pl.BlockSpec((1, 1, f), wmap), pl.BlockSpec((1, 1, d), wmap)],
        out_specs=rows_blk,
        scratch_shapes=[pltpu.VMEM((2, d, 2 * f), F32), pltpu.VMEM((2, f, d), F32),
                        pltpu.VMEM((d, f), BF16), pltpu.VMEM((d, f), BF16), pltpu.VMEM((f, d), BF16),
                        pltpu.SMEM((1,), I32), pltpu.SemaphoreType.DMA((2,))],
    )
    return pl.pallas_call(
        _ffn_body,
        grid_spec=grid_spec,
        out_shape=jax.ShapeDtypeStruct(xs_tiles.shape, I32),
        compiler_params=pltpu.CompilerParams(dimension_semantics=("arbitrary",), vmem_limit_bytes=VMEM_LIMIT),
        name="ffn",
    )(texp, nvalid, ecnt, eoff, xs_tiles, wgu, wd, bg, bl, bd)


COMBINE_SUB = 512


def _combine_body(lpb_ref, gate_ref, gstart_ref, nslab_ref, lstart_ref, gstart2_ref, nslab2_ref, lstart2_ref,
                  h_ref, l2g_ref, l2b_ref, ys_ref, o_ref, stg0_ref, stg1_ref, moe_ref, sem):
    g = pl.program_id(0)
    ng = pl.num_programs(0)
    stages = (stg0_ref, stg1_ref)

    def fetch(gs_ref, ns_ref, ls_ref, j):
        def per_expert(e, c):
            _copy_run(ys_ref, gs_ref[0, 0, e] * PSLAB, stages[j], ls_ref[0, 0, e] * PSLAB, ns_ref[0, 0, e], sem.at[j])
            return c

        lax.fori_loop(0, N_EXPERTS, per_expert, 0)

    def wait_tile(j):
        nbig, nsmall = _chunk_counts(nslab_ref, 0)
        _wait_runs(ys_ref, stages[j], nbig, nsmall, sem.at[j])

    def gather(j):
        stg_ref = stages[j]

        def body(i, c):
            for u in range(SUBLANES):
                tok = i * SUBLANES + u
                acc_lo = acc_hi = None
                for k in range(TOP_K):
                    gk = gate_ref[0, 0, k * TOKEN_TILE + tok]
                    lo, hi = _unpack_words(_load_row(stg_ref, lpb_ref[0, 0, k * TOKEN_TILE + tok], PACK_CHUNKS))
                    acc_lo = gk * lo if k == 0 else acc_lo + gk * lo
                    acc_hi = gk * hi if k == 0 else acc_hi + gk * hi
                _store_row(moe_ref, i * SLAB + u, acc_lo)
                _store_row(moe_ref, i * SLAB + PSLAB + u, acc_hi)
            return c

        lax.fori_loop(0, TOKEN_TILE // SUBLANES, body, 0)

    def finish():
        nblk = COMBINE_SUB // SUBLANES
        for part in range(TOKEN_TILE // COMBINE_SUB):
            moe = jnp.concatenate(
                [jnp.concatenate([moe_ref[pl.ds((part * nblk + i) * SLAB + s * SUBLANES, SUBLANES), :]
                                  for i in range(nblk)], axis=0) for s in range(SUBLANES)], axis=1)
            h1 = _from_tiles(h_ref.at[pl.ds(part * nblk, nblk)])
            o_ref[pl.ds(part * COMBINE_SUB, COMBINE_SUB), :] = _layernorm(DEEPNORM_ALPHA * h1 + moe, l2g_ref[...],
                                                                          l2b_ref[...])

    @pl.when(g == 0)
    def _():
        fetch(gstart_ref, nslab_ref, lstart_ref, 0)

    for j in range(2):
        @pl.when(lax.rem(g, 2) == j)
        def _():
            @pl.when(g + 1 < ng)
            def _():
                fetch(gstart2_ref, nslab2_ref, lstart2_ref, 1 - j)

            wait_tile(j)
            gather(j)

    finish()


def _combine(lpb, gates, gstart, nslab, lstart, h1_tiles, ys_flat, w):
    d = h1_tiles.shape[1] * LANES
    n = h1_tiles.shape[0] * SUBLANES
    t = TOKEN_TILE
    ng = n // t
    full = lambda shape: pl.BlockSpec(shape, lambda i: (0,) * len(shape))
    smem = lambda shape, imap: pl.BlockSpec(shape, imap, memory_space=pltpu.SMEM)
    vec = smem((1, 1, TOP_K * t), lambda i: (i, 0, 0))
    this_step = smem((1, 1, LANES), lambda i: (i, 0, 0))
    next_step = smem((1, 1, LANES), lambda i: (jnp.minimum(i + 1, ng - 1), 0, 0))
    return pl.pallas_call(
        _combine_body,
        grid=(ng,),
        in_specs=[vec, vec, this_step, this_step, this_step, next_step, next_step, next_step,
                  pl.BlockSpec((t // SUBLANES, d // LANES, SUBLANES, LANES), lambda i: (i, 0, 0, 0)),
                  full((1, d)), full((1, d)),
                  pl.BlockSpec(memory_space=pl.ANY)],
        out_specs=pl.BlockSpec((t, d), lambda i: (i, 0)),
        out_shape=jax.ShapeDtypeStruct((n, d), F32),
        scratch_shapes=[pltpu.VMEM((STAGE_FLAT, LANES), I32), pltpu.VMEM((STAGE_FLAT, LANES), I32),
                        pltpu.VMEM((TOKEN_TILE * SUBLANES, LANES), F32), pltpu.SemaphoreType.DMA((2,))],
        compiler_params=pltpu.CompilerParams(dimension_semantics=("arbitrary",), vmem_limit_bytes=VMEM_LIMIT),
        name="combine",
    )(lpb, gates, gstart, nslab, lstart, gstart, nslab, lstart, h1_tiles, w["ln2_g"], w["ln2_b"], ys_flat)


def _rope_tables(pos_1d, row, col):
    inv = ROPE_THETA ** (-jnp.arange(0, ROPE_DIM, 2, dtype=F32) / ROPE_DIM)

    def cs(p):
        ang = p.astype(F32)[:, None] * inv[None, :]
        return jnp.cos(ang), jnp.sin(ang)

    c1, s1 = cs(pos_1d)
    cr, sr = cs(row)
    cc, sc = cs(col)
    return (jnp.concatenate([c1, c1, c1, c1], axis=1), jnp.concatenate([-s1, s1, -s1, s1], axis=1),
            jnp.concatenate([cr, cr, cc, cc], axis=1), jnp.concatenate([-sr, sr, -sc, sc], axis=1))


def _prep_weights(ln_emb_g, ln_emb_b, w_in, g_q_a, w_q_b, g_kv_a, w_kv_b, g_q_gqa, g_k_gqa, g_o_mla, g_o_gqa, w_o,
                  ln1_g, ln1_b, w_router, b_router, ln2_g, ln2_b):
    r2 = lambda v: v.reshape(1, -1).astype(F32)
    o = np.cumsum([0, Q_LORA, KV_LORA, ROPE_DIM, GQA_HEADS * GQA_DIM, GQA_KV_HEADS * GQA_DIM, GQA_KV_HEADS * GQA_DIM])
    wi = w_in[0]
    seg = [wi[:, o[i]:o[i + 1]] for i in range(6)]
    kpe2 = jnp.concatenate([seg[2], seg[2]], axis=1)
    w_in_p = jnp.concatenate([seg[0], seg[1], kpe2, _swap_halves64(kpe2), seg[3], _swap_halves64(seg[3]),
                              seg[4], _swap_halves64(seg[4]), seg[5]], axis=1).astype(BF16)
    wq = w_q_b[0].reshape(Q_LORA, MLA_HEADS, QK_DIM)
    wq_rope = wq[:, :, NOPE_DIM:].reshape(Q_LORA, -1)
    w_qb = jnp.concatenate([wq[:, :, :NOPE_DIM].reshape(Q_LORA, -1), wq_rope, _swap_halves64(wq_rope)],
                           axis=1).astype(BF16)
    wk = w_kv_b[0].reshape(KV_LORA, MLA_HEADS, NOPE_DIM + V_DIM)
    w_kvb = jnp.concatenate([wk[:, :, :NOPE_DIM].reshape(KV_LORA, -1), wk[:, :, NOPE_DIM:].reshape(KV_LORA, -1)],
                            axis=1).astype(BF16)
    wr = jnp.pad(w_router[0].astype(F32), ((0, 0), (0, LANES - N_EXPERTS)))
    wr_hi = wr.astype(BF16)
    w_router_p = jnp.concatenate([wr_hi, (wr - wr_hi.astype(F32)).astype(BF16)], axis=1)
    b_router_p = jnp.pad(b_router[0].astype(F32), (0, LANES - N_EXPERTS), constant_values=NEG_BIG).reshape(1, LANES)
    return dict(
        ln_emb_g=r2(ln_emb_g), ln_emb_b=r2(ln_emb_b), w_in=w_in_p, g_q_a=r2(g_q_a[0]), w_qb=w_qb,
        g_kv_a=r2(g_kv_a[0]), w_kvb=w_kvb, g_q_gqa=r2(g_q_gqa[0]), g_k_gqa=r2(g_k_gqa[0]),
        g_q_gqa_sw=_swap_halves64(r2(g_q_gqa[0])), g_k_gqa_sw=_swap_halves64(r2(g_k_gqa[0])),
        g_o_mla=r2(g_o_mla[0]), g_o_gqa=r2(g_o_gqa[0]), w_o=w_o[0].astype(BF16), ln1_g=r2(ln1_g[0]),
        ln1_b=r2(ln1_b[0]), w_router=w_router_p, b_router=b_router_p, ln2_g=r2(ln2_g[0]), ln2_b=r2(ln2_b[0]))


def kernel(x, meta_tokens, ln_emb_g, ln_emb_b, w_in, g_q_a, w_q_b, g_kv_a, w_kv_b, g_q_gqa, g_k_gqa, g_o_mla, g_o_gqa,
           w_o, ln1_g, ln1_b, w_router, b_router, w_gate_up, b_gate_up, w_down, b_down, ln2_g, ln2_b):
    b, s, d = x.shape
    n = b * s
    w = _prep_weights(ln_emb_g, ln_emb_b, w_in, g_q_a, w_q_b, g_kv_a, w_kv_b, g_q_gqa, g_k_gqa, g_o_mla, g_o_gqa,
                      w_o, ln1_g, ln1_b, w_router, b_router, ln2_g, ln2_b)

    tok = jnp.arange(s, dtype=I32)
    tabs_real = _rope_tables(tok + N_META, tok // GRID_W, tok % GRID_W)
    mt = jnp.arange(META_PAD, dtype=I32)
    tabs_meta = _rope_tables(mt, jnp.full((META_PAD,), -1, I32), mt)

    qm, km, vm, qg, kg, vg = _project(x, tabs_real, w, ROW_TILE)
    meta = jnp.pad(meta_tokens.astype(x.dtype), ((0, META_PAD - N_META), (0, 0))).reshape(1, META_PAD, d)
    _, km_m, vm_m, _, kg_m, vg_m = _project(meta, tabs_meta, w, META_PAD)

    o_mla = _mla_attention(qm, km, vm, km_m, vm_m)
    o_gqa = _gqa_attention(qg, kg, vg, kg_m, vg_m)

    h1_tiles, h1_packed, topi, gates = _merge(o_mla, o_gqa, x.reshape(n, d), w)

    run_rows = n * TOP_K + (n // TOKEN_TILE) * N_EXPERTS * (SUBLANES - 1)
    ntiles = -(-run_rows // MOE_TILE) + N_EXPERTS
    ntp = -(-ntiles // LANES) * LANES
    rows_pad = ntiles * MOE_TILE
    lpb, gstart, nslab, lstart, texp, nvalid, ecnt, eoff = _positions(topi, ntp)
    flat = lambda a: a.reshape(-1, LANES)
    xs_flat = _dispatch(flat(h1_packed), lpb, gstart, nslab, lstart, ecnt.reshape(-1), eoff.reshape(-1), rows_pad)

    bgu = b_gate_up[0].astype(F32)
    tiles = lambda a: a.reshape(-1, PACK_CHUNKS, SUBLANES, LANES)
    ys_tiles = _grouped_ffn(tiles(xs_flat), texp.reshape(-1), nvalid.reshape(-1), ecnt.reshape(-1), eoff.reshape(-1),
                            w_gate_up[0], w_down[0],
                            bgu[:, 0::2].reshape(N_EXPERTS, 1, D_FF), bgu[:, 1::2].reshape(N_EXPERTS, 1, D_FF),
                            b_down[0].astype(F32).reshape(N_EXPERTS, 1, d), ntiles)

    out = _combine(lpb, gates, gstart, nslab, lstart, h1_tiles, flat(ys_tiles), w)
    return out.reshape(b, s, d)
```

```python
import functools

import jax
import jax.numpy as jnp
import numpy as np
from jax import lax
from jax.experimental import pallas as pl
from jax.experimental.pallas import tpu as pltpu

D_MODEL = 1024
N_META = 16
GRID_W = 64
ROPE_THETA = 10000.0
MLA_HEADS = 4
Q_LORA = 256
KV_LORA = 128
NOPE_DIM = 128
ROPE_DIM = 64
V_DIM = 128
QK_DIM = NOPE_DIM + ROPE_DIM
GQA_HEADS = 4
GQA_KV_HEADS = 2
GQA_DIM = 128
N_EXPERTS = 32
TOP_K = 4
D_FF = D_MODEL
SWIGLU_LIMIT = 7.0
SWIGLU_ALPHA = 1.702
RMS_EPS = 1e-6
LN_EPS = 1e-5
DEPTH = 1
DEEPNORM_ALPHA = (2.0 * DEPTH) ** 0.25

LANES = 128
SUBLANES = 8
SLAB = SUBLANES * SUBLANES
PACK_CHUNKS = D_MODEL // 2 // LANES
PSLAB = SUBLANES * PACK_CHUNKS
META_PAD = 128
MLA_K = 2 * LANES
NEG_BIG = -1e30
LOG2E = 1.4426950408889634
V_EXT = 2 * LANES

ROW_TILE = 1024
Q_TILE_MLA = 2048
Q_TILE_GQA = 2048
MLA_HEADS_PER_STEP = 2
Q_SUB_MLA = 128
Q_SUB_GQA = 512
KEY_BLOCK = 512
TOKEN_TILE = 1024
MOE_TILE = 512
VMEM_LIMIT = 56 * 1024 * 1024

F32 = jnp.float32
BF16 = jnp.bfloat16
I32 = jnp.int32


def _layernorm(x, g, b):
    mu = jnp.mean(x, axis=-1, keepdims=True)
    xc = x - mu
    var = jnp.mean(xc * xc, axis=-1, keepdims=True)
    return xc * lax.rsqrt(var + LN_EPS) * g + b


def _rmsnorm(x, g):
    return x * lax.rsqrt(jnp.mean(x * x, axis=-1, keepdims=True) + RMS_EPS) * g


def _swap_halves64(v):
    shp = v.shape
    return v.reshape(shp[:-1] + (shp[-1] // 64, 2, 32))[..., ::-1, :].reshape(shp)


PROJ_SUB = 256


def _proj_chain(x, lng, lnb, win_ref, gqa, wqb_ref, gkva, wkvb_ref, gqg, gqgs, gkg, gkgs, c1, s1, ca, sa, put):
    r = x.shape[0]
    h0 = _layernorm(x, lng, lnb)
    z = jnp.dot(h0.astype(BF16), win_ref[...], preferred_element_type=F32)
    q_a = z[:, 0:256]
    kv_a = z[:, 256:384]
    kpe2 = z[:, 384:512]
    kpe2s = z[:, 512:640]
    q_g = z[:, 640:1152]
    q_gs = z[:, 1152:1664]
    k_g = z[:, 1664:1920]
    k_gs = z[:, 1920:2176]
    v_g = z[:, 2176:2432]

    q = jnp.dot(_rmsnorm(q_a, gqa).astype(BF16), wqb_ref[...], preferred_element_type=F32)
    kv = jnp.dot(_rmsnorm(kv_a, gkva).astype(BF16), wkvb_ref[...], preferred_element_type=F32)
    krot = kpe2 * c1 + kpe2s * s1
    lane = lax.broadcasted_iota(I32, krot.shape, 1)
    scale_a = QK_DIM ** -0.5 * LOG2E
    ones = jnp.ones((r, LANES), F32)
    for c in range(MLA_HEADS // 2):
        lo = LANES * c
        qr = q[:, 512 + lo:512 + lo + LANES] * c1 + q[:, 768 + lo:768 + lo + LANES] * s1
        for hh in range(2):
            h = 2 * c + hh
            slot = jnp.where((lane // 64) == hh, qr, 0.0)
            put("qm", h, (jnp.concatenate([q[:, LANES * h:LANES * (h + 1)], slot], axis=1) * scale_a).astype(BF16))
    for h in range(MLA_HEADS):
        put("km", h, jnp.concatenate([kv[:, LANES * h:LANES * (h + 1)], krot], axis=1).T.astype(BF16))
        put("vm", h, jnp.concatenate([kv[:, 512 + LANES * h:512 + LANES * (h + 1)], ones], axis=1).astype(BF16))

    scale_b = GQA_DIM ** -0.5 * LOG2E
    cq, sq = ca * gqg, sa * gqgs
    ck, sk = ca * gkg, sa * gkgs

    def norm_rot(xh, xs, cg, sg, scale):
        inv = lax.rsqrt(jnp.mean(xh * xh, axis=-1, keepdims=True) + RMS_EPS)
        return (xh * cg + xs * sg) * (inv * scale)

    for h in range(GQA_HEADS):
        sl = slice(LANES * h, LANES * (h + 1))
        put("qg", h, norm_rot(q_g[:, sl], q_gs[:, sl], cq, sq, scale_b).astype(BF16))
    for j in range(GQA_KV_HEADS):
        sl = slice(LANES * j, LANES * (j + 1))
        put("kg", j, norm_rot(k_g[:, sl], k_gs[:, sl], ck, sk, 1.0).T.astype(BF16))
        put("vg", j, jnp.concatenate([v_g[:, sl], ones], axis=1).astype(BF16))


def _proj_body(x_ref, lng_ref, lnb_ref, win_ref, gqa_ref, wqb_ref, gkva_ref, wkvb_ref, gqg_ref, gqgs_ref, gkg_ref,
               gkgs_ref, c1_ref, s1_ref, ca_ref, sa_ref,
               qm_ref, km_ref, vm_ref, qg_ref, kg_ref, vg_ref):
    outs = dict(qm=qm_ref, km=km_ref, vm=vm_ref, qg=qg_ref, kg=kg_ref, vg=vg_ref)
    tile = x_ref.shape[1]
    sub = min(PROJ_SUB, tile)
    for i in range(tile // sub):
        rows = pl.ds(i * sub, sub)

        def put(name, h, val):
            if name in ("km", "kg"):
                outs[name][0, h, :, rows] = val
            else:
                outs[name][0, h, rows, :] = val

        _proj_chain(x_ref[0, rows, :], lng_ref[...], lnb_ref[...], win_ref, gqa_ref[...], wqb_ref, gkva_ref[...],
                    wkvb_ref, gqg_ref[...], gqgs_ref[...], gkg_ref[...], gkgs_ref[...],
                    c1_ref[rows, :], s1_ref[rows, :], ca_ref[rows, :], sa_ref[rows, :], put)


def _project(x3, tabs, w, tile):
    b, s, d = x3.shape
    nst = s // tile
    full = lambda shape: pl.BlockSpec(shape, lambda bi, si: (0,) * len(shape))
    tab = pl.BlockSpec((tile, LANES), lambda bi, si: (si, 0))
    hm = lambda nh, dd: pl.BlockSpec((1, nh, tile, dd), lambda bi, si: (bi, 0, si, 0))
    hmt = lambda nh, dd: pl.BlockSpec((1, nh, dd, tile), lambda bi, si: (bi, 0, 0, si))
    out_shape = [
        jax.ShapeDtypeStruct((b, MLA_HEADS, s, MLA_K), BF16),
        jax.ShapeDtypeStruct((b, MLA_HEADS, MLA_K, s), BF16),
        jax.ShapeDtypeStruct((b, MLA_HEADS, s, V_EXT), BF16),
        jax.ShapeDtypeStruct((b, GQA_HEADS, s, GQA_DIM), BF16),
        jax.ShapeDtypeStruct((b, GQA_KV_HEADS, GQA_DIM, s), BF16),
        jax.ShapeDtypeStruct((b, GQA_KV_HEADS, s, V_EXT), BF16),
    ]
    return pl.pallas_call(
        _proj_body,
        grid=(b, nst),
        in_specs=[
            pl.BlockSpec((1, tile, d), lambda bi, si: (bi, si, 0)),
            full((1, d)), full((1, d)),
            full(w["w_in"].shape), full((1, Q_LORA)), full(w["w_qb"].shape),
            full((1, KV_LORA)), full(w["w_kvb"].shape),
            full((1, GQA_DIM)), full((1, GQA_DIM)), full((1, GQA_DIM)), full((1, GQA_DIM)),
            tab, tab, tab, tab,
        ],
        out_specs=[hm(MLA_HEADS, MLA_K), hmt(MLA_HEADS, MLA_K), hm(MLA_HEADS, V_EXT),
                   hm(GQA_HEADS, GQA_DIM), hmt(GQA_KV_HEADS, GQA_DIM), hm(GQA_KV_HEADS, V_EXT)],
        out_shape=out_shape,
        compiler_params=pltpu.CompilerParams(dimension_semantics=("parallel", "parallel"),
                                             vmem_limit_bytes=VMEM_LIMIT),
        name="proj",
    )(x3, w["ln_emb_g"], w["ln_emb_b"], w["w_in"], w["g_q_a"], w["w_qb"], w["g_kv_a"], w["w_kvb"],
      w["g_q_gqa"], w["g_q_gqa_sw"], w["g_k_gqa"], w["g_k_gqa_sw"], *tabs)


def _softmax_pv(q, kt, v, kmt, vm):
    sm = jnp.dot(q, kmt, preferred_element_type=F32)
    col = lax.broadcasted_iota(I32, sm.shape, 1)
    sm = jnp.where(col < N_META, sm, NEG_BIG)
    nblk = kt.shape[1] // KEY_BLOCK
    blocks = [jnp.dot(q, kt[:, KEY_BLOCK * c:KEY_BLOCK * (c + 1)], preferred_element_type=F32) for c in range(nblk)]
    mx = blocks[0]
    for c in range(1, nblk):
        mx = jnp.maximum(mx, blocks[c])
    m = jnp.maximum(jnp.max(mx, axis=1, keepdims=True), jnp.max(sm, axis=1, keepdims=True))
    acc = jnp.dot(jnp.exp2(sm - m).astype(BF16), vm, preferred_element_type=F32)
    for c in range(nblk):
        p = jnp.exp2(blocks[c] - m).astype(BF16)
        acc = acc + jnp.dot(p, v[KEY_BLOCK * c:KEY_BLOCK * (c + 1), :], preferred_element_type=F32)
    return acc[:, 0:V_DIM] / acc[:, V_DIM:V_EXT]


def _mla_attn_body(q_ref, k_ref, v_ref, km_ref, vm_ref, o_ref):
    for h in range(q_ref.shape[1]):
        for i in range(q_ref.shape[2] // Q_SUB_MLA):
            rows = pl.ds(i * Q_SUB_MLA, Q_SUB_MLA)
            o = _softmax_pv(q_ref[0, h, rows, :], k_ref.at[0, h], v_ref.at[0, h], km_ref[0, h], vm_ref[0, h])
            o_ref[rows, V_DIM * h:V_DIM * (h + 1)] = o.astype(o_ref.dtype)


def _gqa_attn_body(q_ref, k_ref, v_ref, km_ref, vm_ref, o_ref):
    for g in range(2):
        for i in range(q_ref.shape[2] // Q_SUB_GQA):
            rows = pl.ds(i * Q_SUB_GQA, Q_SUB_GQA)
            o = _softmax_pv(q_ref[0, g, rows, :], k_ref.at[0, 0], v_ref.at[0, 0], km_ref[0, 0], vm_ref[0, 0])
            o_ref[rows, GQA_DIM * g:GQA_DIM * (g + 1)] = o.astype(o_ref.dtype)


def _mla_attention(qm, kmt, vm, kmeta_t, vmeta):
    b, h, s, dk = qm.shape
    tq = Q_TILE_MLA
    nq = s // tq
    hs = MLA_HEADS_PER_STEP
    return pl.pallas_call(
        _mla_attn_body,
        grid=(b, h // hs, nq),
        in_specs=[
            pl.BlockSpec((1, hs, tq, dk), lambda bi, hi, qi: (bi, hi, qi, 0)),
            pl.BlockSpec((1, hs, dk, s), lambda bi, hi, qi: (bi, hi, 0, 0)),
            pl.BlockSpec((1, hs, s, V_EXT), lambda bi, hi, qi: (bi, hi, 0, 0)),
            pl.BlockSpec((1, hs, dk, META_PAD), lambda bi, hi, qi: (0, hi, 0, 0)),
            pl.BlockSpec((1, hs, META_PAD, V_EXT), lambda bi, hi, qi: (0, hi, 0, 0)),
        ],
        out_specs=pl.BlockSpec((tq, hs * V_DIM), lambda bi, hi, qi: (bi * nq + qi, hi)),
        out_shape=jax.ShapeDtypeStruct((b * s, h * V_DIM), BF16),
        compiler_params=pltpu.CompilerParams(dimension_semantics=("parallel", "parallel", "parallel"),
                                             vmem_limit_bytes=VMEM_LIMIT),
        name="mla_attn",
    )(qm, kmt, vm, kmeta_t, vmeta)


def _gqa_attention(qg, kgt, vg, kmeta_t, vmeta):
    b, h, s, d = qg.shape
    hk = kgt.shape[1]
    tq = Q_TILE_GQA
    nq = s // tq
    return pl.pallas_call(
        _gqa_attn_body,
        grid=(b, hk, nq),
        in_specs=[
            pl.BlockSpec((1, 2, tq, d), lambda bi, ji, qi: (bi, ji, qi, 0)),
            pl.BlockSpec((1, 1, d, s), lambda bi, ji, qi: (bi, ji, 0, 0)),
            pl.BlockSpec((1, 1, s, V_EXT), lambda bi, ji, qi: (bi, ji, 0, 0)),
            pl.BlockSpec((1, 1, d, META_PAD), lambda bi, ji, qi: (0, ji, 0, 0)),
            pl.BlockSpec((1, 1, META_PAD, V_EXT), lambda bi, ji, qi: (0, ji, 0, 0)),
        ],
        out_specs=pl.BlockSpec((tq, 2 * d), lambda bi, ji, qi: (bi * nq + qi, ji)),
        out_shape=jax.ShapeDtypeStruct((b * s, h * d), BF16),
        compiler_params=pltpu.CompilerParams(dimension_semantics=("parallel", "parallel", "parallel"),
                                             vmem_limit_bytes=VMEM_LIMIT),
        name="gqa_attn",
    )(qg, kgt, vg, kmeta_t, vmeta)


def _to_tiles(ref, x):
    r = x.shape[0]
    for s in range(x.shape[1] // LANES):
        ref[:, s] = x[:, LANES * s:LANES * (s + 1)].reshape(r // SUBLANES, SUBLANES, LANES)


def _from_tiles(ref):
    r = ref.shape[0] * SUBLANES
    return jnp.concatenate([ref[:, s].reshape(r, LANES) for s in range(ref.shape[1])], axis=1)


MERGE_SUB = 256


def _pack_rows(v):
    c = v.shape[1] // 2
    lo = lax.bitcast_convert_type(v[:, 0:c].astype(BF16).astype(F32), I32)
    hi = lax.bitcast_convert_type(v[:, c:2 * c].astype(BF16).astype(F32), I32)
    return lax.shift_right_logical(lo, 16) | hi


def _unpack_words(w):
    return (lax.bitcast_convert_type(w << 16, F32), lax.bitcast_convert_type(w & jnp.int32(-65536), F32))


def _merge_body(om_ref, og_ref, x_ref, lng_ref, lnb_ref, gom_ref, gog_ref, wo_ref, l1g_ref, l1b_ref,
                wr_ref, br_ref, h1_ref, h1p_ref, topi_ref, gate_ref):
    for i in range(x_ref.shape[0] // MERGE_SUB):
        rows = pl.ds(i * MERGE_SUB, MERGE_SUB)
        h1, ti, gt = _merge_chain(om_ref[rows, :], og_ref[rows, :], x_ref[rows, :], lng_ref[...], lnb_ref[...],
                                  gom_ref[...], gog_ref[...], wo_ref, l1g_ref[...], l1b_ref[...], wr_ref, br_ref[...])
        blocks = pl.ds(i * (MERGE_SUB // SUBLANES), MERGE_SUB // SUBLANES)
        _to_tiles(h1_ref.at[blocks], h1)
        _to_tiles(h1p_ref.at[blocks], _pack_rows(h1))
        topi_ref[:, rows] = ti
        for k in range(TOP_K):
            gate_ref[0, :, pl.ds(k * TOKEN_TILE + i * MERGE_SUB, MERGE_SUB)] = gt[k:k + 1, :]


def _merge_chain(om, og, x, lng, lnb, gom, gog, wo_ref, l1g, l1b, wr_ref, br):
    t = x.shape[0]
    h0 = _layernorm(x, lng, lnb)
    nm = _rmsnorm(om.astype(F32), gom).astype(BF16)
    ng = _rmsnorm(og.astype(F32), gog).astype(BF16)
    half = nm.shape[1]
    mix = jnp.dot(nm, wo_ref[0:half, :], preferred_element_type=F32)
    mix = mix + jnp.dot(ng, wo_ref[half:2 * half, :], preferred_element_type=F32)
    h1 = _layernorm(DEEPNORM_ALPHA * h0 + mix, l1g, l1b)

    hi = h1.astype(BF16)
    lo = (h1 - hi.astype(F32)).astype(BF16)
    acc = jnp.dot(hi, wr_ref[...], preferred_element_type=F32) + jnp.dot(lo, wr_ref[...], preferred_element_type=F32)
    logits = acc[:, 0:LANES] + acc[:, LANES:2 * LANES] + br
    cur = logits.T[0:N_EXPERTS, :]
    eidx = lax.broadcasted_iota(I32, cur.shape, 0)
    vals, idxs = [], []
    for _ in range(TOP_K):
        m = jnp.max(cur, axis=0, keepdims=True)
        i = jnp.min(jnp.where(cur == m, eidx, N_EXPERTS), axis=0, keepdims=True)
        vals.append(m)
        idxs.append(i)
        cur = jnp.where(eidx == i, -jnp.inf, cur)
    ex = [jnp.exp(v - vals[0]) for v in vals]
    den = ex[0] + ex[1] + ex[2] + ex[3]
    sub = lax.broadcasted_iota(I32, (8, t), 0)
    ti = jnp.zeros((8, t), I32)
    gt = jnp.zeros((8, t), F32)
    for k in range(TOP_K):
        ti = jnp.where(sub == k, idxs[k], ti)
        gt = jnp.where(sub == k, ex[k] / den, gt)
    return h1, ti[0:TOP_K, :], gt[0:TOP_K, :]


def _merge(o_mla, o_gqa, x2, w):
    n, d = x2.shape
    tile = TOKEN_TILE
    half = o_mla.shape[1]
    full = lambda shape: pl.BlockSpec(shape, lambda i: (0,) * len(shape))
    row = lambda width: pl.BlockSpec((tile, width), lambda i: (i, 0))
    return pl.pallas_call(
        _merge_body,
        grid=(n // tile,),
        in_specs=[row(half), row(half), row(d), full((1, d)), full((1, d)), full((1, half)), full((1, half)),
                  full((d, d)), full((1, d)), full((1, d)), full((d, 2 * LANES)), full((1, LANES))],
        out_specs=[pl.BlockSpec((tile // SUBLANES, d // LANES, SUBLANES, LANES), lambda i: (i, 0, 0, 0)),
                   pl.BlockSpec((tile // SUBLANES, PACK_CHUNKS, SUBLANES, LANES), lambda i: (i, 0, 0, 0)),
                   pl.BlockSpec((TOP_K, tile), lambda i: (0, i)),
                   pl.BlockSpec((1, 1, TOP_K * tile), lambda i: (i, 0, 0))],
        out_shape=[jax.ShapeDtypeStruct((n // SUBLANES, d // LANES, SUBLANES, LANES), F32),
                   jax.ShapeDtypeStruct((n // SUBLANES, PACK_CHUNKS, SUBLANES, LANES), I32),
                   jax.ShapeDtypeStruct((TOP_K, n), I32), jax.ShapeDtypeStruct((n // tile, 1, TOP_K * tile), F32)],
        compiler_params=pltpu.CompilerParams(dimension_semantics=("parallel",), vmem_limit_bytes=VMEM_LIMIT),
        name="merge",
    )(o_mla, o_gqa, x2, w["ln_emb_g"], w["ln_emb_b"], w["g_o_mla"], w["g_o_gqa"], w["w_o"],
      w["ln1_g"], w["ln1_b"], w["w_router"], w["b_router"])


def _lanes_from_sublanes(col):
    diag = lax.broadcasted_iota(I32, col.shape, 0) == lax.broadcasted_iota(I32, col.shape, 1)
    return jnp.sum(jnp.where(diag, col, 0.0), axis=0, keepdims=True)


def _positions_body(topi_ref, upper_ref, lpb_ref, gstart_ref, nslab_ref, lstart_ref, texp_ref, nvalid_ref, ecnt_ref,
                    eoff_ref, cnt_sc, carry_sc, off_sc, *, ntp):
    p = pl.program_id(0)
    j = pl.program_id(1)
    tl = topi_ref.shape[1]
    topi = topi_ref[...]
    eidx = lax.broadcasted_iota(I32, (N_EXPERTS, tl), 0)
    ohs = [eidx == topi[k:k + 1, :] for k in range(TOP_K)]
    onehot = ohs[0].astype(F32) + ohs[1].astype(F32) + ohs[2].astype(F32) + ohs[3].astype(F32)
    cnt = jnp.sum(onehot, axis=1, keepdims=True)
    run = jnp.floor((cnt + (SUBLANES - 1)) * (1.0 / SUBLANES)) * SUBLANES
    tile_run = jnp.broadcast_to(run, (N_EXPERTS, LANES))
    r = lax.broadcasted_iota(I32, (N_EXPERTS, N_EXPERTS), 0)
    c = lax.broadcasted_iota(I32, (N_EXPERTS, N_EXPERTS), 1)
    lower = (c < r).astype(F32)

    @pl.when(jnp.logical_and(p == 0, j == 0))
    def _():
        cnt_sc[...] = jnp.zeros_like(cnt_sc)

    @pl.when(p == 0)
    def _():
        cnt_sc[...] += tile_run

    @pl.when(jnp.logical_and(p == 1, j == 0))
    def _():
        tot = cnt_sc[...]
        pc = jnp.floor((tot + (MOE_TILE - 1)) * (1.0 / MOE_TILE)) * MOE_TILE
        off = jnp.dot(lower, pc, precision=lax.Precision.HIGHEST, preferred_element_type=F32)
        off_sc[...] = off
        carry_sc[...] = jnp.zeros_like(carry_sc)
        cumend = off + pc
        tstart = lax.broadcasted_iota(I32, (N_EXPERTS, ntp), 1).astype(F32) * MOE_TILE
        te = jnp.sum((jnp.broadcast_to(cumend[:, 0:1], (N_EXPERTS, ntp)) <= tstart).astype(I32), axis=0, keepdims=True)
        texp_ref[...] = jnp.minimum(te, N_EXPERTS - 1)
        nvalid_ref[...] = (cumend[N_EXPERTS - 1:N_EXPERTS, :] * (1.0 / MOE_TILE)).astype(I32)
        ecnt_ref[...] = _lanes_from_sublanes(tot).astype(I32)
        eoff_ref[...] = _lanes_from_sublanes(off).astype(I32)

    @pl.when(p == 1)
    def _():
        before = jnp.dot(onehot.astype(BF16), upper_ref[...], preferred_element_type=F32)
        loff = jnp.dot(lower, tile_run, precision=lax.Precision.HIGHEST, preferred_element_type=F32)
        base = before + loff[:, 0:1]
        sub = lax.broadcasted_iota(I32, (SUBLANES, tl), 0)
        out = jnp.zeros((SUBLANES, tl), F32)
        for k in range(TOP_K):
            pk = jnp.sum(jnp.where(ohs[k], base, 0.0), axis=0, keepdims=True)
            out = jnp.where(sub == k, pk, out)
        lp = out[0:TOP_K, :].astype(I32)
        lpb = (lp >> 3) * PSLAB + (lp & (SUBLANES - 1))
        lpb_ref[0] = jnp.concatenate([lpb[k:k + 1, :] for k in range(TOP_K)], axis=1)
        inv = 1.0 / SUBLANES
        gstart_ref[0] = (_lanes_from_sublanes(off_sc[...] + carry_sc[...]) * inv).astype(I32)
        nslab_ref[0] = (_lanes_from_sublanes(tile_run) * inv).astype(I32)
        lstart_ref[0] = (_lanes_from_sublanes(loff) * inv).astype(I32)
        carry_sc[...] += tile_run


def _positions(topi, ntp):
    n = topi.shape[1]
    tl = TOKEN_TILE
    const = lambda shape: pl.BlockSpec(shape, lambda p, j: (0, 0))
    per_tile = pl.BlockSpec((1, 1, LANES), lambda p, j: (j * p, 0, 0))
    tab = jax.ShapeDtypeStruct((n // tl, 1, LANES), I32)
    upper = (jnp.arange(tl, dtype=I32)[:, None] < jnp.arange(tl, dtype=I32)[None, :]).astype(BF16)
    return pl.pallas_call(
        functools.partial(_positions_body, ntp=ntp),
        grid=(2, n // tl),
        in_specs=[pl.BlockSpec((TOP_K, tl), lambda p, j: (0, j)), const((tl, tl))],
        out_specs=[pl.BlockSpec((1, 1, TOP_K * tl), lambda p, j: (j * p, 0, 0)), per_tile, per_tile, per_tile,
                   const((1, ntp)), const((1, LANES)), const((1, LANES)), const((1, LANES))],
        out_shape=[jax.ShapeDtypeStruct((n // tl, 1, TOP_K * tl), I32), tab, tab, tab,
                   jax.ShapeDtypeStruct((1, ntp), I32),
                   jax.ShapeDtypeStruct((1, LANES), I32), jax.ShapeDtypeStruct((1, LANES), I32),
                   jax.ShapeDtypeStruct((1, LANES), I32)],
        scratch_shapes=[pltpu.VMEM((N_EXPERTS, LANES), F32)] * 3,
        compiler_params=pltpu.CompilerParams(dimension_semantics=("arbitrary", "arbitrary"),
                                             vmem_limit_bytes=VMEM_LIMIT),
        name="positions",
    )(topi, upper)


STAGE_ROWS = TOKEN_TILE * TOP_K + N_EXPERTS * SUBLANES
STAGE_FLAT = STAGE_ROWS * PACK_CHUNKS


def _load_row(ref, flat_start, chunks=SUBLANES):
    return ref[pl.ds(flat_start, chunks, stride=SUBLANES), :]


def _store_row(ref, flat_start, v):
    ref[pl.ds(flat_start, v.shape[0], stride=SUBLANES), :] = v


CHUNK_SLABS = 4


def _copy_run(src_ref, src0, dst_ref, dst0, nslab, sem):
    big = CHUNK_SLABS * PSLAB
    nbig = nslab >> 2
    nsmall = nslab & (CHUNK_SLABS - 1)

    def start_big(s, c):
        pltpu.make_async_copy(src_ref.at[pl.ds(src0 + s * big, big)], dst_ref.at[pl.ds(dst0 + s * big, big)], sem).start()
        return c

    def start_small(s, c):
        o = nbig * big + s * PSLAB
        pltpu.make_async_copy(src_ref.at[pl.ds(src0 + o, PSLAB)], dst_ref.at[pl.ds(dst0 + o, PSLAB)], sem).start()
        return c

    lax.fori_loop(0, nbig, start_big, 0)
    lax.fori_loop(0, nsmall, start_small, 0)


def _wait_runs(src_ref, dst_ref, nbig, nsmall, sem):
    big = CHUNK_SLABS * PSLAB

    def wait_big(s, c):
        pltpu.make_async_copy(src_ref.at[pl.ds(0, big)], dst_ref.at[pl.ds(0, big)], sem).wait()
        return c

    def wait_small(s, c):
        pltpu.make_async_copy(src_ref.at[pl.ds(0, PSLAB)], dst_ref.at[pl.ds(0, PSLAB)], sem).wait()
        return c

    lax.fori_loop(0, nbig, wait_big, 0)
    lax.fori_loop(0, nsmall, wait_small, 0)


def _chunk_counts(nslab_ref, j):
    def add(e, c):
        n = nslab_ref[j, 0, e]
        return c[0] + (n >> 2), c[1] + (n & (CHUNK_SLABS - 1))

    return lax.fori_loop(0, N_EXPERTS, add, (jnp.int32(0), jnp.int32(0)))


def _dispatch_body(ecnt_ref, eoff_ref, lpb_ref, gstart_ref, nslab_ref, lstart_ref, h_ref, xs_ref,
                   stg0_ref, stg1_ref, zero_ref, issued_ref, sem, zsem):
    g = pl.program_id(0)
    ng = pl.num_programs(0)
    stages = (stg0_ref, stg1_ref)

    def drain(j):
        _wait_runs(stages[j], xs_ref, issued_ref[2 * j], issued_ref[2 * j + 1], sem.at[j])

    def run(j):
        stg_ref = stages[j]

        @pl.when(g >= 2)
        def _():
            drain(j)

        def zero_last(e, c):
            last = jnp.maximum(lstart_ref[0, 0, e] + nslab_ref[0, 0, e] - 1, 0)
            stg_ref[pl.ds(last * PSLAB, PSLAB), :] = jnp.zeros((PSLAB, LANES), I32)
            return c

        lax.fori_loop(0, N_EXPERTS, zero_last, 0)

        def move(i, c):
            for u in range(SUBLANES):
                v = _load_row(h_ref, i * PSLAB + u, PACK_CHUNKS)
                for k in range(TOP_K):
                    _store_row(stg_ref, lpb_ref[0, 0, k * TOKEN_TILE + i * SUBLANES + u], v)
            return c

        lax.fori_loop(0, TOKEN_TILE // SUBLANES, move, 0)

        def send_run(e, c):
            _copy_run(stg_ref, lstart_ref[0, 0, e] * PSLAB, xs_ref, gstart_ref[0, 0, e] * PSLAB,
                      nslab_ref[0, 0, e], sem.at[j])
            return c

        lax.fori_loop(0, N_EXPERTS, send_run, 0)
        nbig, nsmall = _chunk_counts(nslab_ref, 0)
        issued_ref[2 * j] = nbig
        issued_ref[2 * j + 1] = nsmall

    for j in range(2):
        @pl.when(lax.rem(g, 2) == j)
        def _():
            run(j)

    @pl.when(g == ng - 1)
    def _():
        zero_ref[...] = jnp.zeros_like(zero_ref)

        def zero_copy(dst_slab):
            return pltpu.make_async_copy(zero_ref, xs_ref.at[pl.ds(dst_slab * PSLAB, PSLAB)], zsem)

        def pad_segment(e, c):
            rows = ecnt_ref[e]
            first = (eoff_ref[e] + rows) >> 3
            npad = lax.rem(MOE_TILE - lax.rem(rows, MOE_TILE), MOE_TILE) >> 3
            lax.fori_loop(0, npad, lambda s, cc: (zero_copy(first + s).start(), cc)[1], 0)
            lax.fori_loop(0, npad, lambda s, cc: (zero_copy(first + s).wait(), cc)[1], 0)
            return c

        lax.fori_loop(0, N_EXPERTS, pad_segment, 0)
        for j in range(2):
            @pl.when(jnp.logical_and(ng > 1, lax.rem(g, 2) != j))
            def _():
                drain(j)

        for j in range(2):
            @pl.when(lax.rem(g, 2) == j)
            def _():
                drain(j)


def _dispatch(h1_flat, lpb, gstart, nslab, lstart, ecnt, eoff, rows_pad):
    nflat = h1_flat.shape[0]
    t = TOKEN_TILE
    smem = lambda shape, imap: pl.BlockSpec(shape, imap, memory_space=pltpu.SMEM)
    per_tile = smem((1, 1, LANES), lambda i, c, o: (i, 0, 0))
    grid_spec = pltpu.PrefetchScalarGridSpec(
        num_scalar_prefetch=2,
        grid=(nflat // (t * PACK_CHUNKS),),
        in_specs=[smem((1, 1, TOP_K * t), lambda i, c, o: (i, 0, 0)), per_tile, per_tile, per_tile,
                  pl.BlockSpec((t * PACK_CHUNKS, LANES), lambda i, c, o: (i, 0))],
        out_specs=pl.BlockSpec(memory_space=pl.ANY),
        scratch_shapes=[pltpu.VMEM((STAGE_FLAT, LANES), I32), pltpu.VMEM((STAGE_FLAT, LANES), I32),
                        pltpu.VMEM((PSLAB, LANES), I32),
                        pltpu.SMEM((4,), I32), pltpu.SemaphoreType.DMA((2,)), pltpu.SemaphoreType.DMA],
    )
    return pl.pallas_call(
        _dispatch_body,
        grid_spec=grid_spec,
        out_shape=jax.ShapeDtypeStruct((rows_pad * PACK_CHUNKS, LANES), I32),
        compiler_params=pltpu.CompilerParams(dimension_semantics=("arbitrary",), vmem_limit_bytes=VMEM_LIMIT),
        name="dispatch",
    )(ecnt, eoff, lpb, gstart, nslab, lstart, h1_flat)


def _ffn_body(texp_ref, nvalid_ref, ecnt_ref, eoff_ref, x_ref, wgu_hbm, wd_hbm, bg_ref, bl_ref, bd_ref, y_ref,
              wgu_buf, wd_buf, wg_sc, wl_sc, wd_sc, seg_ref, sem):
    i = pl.program_id(0)
    valid = i < nvalid_ref[0]
    e = texp_ref[i]
    first_of_expert = jnp.logical_or(i == 0, e != texp_ref[jnp.maximum(i - 1, 0)])

    def weight_copies(expert, slot):
        return (pltpu.make_async_copy(wgu_hbm.at[expert], wgu_buf.at[slot], sem.at[slot]),
                pltpu.make_async_copy(wd_hbm.at[expert], wd_buf.at[slot], sem.at[slot]))

    def prepare(slot):
        blk = 2 * LANES
        r = lax.broadcasted_iota(I32, (blk, blk), 0)
        c = lax.broadcasted_iota(I32, (blk, blk), 1)
        sel = (r == jnp.where(c < LANES, 2 * c, 2 * (c - LANES) + 1)).astype(BF16)
        for b in range(wgu_buf.shape[2] // blk):
            wb = wgu_buf[slot, :, blk * b:blk * (b + 1)].astype(BF16)
            y = jnp.dot(wb, sel, preferred_element_type=F32).astype(BF16)
            wg_sc[:, LANES * b:LANES * (b + 1)] = y[:, 0:LANES]
            wl_sc[:, LANES * b:LANES * (b + 1)] = y[:, LANES:blk]
        wd_sc[...] = wd_buf[slot].astype(BF16)

    @pl.when(jnp.logical_and(valid, first_of_expert))
    def _():
        @pl.when(i == 0)
        def _():
            seg_ref[0] = 0
            for cp in weight_copies(e, 0):
                cp.start()

        seg = seg_ref[0]
        seg_ref[0] = seg + 1
        nxt = lax.while_loop(lambda n: jnp.logical_and(n < N_EXPERTS, ecnt_ref[jnp.minimum(n, N_EXPERTS - 1)] == 0),
                             lambda n: n + 1, e + 1)
        for slot in range(2):
            @pl.when(lax.rem(seg, 2) == slot)
            def _():
                for cp in weight_copies(e, slot):
                    cp.wait()

                @pl.when(nxt < N_EXPERTS)
                def _():
                    for cp in weight_copies(nxt, 1 - slot):
                        cp.start()

                prepare(slot)

    def compute(nrows):
        nb = nrows // SUBLANES
        x = jnp.concatenate(_unpack_words(_from_tiles(x_ref.at[pl.ds(0, nb)])), axis=1).astype(BF16)
        hg = jnp.dot(x, wg_sc[...], preferred_element_type=F32) + bg_ref[0]
        hl = jnp.dot(x, wl_sc[...], preferred_element_type=F32) + bl_ref[0]
        g = jnp.minimum(hg, SWIGLU_LIMIT)
        lin = jnp.clip(hl, -SWIGLU_LIMIT, SWIGLU_LIMIT)
        act = g * (1.0 / (1.0 + jnp.exp(-SWIGLU_ALPHA * g))) * (lin + 1.0)
        y = jnp.dot(act.astype(BF16), wd_sc[...], preferred_element_type=F32) + bd_ref[0]
        _to_tiles(y_ref.at[pl.ds(0, nb)], _pack_rows(y))
        rest = y_ref.shape[0] - nb
        if rest:
            y_ref[pl.ds(nb, rest)] = jnp.zeros((rest,) + y_ref.shape[1:], I32)

    used = eoff_ref[e] + ecnt_ref[e] - i * MOE_TILE
    half = MOE_TILE // 2

    @pl.when(jnp.logical_and(valid, used > half))
    def _():
        compute(MOE_TILE)

    @pl.when(jnp.logical_and(valid, used <= half))
    def _():
        compute(half)


def _grouped_ffn(xs_tiles, texp, nvalid, ecnt, eoff, wgu, wd, bg, bl, bd, ntiles):
    d = wgu.shape[1]
    tm = MOE_TILE
    f = wd.shape[1]
    xmap = lambda i, te, nv, ec, eo: (jnp.minimum(i, nv[0] - 1), 0, 0, 0)
    wmap = lambda i, te, nv, ec, eo: (te[i], 0, 0)
    rows_blk = pl.BlockSpec((tm // SUBLANES, PACK_CHUNKS, SUBLANES, LANES), xmap)
    hbm = pl.BlockSpec(memory_space=pl.ANY)
    grid_spec = pltpu.PrefetchScalarGridSpec(
        num_scalar_prefetch=4,
        grid=(ntiles,),
        in_specs=[rows_blk, hbm, hbm,
                  pl.BlockSpec((1, 1, f), wmap), pl.BlockSpec((1, 1, f), wmap), pl.BlockSpec((1, 1, d), wmap)],
        out_specs=rows_blk,
        scratch_shapes=[pltpu.VMEM((2, d, 2 * f), F32), pltpu.VMEM((2, f, d), F32),
                        pltpu.VMEM((d, f), BF16), pltpu.VMEM((d, f), BF16), pltpu.VMEM((f, d), BF16),
                        pltpu.SMEM((1,), I32), pltpu.SemaphoreType.DMA((2,))],
    )
    return pl.pallas_call(
        _ffn_body,
        grid_spec=grid_spec,
        out_shape=jax.ShapeDtypeStruct(xs_tiles.shape, I32),
        compiler_params=pltpu.CompilerParams(dimension_semantics=("arbitrary",), vmem_limit_bytes=VMEM_LIMIT),
        name="ffn",
    )(texp, nvalid, ecnt, eoff, xs_tiles, wgu, wd, bg, bl, bd)


COMBINE_SUB = 512


def _combine_body(lpb_ref, gate_ref, gstart_ref, nslab_ref, lstart_ref, gstart2_ref, nslab2_ref, lstart2_ref,
                  h_ref, l2g_ref, l2b_ref, ys_ref, o_ref, stg0_ref, stg1_ref, moe_ref, sem):
    g = pl.program_id(0)
    ng = pl.num_programs(0)
    stages = (stg0_ref, stg1_ref)

    def fetch(gs_ref, ns_ref, ls_ref, j):
        def per_expert(e, c):
            _copy_run(ys_ref, gs_ref[0, 0, e] * PSLAB, stages[j], ls_ref[0, 0, e] * PSLAB, ns_ref[0, 0, e], sem.at[j])
            return c

        lax.fori_loop(0, N_EXPERTS, per_expert, 0)

    def wait_tile(j):
        nbig, nsmall = _chunk_counts(nslab_ref, 0)
        _wait_runs(ys_ref, stages[j], nbig, nsmall, sem.at[j])

    def gather(j):
        stg_ref = stages[j]

        def body(i, c):
            for u in range(SUBLANES):
                tok = i * SUBLANES + u
                acc_lo = acc_hi = None
                for k in range(TOP_K):
                    gk = gate_ref[0, 0, k * TOKEN_TILE + tok]
                    lo, hi = _unpack_words(_load_row(stg_ref, lpb_ref[0, 0, k * TOKEN_TILE + tok], PACK_CHUNKS))
                    acc_lo = gk * lo if k == 0 else acc_lo + gk * lo
                    acc_hi = gk * hi if k == 0 else acc_hi + gk * hi
                _store_row(moe_ref, i * SLAB + u, acc_lo)
                _store_row(moe_ref, i * SLAB + PSLAB + u, acc_hi)
            return c

        lax.fori_loop(0, TOKEN_TILE // SUBLANES, body, 0)

    def finish():
        nblk = COMBINE_SUB // SUBLANES
        for part in range(TOKEN_TILE // COMBINE_SUB):
            moe = jnp.concatenate(
                [jnp.concatenate([moe_ref[pl.ds((part * nblk + i) * SLAB + s * SUBLANES, SUBLANES), :]
                                  for i in range(nblk)], axis=0) for s in range(SUBLANES)], axis=1)
            h1 = _from_tiles(h_ref.at[pl.ds(part * nblk, nblk)])
            o_ref[pl.ds(part * COMBINE_SUB, COMBINE_SUB), :] = _layernorm(DEEPNORM_ALPHA * h1 + moe, l2g_ref[...],
                                                                          l2b_ref[...])

    @pl.when(g == 0)
    def _():
        fetch(gstart_ref, nslab_ref, lstart_ref, 0)

    for j in range(2):
        @pl.when(lax.rem(g, 2) == j)
        def _():
            @pl.when(g + 1 < ng)
            def _():
                fetch(gstart2_ref, nslab2_ref, lstart2_ref, 1 - j)

            wait_tile(j)
            gather(j)

    finish()


def _combine(lpb, gates, gstart, nslab, lstart, h1_tiles, ys_flat, w):
    d = h1_tiles.shape[1] * LANES
    n = h1_tiles.shape[0] * SUBLANES
    t = TOKEN_TILE
    ng = n // t
    full = lambda shape: pl.BlockSpec(shape, lambda i: (0,) * len(shape))
    smem = lambda shape, imap: pl.BlockSpec(shape, imap, memory_space=pltpu.SMEM)
    vec = smem((1, 1, TOP_K * t), lambda i: (i, 0, 0))
    this_step = smem((1, 1, LANES), lambda i: (i, 0, 0))
    next_step = smem((1, 1, LANES), lambda i: (jnp.minimum(i + 1, ng - 1), 0, 0))
    return pl.pallas_call(
        _combine_body,
        grid=(ng,),
        in_specs=[vec, vec, this_step, this_step, this_step, next_step, next_step, next_step,
                  pl.BlockSpec((t // SUBLANES, d // LANES, SUBLANES, LANES), lambda i: (i, 0, 0, 0)),
                  full((1, d)), full((1, d)),
                  pl.BlockSpec(memory_space=pl.ANY)],
        out_specs=pl.BlockSpec((t, d), lambda i: (i, 0)),
        out_shape=jax.ShapeDtypeStruct((n, d), F32),
        scratch_shapes=[pltpu.VMEM((STAGE_FLAT, LANES), I32), pltpu.VMEM((STAGE_FLAT, LANES), I32),
                        pltpu.VMEM((TOKEN_TILE * SUBLANES, LANES), F32), pltpu.SemaphoreType.DMA((2,))],
        compiler_params=pltpu.CompilerParams(dimension_semantics=("arbitrary",), vmem_limit_bytes=VMEM_LIMIT),
        name="combine",
    )(lpb, gates, gstart, nslab, lstart, gstart, nslab, lstart, h1_tiles, w["ln2_g"], w["ln2_b"], ys_flat)


def _rope_tables(pos_1d, row, col):
    inv = ROPE_THETA ** (-jnp.arange(0, ROPE_DIM, 2, dtype=F32) / ROPE_DIM)

    def cs(p):
        ang = p.astype(F32)[:, None] * inv[None, :]
        return jnp.cos(ang), jnp.sin(ang)

    c1, s1 = cs(pos_1d)
    cr, sr = cs(row)
    cc, sc = cs(col)
    return (jnp.concatenate([c1, c1, c1, c1], axis=1), jnp.concatenate([-s1, s1, -s1, s1], axis=1),
            jnp.concatenate([cr, cr, cc, cc], axis=1), jnp.concatenate([-sr, sr, -sc, sc], axis=1))


def _prep_weights(ln_emb_g, ln_emb_b, w_in, g_q_a, w_q_b, g_kv_a, w_kv_b, g_q_gqa, g_k_gqa, g_o_mla, g_o_gqa, w_o,
                  ln1_g, ln1_b, w_router, b_router, ln2_g, ln2_b):
    r2 = lambda v: v.reshape(1, -1).astype(F32)
    o = np.cumsum([0, Q_LORA, KV_LORA, ROPE_DIM, GQA_HEADS * GQA_DIM, GQA_KV_HEADS * GQA_DIM, GQA_KV_HEADS * GQA_DIM])
    wi = w_in[0]
    seg = [wi[:, o[i]:o[i + 1]] for i in range(6)]
    kpe2 = jnp.concatenate([seg[2], seg[2]], axis=1)
    w_in_p = jnp.concatenate([seg[0], seg[1], kpe2, _swap_halves64(kpe2), seg[3], _swap_halves64(seg[3]),
                              seg[4], _swap_halves64(seg[4]), seg[5]], axis=1).astype(BF16)
    wq = w_q_b[0].reshape(Q_LORA, MLA_HEADS, QK_DIM)
    wq_rope = wq[:, :, NOPE_DIM:].reshape(Q_LORA, -1)
    w_qb = jnp.concatenate([wq[:, :, :NOPE_DIM].reshape(Q_LORA, -1), wq_rope, _swap_halves64(wq_rope)],
                           axis=1).astype(BF16)
    wk = w_kv_b[0].reshape(KV_LORA, MLA_HEADS, NOPE_DIM + V_DIM)
    w_kvb = jnp.concatenate([wk[:, :, :NOPE_DIM].reshape(KV_LORA, -1), wk[:, :, NOPE_DIM:].reshape(KV_LORA, -1)],
                            axis=1).astype(BF16)
    wr = jnp.pad(w_router[0].astype(F32), ((0, 0), (0, LANES - N_EXPERTS)))
    wr_hi = wr.astype(BF16)
    w_router_p = jnp.concatenate([wr_hi, (wr - wr_hi.astype(F32)).astype(BF16)], axis=1)
    b_router_p = jnp.pad(b_router[0].astype(F32), (0, LANES - N_EXPERTS), constant_values=NEG_BIG).reshape(1, LANES)
    return dict(
        ln_emb_g=r2(ln_emb_g), ln_emb_b=r2(ln_emb_b), w_in=w_in_p, g_q_a=r2(g_q_a[0]), w_qb=w_qb,
        g_kv_a=r2(g_kv_a[0]), w_kvb=w_kvb, g_q_gqa=r2(g_q_gqa[0]), g_k_gqa=r2(g_k_gqa[0]),
        g_q_gqa_sw=_swap_halves64(r2(g_q_gqa[0])), g_k_gqa_sw=_swap_halves64(r2(g_k_gqa[0])),
        g_o_mla=r2(g_o_mla[0]), g_o_gqa=r2(g_o_gqa[0]), w_o=w_o[0].astype(BF16), ln1_g=r2(ln1_g[0]),
        ln1_b=r2(ln1_b[0]), w_router=w_router_p, b_router=b_router_p, ln2_g=r2(ln2_g[0]), ln2_b=r2(ln2_b[0]))


def kernel(x, meta_tokens, ln_emb_g, ln_emb_b, w_in, g_q_a, w_q_b, g_kv_a, w_kv_b, g_q_gqa, g_k_gqa, g_o_mla, g_o_gqa,
           w_o, ln1_g, ln1_b, w_router, b_router, w_gate_up, b_gate_up, w_down, b_down, ln2_g, ln2_b):
    b, s, d = x.shape
    n = b * s
    w = _prep_weights(ln_emb_g, ln_emb_b, w_in, g_q_a, w_q_b, g_kv_a, w_kv_b, g_q_gqa, g_k_gqa, g_o_mla, g_o_gqa,
                      w_o, ln1_g, ln1_b, w_router, b_router, ln2_g, ln2_b)

    tok = jnp.arange(s, dtype=I32)
    tabs_real = _rope_tables(tok + N_META, tok // GRID_W, tok % GRID_W)
    mt = jnp.arange(META_PAD, dtype=I32)
    tabs_meta = _rope_tables(mt, jnp.full((META_PAD,), -1, I32), mt)

    qm, km, vm, qg, kg, vg = _project(x, tabs_real, w, ROW_TILE)
    meta = jnp.pad(meta_tokens.astype(x.dtype), ((0, META_PAD - N_META), (0, 0))).reshape(1, META_PAD, d)
    _, km_m, vm_m, _, kg_m, vg_m = _project(meta, tabs_meta, w, META_PAD)

    o_mla = _mla_attention(qm, km, vm, km_m, vm_m)
    o_gqa = _gqa_attention(qg, kg, vg, kg_m, vg_m)

    h1_tiles, h1_packed, topi, gates = _merge(o_mla, o_gqa, x.reshape(n, d), w)

    run_rows = n * TOP_K + (n // TOKEN_TILE) * N_EXPERTS * (SUBLANES - 1)
    ntiles = -(-run_rows // MOE_TILE) + N_EXPERTS
    ntp = -(-ntiles // LANES) * LANES
    rows_pad = ntiles * MOE_TILE
    lpb, gstart, nslab, lstart, texp, nvalid, ecnt, eoff = _positions(topi, ntp)
    flat = lambda a: a.reshape(-1, LANES)
    xs_flat = _dispatch(flat(h1_packed), lpb, gstart, nslab, lstart, ecnt.reshape(-1), eoff.reshape(-1), rows_pad)

    bgu = b_gate_up[0].astype(F32)
    tiles = lambda a: a.reshape(-1, PACK_CHUNKS, SUBLANES, LANES)
    ys_tiles = _grouped_ffn(tiles(xs_flat), texp.reshape(-1), nvalid.reshape(-1), ecnt.reshape(-1), eoff.reshape(-1),
                            w_gate_up[0], w_down[0],
                            bgu[:, 0::2].reshape(N_EXPERTS, 1, D_FF), bgu[:, 1::2].reshape(N_EXPERTS, 1, D_FF),
                            b_down[0].astype(F32).reshape(N_EXPERTS, 1, d), ntiles)

    out = _combine(lpb, gates, gstart, nslab, lstart, h1_tiles, flat(ys_tiles), w)
    return out.reshape(b, s, d)
```

```python
import functools

import jax
import jax.numpy as jnp
import numpy as np
from jax import lax
from jax.experimental import pallas as pl
from jax.experimental.pallas import tpu as pltpu

D_MODEL = 1024
N_META = 16
GRID_W = 64
ROPE_THETA = 10000.0
MLA_HEADS = 4
Q_LORA = 256
KV_LORA = 128
NOPE_DIM = 128
ROPE_DIM = 64
V_DIM = 128
QK_DIM = NOPE_DIM + ROPE_DIM
GQA_HEADS = 4
GQA_KV_HEADS = 2
GQA_DIM = 128
N_EXPERTS = 32
TOP_K = 4
D_FF = D_MODEL
SWIGLU_LIMIT = 7.0
SWIGLU_ALPHA = 1.702
RMS_EPS = 1e-6
LN_EPS = 1e-5
DEPTH = 1
DEEPNORM_ALPHA = (2.0 * DEPTH) ** 0.25

LANES = 128
SUBLANES = 8
SLAB = SUBLANES * SUBLANES
PACK_CHUNKS = D_MODEL // 2 // LANES
PSLAB = SUBLANES * PACK_CHUNKS
META_PAD = 128
MLA_K = 2 * LANES
NEG_BIG = -1e30
LOG2E = 1.4426950408889634
V_EXT = 2 * LANES

ROW_TILE = 1024
Q_TILE_MLA = 2048
Q_TILE_GQA = 2048
MLA_HEADS_PER_STEP = 2
Q_SUB_MLA = 128
Q_SUB_GQA = 512
KEY_BLOCK = 512
TOKEN_TILE = 1024
MOE_TILE = 1024
FFN_CHAIN = 512
VMEM_LIMIT = 56 * 1024 * 1024

F32 = jnp.float32
BF16 = jnp.bfloat16
I32 = jnp.int32


def _layernorm(x, g, b):
    mu = jnp.mean(x, axis=-1, keepdims=True)
    xc = x - mu
    var = jnp.mean(xc * xc, axis=-1, keepdims=True)
    return xc * lax.rsqrt(var + LN_EPS) * g + b


def _rmsnorm(x, g):
    return x * lax.rsqrt(jnp.mean(x * x, axis=-1, keepdims=True) + RMS_EPS) * g


def _swap_halves64(v):
    shp = v.shape
    return v.reshape(shp[:-1] + (shp[-1] // 64, 2, 32))[..., ::-1, :].reshape(shp)


PROJ_SUB = 256


def _proj_chain(x, lng, lnb, win_ref, gqa, wqb_ref, gkva, wkvb_ref, gqg, gqgs, gkg, gkgs, c1, s1, ca, sa, put):
    r = x.shape[0]
    h0 = _layernorm(x, lng, lnb)
    z = jnp.dot(h0.astype(BF16), win_ref[...], preferred_element_type=F32)
    q_a = z[:, 0:256]
    kv_a = z[:, 256:384]
    kpe2 = z[:, 384:512]
    kpe2s = z[:, 512:640]
    q_g = z[:, 640:1152]
    q_gs = z[:, 1152:1664]
    k_g = z[:, 1664:1920]
    k_gs = z[:, 1920:2176]
    v_g = z[:, 2176:2432]

    q = jnp.dot(_rmsnorm(q_a, gqa).astype(BF16), wqb_ref[...], preferred_element_type=F32)
    kv = jnp.dot(_rmsnorm(kv_a, gkva).astype(BF16), wkvb_ref[...], preferred_element_type=F32)
    krot = kpe2 * c1 + kpe2s * s1
    lane = lax.broadcasted_iota(I32, krot.shape, 1)
    scale_a = QK_DIM ** -0.5 * LOG2E
    ones = jnp.ones((r, LANES), F32)
    for c in range(MLA_HEADS // 2):
        lo = LANES * c
        qr = q[:, 512 + lo:512 + lo + LANES] * c1 + q[:, 768 + lo:768 + lo + LANES] * s1
        for hh in range(2):
            h = 2 * c + hh
            slot = jnp.where((lane // 64) == hh, qr, 0.0)
            put("qm", h, (jnp.concatenate([q[:, LANES * h:LANES * (h + 1)], slot], axis=1) * scale_a).astype(BF16))
    for h in range(MLA_HEADS):
        put("km", h, jnp.concatenate([kv[:, LANES * h:LANES * (h + 1)], krot], axis=1).T.astype(BF16))
        put("vm", h, jnp.concatenate([kv[:, 512 + LANES * h:512 + LANES * (h + 1)], ones], axis=1).astype(BF16))

    scale_b = GQA_DIM ** -0.5 * LOG2E
    cq, sq = ca * gqg, sa * gqgs
    ck, sk = ca * gkg, sa * gkgs

    def norm_rot(xh, xs, cg, sg, scale):
        inv = lax.rsqrt(jnp.mean(xh * xh, axis=-1, keepdims=True) + RMS_EPS)
        return (xh * cg + xs * sg) * (inv * scale)

    for h in range(GQA_HEADS):
        sl = slice(LANES * h, LANES * (h + 1))
        put("qg", h, norm_rot(q_g[:, sl], q_gs[:, sl], cq, sq, scale_b).astype(BF16))
    for j in range(GQA_KV_HEADS):
        sl = slice(LANES * j, LANES * (j + 1))
        put("kg", j, norm_rot(k_g[:, sl], k_gs[:, sl], ck, sk, 1.0).T.astype(BF16))
        put("vg", j, jnp.concatenate([v_g[:, sl], ones], axis=1).astype(BF16))


def _proj_body(x_ref, lng_ref, lnb_ref, win_ref, gqa_ref, wqb_ref, gkva_ref, wkvb_ref, gqg_ref, gqgs_ref, gkg_ref,
               gkgs_ref, c1_ref, s1_ref, ca_ref, sa_ref,
               qm_ref, km_ref, vm_ref, qg_ref, kg_ref, vg_ref):
    outs = dict(qm=qm_ref, km=km_ref, vm=vm_ref, qg=qg_ref, kg=kg_ref, vg=vg_ref)
    tile = x_ref.shape[1]
    sub = min(PROJ_SUB, tile)
    for i in range(tile // sub):
        rows = pl.ds(i * sub, sub)

        def put(name, h, val):
            if name in ("km", "kg"):
                outs[name][0, h, :, rows] = val
            else:
                outs[name][0, h, rows, :] = val

        _proj_chain(x_ref[0, rows, :], lng_ref[...], lnb_ref[...], win_ref, gqa_ref[...], wqb_ref, gkva_ref[...],
                    wkvb_ref, gqg_ref[...], gqgs_ref[...], gkg_ref[...], gkgs_ref[...],
                    c1_ref[rows, :], s1_ref[rows, :], ca_ref[rows, :], sa_ref[rows, :], put)


def _project(x3, tabs, w, tile):
    b, s, d = x3.shape
    nst = s // tile
    full = lambda shape: pl.BlockSpec(shape, lambda bi, si: (0,) * len(shape))
    tab = pl.BlockSpec((tile, LANES), lambda bi, si: (si, 0))
    hm = lambda nh, dd: pl.BlockSpec((1, nh, tile, dd), lambda bi, si: (bi, 0, si, 0))
    hmt = lambda nh, dd: pl.BlockSpec((1, nh, dd, tile), lambda bi, si: (bi, 0, 0, si))
    out_shape = [
        jax.ShapeDtypeStruct((b, MLA_HEADS, s, MLA_K), BF16),
        jax.ShapeDtypeStruct((b, MLA_HEADS, MLA_K, s), BF16),
        jax.ShapeDtypeStruct((b, MLA_HEADS, s, V_EXT), BF16),
        jax.ShapeDtypeStruct((b, GQA_HEADS, s, GQA_DIM), BF16),
        jax.ShapeDtypeStruct((b, GQA_KV_HEADS, GQA_DIM, s), BF16),
        jax.ShapeDtypeStruct((b, GQA_KV_HEADS, s, V_EXT), BF16),
    ]
    return pl.pallas_call(
        _proj_body,
        grid=(b, nst),
        in_specs=[
            pl.BlockSpec((1, tile, d), lambda bi, si: (bi, si, 0)),
            full((1, d)), full((1, d)),
            full(w["w_in"].shape), full((1, Q_LORA)), full(w["w_qb"].shape),
            full((1, KV_LORA)), full(w["w_kvb"].shape),
            full((1, GQA_DIM)), full((1, GQA_DIM)), full((1, GQA_DIM)), full((1, GQA_DIM)),
            tab, tab, tab, tab,
        ],
        out_specs=[hm(MLA_HEADS, MLA_K), hmt(MLA_HEADS, MLA_K), hm(MLA_HEADS, V_EXT),
                   hm(GQA_HEADS, GQA_DIM), hmt(GQA_KV_HEADS, GQA_DIM), hm(GQA_KV_HEADS, V_EXT)],
        out_shape=out_shape,
        compiler_params=pltpu.CompilerParams(dimension_semantics=("parallel", "parallel"),
                                             vmem_limit_bytes=VMEM_LIMIT),
        name="proj",
    )(x3, w["ln_emb_g"], w["ln_emb_b"], w["w_in"], w["g_q_a"], w["w_qb"], w["g_kv_a"], w["w_kvb"],
      w["g_q_gqa"], w["g_q_gqa_sw"], w["g_k_gqa"], w["g_k_gqa_sw"], *tabs)


def _softmax_pv(q, kt, v, kmt, vm):
    sm = jnp.dot(q, kmt, preferred_element_type=F32)
    col = lax.broadcasted_iota(I32, sm.shape, 1)
    sm = jnp.where(col < N_META, sm, NEG_BIG)
    nblk = kt.shape[1] // KEY_BLOCK
    blocks = [jnp.dot(q, kt[:, KEY_BLOCK * c:KEY_BLOCK * (c + 1)], preferred_element_type=F32) for c in range(nblk)]
    mx = blocks[0]
    for c in range(1, nblk):
        mx = jnp.maximum(mx, blocks[c])
    m = jnp.maximum(jnp.max(mx, axis=1, keepdims=True), jnp.max(sm, axis=1, keepdims=True))
    acc = jnp.dot(jnp.exp2(sm - m).astype(BF16), vm, preferred_element_type=F32)
    for c in range(nblk):
        p = jnp.exp2(blocks[c] - m).astype(BF16)
        acc = acc + jnp.dot(p, v[KEY_BLOCK * c:KEY_BLOCK * (c + 1), :], preferred_element_type=F32)
    return acc[:, 0:V_DIM] / acc[:, V_DIM:V_EXT]


def _mla_attn_body(q_ref, k_ref, v_ref, km_ref, vm_ref, o_ref):
    for h in range(q_ref.shape[1]):
        for i in range(q_ref.shape[2] // Q_SUB_MLA):
            rows = pl.ds(i * Q_SUB_MLA, Q_SUB_MLA)
            o = _softmax_pv(q_ref[0, h, rows, :], k_ref.at[0, h], v_ref.at[0, h], km_ref[0, h], vm_ref[0, h])
            o_ref[rows, V_DIM * h:V_DIM * (h + 1)] = o.astype(o_ref.dtype)


def _gqa_attn_body(q_ref, k_ref, v_ref, km_ref, vm_ref, o_ref):
    for g in range(2):
        for i in range(q_ref.shape[2] // Q_SUB_GQA):
            rows = pl.ds(i * Q_SUB_GQA, Q_SUB_GQA)
            o = _softmax_pv(q_ref[0, g, rows, :], k_ref.at[0, 0], v_ref.at[0, 0], km_ref[0, 0], vm_ref[0, 0])
            o_ref[rows, GQA_DIM * g:GQA_DIM * (g + 1)] = o.astype(o_ref.dtype)


def _mla_attention(qm, kmt, vm, kmeta_t, vmeta):
    b, h, s, dk = qm.shape
    tq = Q_TILE_MLA
    nq = s // tq
    hs = MLA_HEADS_PER_STEP
    return pl.pallas_call(
        _mla_attn_body,
        grid=(b, h // hs, nq),
        in_specs=[
            pl.BlockSpec((1, hs, tq, dk), lambda bi, hi, qi: (bi, hi, qi, 0)),
            pl.BlockSpec((1, hs, dk, s), lambda bi, hi, qi: (bi, hi, 0, 0)),
            pl.BlockSpec((1, hs, s, V_EXT), lambda bi, hi, qi: (bi, hi, 0, 0)),
            pl.BlockSpec((1, hs, dk, META_PAD), lambda bi, hi, qi: (0, hi, 0, 0)),
            pl.BlockSpec((1, hs, META_PAD, V_EXT), lambda bi, hi, qi: (0, hi, 0, 0)),
        ],
        out_specs=pl.BlockSpec((tq, hs * V_DIM), lambda bi, hi, qi: (bi * nq + qi, hi)),
        out_shape=jax.ShapeDtypeStruct((b * s, h * V_DIM), BF16),
        compiler_params=pltpu.CompilerParams(dimension_semantics=("parallel", "parallel", "parallel"),
                                             vmem_limit_bytes=VMEM_LIMIT),
        name="mla_attn",
    )(qm, kmt, vm, kmeta_t, vmeta)


def _gqa_attention(qg, kgt, vg, kmeta_t, vmeta):
    b, h, s, d = qg.shape
    hk = kgt.shape[1]
    tq = Q_TILE_GQA
    nq = s // tq
    return pl.pallas_call(
        _gqa_attn_body,
        grid=(b, hk, nq),
        in_specs=[
            pl.BlockSpec((1, 2, tq, d), lambda bi, ji, qi: (bi, ji, qi, 0)),
            pl.BlockSpec((1, 1, d, s), lambda bi, ji, qi: (bi, ji, 0, 0)),
            pl.BlockSpec((1, 1, s, V_EXT), lambda bi, ji, qi: (bi, ji, 0, 0)),
            pl.BlockSpec((1, 1, d, META_PAD), lambda bi, ji, qi: (0, ji, 0, 0)),
            pl.BlockSpec((1, 1, META_PAD, V_EXT), lambda bi, ji, qi: (0, ji, 0, 0)),
        ],
        out_specs=pl.BlockSpec((tq, 2 * d), lambda bi, ji, qi: (bi * nq + qi, ji)),
        out_shape=jax.ShapeDtypeStruct((b * s, h * d), BF16),
        compiler_params=pltpu.CompilerParams(dimension_semantics=("parallel", "parallel", "parallel"),
                                             vmem_limit_bytes=VMEM_LIMIT),
        name="gqa_attn",
    )(qg, kgt, vg, kmeta_t, vmeta)


def _to_tiles(ref, x):
    r = x.shape[0]
    for s in range(x.shape[1] // LANES):
        ref[:, s] = x[:, LANES * s:LANES * (s + 1)].reshape(r // SUBLANES, SUBLANES, LANES)


def _from_tiles(ref):
    r = ref.shape[0] * SUBLANES
    return jnp.concatenate([ref[:, s].reshape(r, LANES) for s in range(ref.shape[1])], axis=1)


MERGE_SUB = 256


def _pack_rows(v):
    c = v.shape[1] // 2
    lo = lax.bitcast_convert_type(v[:, 0:c].astype(BF16).astype(F32), I32)
    hi = lax.bitcast_convert_type(v[:, c:2 * c].astype(BF16).astype(F32), I32)
    return lax.shift_right_logical(lo, 16) | hi


def _unpack_words(w):
    return (lax.bitcast_convert_type(w << 16, F32), lax.bitcast_convert_type(w & jnp.int32(-65536), F32))


def _merge_body(om_ref, og_ref, x_ref, lng_ref, lnb_ref, gom_ref, gog_ref, wo_ref, l1g_ref, l1b_ref,
                wr_ref, br_ref, h1_ref, h1p_ref, topi_ref, gate_ref):
    for i in range(x_ref.shape[0] // MERGE_SUB):
        rows = pl.ds(i * MERGE_SUB, MERGE_SUB)
        h1, ti, gt = _merge_chain(om_ref[rows, :], og_ref[rows, :], x_ref[rows, :], lng_ref[...], lnb_ref[...],
                                  gom_ref[...], gog_ref[...], wo_ref, l1g_ref[...], l1b_ref[...], wr_ref, br_ref[...])
        blocks = pl.ds(i * (MERGE_SUB // SUBLANES), MERGE_SUB // SUBLANES)
        _to_tiles(h1_ref.at[blocks], h1)
        _to_tiles(h1p_ref.at[blocks], _pack_rows(h1))
        topi_ref[:, rows] = ti
        for k in range(TOP_K):
            gate_ref[0, :, pl.ds(k * TOKEN_TILE + i * MERGE_SUB, MERGE_SUB)] = gt[k:k + 1, :]


def _merge_chain(om, og, x, lng, lnb, gom, gog, wo_ref, l1g, l1b, wr_ref, br):
    t = x.shape[0]
    h0 = _layernorm(x, lng, lnb)
    nm = _rmsnorm(om.astype(F32), gom).astype(BF16)
    ng = _rmsnorm(og.astype(F32), gog).astype(BF16)
    half = nm.shape[1]
    mix = jnp.dot(nm, wo_ref[0:half, :], preferred_element_type=F32)
    mix = mix + jnp.dot(ng, wo_ref[half:2 * half, :], preferred_element_type=F32)
    h1 = _layernorm(DEEPNORM_ALPHA * h0 + mix, l1g, l1b)

    hi = h1.astype(BF16)
    lo = (h1 - hi.astype(F32)).astype(BF16)
    acc = jnp.dot(hi, wr_ref[...], preferred_element_type=F32) + jnp.dot(lo, wr_ref[...], preferred_element_type=F32)
    logits = acc[:, 0:LANES] + acc[:, LANES:2 * LANES] + br
    cur = logits.T[0:N_EXPERTS, :]
    eidx = lax.broadcasted_iota(I32, cur.shape, 0)
    vals, idxs = [], []
    for _ in range(TOP_K):
        m = jnp.max(cur, axis=0, keepdims=True)
        i = jnp.min(jnp.where(cur == m, eidx, N_EXPERTS), axis=0, keepdims=True)
        vals.append(m)
        idxs.append(i)
        cur = jnp.where(eidx == i, -jnp.inf, cur)
    ex = [jnp.exp(v - vals[0]) for v in vals]
    den = ex[0] + ex[1] + ex[2] + ex[3]
    sub = lax.broadcasted_iota(I32, (8, t), 0)
    ti = jnp.zeros((8, t), I32)
    gt = jnp.zeros((8, t), F32)
    for k in range(TOP_K):
        ti = jnp.where(sub == k, idxs[k], ti)
        gt = jnp.where(sub == k, ex[k] / den, gt)
    return h1, ti[0:TOP_K, :], gt[0:TOP_K, :]


def _merge(o_mla, o_gqa, x2, w):
    n, d = x2.shape
    tile = TOKEN_TILE
    half = o_mla.shape[1]
    full = lambda shape: pl.BlockSpec(shape, lambda i: (0,) * len(shape))
    row = lambda width: pl.BlockSpec((tile, width), lambda i: (i, 0))
    return pl.pallas_call(
        _merge_body,
        grid=(n // tile,),
        in_specs=[row(half), row(half), row(d), full((1, d)), full((1, d)), full((1, half)), full((1, half)),
                  full((d, d)), full((1, d)), full((1, d)), full((d, 2 * LANES)), full((1, LANES))],
        out_specs=[pl.BlockSpec((tile // SUBLANES, d // LANES, SUBLANES, LANES), lambda i: (i, 0, 0, 0)),
                   pl.BlockSpec((tile // SUBLANES, PACK_CHUNKS, SUBLANES, LANES), lambda i: (i, 0, 0, 0)),
                   pl.BlockSpec((TOP_K, tile), lambda i: (0, i)),
                   pl.BlockSpec((1, 1, TOP_K * tile), lambda i: (i, 0, 0))],
        out_shape=[jax.ShapeDtypeStruct((n // SUBLANES, d // LANES, SUBLANES, LANES), F32),
                   jax.ShapeDtypeStruct((n // SUBLANES, PACK_CHUNKS, SUBLANES, LANES), I32),
                   jax.ShapeDtypeStruct((TOP_K, n), I32), jax.ShapeDtypeStruct((n // tile, 1, TOP_K * tile), F32)],
        compiler_params=pltpu.CompilerParams(dimension_semantics=("parallel",), vmem_limit_bytes=VMEM_LIMIT),
        name="merge",
    )(o_mla, o_gqa, x2, w["ln_emb_g"], w["ln_emb_b"], w["g_o_mla"], w["g_o_gqa"], w["w_o"],
      w["ln1_g"], w["ln1_b"], w["w_router"], w["b_router"])


def _lanes_from_sublanes(col):
    diag = lax.broadcasted_iota(I32, col.shape, 0) == lax.broadcasted_iota(I32, col.shape, 1)
    return jnp.sum(jnp.where(diag, col, 0.0), axis=0, keepdims=True)


def _positions_body(topi_ref, upper_ref, lpb_ref, gstart_ref, nslab_ref, lstart_ref, texp_ref, nvalid_ref, ecnt_ref,
                    eoff_ref, cnt_sc, carry_sc, off_sc, *, ntp):
    p = pl.program_id(0)
    j = pl.program_id(1)
    tl = topi_ref.shape[1]
    topi = topi_ref[...]
    eidx = lax.broadcasted_iota(I32, (N_EXPERTS, tl), 0)
    ohs = [eidx == topi[k:k + 1, :] for k in range(TOP_K)]
    onehot = ohs[0].astype(F32) + ohs[1].astype(F32) + ohs[2].astype(F32) + ohs[3].astype(F32)
    cnt = jnp.sum(onehot, axis=1, keepdims=True)
    run = jnp.floor((cnt + (SUBLANES - 1)) * (1.0 / SUBLANES)) * SUBLANES
    tile_run = jnp.broadcast_to(run, (N_EXPERTS, LANES))
    r = lax.broadcasted_iota(I32, (N_EXPERTS, N_EXPERTS), 0)
    c = lax.broadcasted_iota(I32, (N_EXPERTS, N_EXPERTS), 1)
    lower = (c < r).astype(F32)

    @pl.when(jnp.logical_and(p == 0, j == 0))
    def _():
        cnt_sc[...] = jnp.zeros_like(cnt_sc)

    @pl.when(p == 0)
    def _():
        cnt_sc[...] += tile_run

    @pl.when(jnp.logical_and(p == 1, j == 0))
    def _():
        tot = cnt_sc[...]
        pc = jnp.floor((tot + (MOE_TILE - 1)) * (1.0 / MOE_TILE)) * MOE_TILE
        off = jnp.dot(lower, pc, precision=lax.Precision.HIGHEST, preferred_element_type=F32)
        off_sc[...] = off
        carry_sc[...] = jnp.zeros_like(carry_sc)
        cumend = off + pc
        tstart = lax.broadcasted_iota(I32, (N_EXPERTS, ntp), 1).astype(F32) * MOE_TILE
        te = jnp.sum((jnp.broadcast_to(cumend[:, 0:1], (N_EXPERTS, ntp)) <= tstart).astype(I32), axis=0, keepdims=True)
        texp_ref[...] = jnp.minimum(te, N_EXPERTS - 1)
        nvalid_ref[...] = (cumend[N_EXPERTS - 1:N_EXPERTS, :] * (1.0 / MOE_TILE)).astype(I32)
        ecnt_ref[...] = _lanes_from_sublanes(tot).astype(I32)
        eoff_ref[...] = _lanes_from_sublanes(off).astype(I32)

    @pl.when(p == 1)
    def _():
        before = jnp.dot(onehot.astype(BF16), upper_ref[...], preferred_element_type=F32)
        loff = jnp.dot(lower, tile_run, precision=lax.Precision.HIGHEST, preferred_element_type=F32)
        base = before + loff[:, 0:1]
        sub = lax.broadcasted_iota(I32, (SUBLANES, tl), 0)
        out = jnp.zeros((SUBLANES, tl), F32)
        for k in range(TOP_K):
            pk = jnp.sum(jnp.where(ohs[k], base, 0.0), axis=0, keepdims=True)
            out = jnp.where(sub == k, pk, out)
        lp = out[0:TOP_K, :].astype(I32)
        lpb = (lp >> 3) * PSLAB + (lp & (SUBLANES - 1))
        lpb_ref[0] = jnp.concatenate([lpb[k:k + 1, :] for k in range(TOP_K)], axis=1)
        inv = 1.0 / SUBLANES
        gstart_ref[0] = (_lanes_from_sublanes(off_sc[...] + carry_sc[...]) * inv).astype(I32)
        nslab_ref[0] = (_lanes_from_sublanes(tile_run) * inv).astype(I32)
        lstart_ref[0] = (_lanes_from_sublanes(loff) * inv).astype(I32)
        carry_sc[...] += tile_run


def _positions(topi, ntp):
    n = topi.shape[1]
    tl = TOKEN_TILE
    const = lambda shape: pl.BlockSpec(shape, lambda p, j: (0, 0))
    per_tile = pl.BlockSpec((1, 1, LANES), lambda p, j: (j * p, 0, 0))
    tab = jax.ShapeDtypeStruct((n // tl, 1, LANES), I32)
    upper = (jnp.arange(tl, dtype=I32)[:, None] < jnp.arange(tl, dtype=I32)[None, :]).astype(BF16)
    return pl.pallas_call(
        functools.partial(_positions_body, ntp=ntp),
        grid=(2, n // tl),
        in_specs=[pl.BlockSpec((TOP_K, tl), lambda p, j: (0, j)), const((tl, tl))],
        out_specs=[pl.BlockSpec((1, 1, TOP_K * tl), lambda p, j: (j * p, 0, 0)), per_tile, per_tile, per_tile,
                   const((1, ntp)), const((1, LANES)), const((1, LANES)), const((1, LANES))],
        out_shape=[jax.ShapeDtypeStruct((n // tl, 1, TOP_K * tl), I32), tab, tab, tab,
                   jax.ShapeDtypeStruct((1, ntp), I32),
                   jax.ShapeDtypeStruct((1, LANES), I32), jax.ShapeDtypeStruct((1, LANES), I32),
                   jax.ShapeDtypeStruct((1, LANES), I32)],
        scratch_shapes=[pltpu.VMEM((N_EXPERTS, LANES), F32)] * 3,
        compiler_params=pltpu.CompilerParams(dimension_semantics=("arbitrary", "arbitrary"),
                                             vmem_limit_bytes=VMEM_LIMIT),
        name="positions",
    )(topi, upper)


STAGE_ROWS = TOKEN_TILE * TOP_K + N_EXPERTS * SUBLANES
STAGE_FLAT = STAGE_ROWS * PACK_CHUNKS


def _load_row(ref, flat_start, chunks=SUBLANES):
    return ref[pl.ds(flat_start, chunks, stride=SUBLANES), :]


def _store_row(ref, flat_start, v):
    ref[pl.ds(flat_start, v.shape[0], stride=SUBLANES), :] = v


CHUNK_SLABS = 4


def _copy_run(src_ref, src0, dst_ref, dst0, nslab, sem):
    big = CHUNK_SLABS * PSLAB
    nbig = nslab >> 2
    nsmall = nslab & (CHUNK_SLABS - 1)

    def start_big(s, c):
        pltpu.make_async_copy(src_ref.at[pl.ds(src0 + s * big, big)], dst_ref.at[pl.ds(dst0 + s * big, big)], sem).start()
        return c

    def start_small(s, c):
        o = nbig * big + s * PSLAB
        pltpu.make_async_copy(src_ref.at[pl.ds(src0 + o, PSLAB)], dst_ref.at[pl.ds(dst0 + o, PSLAB)], sem).start()
        return c

    lax.fori_loop(0, nbig, start_big, 0)
    lax.fori_loop(0, nsmall, start_small, 0)


def _wait_runs(src_ref, dst_ref, nbig, nsmall, sem):
    big = CHUNK_SLABS * PSLAB

    def wait_big(s, c):
        pltpu.make_async_copy(src_ref.at[pl.ds(0, big)], dst_ref.at[pl.ds(0, big)], sem).wait()
        return c

    def wait_small(s, c):
        pltpu.make_async_copy(src_ref.at[pl.ds(0, PSLAB)], dst_ref.at[pl.ds(0, PSLAB)], sem).wait()
        return c

    lax.fori_loop(0, nbig, wait_big, 0)
    lax.fori_loop(0, nsmall, wait_small, 0)


def _chunk_counts(nslab_ref, j):
    def add(e, c):
        n = nslab_ref[j, 0, e]
        return c[0] + (n >> 2), c[1] + (n & (CHUNK_SLABS - 1))

    return lax.fori_loop(0, N_EXPERTS, add, (jnp.int32(0), jnp.int32(0)))


def _dispatch_body(ecnt_ref, eoff_ref, lpb_ref, gstart_ref, nslab_ref, lstart_ref, h_ref, xs_ref,
                   stg0_ref, stg1_ref, zero_ref, issued_ref, sem, zsem):
    g = pl.program_id(0)
    ng = pl.num_programs(0)
    stages = (stg0_ref, stg1_ref)

    def drain(j):
        _wait_runs(stages[j], xs_ref, issued_ref[2 * j], issued_ref[2 * j + 1], sem.at[j])

    def run(j):
        stg_ref = stages[j]

        @pl.when(g >= 2)
        def _():
            drain(j)

        def zero_last(e, c):
            last = jnp.maximum(lstart_ref[0, 0, e] + nslab_ref[0, 0, e] - 1, 0)
            stg_ref[pl.ds(last * PSLAB, PSLAB), :] = jnp.zeros((PSLAB, LANES), I32)
            return c

        lax.fori_loop(0, N_EXPERTS, zero_last, 0)

        def move(i, c):
            for u in range(SUBLANES):
                v = _load_row(h_ref, i * PSLAB + u, PACK_CHUNKS)
                for k in range(TOP_K):
                    _store_row(stg_ref, lpb_ref[0, 0, k * TOKEN_TILE + i * SUBLANES + u], v)
            return c

        lax.fori_loop(0, TOKEN_TILE // SUBLANES, move, 0)

        def send_run(e, c):
            _copy_run(stg_ref, lstart_ref[0, 0, e] * PSLAB, xs_ref, gstart_ref[0, 0, e] * PSLAB,
                      nslab_ref[0, 0, e], sem.at[j])
            return c

        lax.fori_loop(0, N_EXPERTS, send_run, 0)
        nbig, nsmall = _chunk_counts(nslab_ref, 0)
        issued_ref[2 * j] = nbig
        issued_ref[2 * j + 1] = nsmall

    for j in range(2):
        @pl.when(lax.rem(g, 2) == j)
        def _():
            run(j)

    @pl.when(g == ng - 1)
    def _():
        zero_ref[...] = jnp.zeros_like(zero_ref)

        def zero_copy(dst_slab):
            return pltpu.make_async_copy(zero_ref, xs_ref.at[pl.ds(dst_slab * PSLAB, PSLAB)], zsem)

        def pad_segment(e, c):
            rows = ecnt_ref[e]
            first = (eoff_ref[e] + rows) >> 3
            npad = lax.rem(MOE_TILE - lax.rem(rows, MOE_TILE), MOE_TILE) >> 3
            lax.fori_loop(0, npad, lambda s, cc: (zero_copy(first + s).start(), cc)[1], 0)
            lax.fori_loop(0, npad, lambda s, cc: (zero_copy(first + s).wait(), cc)[1], 0)
            return c

        lax.fori_loop(0, N_EXPERTS, pad_segment, 0)
        for j in range(2):
            @pl.when(jnp.logical_and(ng > 1, lax.rem(g, 2) != j))
            def _():
                drain(j)

        for j in range(2):
            @pl.when(lax.rem(g, 2) == j)
            def _():
                drain(j)


def _dispatch(h1_flat, lpb, gstart, nslab, lstart, ecnt, eoff, rows_pad):
    nflat = h1_flat.shape[0]
    t = TOKEN_TILE
    smem = lambda shape, imap: pl.BlockSpec(shape, imap, memory_space=pltpu.SMEM)
    per_tile = smem((1, 1, LANES), lambda i, c, o: (i, 0, 0))
    grid_spec = pltpu.PrefetchScalarGridSpec(
        num_scalar_prefetch=2,
        grid=(nflat // (t * PACK_CHUNKS),),
        in_specs=[smem((1, 1, TOP_K * t), lambda i, c, o: (i, 0, 0)), per_tile, per_tile, per_tile,
                  pl.BlockSpec((t * PACK_CHUNKS, LANES), lambda i, c, o: (i, 0))],
        out_specs=pl.BlockSpec(memory_space=pl.ANY),
        scratch_shapes=[pltpu.VMEM((STAGE_FLAT, LANES), I32), pltpu.VMEM((STAGE_FLAT, LANES), I32),
                        pltpu.VMEM((PSLAB, LANES), I32),
                        pltpu.SMEM((4,), I32), pltpu.SemaphoreType.DMA((2,)), pltpu.SemaphoreType.DMA],
    )
    return pl.pallas_call(
        _dispatch_body,
        grid_spec=grid_spec,
        out_shape=jax.ShapeDtypeStruct((rows_pad * PACK_CHUNKS, LANES), I32),
        compiler_params=pltpu.CompilerParams(dimension_semantics=("arbitrary",), vmem_limit_bytes=VMEM_LIMIT),
        name="dispatch",
    )(ecnt, eoff, lpb, gstart, nslab, lstart, h1_flat)


def _ffn_body(texp_ref, nvalid_ref, ecnt_ref, eoff_ref, x_ref, wgu_hbm, wd_hbm, bg_ref, bl_ref, bd_ref, y_ref,
              wgu_buf, wd_buf, wg_sc, wl_sc, wd_sc, seg_ref, sem):
    i = pl.program_id(0)
    valid = i < nvalid_ref[0]
    e = texp_ref[i]
    first_of_expert = jnp.logical_or(i == 0, e != texp_ref[jnp.maximum(i - 1, 0)])

    def weight_copies(expert, slot):
        return (pltpu.make_async_copy(wgu_hbm.at[expert], wgu_buf.at[slot], sem.at[slot]),
                pltpu.make_async_copy(wd_hbm.at[expert], wd_buf.at[slot], sem.at[slot]))

    def prepare(slot):
        blk = 2 * LANES
        r = lax.broadcasted_iota(I32, (blk, blk), 0)
        c = lax.broadcasted_iota(I32, (blk, blk), 1)
        sel = (r == jnp.where(c < LANES, 2 * c, 2 * (c - LANES) + 1)).astype(BF16)
        for b in range(wgu_buf.shape[2] // blk):
            wb = wgu_buf[slot, :, blk * b:blk * (b + 1)].astype(BF16)
            y = jnp.dot(wb, sel, preferred_element_type=F32).astype(BF16)
            wg_sc[:, LANES * b:LANES * (b + 1)] = y[:, 0:LANES]
            wl_sc[:, LANES * b:LANES * (b + 1)] = y[:, LANES:blk]
        wd_sc[...] = wd_buf[slot].astype(BF16)

    @pl.when(jnp.logical_and(valid, first_of_expert))
    def _():
        @pl.when(i == 0)
        def _():
            seg_ref[0] = 0
            for cp in weight_copies(e, 0):
                cp.start()

        seg = seg_ref[0]
        seg_ref[0] = seg + 1
        nxt = lax.while_loop(lambda n: jnp.logical_and(n < N_EXPERTS, ecnt_ref[jnp.minimum(n, N_EXPERTS - 1)] == 0),
                             lambda n: n + 1, e + 1)
        for slot in range(2):
            @pl.when(lax.rem(seg, 2) == slot)
            def _():
                for cp in weight_copies(e, slot):
                    cp.wait()

                @pl.when(nxt < N_EXPERTS)
                def _():
                    for cp in weight_copies(nxt, 1 - slot):
                        cp.start()

                prepare(slot)

    def chain(b0, nb):
        blocks = pl.ds(b0, nb)
        x = jnp.concatenate(_unpack_words(_from_tiles(x_ref.at[blocks])), axis=1).astype(BF16)
        hg = jnp.dot(x, wg_sc[...], preferred_element_type=F32) + bg_ref[0]
        hl = jnp.dot(x, wl_sc[...], preferred_element_type=F32) + bl_ref[0]
        g = jnp.minimum(hg, SWIGLU_LIMIT)
        lin = jnp.clip(hl, -SWIGLU_LIMIT, SWIGLU_LIMIT)
        act = g * (1.0 / (1.0 + jnp.exp(-SWIGLU_ALPHA * g))) * (lin + 1.0)
        y = jnp.dot(act.astype(BF16), wd_sc[...], preferred_element_type=F32) + bd_ref[0]
        _to_tiles(y_ref.at[blocks], _pack_rows(y))

    def compute(nrows):
        done = 0
        while done < nrows:
            n = min(FFN_CHAIN, nrows - done)
            chain(done // SUBLANES, n // SUBLANES)
            done += n
        rest = y_ref.shape[0] - nrows // SUBLANES
        if rest:
            y_ref[pl.ds(nrows // SUBLANES, rest)] = jnp.zeros((rest,) + y_ref.shape[1:], I32)

    used = eoff_ref[e] + ecnt_ref[e] - i * MOE_TILE
    quarter = MOE_TILE // 4
    for q in range(1, 5):
        lo, hi = (q - 1) * quarter, q * quarter
        in_range = jnp.logical_and(used > lo, used <= hi) if q < 4 else used > lo

        @pl.when(jnp.logical_and(valid, in_range))
        def _():
            compute(hi)


def _grouped_ffn(xs_tiles, texp, nvalid, ecnt, eoff, wgu, wd, bg, bl, bd, ntiles):
    d = wgu.shape[1]
    tm = MOE_TILE
    f = wd.shape[1]
    xmap = lambda i, te, nv, ec, eo: (jnp.minimum(i, nv[0] - 1), 0, 0, 0)
    wmap = lambda i, te, nv, ec, eo: (te[i], 0, 0)
    rows_blk = pl.BlockSpec((tm // SUBLANES, PACK_CHUNKS, SUBLANES, LANES), xmap)
    hbm = pl.BlockSpec(memory_space=pl.ANY)
    grid_spec = pltpu.PrefetchScalarGridSpec(
        num_scalar_prefetch=4,
        grid=(ntiles,),
        in_specs=[rows_blk, hbm, hbm,
                  pl.BlockSpec((1, 1, f), wmap), pl.BlockSpec((1, 1, f), wmap), pl.BlockSpec((1, 1, d), wmap)],
        out_specs=rows_blk,
        scratch_shapes=[pltpu.VMEM((2, d, 2 * f), F32), pltpu.VMEM((2, f, d), F32),
                        pltpu.VMEM((d, f), BF16), pltpu.VMEM((d, f), BF16), pltpu.VMEM((f, d), BF16),
                        pltpu.SMEM((1,), I32), pltpu.SemaphoreType.DMA((2,))],
    )
    return pl.pallas_call(
        _ffn_body,
        grid_spec=grid_spec,
        out_shape=jax.ShapeDtypeStruct(xs_tiles.shape, I32),
        compiler_params=pltpu.CompilerParams(dimension_semantics=("arbitrary",), vmem_limit_bytes=VMEM_LIMIT),
        name="ffn",
    )(texp, nvalid, ecnt, eoff, xs_tiles, wgu, wd, bg, bl, bd)


COMBINE_SUB = 512


def _combine_body(lpb_ref, gate_ref, gstart_ref, nslab_ref, lstart_ref, gstart2_ref, nslab2_ref, lstart2_ref,
                  h_ref, l2g_ref, l2b_ref, ys_ref, o_ref, stg0_ref, stg1_ref, moe_ref, sem):
    g = pl.program_id(0)
    ng = pl.num_programs(0)
    stages = (stg0_ref, stg1_ref)

    def fetch(gs_ref, ns_ref, ls_ref, j):
        def per_expert(e, c):
            _copy_run(ys_ref, gs_ref[0, 0, e] * PSLAB, stages[j], ls_ref[0, 0, e] * PSLAB, ns_ref[0, 0, e], sem.at[j])
            return c

        lax.fori_loop(0, N_EXPERTS, per_expert, 0)

    def wait_tile(j):
        nbig, nsmall = _chunk_counts(nslab_ref, 0)
        _wait_runs(ys_ref, stages[j], nbig, nsmall, sem.at[j])

    def gather(j):
        stg_ref = stages[j]

        def body(i, c):
            for u in range(SUBLANES):
                tok = i * SUBLANES + u
                acc_lo = acc_hi = None
                for k in range(TOP_K):
                    gk = gate_ref[0, 0, k * TOKEN_TILE + tok]
                    lo, hi = _unpack_words(_load_row(stg_ref, lpb_ref[0, 0, k * TOKEN_TILE + tok], PACK_CHUNKS))
                    acc_lo = gk * lo if k == 0 else acc_lo + gk * lo
                    acc_hi = gk * hi if k == 0 else acc_hi + gk * hi
                _store_row(moe_ref, i * SLAB + u, acc_lo)
                _store_row(moe_ref, i * SLAB + PSLAB + u, acc_hi)
            return c

        lax.fori_loop(0, TOKEN_TILE // SUBLANES, body, 0)

    def finish():
        nblk = COMBINE_SUB // SUBLANES
        for part in range(TOKEN_TILE // COMBINE_SUB):
            moe = jnp.concatenate(
                [jnp.concatenate([moe_ref[pl.ds((part * nblk + i) * SLAB + s * SUBLANES, SUBLANES), :]
                                  for i in range(nblk)], axis=0) for s in range(SUBLANES)], axis=1)
            h1 = _from_tiles(h_ref.at[pl.ds(part * nblk, nblk)])
            o_ref[pl.ds(part * COMBINE_SUB, COMBINE_SUB), :] = _layernorm(DEEPNORM_ALPHA * h1 + moe, l2g_ref[...],
                                                                          l2b_ref[...])

    @pl.when(g == 0)
    def _():
        fetch(gstart_ref, nslab_ref, lstart_ref, 0)

    for j in range(2):
        @pl.when(lax.rem(g, 2) == j)
        def _():
            @pl.when(g + 1 < ng)
            def _():
                fetch(gstart2_ref, nslab2_ref, lstart2_ref, 1 - j)

            wait_tile(j)
            gather(j)

    finish()


def _combine(lpb, gates, gstart, nslab, lstart, h1_tiles, ys_flat, w):
    d = h1_tiles.shape[1] * LANES
    n = h1_tiles.shape[0] * SUBLANES
    t = TOKEN_TILE
    ng = n // t
    full = lambda shape: pl.BlockSpec(shape, lambda i: (0,) * len(shape))
    smem = lambda shape, imap: pl.BlockSpec(shape, imap, memory_space=pltpu.SMEM)
    vec = smem((1, 1, TOP_K * t), lambda i: (i, 0, 0))
    this_step = smem((1, 1, LANES), lambda i: (i, 0, 0))
    next_step = smem((1, 1, LANES), lambda i: (jnp.minimum(i + 1, ng - 1), 0, 0))
    return pl.pallas_call(
        _combine_body,
        grid=(ng,),
        in_specs=[vec, vec, this_step, this_step, this_step, next_step, next_step, next_step,
                  pl.BlockSpec((t // SUBLANES, d // LANES, SUBLANES, LANES), lambda i: (i, 0, 0, 0)),
                  full((1, d)), full((1, d)),
                  pl.BlockSpec(memory_space=pl.ANY)],
        out_specs=pl.BlockSpec((t, d), lambda i: (i, 0)),
        out_shape=jax.ShapeDtypeStruct((n, d), F32),
        scratch_shapes=[pltpu.VMEM((STAGE_FLAT, LANES), I32), pltpu.VMEM((STAGE_FLAT, LANES), I32),
                        pltpu.VMEM((TOKEN_TILE * SUBLANES, LANES), F32), pltpu.SemaphoreType.DMA((2,))],
        compiler_params=pltpu.CompilerParams(dimension_semantics=("arbitrary",), vmem_limit_bytes=VMEM_LIMIT),
        name="combine",
    )(lpb, gates, gstart, nslab, lstart, gstart, nslab, lstart, h1_tiles, w["ln2_g"], w["ln2_b"], ys_flat)


def _rope_tables(pos_1d, row, col):
    inv = ROPE_THETA ** (-jnp.arange(0, ROPE_DIM, 2, dtype=F32) / ROPE_DIM)

    def cs(p):
        ang = p.astype(F32)[:, None] * inv[None, :]
        return jnp.cos(ang), jnp.sin(ang)

    c1, s1 = cs(pos_1d)
    cr, sr = cs(row)
    cc, sc = cs(col)
    return (jnp.concatenate([c1, c1, c1, c1], axis=1), jnp.concatenate([-s1, s1, -s1, s1], axis=1),
            jnp.concatenate([cr, cr, cc, cc], axis=1), jnp.concatenate([-sr, sr, -sc, sc], axis=1))


def _prep_weights(ln_emb_g, ln_emb_b, w_in, g_q_a, w_q_b, g_kv_a, w_kv_b, g_q_gqa, g_k_gqa, g_o_mla, g_o_gqa, w_o,
                  ln1_g, ln1_b, w_router, b_router, ln2_g, ln2_b):
    r2 = lambda v: v.reshape(1, -1).astype(F32)
    o = np.cumsum([0, Q_LORA, KV_LORA, ROPE_DIM, GQA_HEADS * GQA_DIM, GQA_KV_HEADS * GQA_DIM, GQA_KV_HEADS * GQA_DIM])
    wi = w_in[0]
    seg = [wi[:, o[i]:o[i + 1]] for i in range(6)]
    kpe2 = jnp.concatenate([seg[2], seg[2]], axis=1)
    w_in_p = jnp.concatenate([seg[0], seg[1], kpe2, _swap_halves64(kpe2), seg[3], _swap_halves64(seg[3]),
                              seg[4], _swap_halves64(seg[4]), seg[5]], axis=1).astype(BF16)
    wq = w_q_b[0].reshape(Q_LORA, MLA_HEADS, QK_DIM)
    wq_rope = wq[:, :, NOPE_DIM:].reshape(Q_LORA, -1)
    w_qb = jnp.concatenate([wq[:, :, :NOPE_DIM].reshape(Q_LORA, -1), wq_rope, _swap_halves64(wq_rope)],
                           axis=1).astype(BF16)
    wk = w_kv_b[0].reshape(KV_LORA, MLA_HEADS, NOPE_DIM + V_DIM)
    w_kvb = jnp.concatenate([wk[:, :, :NOPE_DIM].reshape(KV_LORA, -1), wk[:, :, NOPE_DIM:].reshape(KV_LORA, -1)],
                            axis=1).astype(BF16)
    wr = jnp.pad(w_router[0].astype(F32), ((0, 0), (0, LANES - N_EXPERTS)))
    wr_hi = wr.astype(BF16)
    w_router_p = jnp.concatenate([wr_hi, (wr - wr_hi.astype(F32)).astype(BF16)], axis=1)
    b_router_p = jnp.pad(b_router[0].astype(F32), (0, LANES - N_EXPERTS), constant_values=NEG_BIG).reshape(1, LANES)
    return dict(
        ln_emb_g=r2(ln_emb_g), ln_emb_b=r2(ln_emb_b), w_in=w_in_p, g_q_a=r2(g_q_a[0]), w_qb=w_qb,
        g_kv_a=r2(g_kv_a[0]), w_kvb=w_kvb, g_q_gqa=r2(g_q_gqa[0]), g_k_gqa=r2(g_k_gqa[0]),
        g_q_gqa_sw=_swap_halves64(r2(g_q_gqa[0])), g_k_gqa_sw=_swap_halves64(r2(g_k_gqa[0])),
        g_o_mla=r2(g_o_mla[0]), g_o_gqa=r2(g_o_gqa[0]), w_o=w_o[0].astype(BF16), ln1_g=r2(ln1_g[0]),
        ln1_b=r2(ln1_b[0]), w_router=w_router_p, b_router=b_router_p, ln2_g=r2(ln2_g[0]), ln2_b=r2(ln2_b[0]))


def kernel(x, meta_tokens, ln_emb_g, ln_emb_b, w_in, g_q_a, w_q_b, g_kv_a, w_kv_b, g_q_gqa, g_k_gqa, g_o_mla, g_o_gqa,
           w_o, ln1_g, ln1_b, w_router, b_router, w_gate_up, b_gate_up, w_down, b_down, ln2_g, ln2_b):
    b, s, d = x.shape
    n = b * s
    w = _prep_weights(ln_emb_g, ln_emb_b, w_in, g_q_a, w_q_b, g_kv_a, w_kv_b, g_q_gqa, g_k_gqa, g_o_mla, g_o_gqa,
                      w_o, ln1_g, ln1_b, w_router, b_router, ln2_g, ln2_b)

    tok = jnp.arange(s, dtype=I32)
    tabs_real = _rope_tables(tok + N_META, tok // GRID_W, tok % GRID_W)
    mt = jnp.arange(META_PAD, dtype=I32)
    tabs_meta = _rope_tables(mt, jnp.full((META_PAD,), -1, I32), mt)

    qm, km, vm, qg, kg, vg = _project(x, tabs_real, w, ROW_TILE)
    meta = jnp.pad(meta_tokens.astype(x.dtype), ((0, META_PAD - N_META), (0, 0))).reshape(1, META_PAD, d)
    _, km_m, vm_m, _, kg_m, vg_m = _project(meta, tabs_meta, w, META_PAD)

    o_mla = _mla_attention(qm, km, vm, km_m, vm_m)
    o_gqa = _gqa_attention(qg, kg, vg, kg_m, vg_m)

    h1_tiles, h1_packed, topi, gates = _merge(o_mla, o_gqa, x.reshape(n, d), w)

    run_rows = n * TOP_K + (n // TOKEN_TILE) * N_EXPERTS * (SUBLANES - 1)
    ntiles = -(-run_rows // MOE_TILE) + N_EXPERTS
    ntp = -(-ntiles // LANES) * LANES
    rows_pad = ntiles * MOE_TILE
    lpb, gstart, nslab, lstart, texp, nvalid, ecnt, eoff = _positions(topi, ntp)
    flat = lambda a: a.reshape(-1, LANES)
    xs_flat = _dispatch(flat(h1_packed), lpb, gstart, nslab, lstart, ecnt.reshape(-1), eoff.reshape(-1), rows_pad)

    bgu = b_gate_up[0].astype(F32)
    tiles = lambda a: a.reshape(-1, PACK_CHUNKS, SUBLANES, LANES)
    ys_tiles = _grouped_ffn(tiles(xs_flat), texp.reshape(-1), nvalid.reshape(-1), ecnt.reshape(-1), eoff.reshape(-1),
                            w_gate_up[0], w_down[0],
                            bgu[:, 0::2].reshape(N_EXPERTS, 1, D_FF), bgu[:, 1::2].reshape(N_EXPERTS, 1, D_FF),
                            b_down[0].astype(F32).reshape(N_EXPERTS, 1, d), ntiles)

    out = _combine(lpb, gates, gstart, nslab, lstart, h1_tiles, flat(ys_tiles), w)
    return out.reshape(b, s, d)
```

```python
import functools

import jax
import jax.numpy as jnp
import numpy as np
from jax import lax
from jax.experimental import pallas as pl
from jax.experimental.pallas import tpu as pltpu

D_MODEL = 1024
N_META = 16
GRID_W = 64
ROPE_THETA = 10000.0
MLA_HEADS = 4
Q_LORA = 256
KV_LORA = 128
NOPE_DIM = 128
ROPE_DIM = 64
V_DIM = 128
QK_DIM = NOPE_DIM + ROPE_DIM
GQA_HEADS = 4
GQA_KV_HEADS = 2
GQA_DIM = 128
N_EXPERTS = 32
TOP_K = 4
D_FF = D_MODEL
SWIGLU_LIMIT = 7.0
SWIGLU_ALPHA = 1.702
RMS_EPS = 1e-6
LN_EPS = 1e-5
DEPTH = 1
DEEPNORM_ALPHA = (2.0 * DEPTH) ** 0.25

LANES = 128
SUBLANE_SHIFT = 3
SUBLANES = 1 << SUBLANE_SHIFT
SLAB = SUBLANES * SUBLANES
PACK_CHUNKS = D_MODEL // 2 // LANES
PSLAB = SUBLANES * PACK_CHUNKS
META_PAD = 128
MLA_K = 2 * LANES
NEG_BIG = -1e30
LOG2E = 1.4426950408889634
V_EXT = 2 * LANES

ROW_TILE = 1024
Q_TILE_MLA = 2048
Q_TILE_GQA = 2048
MLA_HEADS_PER_STEP = 2
Q_SUB_MLA = 128
Q_SUB_GQA = 512
KEY_BLOCK = 512
TOKEN_TILE = 1024
MOE_TILE = 1024
FFN_CHAIN = 512
VMEM_LIMIT = 56 * 1024 * 1024

F32 = jnp.float32
BF16 = jnp.bfloat16
I32 = jnp.int32


def _layernorm(x, g, b):
    mu = jnp.mean(x, axis=-1, keepdims=True)
    xc = x - mu
    var = jnp.mean(xc * xc, axis=-1, keepdims=True)
    return xc * lax.rsqrt(var + LN_EPS) * g + b


def _rmsnorm(x, g):
    return x * lax.rsqrt(jnp.mean(x * x, axis=-1, keepdims=True) + RMS_EPS) * g


def _swap_halves64(v):
    shp = v.shape
    return v.reshape(shp[:-1] + (shp[-1] // 64, 2, 32))[..., ::-1, :].reshape(shp)


PROJ_SUB = 256


def _swap_blocks32(xt):
    return jnp.concatenate([xt[o + b:o + b + 32] for o in range(0, xt.shape[0], 64) for b in (32, 0)], axis=0)


def _proj_chain(x, lng, lnb, win_ref, wkt_ref, gqa, wqb_ref, gkva, wkvbv_ref, wkvbkt_ref, gqg, gqgs,
                c1, s1, ca, sa, c1t, s1t, ckt, skt, put):
    r = x.shape[0]
    nt = (((1,), (1,)), ((), ()))
    h0b = _layernorm(x, lng, lnb).astype(BF16)
    z = jnp.dot(h0b, win_ref[...], preferred_element_type=F32)
    q_a = z[:, 0:256]
    kv_a = z[:, 256:384]
    q_g = z[:, 384:896]
    q_gs = z[:, 896:1408]
    v_g = z[:, 1408:1664]
    kt = lax.dot_general(wkt_ref[...], h0b, nt, preferred_element_type=F32)

    q = jnp.dot(_rmsnorm(q_a, gqa).astype(BF16), wqb_ref[...], preferred_element_type=F32)
    kvn = _rmsnorm(kv_a, gkva).astype(BF16)
    v_a = jnp.dot(kvn, wkvbv_ref[...], preferred_element_type=F32)
    knt = lax.dot_general(wkvbkt_ref[...], kvn, nt, preferred_element_type=F32)
    kpet = kt[0:ROPE_DIM]
    krot = kpet * c1t + _swap_blocks32(kpet) * s1t
    lane = lax.broadcasted_iota(I32, (r, LANES), 1)
    scale_a = QK_DIM ** -0.5 * LOG2E
    ones = jnp.ones((r, LANES), F32)
    for c in range(MLA_HEADS // 2):
        lo = LANES * c
        qr = q[:, 512 + lo:512 + lo + LANES] * c1 + q[:, 768 + lo:768 + lo + LANES] * s1
        for hh in range(2):
            h = 2 * c + hh
            slot = jnp.where((lane // 64) == hh, qr, 0.0)
            put("qm", h, (jnp.concatenate([q[:, LANES * h:LANES * (h + 1)], slot], axis=1) * scale_a).astype(BF16))
    for h in range(MLA_HEADS):
        put("km", h, jnp.concatenate([knt[LANES * h:LANES * (h + 1)], krot, krot], axis=0).astype(BF16))
        put("vm", h, jnp.concatenate([v_a[:, LANES * h:LANES * (h + 1)], ones], axis=1).astype(BF16))

    scale_b = GQA_DIM ** -0.5 * LOG2E
    cq, sq = ca * gqg, sa * gqgs
    for h in range(GQA_HEADS):
        sl = slice(LANES * h, LANES * (h + 1))
        xh = q_g[:, sl]
        inv = lax.rsqrt(jnp.mean(xh * xh, axis=-1, keepdims=True) + RMS_EPS)
        put("qg", h, ((xh * cq + q_gs[:, sl] * sq) * (inv * scale_b)).astype(BF16))
    for j in range(GQA_KV_HEADS):
        xt = kt[ROPE_DIM + LANES * j:ROPE_DIM + LANES * (j + 1)]
        inv = lax.rsqrt(jnp.mean(xt * xt, axis=0, keepdims=True) + RMS_EPS)
        put("kg", j, ((xt * ckt + _swap_blocks32(xt) * skt) * inv).astype(BF16))
        put("vg", j, jnp.concatenate([v_g[:, LANES * j:LANES * (j + 1)], ones], axis=1).astype(BF16))


def _proj_body(x_ref, lng_ref, lnb_ref, win_ref, wkt_ref, gqa_ref, wqb_ref, gkva_ref, wkvbv_ref, wkvbkt_ref,
               gqg_ref, gqgs_ref, c1_ref, s1_ref, ca_ref, sa_ref, c1t_ref, s1t_ref, ckt_ref, skt_ref,
               qm_ref, km_ref, vm_ref, qg_ref, kg_ref, vg_ref):
    outs = dict(qm=qm_ref, km=km_ref, vm=vm_ref, qg=qg_ref, kg=kg_ref, vg=vg_ref)
    tile = x_ref.shape[1]
    sub = min(PROJ_SUB, tile)
    for i in range(tile // sub):
        rows = pl.ds(i * sub, sub)

        def put(name, h, val):
            if name in ("km", "kg"):
                outs[name][0, h, :, rows] = val
            else:
                outs[name][0, h, rows, :] = val

        _proj_chain(x_ref[0, rows, :], lng_ref[...], lnb_ref[...], win_ref, wkt_ref, gqa_ref[...], wqb_ref,
                    gkva_ref[...], wkvbv_ref, wkvbkt_ref, gqg_ref[...], gqgs_ref[...],
                    c1_ref[rows, :], s1_ref[rows, :], ca_ref[rows, :], sa_ref[rows, :],
                    c1t_ref[:, rows], s1t_ref[:, rows], ckt_ref[:, rows], skt_ref[:, rows], put)


def _project(x3, tabs, w, tile):
    b, s, d = x3.shape
    nst = s // tile
    full = lambda shape: pl.BlockSpec(shape, lambda bi, si: (0,) * len(shape))
    tab = pl.BlockSpec((tile, LANES), lambda bi, si: (si, 0))
    tabt = lambda dims: pl.BlockSpec((dims, tile), lambda bi, si: (0, si))
    hm = lambda nh, dd: pl.BlockSpec((1, nh, tile, dd), lambda bi, si: (bi, 0, si, 0))
    hmt = lambda nh, dd: pl.BlockSpec((1, nh, dd, tile), lambda bi, si: (bi, 0, 0, si))
    out_shape = [
        jax.ShapeDtypeStruct((b, MLA_HEADS, s, MLA_K), BF16),
        jax.ShapeDtypeStruct((b, MLA_HEADS, MLA_K, s), BF16),
        jax.ShapeDtypeStruct((b, MLA_HEADS, s, V_EXT), BF16),
        jax.ShapeDtypeStruct((b, GQA_HEADS, s, GQA_DIM), BF16),
        jax.ShapeDtypeStruct((b, GQA_KV_HEADS, GQA_DIM, s), BF16),
        jax.ShapeDtypeStruct((b, GQA_KV_HEADS, s, V_EXT), BF16),
    ]
    return pl.pallas_call(
        _proj_body,
        grid=(b, nst),
        in_specs=[
            pl.BlockSpec((1, tile, d), lambda bi, si: (bi, si, 0)),
            full((1, d)), full((1, d)),
            full(w["w_in"].shape), full(w["w_kt"].shape), full((1, Q_LORA)), full(w["w_qb"].shape),
            full((1, KV_LORA)), full(w["w_kvb_v"].shape), full(w["w_kvb_kt"].shape),
            full((1, GQA_DIM)), full((1, GQA_DIM)),
            tab, tab, tab, tab, tabt(ROPE_DIM), tabt(ROPE_DIM), tabt(GQA_DIM), tabt(GQA_DIM),
        ],
        out_specs=[hm(MLA_HEADS, MLA_K), hmt(MLA_HEADS, MLA_K), hm(MLA_HEADS, V_EXT),
                   hm(GQA_HEADS, GQA_DIM), hmt(GQA_KV_HEADS, GQA_DIM), hm(GQA_KV_HEADS, V_EXT)],
        out_shape=out_shape,
        compiler_params=pltpu.CompilerParams(dimension_semantics=("parallel", "parallel"),
                                             vmem_limit_bytes=VMEM_LIMIT),
        name="proj",
    )(x3, w["ln_emb_g"], w["ln_emb_b"], w["w_in"], w["w_kt"], w["g_q_a"], w["w_qb"], w["g_kv_a"], w["w_kvb_v"],
      w["w_kvb_kt"], w["g_q_gqa"], w["g_q_gqa_sw"], *tabs)


def _softmax_pv(q, kt, v, kmt, vm):
    sm = jnp.dot(q, kmt, preferred_element_type=F32)
    col = lax.broadcasted_iota(I32, sm.shape, 1)
    sm = jnp.where(col < N_META, sm, NEG_BIG)
    nblk = kt.shape[1] // KEY_BLOCK
    blocks = [jnp.dot(q, kt[:, KEY_BLOCK * c:KEY_BLOCK * (c + 1)], preferred_element_type=F32) for c in range(nblk)]
    mx = blocks[0]
    for c in range(1, nblk):
        mx = jnp.maximum(mx, blocks[c])
    m = jnp.maximum(jnp.max(mx, axis=1, keepdims=True), jnp.max(sm, axis=1, keepdims=True))
    acc = jnp.dot(jnp.exp2(sm - m).astype(BF16), vm, preferred_element_type=F32)
    for c in range(nblk):
        p = jnp.exp2(blocks[c] - m).astype(BF16)
        acc = acc + jnp.dot(p, v[KEY_BLOCK * c:KEY_BLOCK * (c + 1), :], preferred_element_type=F32)
    return acc[:, 0:V_DIM] / acc[:, V_DIM:V_EXT]


def _mla_attn_body(q_ref, k_ref, v_ref, km_ref, vm_ref, o_ref):
    for h in range(q_ref.shape[1]):
        for i in range(q_ref.shape[2] // Q_SUB_MLA):
            rows = pl.ds(i * Q_SUB_MLA, Q_SUB_MLA)
            o = _softmax_pv(q_ref[0, h, rows, :], k_ref.at[0, h], v_ref.at[0, h], km_ref[0, h], vm_ref[0, h])
            o_ref[rows, V_DIM * h:V_DIM * (h + 1)] = o.astype(o_ref.dtype)


def _gqa_attn_body(q_ref, k_ref, v_ref, km_ref, vm_ref, o_ref):
    for g in range(2):
        for i in range(q_ref.shape[2] // Q_SUB_GQA):
            rows = pl.ds(i * Q_SUB_GQA, Q_SUB_GQA)
            o = _softmax_pv(q_ref[0, g, rows, :], k_ref.at[0, 0], v_ref.at[0, 0], km_ref[0, 0], vm_ref[0, 0])
            o_ref[rows, GQA_DIM * g:GQA_DIM * (g + 1)] = o.astype(o_ref.dtype)


def _mla_attention(qm, kmt, vm, kmeta_t, vmeta):
    b, h, s, dk = qm.shape
    tq = Q_TILE_MLA
    nq = s // tq
    hs = MLA_HEADS_PER_STEP
    return pl.pallas_call(
        _mla_attn_body,
        grid=(b, h // hs, nq),
        in_specs=[
            pl.BlockSpec((1, hs, tq, dk), lambda bi, hi, qi: (bi, hi, qi, 0)),
            pl.BlockSpec((1, hs, dk, s), lambda bi, hi, qi: (bi, hi, 0, 0)),
            pl.BlockSpec((1, hs, s, V_EXT), lambda bi, hi, qi: (bi, hi, 0, 0)),
            pl.BlockSpec((1, hs, dk, META_PAD), lambda bi, hi, qi: (0, hi, 0, 0)),
            pl.BlockSpec((1, hs, META_PAD, V_EXT), lambda bi, hi, qi: (0, hi, 0, 0)),
        ],
        out_specs=pl.BlockSpec((tq, hs * V_DIM), lambda bi, hi, qi: (bi * nq + qi, hi)),
        out_shape=jax.ShapeDtypeStruct((b * s, h * V_DIM), BF16),
        compiler_params=pltpu.CompilerParams(dimension_semantics=("parallel", "parallel", "parallel"),
                                             vmem_limit_bytes=VMEM_LIMIT),
        name="mla_attn",
    )(qm, kmt, vm, kmeta_t, vmeta)


def _gqa_attention(qg, kgt, vg, kmeta_t, vmeta):
    b, h, s, d = qg.shape
    hk = kgt.shape[1]
    tq = Q_TILE_GQA
    nq = s // tq
    return pl.pallas_call(
        _gqa_attn_body,
        grid=(b, hk, nq),
        in_specs=[
            pl.BlockSpec((1, 2, tq, d), lambda bi, ji, qi: (bi, ji, qi, 0)),
            pl.BlockSpec((1, 1, d, s), lambda bi, ji, qi: (bi, ji, 0, 0)),
            pl.BlockSpec((1, 1, s, V_EXT), lambda bi, ji, qi: (bi, ji, 0, 0)),
            pl.BlockSpec((1, 1, d, META_PAD), lambda bi, ji, qi: (0, ji, 0, 0)),
            pl.BlockSpec((1, 1, META_PAD, V_EXT), lambda bi, ji, qi: (0, ji, 0, 0)),
        ],
        out_specs=pl.BlockSpec((tq, 2 * d), lambda bi, ji, qi: (bi * nq + qi, ji)),
        out_shape=jax.ShapeDtypeStruct((b * s, h * d), BF16),
        compiler_params=pltpu.CompilerParams(dimension_semantics=("parallel", "parallel", "parallel"),
                                             vmem_limit_bytes=VMEM_LIMIT),
        name="gqa_attn",
    )(qg, kgt, vg, kmeta_t, vmeta)


def _to_tiles(ref, x):
    r = x.shape[0]
    for s in range(x.shape[1] // LANES):
        ref[:, s] = x[:, LANES * s:LANES * (s + 1)].reshape(r // SUBLANES, SUBLANES, LANES)


def _from_tiles(ref):
    r = ref.shape[0] * SUBLANES
    return jnp.concatenate([ref[:, s].reshape(r, LANES) for s in range(ref.shape[1])], axis=1)


MERGE_SUB = 256


def _pack_rows(v):
    c = v.shape[1] // 2
    lo = lax.bitcast_convert_type(v[:, 0:c].astype(BF16).astype(F32), I32)
    hi = lax.bitcast_convert_type(v[:, c:2 * c].astype(BF16).astype(F32), I32)
    return lax.shift_right_logical(lo, 16) | hi


def _unpack_words(w):
    return (lax.bitcast_convert_type(w << 16, F32), lax.bitcast_convert_type(w & jnp.int32(-65536), F32))


def _merge_body(om_ref, og_ref, x_ref, lng_ref, lnb_ref, gom_ref, gog_ref, wo_ref, l1g_ref, l1b_ref,
                wr_ref, br_ref, h1_ref, h1p_ref, topi_ref, gate_ref):
    for i in range(x_ref.shape[0] // MERGE_SUB):
        rows = pl.ds(i * MERGE_SUB, MERGE_SUB)
        h1, ti, gt = _merge_chain(om_ref[rows, :], og_ref[rows, :], x_ref[rows, :], lng_ref[...], lnb_ref[...],
                                  gom_ref[...], gog_ref[...], wo_ref, l1g_ref[...], l1b_ref[...], wr_ref, br_ref[...])
        blocks = pl.ds(i * (MERGE_SUB // SUBLANES), MERGE_SUB // SUBLANES)
        _to_tiles(h1_ref.at[blocks], h1)
        _to_tiles(h1p_ref.at[blocks], _pack_rows(h1))
        topi_ref[:, rows] = ti
        for k in range(TOP_K):
            gate_ref[0, :, pl.ds(k * TOKEN_TILE + i * MERGE_SUB, MERGE_SUB)] = gt[k:k + 1, :]


def _merge_chain(om, og, x, lng, lnb, gom, gog, wo_ref, l1g, l1b, wr_ref, br):
    t = x.shape[0]
    h0 = _layernorm(x, lng, lnb)
    nm = _rmsnorm(om.astype(F32), gom).astype(BF16)
    ng = _rmsnorm(og.astype(F32), gog).astype(BF16)
    half = nm.shape[1]
    mix = jnp.dot(nm, wo_ref[0:half, :], preferred_element_type=F32)
    mix = mix + jnp.dot(ng, wo_ref[half:2 * half, :], preferred_element_type=F32)
    h1 = _layernorm(DEEPNORM_ALPHA * h0 + mix, l1g, l1b)

    hi = h1.astype(BF16)
    lo = (h1 - hi.astype(F32)).astype(BF16)
    acc = jnp.dot(hi, wr_ref[...], preferred_element_type=F32) + jnp.dot(lo, wr_ref[...], preferred_element_type=F32)
    logits = acc[:, 0:LANES] + acc[:, LANES:2 * LANES] + br
    cur = logits.T[0:N_EXPERTS, :]
    eidx = lax.broadcasted_iota(I32, cur.shape, 0)
    vals, idxs = [], []
    for _ in range(TOP_K):
        m = jnp.max(cur, axis=0, keepdims=True)
        i = jnp.min(jnp.where(cur == m, eidx, N_EXPERTS), axis=0, keepdims=True)
        vals.append(m)
        idxs.append(i)
        cur = jnp.where(eidx == i, -jnp.inf, cur)
    ex = [jnp.exp(v - vals[0]) for v in vals]
    den = ex[0] + ex[1] + ex[2] + ex[3]
    sub = lax.broadcasted_iota(I32, (8, t), 0)
    ti = jnp.zeros((8, t), I32)
    gt = jnp.zeros((8, t), F32)
    for k in range(TOP_K):
        ti = jnp.where(sub == k, idxs[k], ti)
        gt = jnp.where(sub == k, ex[k] / den, gt)
    return h1, ti[0:TOP_K, :], gt[0:TOP_K, :]


def _merge(o_mla, o_gqa, x2, w):
    n, d = x2.shape
    tile = TOKEN_TILE
    half = o_mla.shape[1]
    full = lambda shape: pl.BlockSpec(shape, lambda i: (0,) * len(shape))
    row = lambda width: pl.BlockSpec((tile, width), lambda i: (i, 0))
    return pl.pallas_call(
        _merge_body,
        grid=(n // tile,),
        in_specs=[row(half), row(half), row(d), full((1, d)), full((1, d)), full((1, half)), full((1, half)),
                  full((d, d)), full((1, d)), full((1, d)), full((d, 2 * LANES)), full((1, LANES))],
        out_specs=[pl.BlockSpec((tile // SUBLANES, d // LANES, SUBLANES, LANES), lambda i: (i, 0, 0, 0)),
                   pl.BlockSpec((tile // SUBLANES, PACK_CHUNKS, SUBLANES, LANES), lambda i: (i, 0, 0, 0)),
                   pl.BlockSpec((TOP_K, tile), lambda i: (0, i)),
                   pl.BlockSpec((1, 1, TOP_K * tile), lambda i: (i, 0, 0))],
        out_shape=[jax.ShapeDtypeStruct((n // SUBLANES, d // LANES, SUBLANES, LANES), F32),
                   jax.ShapeDtypeStruct((n // SUBLANES, PACK_CHUNKS, SUBLANES, LANES), I32),
                   jax.ShapeDtypeStruct((TOP_K, n), I32), jax.ShapeDtypeStruct((n // tile, 1, TOP_K * tile), F32)],
        compiler_params=pltpu.CompilerParams(dimension_semantics=("parallel",), vmem_limit_bytes=VMEM_LIMIT),
        name="merge",
    )(o_mla, o_gqa, x2, w["ln_emb_g"], w["ln_emb_b"], w["g_o_mla"], w["g_o_gqa"], w["w_o"],
      w["ln1_g"], w["ln1_b"], w["w_router"], w["b_router"])


def _lanes_from_sublanes(col):
    diag = lax.broadcasted_iota(I32, col.shape, 0) == lax.broadcasted_iota(I32, col.shape, 1)
    return jnp.sum(jnp.where(diag, col, 0.0), axis=0, keepdims=True)


def _positions_body(topi_ref, upper_ref, lpb_ref, gstart_ref, nslab_ref, lstart_ref, texp_ref, nvalid_ref, ecnt_ref,
                    eoff_ref, cnt_sc, carry_sc, off_sc, *, ntp):
    p = pl.program_id(0)
    j = pl.program_id(1)
    tl = topi_ref.shape[1]
    topi = topi_ref[...]
    eidx = lax.broadcasted_iota(I32, (N_EXPERTS, tl), 0)
    ohs = [eidx == topi[k:k + 1, :] for k in range(TOP_K)]
    onehot = ohs[0].astype(F32) + ohs[1].astype(F32) + ohs[2].astype(F32) + ohs[3].astype(F32)
    cnt = jnp.sum(onehot, axis=1, keepdims=True)
    run = jnp.floor((cnt + (SUBLANES - 1)) * (1.0 / SUBLANES)) * SUBLANES
    tile_run = jnp.broadcast_to(run, (N_EXPERTS, LANES))
    r = lax.broadcasted_iota(I32, (N_EXPERTS, N_EXPERTS), 0)
    c = lax.broadcasted_iota(I32, (N_EXPERTS, N_EXPERTS), 1)
    lower = (c < r).astype(F32)

    @pl.when(jnp.logical_and(p == 0, j == 0))
    def _():
        cnt_sc[...] = jnp.zeros_like(cnt_sc)

    @pl.when(p == 0)
    def _():
        cnt_sc[...] += tile_run

    @pl.when(jnp.logical_and(p == 1, j == 0))
    def _():
        tot = cnt_sc[...]
        pc = jnp.floor((tot + (MOE_TILE - 1)) * (1.0 / MOE_TILE)) * MOE_TILE
        off = jnp.dot(lower, pc, precision=lax.Precision.HIGHEST, preferred_element_type=F32)
        off_sc[...] = off
        carry_sc[...] = jnp.zeros_like(carry_sc)
        cumend = off + pc
        tstart = lax.broadcasted_iota(I32, (N_EXPERTS, ntp), 1).astype(F32) * MOE_TILE
        te = jnp.sum((jnp.broadcast_to(cumend[:, 0:1], (N_EXPERTS, ntp)) <= tstart).astype(I32), axis=0, keepdims=True)
        texp_ref[...] = jnp.minimum(te, N_EXPERTS - 1)
        nvalid_ref[...] = (cumend[N_EXPERTS - 1:N_EXPERTS, :] * (1.0 / MOE_TILE)).astype(I32)
        ecnt_ref[...] = _lanes_from_sublanes(tot).astype(I32)
        eoff_ref[...] = _lanes_from_sublanes(off).astype(I32)

    @pl.when(p == 1)
    def _():
        before = jnp.dot(onehot.astype(BF16), upper_ref[...], preferred_element_type=F32)
        loff = jnp.dot(lower, tile_run, precision=lax.Precision.HIGHEST, preferred_element_type=F32)
        base = before + loff[:, 0:1]
        sub = lax.broadcasted_iota(I32, (SUBLANES, tl), 0)
        out = jnp.zeros((SUBLANES, tl), F32)
        for k in range(TOP_K):
            pk = jnp.sum(jnp.where(ohs[k], base, 0.0), axis=0, keepdims=True)
            out = jnp.where(sub == k, pk, out)
        lp = out[0:TOP_K, :].astype(I32)
        lpb = (lp >> SUBLANE_SHIFT) * PSLAB + (lp & (SUBLANES - 1))
        lpb_ref[0] = jnp.concatenate([lpb[k:k + 1, :] for k in range(TOP_K)], axis=1)
        inv = 1.0 / SUBLANES
        gstart_ref[0] = (_lanes_from_sublanes(off_sc[...] + carry_sc[...]) * inv).astype(I32)
        nslab_ref[0] = (_lanes_from_sublanes(tile_run) * inv).astype(I32)
        lstart_ref[0] = (_lanes_from_sublanes(loff) * inv).astype(I32)
        carry_sc[...] += tile_run


def _positions(topi, ntp):
    n = topi.shape[1]
    tl = TOKEN_TILE
    const = lambda shape: pl.BlockSpec(shape, lambda p, j: (0, 0))
    per_tile = pl.BlockSpec((1, 1, LANES), lambda p, j: (j * p, 0, 0))
    tab = jax.ShapeDtypeStruct((n // tl, 1, LANES), I32)
    upper = (jnp.arange(tl, dtype=I32)[:, None] < jnp.arange(tl, dtype=I32)[None, :]).astype(BF16)
    return pl.pallas_call(
        functools.partial(_positions_body, ntp=ntp),
        grid=(2, n // tl),
        in_specs=[pl.BlockSpec((TOP_K, tl), lambda p, j: (0, j)), const((tl, tl))],
        out_specs=[pl.BlockSpec((1, 1, TOP_K * tl), lambda p, j: (j * p, 0, 0)), per_tile, per_tile, per_tile,
                   const((1, ntp)), const((1, LANES)), const((1, LANES)), const((1, LANES))],
        out_shape=[jax.ShapeDtypeStruct((n // tl, 1, TOP_K * tl), I32), tab, tab, tab,
                   jax.ShapeDtypeStruct((1, ntp), I32),
                   jax.ShapeDtypeStruct((1, LANES), I32), jax.ShapeDtypeStruct((1, LANES), I32),
                   jax.ShapeDtypeStruct((1, LANES), I32)],
        scratch_shapes=[pltpu.VMEM((N_EXPERTS, LANES), F32)] * 3,
        compiler_params=pltpu.CompilerParams(dimension_semantics=("arbitrary", "arbitrary"),
                                             vmem_limit_bytes=VMEM_LIMIT),
        name="positions",
    )(topi, upper)


STAGE_ROWS = TOKEN_TILE * TOP_K + N_EXPERTS * SUBLANES
STAGE_FLAT = STAGE_ROWS * PACK_CHUNKS


def _load_row(ref, flat_start, chunks=SUBLANES):
    return ref[pl.ds(flat_start, chunks, stride=SUBLANES), :]


def _store_row(ref, flat_start, v):
    ref[pl.ds(flat_start, v.shape[0], stride=SUBLANES), :] = v


CHUNK_SHIFT = 2
CHUNK_SLABS = 1 << CHUNK_SHIFT


def _copy_run(src_ref, src0, dst_ref, dst0, nslab, sem):
    big = CHUNK_SLABS * PSLAB
    nbig = nslab >> CHUNK_SHIFT
    nsmall = nslab & (CHUNK_SLABS - 1)

    def start_big(s, c):
        pltpu.make_async_copy(src_ref.at[pl.ds(src0 + s * big, big)], dst_ref.at[pl.ds(dst0 + s * big, big)], sem).start()
        return c

    def start_small(s, c):
        o = nbig * big + s * PSLAB
        pltpu.make_async_copy(src_ref.at[pl.ds(src0 + o, PSLAB)], dst_ref.at[pl.ds(dst0 + o, PSLAB)], sem).start()
        return c

    lax.fori_loop(0, nbig, start_big, 0)
    lax.fori_loop(0, nsmall, start_small, 0)


def _wait_runs(src_ref, dst_ref, nbig, nsmall, sem):
    big = CHUNK_SLABS * PSLAB

    def wait_big(s, c):
        pltpu.make_async_copy(src_ref.at[pl.ds(0, big)], dst_ref.at[pl.ds(0, big)], sem).wait()
        return c

    def wait_small(s, c):
        pltpu.make_async_copy(src_ref.at[pl.ds(0, PSLAB)], dst_ref.at[pl.ds(0, PSLAB)], sem).wait()
        return c

    lax.fori_loop(0, nbig, wait_big, 0)
    lax.fori_loop(0, nsmall, wait_small, 0)


def _chunk_counts(nslab_ref, j):
    def add(e, c):
        n = nslab_ref[j, 0, e]
        return c[0] + (n >> CHUNK_SHIFT), c[1] + (n & (CHUNK_SLABS - 1))

    return lax.fori_loop(0, N_EXPERTS, add, (jnp.int32(0), jnp.int32(0)))


def _dispatch_body(ecnt_ref, eoff_ref, lpb_ref, gstart_ref, nslab_ref, lstart_ref, h_ref, xs_ref,
                   stg0_ref, stg1_ref, zero_ref, issued_ref, sem, zsem):
    g = pl.program_id(0)
    ng = pl.num_programs(0)
    stages = (stg0_ref, stg1_ref)

    def drain(j):
        _wait_runs(stages[j], xs_ref, issued_ref[2 * j], issued_ref[2 * j + 1], sem.at[j])

    def run(j):
        stg_ref = stages[j]

        @pl.when(g >= 2)
        def _():
            drain(j)

        def zero_last(e, c):
            last = jnp.maximum(lstart_ref[0, 0, e] + nslab_ref[0, 0, e] - 1, 0)
            stg_ref[pl.ds(last * PSLAB, PSLAB), :] = jnp.zeros((PSLAB, LANES), I32)
            return c

        lax.fori_loop(0, N_EXPERTS, zero_last, 0)

        def move(i, c):
            for u in range(SUBLANES):
                v = _load_row(h_ref, i * PSLAB + u, PACK_CHUNKS)
                for k in range(TOP_K):
                    _store_row(stg_ref, lpb_ref[0, 0, k * TOKEN_TILE + i * SUBLANES + u], v)
            return c

        lax.fori_loop(0, TOKEN_TILE // SUBLANES, move, 0)

        def send_run(e, c):
            _copy_run(stg_ref, lstart_ref[0, 0, e] * PSLAB, xs_ref, gstart_ref[0, 0, e] * PSLAB,
                      nslab_ref[0, 0, e], sem.at[j])
            return c

        lax.fori_loop(0, N_EXPERTS, send_run, 0)
        nbig, nsmall = _chunk_counts(nslab_ref, 0)
        issued_ref[2 * j] = nbig
        issued_ref[2 * j + 1] = nsmall

    for j in range(2):
        @pl.when(lax.rem(g, 2) == j)
        def _():
            run(j)

    @pl.when(g == ng - 1)
    def _():
        zero_ref[...] = jnp.zeros_like(zero_ref)

        def zero_copy(dst_slab):
            return pltpu.make_async_copy(zero_ref, xs_ref.at[pl.ds(dst_slab * PSLAB, PSLAB)], zsem)

        def pad_segment(e, c):
            rows = ecnt_ref[e]
            first = (eoff_ref[e] + rows) >> SUBLANE_SHIFT
            npad = lax.rem(MOE_TILE - lax.rem(rows, MOE_TILE), MOE_TILE) >> SUBLANE_SHIFT
            lax.fori_loop(0, npad, lambda s, cc: (zero_copy(first + s).start(), cc)[1], 0)
            lax.fori_loop(0, npad, lambda s, cc: (zero_copy(first + s).wait(), cc)[1], 0)
            return c

        lax.fori_loop(0, N_EXPERTS, pad_segment, 0)
        for j in range(2):
            @pl.when(jnp.logical_and(ng > 1, lax.rem(g, 2) != j))
            def _():
                drain(j)

        for j in range(2):
            @pl.when(lax.rem(g, 2) == j)
            def _():
                drain(j)


def _dispatch(h1_flat, lpb, gstart, nslab, lstart, ecnt, eoff, rows_pad):
    nflat = h1_flat.shape[0]
    t = TOKEN_TILE
    smem = lambda shape, imap: pl.BlockSpec(shape, imap, memory_space=pltpu.SMEM)
    per_tile = smem((1, 1, LANES), lambda i, c, o: (i, 0, 0))
    grid_spec = pltpu.PrefetchScalarGridSpec(
        num_scalar_prefetch=2,
        grid=(nflat // (t * PACK_CHUNKS),),
        in_specs=[smem((1, 1, TOP_K * t), lambda i, c, o: (i, 0, 0)), per_tile, per_tile, per_tile,
                  pl.BlockSpec((t * PACK_CHUNKS, LANES), lambda i, c, o: (i, 0))],
        out_specs=pl.BlockSpec(memory_space=pl.ANY),
        scratch_shapes=[pltpu.VMEM((STAGE_FLAT, LANES), I32), pltpu.VMEM((STAGE_FLAT, LANES), I32),
                        pltpu.VMEM((PSLAB, LANES), I32),
                        pltpu.SMEM((4,), I32), pltpu.SemaphoreType.DMA((2,)), pltpu.SemaphoreType.DMA],
    )
    return pl.pallas_call(
        _dispatch_body,
        grid_spec=grid_spec,
        out_shape=jax.ShapeDtypeStruct((rows_pad * PACK_CHUNKS, LANES), I32),
        compiler_params=pltpu.CompilerParams(dimension_semantics=("arbitrary",), vmem_limit_bytes=VMEM_LIMIT),
        name="dispatch",
    )(ecnt, eoff, lpb, gstart, nslab, lstart, h1_flat)


def _ffn_body(texp_ref, nvalid_ref, ecnt_ref, eoff_ref, x_ref, wgu_hbm, wd_hbm, bg_ref, bl_ref, bd_ref, y_ref,
              wgu_buf, wd_buf, wg_sc, wl_sc, wd_sc, seg_ref, sem):
    i = pl.program_id(0)
    valid = i < nvalid_ref[0]
    e = texp_ref[i]
    first_of_expert = jnp.logical_or(i == 0, e != texp_ref[jnp.maximum(i - 1, 0)])

    def weight_copies(expert, slot):
        return (pltpu.make_async_copy(wgu_hbm.at[expert], wgu_buf.at[slot], sem.at[slot]),
                pltpu.make_async_copy(wd_hbm.at[expert], wd_buf.at[slot], sem.at[slot]))

    def prepare(slot):
        blk = 2 * LANES
        r = lax.broadcasted_iota(I32, (blk, blk), 0)
        c = lax.broadcasted_iota(I32, (blk, blk), 1)
        sel = (r == jnp.where(c < LANES, 2 * c, 2 * (c - LANES) + 1)).astype(BF16)
        for b in range(wgu_buf.shape[2] // blk):
            wb = wgu_buf[slot, :, blk * b:blk * (b + 1)].astype(BF16)
            y = jnp.dot(wb, sel, preferred_element_type=F32).astype(BF16)
            wg_sc[:, LANES * b:LANES * (b + 1)] = y[:, 0:LANES]
            wl_sc[:, LANES * b:LANES * (b + 1)] = y[:, LANES:blk]
        wd_sc[...] = wd_buf[slot].astype(BF16)

    @pl.when(jnp.logical_and(valid, first_of_expert))
    def _():
        @pl.when(i == 0)
        def _():
            seg_ref[0] = 0
            for cp in weight_copies(e, 0):
                cp.start()

        seg = seg_ref[0]
        seg_ref[0] = seg + 1
        nxt = lax.while_loop(lambda n: jnp.logical_and(n < N_EXPERTS, ecnt_ref[jnp.minimum(n, N_EXPERTS - 1)] == 0),
                             lambda n: n + 1, e + 1)
        for slot in range(2):
            @pl.when(lax.rem(seg, 2) == slot)
            def _():
                for cp in weight_copies(e, slot):
                    cp.wait()

                @pl.when(nxt < N_EXPERTS)
                def _():
                    for cp in weight_copies(nxt, 1 - slot):
                        cp.start()

                prepare(slot)

    def chain(b0, nb):
        blocks = pl.ds(b0, nb)
        x = jnp.concatenate(_unpack_words(_from_tiles(x_ref.at[blocks])), axis=1).astype(BF16)
        hg = jnp.dot(x, wg_sc[...], preferred_element_type=F32) + bg_ref[0]
        hl = jnp.dot(x, wl_sc[...], preferred_element_type=F32) + bl_ref[0]
        g = jnp.minimum(hg, SWIGLU_LIMIT)
        lin = jnp.clip(hl, -SWIGLU_LIMIT, SWIGLU_LIMIT)
        act = g * (1.0 / (1.0 + jnp.exp(-SWIGLU_ALPHA * g))) * (lin + 1.0)
        y = jnp.dot(act.astype(BF16), wd_sc[...], preferred_element_type=F32) + bd_ref[0]
        _to_tiles(y_ref.at[blocks], _pack_rows(y))

    def compute(nrows):
        done = 0
        while done < nrows:
            n = min(FFN_CHAIN, nrows - done)
            chain(done // SUBLANES, n // SUBLANES)
            done += n
        rest = y_ref.shape[0] - nrows // SUBLANES
        if rest:
            y_ref[pl.ds(nrows // SUBLANES, rest)] = jnp.zeros((rest,) + y_ref.shape[1:], I32)

    used = eoff_ref[e] + ecnt_ref[e] - i * MOE_TILE
    quarter = MOE_TILE // 4
    for q in range(1, 5):
        lo, hi = (q - 1) * quarter, q * quarter
        in_range = jnp.logical_and(used > lo, used <= hi) if q < 4 else used > lo

        @pl.when(jnp.logical_and(valid, in_range))
        def _():
            compute(hi)


def _grouped_ffn(xs_tiles, texp, nvalid, ecnt, eoff, wgu, wd, bg, bl, bd, ntiles):
    d = wgu.shape[1]
    tm = MOE_TILE
    f = wd.shape[1]
    xmap = lambda i, te, nv, ec, eo: (jnp.minimum(i, nv[0] - 1), 0, 0, 0)
    wmap = lambda i, te, nv, ec, eo: (te[i], 0, 0)
    rows_blk = pl.BlockSpec((tm // SUBLANES, PACK_CHUNKS, SUBLANES, LANES), xmap)
    hbm = pl.BlockSpec(memory_space=pl.ANY)
    grid_spec = pltpu.PrefetchScalarGridSpec(
        num_scalar_prefetch=4,
        grid=(ntiles,),
        in_specs=[rows_blk, hbm, hbm,
                  pl.BlockSpec((1, 1, f), wmap), pl.BlockSpec((1, 1, f), wmap), pl.BlockSpec((1, 1, d), wmap)],
        out_specs=rows_blk,
        scratch_shapes=[pltpu.VMEM((2, d, 2 * f), F32), pltpu.VMEM((2, f, d), F32),
                        pltpu.VMEM((d, f), BF16), pltpu.VMEM((d, f), BF16), pltpu.VMEM((f, d), BF16),
                        pltpu.SMEM((1,), I32), pltpu.SemaphoreType.DMA((2,))],
    )
    return pl.pallas_call(
        _ffn_body,
        grid_spec=grid_spec,
        out_shape=jax.ShapeDtypeStruct(xs_tiles.shape, I32),
        compiler_params=pltpu.CompilerParams(dimension_semantics=("arbitrary",), vmem_limit_bytes=VMEM_LIMIT),
        name="ffn",
    )(texp, nvalid, ecnt, eoff, xs_tiles, wgu, wd, bg, bl, bd)


COMBINE_SUB = 512


def _combine_body(lpb_ref, gate_ref, gstart_ref, nslab_ref, lstart_ref, gstart2_ref, nslab2_ref, lstart2_ref,
                  h_ref, l2g_ref, l2b_ref, ys_ref, o_ref, stg0_ref, stg1_ref, moe_ref, sem):
    g = pl.program_id(0)
    ng = pl.num_programs(0)
    stages = (stg0_ref, stg1_ref)

    def fetch(gs_ref, ns_ref, ls_ref, j):
        def per_expert(e, c):
            _copy_run(ys_ref, gs_ref[0, 0, e] * PSLAB, stages[j], ls_ref[0, 0, e] * PSLAB, ns_ref[0, 0, e], sem.at[j])
            return c

        lax.fori_loop(0, N_EXPERTS, per_expert, 0)

    def wait_tile(j):
        nbig, nsmall = _chunk_counts(nslab_ref, 0)
        _wait_runs(ys_ref, stages[j], nbig, nsmall, sem.at[j])

    def gather(j):
        stg_ref = stages[j]

        def body(i, c):
            for u in range(SUBLANES):
                tok = i * SUBLANES + u
                acc_lo = acc_hi = None
                for k in range(TOP_K):
                    gk = gate_ref[0, 0, k * TOKEN_TILE + tok]
                    lo, hi = _unpack_words(_load_row(stg_ref, lpb_ref[0, 0, k * TOKEN_TILE + tok], PACK_CHUNKS))
                    acc_lo = gk * lo if k == 0 else acc_lo + gk * lo
                    acc_hi = gk * hi if k == 0 else acc_hi + gk * hi
                _store_row(moe_ref, i * SLAB + u, acc_lo)
                _store_row(moe_ref, i * SLAB + PSLAB + u, acc_hi)
            return c

        lax.fori_loop(0, TOKEN_TILE // SUBLANES, body, 0)

    def finish():
        nblk = COMBINE_SUB // SUBLANES
        for part in range(TOKEN_TILE // COMBINE_SUB):
            moe = jnp.concatenate(
                [jnp.concatenate([moe_ref[pl.ds((part * nblk + i) * SLAB + s * SUBLANES, SUBLANES), :]
                                  for i in range(nblk)], axis=0) for s in range(SUBLANES)], axis=1)
            h1 = _from_tiles(h_ref.at[pl.ds(part * nblk, nblk)])
            o_ref[pl.ds(part * COMBINE_SUB, COMBINE_SUB), :] = _layernorm(DEEPNORM_ALPHA * h1 + moe, l2g_ref[...],
                                                                          l2b_ref[...])

    @pl.when(g == 0)
    def _():
        fetch(gstart_ref, nslab_ref, lstart_ref, 0)

    for j in range(2):
        @pl.when(lax.rem(g, 2) == j)
        def _():
            @pl.when(g + 1 < ng)
            def _():
                fetch(gstart2_ref, nslab2_ref, lstart2_ref, 1 - j)

            wait_tile(j)
            gather(j)

    finish()


def _combine(lpb, gates, gstart, nslab, lstart, h1_tiles, ys_flat, w):
    d = h1_tiles.shape[1] * LANES
    n = h1_tiles.shape[0] * SUBLANES
    t = TOKEN_TILE
    ng = n // t
    full = lambda shape: pl.BlockSpec(shape, lambda i: (0,) * len(shape))
    smem = lambda shape, imap: pl.BlockSpec(shape, imap, memory_space=pltpu.SMEM)
    vec = smem((1, 1, TOP_K * t), lambda i: (i, 0, 0))
    this_step = smem((1, 1, LANES), lambda i: (i, 0, 0))
    next_step = smem((1, 1, LANES), lambda i: (jnp.minimum(i + 1, ng - 1), 0, 0))
    return pl.pallas_call(
        _combine_body,
        grid=(ng,),
        in_specs=[vec, vec, this_step, this_step, this_step, next_step, next_step, next_step,
                  pl.BlockSpec((t // SUBLANES, d // LANES, SUBLANES, LANES), lambda i: (i, 0, 0, 0)),
                  full((1, d)), full((1, d)),
                  pl.BlockSpec(memory_space=pl.ANY)],
        out_specs=pl.BlockSpec((t, d), lambda i: (i, 0)),
        out_shape=jax.ShapeDtypeStruct((n, d), F32),
        scratch_shapes=[pltpu.VMEM((STAGE_FLAT, LANES), I32), pltpu.VMEM((STAGE_FLAT, LANES), I32),
                        pltpu.VMEM((TOKEN_TILE * SUBLANES, LANES), F32), pltpu.SemaphoreType.DMA((2,))],
        compiler_params=pltpu.CompilerParams(dimension_semantics=("arbitrary",), vmem_limit_bytes=VMEM_LIMIT),
        name="combine",
    )(lpb, gates, gstart, nslab, lstart, gstart, nslab, lstart, h1_tiles, w["ln2_g"], w["ln2_b"], ys_flat)


def _rope_tables(pos_1d, row, col, g_k):
    inv = ROPE_THETA ** (-jnp.arange(0, ROPE_DIM, 2, dtype=F32) / ROPE_DIM)

    def cs(p):
        ang = p.astype(F32)[:, None] * inv[None, :]
        return jnp.cos(ang), jnp.sin(ang)

    c1, s1 = cs(pos_1d)
    cr, sr = cs(row)
    cc, sc = cs(col)
    c1w, s1w = jnp.concatenate([c1, c1, c1, c1], axis=1), jnp.concatenate([-s1, s1, -s1, s1], axis=1)
    caw, saw = jnp.concatenate([cr, cr, cc, cc], axis=1), jnp.concatenate([-sr, sr, -sc, sc], axis=1)
    ckt = caw.T * g_k[:, None]
    skt = saw.T * _swap_halves64(g_k)[:, None]
    return (c1w, s1w, caw, saw, c1w[:, :ROPE_DIM].T, s1w[:, :ROPE_DIM].T, ckt, skt)


def _prep_weights(ln_emb_g, ln_emb_b, w_in, g_q_a, w_q_b, g_kv_a, w_kv_b, g_q_gqa, g_k_gqa, g_o_mla, g_o_gqa, w_o,
                  ln1_g, ln1_b, w_router, b_router, ln2_g, ln2_b):
    r2 = lambda v: v.reshape(1, -1).astype(F32)
    o = np.cumsum([0, Q_LORA, KV_LORA, ROPE_DIM, GQA_HEADS * GQA_DIM, GQA_KV_HEADS * GQA_DIM, GQA_KV_HEADS * GQA_DIM])
    wi = w_in[0]
    seg = [wi[:, o[i]:o[i + 1]] for i in range(6)]
    w_in_p = jnp.concatenate([seg[0], seg[1], seg[3], _swap_halves64(seg[3]), seg[5]], axis=1).astype(BF16)
    w_kt = jnp.concatenate([seg[2], seg[4]], axis=1).T.astype(BF16)
    wq = w_q_b[0].reshape(Q_LORA, MLA_HEADS, QK_DIM)
    wq_rope = wq[:, :, NOPE_DIM:].reshape(Q_LORA, -1)
    w_qb = jnp.concatenate([wq[:, :, :NOPE_DIM].reshape(Q_LORA, -1), wq_rope, _swap_halves64(wq_rope)],
                           axis=1).astype(BF16)
    wk = w_kv_b[0].reshape(KV_LORA, MLA_HEADS, NOPE_DIM + V_DIM)
    w_kvb_kt = wk[:, :, :NOPE_DIM].reshape(KV_LORA, -1).T.astype(BF16)
    w_kvb_v = wk[:, :, NOPE_DIM:].reshape(KV_LORA, -1).astype(BF16)
    wr = jnp.pad(w_router[0].astype(F32), ((0, 0), (0, LANES - N_EXPERTS)))
    wr_hi = wr.astype(BF16)
    w_router_p = jnp.concatenate([wr_hi, (wr - wr_hi.astype(F32)).astype(BF16)], axis=1)
    b_router_p = jnp.pad(b_router[0].astype(F32), (0, LANES - N_EXPERTS), constant_values=NEG_BIG).reshape(1, LANES)
    return dict(
        ln_emb_g=r2(ln_emb_g), ln_emb_b=r2(ln_emb_b), w_in=w_in_p, w_kt=w_kt, g_q_a=r2(g_q_a[0]), w_qb=w_qb,
        g_kv_a=r2(g_kv_a[0]), w_kvb_v=w_kvb_v, w_kvb_kt=w_kvb_kt, g_q_gqa=r2(g_q_gqa[0]),
        g_q_gqa_sw=_swap_halves64(r2(g_q_gqa[0])),
        g_o_mla=r2(g_o_mla[0]), g_o_gqa=r2(g_o_gqa[0]), w_o=w_o[0].astype(BF16), ln1_g=r2(ln1_g[0]),
        ln1_b=r2(ln1_b[0]), w_router=w_router_p, b_router=b_router_p, ln2_g=r2(ln2_g[0]), ln2_b=r2(ln2_b[0]))


def kernel(x, meta_tokens, ln_emb_g, ln_emb_b, w_in, g_q_a, w_q_b, g_kv_a, w_kv_b, g_q_gqa, g_k_gqa, g_o_mla, g_o_gqa,
           w_o, ln1_g, ln1_b, w_router, b_router, w_gate_up, b_gate_up, w_down, b_down, ln2_g, ln2_b):
    b, s, d = x.shape
    n = b * s
    w = _prep_weights(ln_emb_g, ln_emb_b, w_in, g_q_a, w_q_b, g_kv_a, w_kv_b, g_q_gqa, g_k_gqa, g_o_mla, g_o_gqa,
                      w_o, ln1_g, ln1_b, w_router, b_router, ln2_g, ln2_b)

    tok = jnp.arange(s, dtype=I32)
    g_k = g_k_gqa[0].astype(F32)
    tabs_real = _rope_tables(tok + N_META, tok // GRID_W, tok % GRID_W, g_k)
    mt = jnp.arange(META_PAD, dtype=I32)
    tabs_meta = _rope_tables(mt, jnp.full((META_PAD,), -1, I32), mt, g_k)

    qm, km, vm, qg, kg, vg = _project(x, tabs_real, w, ROW_TILE)
    meta = jnp.pad(meta_tokens.astype(x.dtype), ((0, META_PAD - N_META), (0, 0))).reshape(1, META_PAD, d)
    _, km_m, vm_m, _, kg_m, vg_m = _project(meta, tabs_meta, w, META_PAD)

    o_mla = _mla_attention(qm, km, vm, km_m, vm_m)
    o_gqa = _gqa_attention(qg, kg, vg, kg_m, vg_m)

    h1_tiles, h1_packed, topi, gates = _merge(o_mla, o_gqa, x.reshape(n, d), w)

    run_rows = n * TOP_K + (n // TOKEN_TILE) * N_EXPERTS * (SUBLANES - 1)
    ntiles = -(-run_rows // MOE_TILE) + N_EXPERTS
    ntp = -(-ntiles // LANES) * LANES
    rows_pad = ntiles * MOE_TILE
    lpb, gstart, nslab, lstart, texp, nvalid, ecnt, eoff = _positions(topi, ntp)
    flat = lambda a: a.reshape(-1, LANES)
    xs_flat = _dispatch(flat(h1_packed), lpb, gstart, nslab, lstart, ecnt.reshape(-1), eoff.reshape(-1), rows_pad)

    bgu = b_gate_up[0].astype(F32)
    tiles = lambda a: a.reshape(-1, PACK_CHUNKS, SUBLANES, LANES)
    ys_tiles = _grouped_ffn(tiles(xs_flat), texp.reshape(-1), nvalid.reshape(-1), ecnt.reshape(-1), eoff.reshape(-1),
                            w_gate_up[0], w_down[0],
                            bgu[:, 0::2].reshape(N_EXPERTS, 1, D_FF), bgu[:, 1::2].reshape(N_EXPERTS, 1, D_FF),
                            b_down[0].astype(F32).reshape(N_EXPERTS, 1, d), ntiles)

    out = _combine(lpb, gates, gstart, nslab, lstart, h1_tiles, flat(ys_tiles), w)
    return out.reshape(b, s, d)
```

```python
import functools

import jax
import jax.numpy as jnp
import numpy as np
from jax import lax
from jax.experimental import pallas as pl
from jax.experimental.pallas import tpu as pltpu

D_MODEL = 1024
N_META = 16
GRID_W = 64
ROPE_THETA = 10000.0
MLA_HEADS = 4
Q_LORA = 256
KV_LORA = 128
NOPE_DIM = 128
ROPE_DIM = 64
V_DIM = 128
QK_DIM = NOPE_DIM + ROPE_DIM
GQA_HEADS = 4
GQA_KV_HEADS = 2
GQA_DIM = 128
N_EXPERTS = 32
TOP_K = 4
D_FF = D_MODEL
SWIGLU_LIMIT = 7.0
SWIGLU_ALPHA = 1.702
RMS_EPS = 1e-6
LN_EPS = 1e-5
DEPTH = 1
DEEPNORM_ALPHA = (2.0 * DEPTH) ** 0.25

LANES = 128
SUBLANE_SHIFT = 3
SUBLANES = 1 << SUBLANE_SHIFT
SLAB = SUBLANES * SUBLANES
PACK_CHUNKS = D_MODEL // 2 // LANES
PSLAB = SUBLANES * PACK_CHUNKS
META_PAD = 128
MLA_K = 2 * LANES
NEG_BIG = -1e30
LOG2E = 1.4426950408889634
V_EXT = 2 * LANES

ROW_TILE = 1024
Q_TILE_MLA = 2048
Q_TILE_GQA = 2048
MLA_HEADS_PER_STEP = 2
Q_SUB_MLA = 128
Q_SUB_GQA = 512
KEY_BLOCK = 512
TOKEN_TILE = 1024
MOE_TILE = 1024
FFN_CHAIN = 512
VMEM_LIMIT = 56 * 1024 * 1024

F32 = jnp.float32
BF16 = jnp.bfloat16
I32 = jnp.int32


def _layernorm(x, g, b):
    mu = jnp.mean(x, axis=-1, keepdims=True)
    xc = x - mu
    var = jnp.mean(xc * xc, axis=-1, keepdims=True)
    return xc * lax.rsqrt(var + LN_EPS) * g + b


def _rmsnorm(x, g):
    return x * lax.rsqrt(jnp.mean(x * x, axis=-1, keepdims=True) + RMS_EPS) * g


def _swap_halves64(v):
    shp = v.shape
    return v.reshape(shp[:-1] + (shp[-1] // 64, 2, 32))[..., ::-1, :].reshape(shp)


PROJ_SUB = 256


def _swap_blocks32(xt):
    return jnp.concatenate([xt[o + b:o + b + 32] for o in range(0, xt.shape[0], 64) for b in (32, 0)], axis=0)


def _proj_chain(x, lng, lnb, win_ref, wkt_ref, gqa, wqb_ref, gkva, wkvbv_ref, wkvbkt_ref, gqg, gqgs,
                c1, s1, ca, sa, c1t, s1t, ckt, skt, put):
    r = x.shape[0]
    nt = (((1,), (1,)), ((), ()))
    h0b = _layernorm(x, lng, lnb).astype(BF16)
    z = jnp.dot(h0b, win_ref[...], preferred_element_type=F32)
    q_a = z[:, 0:256]
    kv_a = z[:, 256:384]
    q_g = z[:, 384:896]
    q_gs = z[:, 896:1408]
    v_g = z[:, 1408:1664]
    kt = lax.dot_general(wkt_ref[...], h0b, nt, preferred_element_type=F32)

    q = jnp.dot(_rmsnorm(q_a, gqa).astype(BF16), wqb_ref[...], preferred_element_type=F32)
    kvn = _rmsnorm(kv_a, gkva).astype(BF16)
    v_a = jnp.dot(kvn, wkvbv_ref[...], preferred_element_type=F32)
    knt = lax.dot_general(wkvbkt_ref[...], kvn, nt, preferred_element_type=F32)
    kpet = kt[0:ROPE_DIM]
    krot = kpet * c1t + _swap_blocks32(kpet) * s1t
    lane = lax.broadcasted_iota(I32, (r, LANES), 1)
    scale_a = QK_DIM ** -0.5 * LOG2E
    ones = jnp.ones((r, LANES), F32)
    for c in range(MLA_HEADS // 2):
        lo = LANES * c
        qr = q[:, 512 + lo:512 + lo + LANES] * c1 + q[:, 768 + lo:768 + lo + LANES] * s1
        for hh in range(2):
            h = 2 * c + hh
            slot = jnp.where((lane // 64) == hh, qr, 0.0)
            put("qm", h, (jnp.concatenate([q[:, LANES * h:LANES * (h + 1)], slot], axis=1) * scale_a).astype(BF16))
    for h in range(MLA_HEADS):
        put("km", h, jnp.concatenate([knt[LANES * h:LANES * (h + 1)], krot, krot], axis=0).astype(BF16))
        put("vm", h, jnp.concatenate([v_a[:, LANES * h:LANES * (h + 1)], ones], axis=1).astype(BF16))

    scale_b = GQA_DIM ** -0.5 * LOG2E
    cq, sq = ca * gqg, sa * gqgs
    for h in range(GQA_HEADS):
        sl = slice(LANES * h, LANES * (h + 1))
        xh = q_g[:, sl]
        inv = lax.rsqrt(jnp.mean(xh * xh, axis=-1, keepdims=True) + RMS_EPS)
        put("qg", h, ((xh * cq + q_gs[:, sl] * sq) * (inv * scale_b)).astype(BF16))
    for j in range(GQA_KV_HEADS):
        xt = kt[ROPE_DIM + LANES * j:ROPE_DIM + LANES * (j + 1)]
        inv = lax.rsqrt(jnp.mean(xt * xt, axis=0, keepdims=True) + RMS_EPS)
        put("kg", j, ((xt * ckt + _swap_blocks32(xt) * skt) * inv).astype(BF16))
        put("vg", j, jnp.concatenate([v_g[:, LANES * j:LANES * (j + 1)], ones], axis=1).astype(BF16))


def _proj_body(x_ref, lng_ref, lnb_ref, win_ref, wkt_ref, gqa_ref, wqb_ref, gkva_ref, wkvbv_ref, wkvbkt_ref,
               gqg_ref, gqgs_ref, c1_ref, s1_ref, ca_ref, sa_ref, c1t_ref, s1t_ref, ckt_ref, skt_ref,
               qm_ref, km_ref, vm_ref, qg_ref, kg_ref, vg_ref):
    outs = dict(qm=qm_ref, km=km_ref, vm=vm_ref, qg=qg_ref, kg=kg_ref, vg=vg_ref)
    tile = x_ref.shape[1]
    sub = min(PROJ_SUB, tile)
    for i in range(tile // sub):
        rows = pl.ds(i * sub, sub)

        def put(name, h, val):
            if name in ("km", "kg"):
                outs[name][0, h, :, rows] = val
            else:
                outs[name][0, h, rows, :] = val

        _proj_chain(x_ref[0, rows, :], lng_ref[...], lnb_ref[...], win_ref, wkt_ref, gqa_ref[...], wqb_ref,
                    gkva_ref[...], wkvbv_ref, wkvbkt_ref, gqg_ref[...], gqgs_ref[...],
                    c1_ref[rows, :], s1_ref[rows, :], ca_ref[rows, :], sa_ref[rows, :],
                    c1t_ref[:, rows], s1t_ref[:, rows], ckt_ref[:, rows], skt_ref[:, rows], put)


def _project(x3, tabs, w, tile):
    b, s, d = x3.shape
    nst = s // tile
    full = lambda shape: pl.BlockSpec(shape, lambda bi, si: (0,) * len(shape))
    tab = pl.BlockSpec((tile, LANES), lambda bi, si: (si, 0))
    tabt = lambda dims: pl.BlockSpec((dims, tile), lambda bi, si: (0, si))
    hm = lambda nh, dd: pl.BlockSpec((1, nh, tile, dd), lambda bi, si: (bi, 0, si, 0))
    hmt = lambda nh, dd: pl.BlockSpec((1, nh, dd, tile), lambda bi, si: (bi, 0, 0, si))
    out_shape = [
        jax.ShapeDtypeStruct((b, MLA_HEADS, s, MLA_K), BF16),
        jax.ShapeDtypeStruct((b, MLA_HEADS, MLA_K, s), BF16),
        jax.ShapeDtypeStruct((b, MLA_HEADS, s, V_EXT), BF16),
        jax.ShapeDtypeStruct((b, GQA_HEADS, s, GQA_DIM), BF16),
        jax.ShapeDtypeStruct((b, GQA_KV_HEADS, GQA_DIM, s), BF16),
        jax.ShapeDtypeStruct((b, GQA_KV_HEADS, s, V_EXT), BF16),
    ]
    return pl.pallas_call(
        _proj_body,
        grid=(b, nst),
        in_specs=[
            pl.BlockSpec((1, tile, d), lambda bi, si: (bi, si, 0)),
            full((1, d)), full((1, d)),
            full(w["w_in"].shape), full(w["w_kt"].shape), full((1, Q_LORA)), full(w["w_qb"].shape),
            full((1, KV_LORA)), full(w["w_kvb_v"].shape), full(w["w_kvb_kt"].shape),
            full((1, GQA_DIM)), full((1, GQA_DIM)),
            tab, tab, tab, tab, tabt(ROPE_DIM), tabt(ROPE_DIM), tabt(GQA_DIM), tabt(GQA_DIM),
        ],
        out_specs=[hm(MLA_HEADS, MLA_K), hmt(MLA_HEADS, MLA_K), hm(MLA_HEADS, V_EXT),
                   hm(GQA_HEADS, GQA_DIM), hmt(GQA_KV_HEADS, GQA_DIM), hm(GQA_KV_HEADS, V_EXT)],
        out_shape=out_shape,
        compiler_params=pltpu.CompilerParams(dimension_semantics=("parallel", "parallel"),
                                             vmem_limit_bytes=VMEM_LIMIT),
        name="proj",
    )(x3, w["ln_emb_g"], w["ln_emb_b"], w["w_in"], w["w_kt"], w["g_q_a"], w["w_qb"], w["g_kv_a"], w["w_kvb_v"],
      w["w_kvb_kt"], w["g_q_gqa"], w["g_q_gqa_sw"], *tabs)


def _softmax_pv(q, kt, v, kmt, vm):
    sm = jnp.dot(q, kmt, preferred_element_type=F32)
    col = lax.broadcasted_iota(I32, sm.shape, 1)
    sm = jnp.where(col < N_META, sm, NEG_BIG)
    nblk = kt.shape[1] // KEY_BLOCK
    blocks = [jnp.dot(q, kt[:, KEY_BLOCK * c:KEY_BLOCK * (c + 1)], preferred_element_type=F32) for c in range(nblk)]
    mx = blocks[0]
    for c in range(1, nblk):
        mx = jnp.maximum(mx, blocks[c])
    m = jnp.maximum(jnp.max(mx, axis=1, keepdims=True), jnp.max(sm, axis=1, keepdims=True))
    acc = jnp.dot(jnp.exp2(sm - m).astype(BF16), vm, preferred_element_type=F32)
    for c in range(nblk):
        p = jnp.exp2(blocks[c] - m).astype(BF16)
        acc = acc + jnp.dot(p, v[KEY_BLOCK * c:KEY_BLOCK * (c + 1), :], preferred_element_type=F32)
    return acc[:, 0:V_DIM] / acc[:, V_DIM:V_EXT]


def _mla_attn_body(q_ref, k_ref, v_ref, km_ref, vm_ref, o_ref):
    for h in range(q_ref.shape[1]):
        for i in range(q_ref.shape[2] // Q_SUB_MLA):
            rows = pl.ds(i * Q_SUB_MLA, Q_SUB_MLA)
            o = _softmax_pv(q_ref[0, h, rows, :], k_ref.at[0, h], v_ref.at[0, h], km_ref[0, h], vm_ref[0, h])
            o_ref[rows, V_DIM * h:V_DIM * (h + 1)] = o.astype(o_ref.dtype)


def _gqa_attn_body(q_ref, k_ref, v_ref, km_ref, vm_ref, o_ref):
    for g in range(2):
        for i in range(q_ref.shape[2] // Q_SUB_GQA):
            rows = pl.ds(i * Q_SUB_GQA, Q_SUB_GQA)
            o = _softmax_pv(q_ref[0, g, rows, :], k_ref.at[0, 0], v_ref.at[0, 0], km_ref[0, 0], vm_ref[0, 0])
            o_ref[rows, GQA_DIM * g:GQA_DIM * (g + 1)] = o.astype(o_ref.dtype)


def _mla_attention(qm, kmt, vm, kmeta_t, vmeta):
    b, h, s, dk = qm.shape
    tq = Q_TILE_MLA
    nq = s // tq
    hs = MLA_HEADS_PER_STEP
    return pl.pallas_call(
        _mla_attn_body,
        grid=(b, h // hs, nq),
        in_specs=[
            pl.BlockSpec((1, hs, tq, dk), lambda bi, hi, qi: (bi, hi, qi, 0)),
            pl.BlockSpec((1, hs, dk, s), lambda bi, hi, qi: (bi, hi, 0, 0)),
            pl.BlockSpec((1, hs, s, V_EXT), lambda bi, hi, qi: (bi, hi, 0, 0)),
            pl.BlockSpec((1, hs, dk, META_PAD), lambda bi, hi, qi: (0, hi, 0, 0)),
            pl.BlockSpec((1, hs, META_PAD, V_EXT), lambda bi, hi, qi: (0, hi, 0, 0)),
        ],
        out_specs=pl.BlockSpec((tq, hs * V_DIM), lambda bi, hi, qi: (bi * nq + qi, hi)),
        out_shape=jax.ShapeDtypeStruct((b * s, h * V_DIM), BF16),
        compiler_params=pltpu.CompilerParams(dimension_semantics=("parallel", "parallel", "parallel"),
                                             vmem_limit_bytes=VMEM_LIMIT),
        name="mla_attn",
    )(qm, kmt, vm, kmeta_t, vmeta)


def _gqa_attention(qg, kgt, vg, kmeta_t, vmeta):
    b, h, s, d = qg.shape
    hk = kgt.shape[1]
    tq = Q_TILE_GQA
    nq = s // tq
    return pl.pallas_call(
        _gqa_attn_body,
        grid=(b, hk, nq),
        in_specs=[
            pl.BlockSpec((1, 2, tq, d), lambda bi, ji, qi: (bi, ji, qi, 0)),
            pl.BlockSpec((1, 1, d, s), lambda bi, ji, qi: (bi, ji, 0, 0)),
            pl.BlockSpec((1, 1, s, V_EXT), lambda bi, ji, qi: (bi, ji, 0, 0)),
            pl.BlockSpec((1, 1, d, META_PAD), lambda bi, ji, qi: (0, ji, 0, 0)),
            pl.BlockSpec((1, 1, META_PAD, V_EXT), lambda bi, ji, qi: (0, ji, 0, 0)),
        ],
        out_specs=pl.BlockSpec((tq, 2 * d), lambda bi, ji, qi: (bi * nq + qi, ji)),
        out_shape=jax.ShapeDtypeStruct((b * s, h * d), BF16),
        compiler_params=pltpu.CompilerParams(dimension_semantics=("parallel", "parallel", "parallel"),
                                             vmem_limit_bytes=VMEM_LIMIT),
        name="gqa_attn",
    )(qg, kgt, vg, kmeta_t, vmeta)


def _to_tiles(ref, x):
    r = x.shape[0]
    for s in range(x.shape[1] // LANES):
        ref[:, s] = x[:, LANES * s:LANES * (s + 1)].reshape(r // SUBLANES, SUBLANES, LANES)


def _from_tiles(ref):
    r = ref.shape[0] * SUBLANES
    return jnp.concatenate([ref[:, s].reshape(r, LANES) for s in range(ref.shape[1])], axis=1)


MERGE_SUB = 256


def _pack_rows(v):
    c = v.shape[1] // 2
    lo = lax.bitcast_convert_type(v[:, 0:c].astype(BF16).astype(F32), I32)
    hi = lax.bitcast_convert_type(v[:, c:2 * c].astype(BF16).astype(F32), I32)
    return lax.shift_right_logical(lo, 16) | hi


def _unpack_words(w):
    return (lax.bitcast_convert_type(w << 16, F32), lax.bitcast_convert_type(w & jnp.int32(-65536), F32))


def _merge_body(om_ref, og_ref, x_ref, lng_ref, lnb_ref, gom_ref, gog_ref, wo_ref, l1g_ref, l1b_ref,
                wr_ref, br_ref, h1_ref, h1p_ref, topi_ref, gate_ref):
    for i in range(x_ref.shape[0] // MERGE_SUB):
        rows = pl.ds(i * MERGE_SUB, MERGE_SUB)
        h1, ti, gt = _merge_chain(om_ref[rows, :], og_ref[rows, :], x_ref[rows, :], lng_ref[...], lnb_ref[...],
                                  gom_ref[...], gog_ref[...], wo_ref, l1g_ref[...], l1b_ref[...], wr_ref, br_ref[...])
        blocks = pl.ds(i * (MERGE_SUB // SUBLANES), MERGE_SUB // SUBLANES)
        _to_tiles(h1_ref.at[blocks], h1)
        _to_tiles(h1p_ref.at[blocks], _pack_rows(h1))
        topi_ref[0, :, rows] = ti
        for k in range(TOP_K):
            gate_ref[0, :, pl.ds(k * TOKEN_TILE + i * MERGE_SUB, MERGE_SUB)] = gt[k:k + 1, :]


def _merge_chain(om, og, x, lng, lnb, gom, gog, wo_ref, l1g, l1b, wr_ref, br):
    t = x.shape[0]
    h0 = _layernorm(x, lng, lnb)
    nm = _rmsnorm(om.astype(F32), gom).astype(BF16)
    ng = _rmsnorm(og.astype(F32), gog).astype(BF16)
    half = nm.shape[1]
    mix = jnp.dot(nm, wo_ref[0:half, :], preferred_element_type=F32)
    mix = mix + jnp.dot(ng, wo_ref[half:2 * half, :], preferred_element_type=F32)
    h1 = _layernorm(DEEPNORM_ALPHA * h0 + mix, l1g, l1b)

    hi = h1.astype(BF16)
    lo = (h1 - hi.astype(F32)).astype(BF16)
    acc = jnp.dot(hi, wr_ref[...], preferred_element_type=F32) + jnp.dot(lo, wr_ref[...], preferred_element_type=F32)
    logits = acc[:, 0:LANES] + acc[:, LANES:2 * LANES] + br
    cur = logits.T[0:N_EXPERTS, :]
    eidx = lax.broadcasted_iota(I32, cur.shape, 0)
    vals, idxs = [], []
    for _ in range(TOP_K):
        m = jnp.max(cur, axis=0, keepdims=True)
        i = jnp.min(jnp.where(cur == m, eidx, N_EXPERTS), axis=0, keepdims=True)
        vals.append(m)
        idxs.append(i)
        cur = jnp.where(eidx == i, -jnp.inf, cur)
    ex = [jnp.exp(v - vals[0]) for v in vals]
    den = ex[0] + ex[1] + ex[2] + ex[3]
    sub = lax.broadcasted_iota(I32, (8, t), 0)
    ti = jnp.zeros((8, t), I32)
    gt = jnp.zeros((8, t), F32)
    for k in range(TOP_K):
        ti = jnp.where(sub == k, idxs[k], ti)
        gt = jnp.where(sub == k, ex[k] / den, gt)
    return h1, ti[0:TOP_K, :], gt[0:TOP_K, :]


def _merge(o_mla, o_gqa, x2, w):
    n, d = x2.shape
    tile = TOKEN_TILE
    half = o_mla.shape[1]
    full = lambda shape: pl.BlockSpec(shape, lambda i: (0,) * len(shape))
    row = lambda width: pl.BlockSpec((tile, width), lambda i: (i, 0))
    return pl.pallas_call(
        _merge_body,
        grid=(n // tile,),
        in_specs=[row(half), row(half), row(d), full((1, d)), full((1, d)), full((1, half)), full((1, half)),
                  full((d, d)), full((1, d)), full((1, d)), full((d, 2 * LANES)), full((1, LANES))],
        out_specs=[pl.BlockSpec((tile // SUBLANES, d // LANES, SUBLANES, LANES), lambda i: (i, 0, 0, 0)),
                   pl.BlockSpec((tile // SUBLANES, PACK_CHUNKS, SUBLANES, LANES), lambda i: (i, 0, 0, 0)),
                   pl.BlockSpec((1, TOP_K, tile), lambda i: (i, 0, 0)),
                   pl.BlockSpec((1, 1, TOP_K * tile), lambda i: (i, 0, 0))],
        out_shape=[jax.ShapeDtypeStruct((n // SUBLANES, d // LANES, SUBLANES, LANES), F32),
                   jax.ShapeDtypeStruct((n // SUBLANES, PACK_CHUNKS, SUBLANES, LANES), I32),
                   jax.ShapeDtypeStruct((n // tile, TOP_K, tile), I32),
                   jax.ShapeDtypeStruct((n // tile, 1, TOP_K * tile), F32)],
        compiler_params=pltpu.CompilerParams(dimension_semantics=("parallel",), vmem_limit_bytes=VMEM_LIMIT),
        name="merge",
    )(o_mla, o_gqa, x2, w["ln_emb_g"], w["ln_emb_b"], w["g_o_mla"], w["g_o_gqa"], w["w_o"],
      w["ln1_g"], w["ln1_b"], w["w_router"], w["b_router"])


def _lanes_from_sublanes(col):
    diag = lax.broadcasted_iota(I32, col.shape, 0) == lax.broadcasted_iota(I32, col.shape, 1)
    return jnp.sum(jnp.where(diag, col, 0.0), axis=0, keepdims=True)


def _positions_body(topi_ref, upper_ref, lpb_ref, gstart_ref, nslab_ref, lstart_ref, texp_ref, nvalid_ref, ecnt_ref,
                    eoff_ref, *, ntp):
    ntile, _, tl = topi_ref.shape
    r = lax.broadcasted_iota(I32, (N_EXPERTS, N_EXPERTS), 0)
    c = lax.broadcasted_iota(I32, (N_EXPERTS, N_EXPERTS), 1)
    lower = (c < r).astype(F32)
    eidx = lax.broadcasted_iota(I32, (N_EXPERTS, tl), 0)

    def tile_onehots(j):
        topi = topi_ref[j]
        ohs = [eidx == topi[k:k + 1, :] for k in range(TOP_K)]
        onehot = ohs[0].astype(F32) + ohs[1].astype(F32) + ohs[2].astype(F32) + ohs[3].astype(F32)
        cnt = jnp.sum(onehot, axis=1, keepdims=True)
        run = jnp.floor((cnt + (SUBLANES - 1)) * (1.0 / SUBLANES)) * SUBLANES
        return ohs, onehot, jnp.broadcast_to(run, (N_EXPERTS, LANES))

    tot = lax.fori_loop(0, ntile, lambda j, acc: acc + tile_onehots(j)[2], jnp.zeros((N_EXPERTS, LANES), F32))
    pc = jnp.floor((tot + (MOE_TILE - 1)) * (1.0 / MOE_TILE)) * MOE_TILE
    off = jnp.dot(lower, pc, precision=lax.Precision.HIGHEST, preferred_element_type=F32)
    cumend = off + pc
    tstart = lax.broadcasted_iota(I32, (N_EXPERTS, ntp), 1).astype(F32) * MOE_TILE
    te = jnp.sum((jnp.broadcast_to(cumend[:, 0:1], (N_EXPERTS, ntp)) <= tstart).astype(I32), axis=0, keepdims=True)
    texp_ref[...] = jnp.minimum(te, N_EXPERTS - 1)
    nvalid_ref[...] = (cumend[N_EXPERTS - 1:N_EXPERTS, :] * (1.0 / MOE_TILE)).astype(I32)
    ecnt_ref[...] = _lanes_from_sublanes(tot).astype(I32)
    eoff_ref[...] = _lanes_from_sublanes(off).astype(I32)

    def place(j, carry):
        ohs, onehot, tile_run = tile_onehots(j)
        before = jnp.dot(onehot.astype(BF16), upper_ref[...], preferred_element_type=F32)
        loff = jnp.dot(lower, tile_run, precision=lax.Precision.HIGHEST, preferred_element_type=F32)
        base = before + loff[:, 0:1]
        sub = lax.broadcasted_iota(I32, (SUBLANES, tl), 0)
        out = jnp.zeros((SUBLANES, tl), F32)
        for k in range(TOP_K):
            pk = jnp.sum(jnp.where(ohs[k], base, 0.0), axis=0, keepdims=True)
            out = jnp.where(sub == k, pk, out)
        lp = out[0:TOP_K, :].astype(I32)
        lpb = (lp >> SUBLANE_SHIFT) * PSLAB + (lp & (SUBLANES - 1))
        lpb_ref[j] = jnp.concatenate([lpb[k:k + 1, :] for k in range(TOP_K)], axis=1)
        inv = 1.0 / SUBLANES
        gstart_ref[j] = (_lanes_from_sublanes(off + carry) * inv).astype(I32)
        nslab_ref[j] = (_lanes_from_sublanes(tile_run) * inv).astype(I32)
        lstart_ref[j] = (_lanes_from_sublanes(loff) * inv).astype(I32)
        return carry + tile_run

    lax.fori_loop(0, ntile, place, jnp.zeros((N_EXPERTS, LANES), F32))


def _positions(topi, ntp):
    ntile, _, tl = topi.shape
    tab = jax.ShapeDtypeStruct((ntile, 1, LANES), I32)
    small = jax.ShapeDtypeStruct((1, LANES), I32)
    upper = (jnp.arange(tl, dtype=I32)[:, None] < jnp.arange(tl, dtype=I32)[None, :]).astype(BF16)
    return pl.pallas_call(
        functools.partial(_positions_body, ntp=ntp),
        out_shape=[jax.ShapeDtypeStruct((ntile, 1, TOP_K * tl), I32), tab, tab, tab,
                   jax.ShapeDtypeStruct((1, ntp), I32), small, small, small],
        compiler_params=pltpu.CompilerParams(vmem_limit_bytes=VMEM_LIMIT),
        name="positions",
    )(topi, upper)


STAGE_ROWS = TOKEN_TILE * TOP_K + N_EXPERTS * SUBLANES
STAGE_FLAT = STAGE_ROWS * PACK_CHUNKS


def _load_row(ref, flat_start, chunks=SUBLANES):
    return ref[pl.ds(flat_start, chunks, stride=SUBLANES), :]


def _store_row(ref, flat_start, v):
    ref[pl.ds(flat_start, v.shape[0], stride=SUBLANES), :] = v


CHUNK_SHIFT = 2
CHUNK_SLABS = 1 << CHUNK_SHIFT


def _copy_run(src_ref, src0, dst_ref, dst0, nslab, sem):
    big = CHUNK_SLABS * PSLAB
    nbig = nslab >> CHUNK_SHIFT
    nsmall = nslab & (CHUNK_SLABS - 1)

    def start_big(s, c):
        pltpu.make_async_copy(src_ref.at[pl.ds(src0 + s * big, big)], dst_ref.at[pl.ds(dst0 + s * big, big)], sem).start()
        return c

    def start_small(s, c):
        o = nbig * big + s * PSLAB
        pltpu.make_async_copy(src_ref.at[pl.ds(src0 + o, PSLAB)], dst_ref.at[pl.ds(dst0 + o, PSLAB)], sem).start()
        return c

    lax.fori_loop(0, nbig, start_big, 0)
    lax.fori_loop(0, nsmall, start_small, 0)


def _wait_runs(src_ref, dst_ref, nbig, nsmall, sem):
    big = CHUNK_SLABS * PSLAB

    def wait_big(s, c):
        pltpu.make_async_copy(src_ref.at[pl.ds(0, big)], dst_ref.at[pl.ds(0, big)], sem).wait()
        return c

    def wait_small(s, c):
        pltpu.make_async_copy(src_ref.at[pl.ds(0, PSLAB)], dst_ref.at[pl.ds(0, PSLAB)], sem).wait()
        return c

    lax.fori_loop(0, nbig, wait_big, 0)
    lax.fori_loop(0, nsmall, wait_small, 0)


def _chunk_counts(nslab_ref, j):
    def add(e, c):
        n = nslab_ref[j, 0, e]
        return c[0] + (n >> CHUNK_SHIFT), c[1] + (n & (CHUNK_SLABS - 1))

    return lax.fori_loop(0, N_EXPERTS, add, (jnp.int32(0), jnp.int32(0)))


def _dispatch_body(ecnt_ref, eoff_ref, lpb_ref, gstart_ref, nslab_ref, lstart_ref, h_ref, xs_ref,
                   stg0_ref, stg1_ref, zero_ref, issued_ref, sem, zsem):
    g = pl.program_id(0)
    ng = pl.num_programs(0)
    stages = (stg0_ref, stg1_ref)

    def drain(j):
        _wait_runs(stages[j], xs_ref, issued_ref[2 * j], issued_ref[2 * j + 1], sem.at[j])

    def run(j):
        stg_ref = stages[j]

        @pl.when(g >= 2)
        def _():
            drain(j)

        def zero_last(e, c):
            last = jnp.maximum(lstart_ref[0, 0, e] + nslab_ref[0, 0, e] - 1, 0)
            stg_ref[pl.ds(last * PSLAB, PSLAB), :] = jnp.zeros((PSLAB, LANES), I32)
            return c

        lax.fori_loop(0, N_EXPERTS, zero_last, 0)

        def move(i, c):
            for u in range(SUBLANES):
                v = _load_row(h_ref, i * PSLAB + u, PACK_CHUNKS)
                for k in range(TOP_K):
                    _store_row(stg_ref, lpb_ref[0, 0, k * TOKEN_TILE + i * SUBLANES + u], v)
            return c

        lax.fori_loop(0, TOKEN_TILE // SUBLANES, move, 0)

        def send_run(e, c):
            _copy_run(stg_ref, lstart_ref[0, 0, e] * PSLAB, xs_ref, gstart_ref[0, 0, e] * PSLAB,
                      nslab_ref[0, 0, e], sem.at[j])
            return c

        lax.fori_loop(0, N_EXPERTS, send_run, 0)
        nbig, nsmall = _chunk_counts(nslab_ref, 0)
        issued_ref[2 * j] = nbig
        issued_ref[2 * j + 1] = nsmall

    for j in range(2):
        @pl.when(lax.rem(g, 2) == j)
        def _():
            run(j)

    @pl.when(g == ng - 1)
    def _():
        zero_ref[...] = jnp.zeros_like(zero_ref)

        def zero_copy(dst_slab):
            return pltpu.make_async_copy(zero_ref, xs_ref.at[pl.ds(dst_slab * PSLAB, PSLAB)], zsem)

        def pad_segment(e, c):
            rows = ecnt_ref[e]
            first = (eoff_ref[e] + rows) >> SUBLANE_SHIFT
            npad = lax.rem(MOE_TILE - lax.rem(rows, MOE_TILE), MOE_TILE) >> SUBLANE_SHIFT
            lax.fori_loop(0, npad, lambda s, cc: (zero_copy(first + s).start(), cc)[1], 0)
            lax.fori_loop(0, npad, lambda s, cc: (zero_copy(first + s).wait(), cc)[1], 0)
            return c

        lax.fori_loop(0, N_EXPERTS, pad_segment, 0)
        for j in range(2):
            @pl.when(jnp.logical_and(ng > 1, lax.rem(g, 2) != j))
            def _():
                drain(j)

        for j in range(2):
            @pl.when(lax.rem(g, 2) == j)
            def _():
                drain(j)


def _dispatch(h1_flat, lpb, gstart, nslab, lstart, ecnt, eoff, rows_pad):
    nflat = h1_flat.shape[0]
    t = TOKEN_TILE
    smem = lambda shape, imap: pl.BlockSpec(shape, imap, memory_space=pltpu.SMEM)
    per_tile = smem((1, 1, LANES), lambda i, c, o: (i, 0, 0))
    grid_spec = pltpu.PrefetchScalarGridSpec(
        num_scalar_prefetch=2,
        grid=(nflat // (t * PACK_CHUNKS),),
        in_specs=[smem((1, 1, TOP_K * t), lambda i, c, o: (i, 0, 0)), per_tile, per_tile, per_tile,
                  pl.BlockSpec((t * PACK_CHUNKS, LANES), lambda i, c, o: (i, 0))],
        out_specs=pl.BlockSpec(memory_space=pl.ANY),
        scratch_shapes=[pltpu.VMEM((STAGE_FLAT, LANES), I32), pltpu.VMEM((STAGE_FLAT, LANES), I32),
                        pltpu.VMEM((PSLAB, LANES), I32),
                        pltpu.SMEM((4,), I32), pltpu.SemaphoreType.DMA((2,)), pltpu.SemaphoreType.DMA],
    )
    return pl.pallas_call(
        _dispatch_body,
        grid_spec=grid_spec,
        out_shape=jax.ShapeDtypeStruct((rows_pad * PACK_CHUNKS, LANES), I32),
        compiler_params=pltpu.CompilerParams(dimension_semantics=("arbitrary",), vmem_limit_bytes=VMEM_LIMIT),
        name="dispatch",
    )(ecnt, eoff, lpb, gstart, nslab, lstart, h1_flat)


def _ffn_body(texp_ref, nvalid_ref, ecnt_ref, eoff_ref, x_ref, wgu_hbm, wd_hbm, bg_ref, bl_ref, bd_ref, y_ref,
              wgu_buf, wd_buf, wg_sc, wl_sc, wd_sc, seg_ref, sem):
    i = pl.program_id(0)
    valid = i < nvalid_ref[0]
    e = texp_ref[i]
    first_of_expert = jnp.logical_or(i == 0, e != texp_ref[jnp.maximum(i - 1, 0)])

    def weight_copies(expert, slot):
        return (pltpu.make_async_copy(wgu_hbm.at[expert], wgu_buf.at[slot], sem.at[slot]),
                pltpu.make_async_copy(wd_hbm.at[expert], wd_buf.at[slot], sem.at[slot]))

    def prepare(slot):
        blk = 2 * LANES
        r = lax.broadcasted_iota(I32, (blk, blk), 0)
        c = lax.broadcasted_iota(I32, (blk, blk), 1)
        sel = (r == jnp.where(c < LANES, 2 * c, 2 * (c - LANES) + 1)).astype(BF16)
        for b in range(wgu_buf.shape[2] // blk):
            wb = wgu_buf[slot, :, blk * b:blk * (b + 1)].astype(BF16)
            y = jnp.dot(wb, sel, preferred_element_type=F32).astype(BF16)
            wg_sc[:, LANES * b:LANES * (b + 1)] = y[:, 0:LANES]
            wl_sc[:, LANES * b:LANES * (b + 1)] = y[:, LANES:blk]
        wd_sc[...] = wd_buf[slot].astype(BF16)

    @pl.when(jnp.logical_and(valid, first_of_expert))
    def _():
        @pl.when(i == 0)
        def _():
            seg_ref[0] = 0
            for cp in weight_copies(e, 0):
                cp.start()

        seg = seg_ref[0]
        seg_ref[0] = seg + 1
        nxt = lax.while_loop(lambda n: jnp.logical_and(n < N_EXPERTS, ecnt_ref[jnp.minimum(n, N_EXPERTS - 1)] == 0),
                             lambda n: n + 1, e + 1)
        for slot in range(2):
            @pl.when(lax.rem(seg, 2) == slot)
            def _():
                for cp in weight_copies(e, slot):
                    cp.wait()

                @pl.when(nxt < N_EXPERTS)
                def _():
                    for cp in weight_copies(nxt, 1 - slot):
                        cp.start()

                prepare(slot)

    def chain(b0, nb):
        blocks = pl.ds(b0, nb)
        x = jnp.concatenate(_unpack_words(_from_tiles(x_ref.at[blocks])), axis=1).astype(BF16)
        hg = jnp.dot(x, wg_sc[...], preferred_element_type=F32) + bg_ref[0]
        hl = jnp.dot(x, wl_sc[...], preferred_element_type=F32) + bl_ref[0]
        g = jnp.minimum(hg, SWIGLU_LIMIT)
        lin = jnp.clip(hl, -SWIGLU_LIMIT, SWIGLU_LIMIT)
        act = g * (1.0 / (1.0 + jnp.exp(-SWIGLU_ALPHA * g))) * (lin + 1.0)
        y = jnp.dot(act.astype(BF16), wd_sc[...], preferred_element_type=F32) + bd_ref[0]
        _to_tiles(y_ref.at[blocks], _pack_rows(y))

    def compute(nrows):
        done = 0
        while done < nrows:
            n = min(FFN_CHAIN, nrows - done)
            chain(done // SUBLANES, n // SUBLANES)
            done += n
        rest = y_ref.shape[0] - nrows // SUBLANES
        if rest:
            y_ref[pl.ds(nrows // SUBLANES, rest)] = jnp.zeros((rest,) + y_ref.shape[1:], I32)

    used = eoff_ref[e] + ecnt_ref[e] - i * MOE_TILE
    quarter = MOE_TILE // 4
    for q in range(1, 5):
        lo, hi = (q - 1) * quarter, q * quarter
        in_range = jnp.logical_and(used > lo, used <= hi) if q < 4 else used > lo

        @pl.when(jnp.logical_and(valid, in_range))
        def _():
            compute(hi)


def _grouped_ffn(xs_tiles, texp, nvalid, ecnt, eoff, wgu, wd, bg, bl, bd, ntiles):
    d = wgu.shape[1]
    tm = MOE_TILE
    f = wd.shape[1]
    xmap = lambda i, te, nv, ec, eo: (jnp.minimum(i, nv[0] - 1), 0, 0, 0)
    wmap = lambda i, te, nv, ec, eo: (te[i], 0, 0)
    rows_blk = pl.BlockSpec((tm // SUBLANES, PACK_CHUNKS, SUBLANES, LANES), xmap)
    hbm = pl.BlockSpec(memory_space=pl.ANY)
    grid_spec = pltpu.PrefetchScalarGridSpec(
        num_scalar_prefetch=4,
        grid=(ntiles,),
        in_specs=[rows_blk, hbm, hbm,
                  pl.BlockSpec((1, 1, f), wmap), pl.BlockSpec((1, 1, f), wmap), pl.BlockSpec((1, 1, d), wmap)],
        out_specs=rows_blk,
        scratch_shapes=[pltpu.VMEM((2, d, 2 * f), F32), pltpu.VMEM((2, f, d), F32),
                        pltpu.VMEM((d, f), BF16), pltpu.VMEM((d, f), BF16), pltpu.VMEM((f, d), BF16),
                        pltpu.SMEM((1,), I32), pltpu.SemaphoreType.DMA((2,))],
    )
    return pl.pallas_call(
        _ffn_body,
        grid_spec=grid_spec,
        out_shape=jax.ShapeDtypeStruct(xs_tiles.shape, I32),
        compiler_params=pltpu.CompilerParams(dimension_semantics=("arbitrary",), vmem_limit_bytes=VMEM_LIMIT),
        name="ffn",
    )(texp, nvalid, ecnt, eoff, xs_tiles, wgu, wd, bg, bl, bd)


COMBINE_SUB = 512


def _combine_body(lpb_ref, gate_ref, gstart_ref, nslab_ref, lstart_ref, gstart2_ref, nslab2_ref, lstart2_ref,
                  h_ref, l2g_ref, l2b_ref, ys_ref, o_ref, stg0_ref, stg1_ref, moe_ref, sem):
    g = pl.program_id(0)
    ng = pl.num_programs(0)
    stages = (stg0_ref, stg1_ref)

    def fetch(gs_ref, ns_ref, ls_ref, j):
        def per_expert(e, c):
            _copy_run(ys_ref, gs_ref[0, 0, e] * PSLAB, stages[j], ls_ref[0, 0, e] * PSLAB, ns_ref[0, 0, e], sem.at[j])
            return c

        lax.fori_loop(0, N_EXPERTS, per_expert, 0)

    def wait_tile(j):
        nbig, nsmall = _chunk_counts(nslab_ref, 0)
        _wait_runs(ys_ref, stages[j], nbig, nsmall, sem.at[j])

    def gather(j):
        stg_ref = stages[j]

        def body(i, c):
            for u in range(SUBLANES):
                tok = i * SUBLANES + u
                acc_lo = acc_hi = None
                for k in range(TOP_K):
                    gk = gate_ref[0, 0, k * TOKEN_TILE + tok]
                    lo, hi = _unpack_words(_load_row(stg_ref, lpb_ref[0, 0, k * TOKEN_TILE + tok], PACK_CHUNKS))
                    acc_lo = gk * lo if k == 0 else acc_lo + gk * lo
                    acc_hi = gk * hi if k == 0 else acc_hi + gk * hi
                _store_row(moe_ref, i * SLAB + u, acc_lo)
                _store_row(moe_ref, i * SLAB + PSLAB + u, acc_hi)
            return c

        lax.fori_loop(0, TOKEN_TILE // SUBLANES, body, 0)

    def finish():
        nblk = COMBINE_SUB // SUBLANES
        for part in range(TOKEN_TILE // COMBINE_SUB):
            moe = jnp.concatenate(
                [jnp.concatenate([moe_ref[pl.ds((part * nblk + i) * SLAB + s * SUBLANES, SUBLANES), :]
                                  for i in range(nblk)], axis=0) for s in range(SUBLANES)], axis=1)
            h1 = _from_tiles(h_ref.at[pl.ds(part * nblk, nblk)])
            o_ref[pl.ds(part * COMBINE_SUB, COMBINE_SUB), :] = _layernorm(DEEPNORM_ALPHA * h1 + moe, l2g_ref[...],
                                                                          l2b_ref[...])

    @pl.when(g == 0)
    def _():
        fetch(gstart_ref, nslab_ref, lstart_ref, 0)

    for j in range(2):
        @pl.when(lax.rem(g, 2) == j)
        def _():
            @pl.when(g + 1 < ng)
            def _():
                fetch(gstart2_ref, nslab2_ref, lstart2_ref, 1 - j)

            wait_tile(j)
            gather(j)

    finish()


def _combine(lpb, gates, gstart, nslab, lstart, h1_tiles, ys_flat, w):
    d = h1_tiles.shape[1] * LANES
    n = h1_tiles.shape[0] * SUBLANES
    t = TOKEN_TILE
    ng = n // t
    full = lambda shape: pl.BlockSpec(shape, lambda i: (0,) * len(shape))
    smem = lambda shape, imap: pl.BlockSpec(shape, imap, memory_space=pltpu.SMEM)
    vec = smem((1, 1, TOP_K * t), lambda i: (i, 0, 0))
    this_step = smem((1, 1, LANES), lambda i: (i, 0, 0))
    next_step = smem((1, 1, LANES), lambda i: (jnp.minimum(i + 1, ng - 1), 0, 0))
    return pl.pallas_call(
        _combine_body,
        grid=(ng,),
        in_specs=[vec, vec, this_step, this_step, this_step, next_step, next_step, next_step,
                  pl.BlockSpec((t // SUBLANES, d // LANES, SUBLANES, LANES), lambda i: (i, 0, 0, 0)),
                  full((1, d)), full((1, d)),
                  pl.BlockSpec(memory_space=pl.ANY)],
        out_specs=pl.BlockSpec((t, d), lambda i: (i, 0)),
        out_shape=jax.ShapeDtypeStruct((n, d), F32),
        scratch_shapes=[pltpu.VMEM((STAGE_FLAT, LANES), I32), pltpu.VMEM((STAGE_FLAT, LANES), I32),
                        pltpu.VMEM((TOKEN_TILE * SUBLANES, LANES), F32), pltpu.SemaphoreType.DMA((2,))],
        compiler_params=pltpu.CompilerParams(dimension_semantics=("arbitrary",), vmem_limit_bytes=VMEM_LIMIT),
        name="combine",
    )(lpb, gates, gstart, nslab, lstart, gstart, nslab, lstart, h1_tiles, w["ln2_g"], w["ln2_b"], ys_flat)


def _rope_tables(pos_1d, row, col, g_k):
    inv = ROPE_THETA ** (-jnp.arange(0, ROPE_DIM, 2, dtype=F32) / ROPE_DIM)

    def cs(p):
        ang = p.astype(F32)[:, None] * inv[None, :]
        return jnp.cos(ang), jnp.sin(ang)

    c1, s1 = cs(pos_1d)
    cr, sr = cs(row)
    cc, sc = cs(col)
    c1w, s1w = jnp.concatenate([c1, c1, c1, c1], axis=1), jnp.concatenate([-s1, s1, -s1, s1], axis=1)
    caw, saw = jnp.concatenate([cr, cr, cc, cc], axis=1), jnp.concatenate([-sr, sr, -sc, sc], axis=1)
    ckt = caw.T * g_k[:, None]
    skt = saw.T * _swap_halves64(g_k)[:, None]
    return (c1w, s1w, caw, saw, c1w[:, :ROPE_DIM].T, s1w[:, :ROPE_DIM].T, ckt, skt)


def _prep_weights(ln_emb_g, ln_emb_b, w_in, g_q_a, w_q_b, g_kv_a, w_kv_b, g_q_gqa, g_k_gqa, g_o_mla, g_o_gqa, w_o,
                  ln1_g, ln1_b, w_router, b_router, ln2_g, ln2_b):
    r2 = lambda v: v.reshape(1, -1).astype(F32)
    o = np.cumsum([0, Q_LORA, KV_LORA, ROPE_DIM, GQA_HEADS * GQA_DIM, GQA_KV_HEADS * GQA_DIM, GQA_KV_HEADS * GQA_DIM])
    wi = w_in[0]
    seg = [wi[:, o[i]:o[i + 1]] for i in range(6)]
    w_in_p = jnp.concatenate([seg[0], seg[1], seg[3], _swap_halves64(seg[3]), seg[5]], axis=1).astype(BF16)
    w_kt = jnp.concatenate([seg[2], seg[4]], axis=1).T.astype(BF16)
    wq = w_q_b[0].reshape(Q_LORA, MLA_HEADS, QK_DIM)
    wq_rope = wq[:, :, NOPE_DIM:].reshape(Q_LORA, -1)
    w_qb = jnp.concatenate([wq[:, :, :NOPE_DIM].reshape(Q_LORA, -1), wq_rope, _swap_halves64(wq_rope)],
                           axis=1).astype(BF16)
    wk = w_kv_b[0].reshape(KV_LORA, MLA_HEADS, NOPE_DIM + V_DIM)
    w_kvb_kt = wk[:, :, :NOPE_DIM].reshape(KV_LORA, -1).T.astype(BF16)
    w_kvb_v = wk[:, :, NOPE_DIM:].reshape(KV_LORA, -1).astype(BF16)
    wr = jnp.pad(w_router[0].astype(F32), ((0, 0), (0, LANES - N_EXPERTS)))
    wr_hi = wr.astype(BF16)
    w_router_p = jnp.concatenate([wr_hi, (wr - wr_hi.astype(F32)).astype(BF16)], axis=1)
    b_router_p = jnp.pad(b_router[0].astype(F32), (0, LANES - N_EXPERTS), constant_values=NEG_BIG).reshape(1, LANES)
    return dict(
        ln_emb_g=r2(ln_emb_g), ln_emb_b=r2(ln_emb_b), w_in=w_in_p, w_kt=w_kt, g_q_a=r2(g_q_a[0]), w_qb=w_qb,
        g_kv_a=r2(g_kv_a[0]), w_kvb_v=w_kvb_v, w_kvb_kt=w_kvb_kt, g_q_gqa=r2(g_q_gqa[0]),
        g_q_gqa_sw=_swap_halves64(r2(g_q_gqa[0])),
        g_o_mla=r2(g_o_mla[0]), g_o_gqa=r2(g_o_gqa[0]), w_o=w_o[0].astype(BF16), ln1_g=r2(ln1_g[0]),
        ln1_b=r2(ln1_b[0]), w_router=w_router_p, b_router=b_router_p, ln2_g=r2(ln2_g[0]), ln2_b=r2(ln2_b[0]))


def kernel(x, meta_tokens, ln_emb_g, ln_emb_b, w_in, g_q_a, w_q_b, g_kv_a, w_kv_b, g_q_gqa, g_k_gqa, g_o_mla, g_o_gqa,
           w_o, ln1_g, ln1_b, w_router, b_router, w_gate_up, b_gate_up, w_down, b_down, ln2_g, ln2_b):
    b, s, d = x.shape
    n = b * s
    w = _prep_weights(ln_emb_g, ln_emb_b, w_in, g_q_a, w_q_b, g_kv_a, w_kv_b, g_q_gqa, g_k_gqa, g_o_mla, g_o_gqa,
                      w_o, ln1_g, ln1_b, w_router, b_router, ln2_g, ln2_b)

    tok = jnp.arange(s, dtype=I32)
    g_k = g_k_gqa[0].astype(F32)
    tabs_real = _rope_tables(tok + N_META, tok // GRID_W, tok % GRID_W, g_k)
    mt = jnp.arange(META_PAD, dtype=I32)
    tabs_meta = _rope_tables(mt, jnp.full((META_PAD,), -1, I32), mt, g_k)

    qm, km, vm, qg, kg, vg = _project(x, tabs_real, w, ROW_TILE)
    meta = jnp.pad(meta_tokens.astype(x.dtype), ((0, META_PAD - N_META), (0, 0))).reshape(1, META_PAD, d)
    _, km_m, vm_m, _, kg_m, vg_m = _project(meta, tabs_meta, w, META_PAD)

    o_mla = _mla_attention(qm, km, vm, km_m, vm_m)
    o_gqa = _gqa_attention(qg, kg, vg, kg_m, vg_m)

    h1_tiles, h1_packed, topi, gates = _merge(o_mla, o_gqa, x.reshape(n, d), w)

    run_rows = n * TOP_K + (n // TOKEN_TILE) * N_EXPERTS * (SUBLANES - 1)
    ntiles = -(-run_rows // MOE_TILE) + N_EXPERTS
    ntp = -(-ntiles // LANES) * LANES
    rows_pad = ntiles * MOE_TILE
    lpb, gstart, nslab, lstart, texp, nvalid, ecnt, eoff = _positions(topi, ntp)
    flat = lambda a: a.reshape(-1, LANES)
    xs_flat = _dispatch(flat(h1_packed), lpb, gstart, nslab, lstart, ecnt.reshape(-1), eoff.reshape(-1), rows_pad)

    bgu = b_gate_up[0].astype(F32)
    tiles = lambda a: a.reshape(-1, PACK_CHUNKS, SUBLANES, LANES)
    ys_tiles = _grouped_ffn(tiles(xs_flat), texp.reshape(-1), nvalid.reshape(-1), ecnt.reshape(-1), eoff.reshape(-1),
                            w_gate_up[0], w_down[0],
                            bgu[:, 0::2].reshape(N_EXPERTS, 1, D_FF), bgu[:, 1::2].reshape(N_EXPERTS, 1, D_FF),
                            b_down[0].astype(F32).reshape(N_EXPERTS, 1, d), ntiles)

    out = _combine(lpb, gates, gstart, nslab, lstart, h1_tiles, flat(ys_tiles), w)
    return out.reshape(b, s, d)
```

```python
import functools

import jax
import jax.numpy as jnp
import numpy as np
from jax import lax
from jax.experimental import pallas as pl
from jax.experimental.pallas import tpu as pltpu

D_MODEL = 1024
N_META = 16
GRID_W = 64
ROPE_THETA = 10000.0
MLA_HEADS = 4
Q_LORA = 256
KV_LORA = 128
NOPE_DIM = 128
ROPE_DIM = 64
V_DIM = 128
QK_DIM = NOPE_DIM + ROPE_DIM
GQA_HEADS = 4
GQA_KV_HEADS = 2
GQA_DIM = 128
N_EXPERTS = 32
TOP_K = 4
D_FF = D_MODEL
SWIGLU_LIMIT = 7.0
SWIGLU_ALPHA = 1.702
RMS_EPS = 1e-6
LN_EPS = 1e-5
DEPTH = 1
DEEPNORM_ALPHA = (2.0 * DEPTH) ** 0.25

LANES = 128
SUBLANE_SHIFT = 3
SUBLANES = 1 << SUBLANE_SHIFT
SLAB = SUBLANES * SUBLANES
PACK_CHUNKS = D_MODEL // 2 // LANES
PSLAB = SUBLANES * PACK_CHUNKS
META_PAD = 128
MLA_K = 2 * LANES
NEG_BIG = -1e30
LOG2E = 1.4426950408889634
V_EXT = 2 * LANES

ROW_TILE = 1024
Q_TILE_MLA = 2048
Q_TILE_GQA = 2048
MLA_HEADS_PER_STEP = 2
Q_SUB_MLA = 128
Q_SUB_GQA = 512
KEY_BLOCK = 512
TOKEN_TILE = 1024
MOE_TILE = 1024
FFN_CHAIN = 512
VMEM_LIMIT = 56 * 1024 * 1024

F32 = jnp.float32
BF16 = jnp.bfloat16
I32 = jnp.int32


def _layernorm(x, g, b):
    mu = jnp.mean(x, axis=-1, keepdims=True)
    xc = x - mu
    var = jnp.mean(xc * xc, axis=-1, keepdims=True)
    return xc * lax.rsqrt(var + LN_EPS) * g + b


def _rmsnorm(x, g):
    return x * lax.rsqrt(jnp.mean(x * x, axis=-1, keepdims=True) + RMS_EPS) * g


def _swap_halves64(v):
    shp = v.shape
    return v.reshape(shp[:-1] + (shp[-1] // 64, 2, 32))[..., ::-1, :].reshape(shp)


PROJ_SUB = 256


def _swap_blocks32(xt):
    return jnp.concatenate([xt[o + b:o + b + 32] for o in range(0, xt.shape[0], 64) for b in (32, 0)], axis=0)


def _proj_chain(x, lng, lnb, win_ref, wkt_ref, gqa, wqb_ref, gkva, wkvbv_ref, wkvbkt_ref, gqg, gqgs,
                c1, s1, ca, sa, c1t, s1t, ckt, skt, put):
    r = x.shape[0]
    nt = (((1,), (1,)), ((), ()))
    h0b = _layernorm(x, lng, lnb).astype(BF16)
    z = jnp.dot(h0b, win_ref[...], preferred_element_type=F32)
    q_a = z[:, 0:256]
    kv_a = z[:, 256:384]
    q_g = z[:, 384:896]
    q_gs = z[:, 896:1408]
    v_g = z[:, 1408:1664]
    kt = lax.dot_general(wkt_ref[...], h0b, nt, preferred_element_type=F32)

    q = jnp.dot(_rmsnorm(q_a, gqa).astype(BF16), wqb_ref[...], preferred_element_type=F32)
    kvn = _rmsnorm(kv_a, gkva).astype(BF16)
    v_a = jnp.dot(kvn, wkvbv_ref[...], preferred_element_type=F32)
    knt = lax.dot_general(wkvbkt_ref[...], kvn, nt, preferred_element_type=F32)
    kpet = kt[0:ROPE_DIM]
    krot = kpet * c1t + _swap_blocks32(kpet) * s1t
    lane = lax.broadcasted_iota(I32, (r, LANES), 1)
    scale_a = QK_DIM ** -0.5 * LOG2E
    ones = jnp.ones((r, LANES), F32)
    for c in range(MLA_HEADS // 2):
        lo = LANES * c
        qr = q[:, 512 + lo:512 + lo + LANES] * c1 + q[:, 768 + lo:768 + lo + LANES] * s1
        for hh in range(2):
            h = 2 * c + hh
            slot = jnp.where((lane // 64) == hh, qr, 0.0)
            put("qm", h, (jnp.concatenate([q[:, LANES * h:LANES * (h + 1)], slot], axis=1) * scale_a).astype(BF16))
    for h in range(MLA_HEADS):
        put("km", h, jnp.concatenate([knt[LANES * h:LANES * (h + 1)], krot, krot], axis=0).astype(BF16))
        put("vm", h, jnp.concatenate([v_a[:, LANES * h:LANES * (h + 1)], ones], axis=1).astype(BF16))

    scale_b = GQA_DIM ** -0.5 * LOG2E
    cq, sq = ca * gqg, sa * gqgs
    for h in range(GQA_HEADS):
        sl = slice(LANES * h, LANES * (h + 1))
        xh = q_g[:, sl]
        inv = lax.rsqrt(jnp.mean(xh * xh, axis=-1, keepdims=True) + RMS_EPS)
        put("qg", h, ((xh * cq + q_gs[:, sl] * sq) * (inv * scale_b)).astype(BF16))
    for j in range(GQA_KV_HEADS):
        xt = kt[ROPE_DIM + LANES * j:ROPE_DIM + LANES * (j + 1)]
        inv = lax.rsqrt(jnp.mean(xt * xt, axis=0, keepdims=True) + RMS_EPS)
        put("kg", j, ((xt * ckt + _swap_blocks32(xt) * skt) * inv).astype(BF16))
        put("vg", j, jnp.concatenate([v_g[:, LANES * j:LANES * (j + 1)], ones], axis=1).astype(BF16))


def _proj_body(x_ref, lng_ref, lnb_ref, win_ref, wkt_ref, gqa_ref, wqb_ref, gkva_ref, wkvbv_ref, wkvbkt_ref,
               gqg_ref, gqgs_ref, c1_ref, s1_ref, ca_ref, sa_ref, c1t_ref, s1t_ref, ckt_ref, skt_ref,
               qm_ref, km_ref, vm_ref, qg_ref, kg_ref, vg_ref):
    outs = dict(qm=qm_ref, km=km_ref, vm=vm_ref, qg=qg_ref, kg=kg_ref, vg=vg_ref)
    tile = x_ref.shape[1]
    sub = min(PROJ_SUB, tile)
    for i in range(tile // sub):
        rows = pl.ds(i * sub, sub)

        def put(name, h, val):
            if name in ("km", "kg"):
                outs[name][0, h, :, rows] = val
            else:
                outs[name][0, h, rows, :] = val

        _proj_chain(x_ref[0, rows, :], lng_ref[...], lnb_ref[...], win_ref, wkt_ref, gqa_ref[...], wqb_ref,
                    gkva_ref[...], wkvbv_ref, wkvbkt_ref, gqg_ref[...], gqgs_ref[...],
                    c1_ref[rows, :], s1_ref[rows, :], ca_ref[rows, :], sa_ref[rows, :],
                    c1t_ref[:, rows], s1t_ref[:, rows], ckt_ref[:, rows], skt_ref[:, rows], put)


def _project(x3, tabs, w, tile):
    b, s, d = x3.shape
    nst = s // tile
    full = lambda shape: pl.BlockSpec(shape, lambda bi, si: (0,) * len(shape))
    tab = pl.BlockSpec((tile, LANES), lambda bi, si: (si, 0))
    tabt = lambda dims: pl.BlockSpec((dims, tile), lambda bi, si: (0, si))
    hm = lambda nh, dd: pl.BlockSpec((1, nh, tile, dd), lambda bi, si: (bi, 0, si, 0))
    hmt = lambda nh, dd: pl.BlockSpec((1, nh, dd, tile), lambda bi, si: (bi, 0, 0, si))
    out_shape = [
        jax.ShapeDtypeStruct((b, MLA_HEADS, s, MLA_K), BF16),
        jax.ShapeDtypeStruct((b, MLA_HEADS, MLA_K, s), BF16),
        jax.ShapeDtypeStruct((b, MLA_HEADS, s, V_EXT), BF16),
        jax.ShapeDtypeStruct((b, GQA_HEADS, s, GQA_DIM), BF16),
        jax.ShapeDtypeStruct((b, GQA_KV_HEADS, GQA_DIM, s), BF16),
        jax.ShapeDtypeStruct((b, GQA_KV_HEADS, s, V_EXT), BF16),
    ]
    return pl.pallas_call(
        _proj_body,
        grid=(b, nst),
        in_specs=[
            pl.BlockSpec((1, tile, d), lambda bi, si: (bi, si, 0)),
            full((1, d)), full((1, d)),
            full(w["w_in"].shape), full(w["w_kt"].shape), full((1, Q_LORA)), full(w["w_qb"].shape),
            full((1, KV_LORA)), full(w["w_kvb_v"].shape), full(w["w_kvb_kt"].shape),
            full((1, GQA_DIM)), full((1, GQA_DIM)),
            tab, tab, tab, tab, tabt(ROPE_DIM), tabt(ROPE_DIM), tabt(GQA_DIM), tabt(GQA_DIM),
        ],
        out_specs=[hm(MLA_HEADS, MLA_K), hmt(MLA_HEADS, MLA_K), hm(MLA_HEADS, V_EXT),
                   hm(GQA_HEADS, GQA_DIM), hmt(GQA_KV_HEADS, GQA_DIM), hm(GQA_KV_HEADS, V_EXT)],
        out_shape=out_shape,
        compiler_params=pltpu.CompilerParams(dimension_semantics=("parallel", "parallel"),
                                             vmem_limit_bytes=VMEM_LIMIT),
        name="proj",
    )(x3, w["ln_emb_g"], w["ln_emb_b"], w["w_in"], w["w_kt"], w["g_q_a"], w["w_qb"], w["g_kv_a"], w["w_kvb_v"],
      w["w_kvb_kt"], w["g_q_gqa"], w["g_q_gqa_sw"], *tabs)


def _softmax_pv(q, kt, v, kmt, vm):
    sm = jnp.dot(q, kmt, preferred_element_type=F32)
    col = lax.broadcasted_iota(I32, sm.shape, 1)
    sm = jnp.where(col < N_META, sm, NEG_BIG)
    nblk = kt.shape[1] // KEY_BLOCK
    blocks = [jnp.dot(q, kt[:, KEY_BLOCK * c:KEY_BLOCK * (c + 1)], preferred_element_type=F32) for c in range(nblk)]
    mx = blocks[0]
    for c in range(1, nblk):
        mx = jnp.maximum(mx, blocks[c])
    m = jnp.maximum(jnp.max(mx, axis=1, keepdims=True), jnp.max(sm, axis=1, keepdims=True))
    acc = jnp.dot(jnp.exp2(sm - m).astype(BF16), vm, preferred_element_type=F32)
    for c in range(nblk):
        p = jnp.exp2(blocks[c] - m).astype(BF16)
        acc = acc + jnp.dot(p, v[KEY_BLOCK * c:KEY_BLOCK * (c + 1), :], preferred_element_type=F32)
    return acc[:, 0:V_DIM] / acc[:, V_DIM:V_EXT]


def _mla_attn_body(q_ref, k_ref, v_ref, km_ref, vm_ref, o_ref):
    for h in range(q_ref.shape[1]):
        for i in range(q_ref.shape[2] // Q_SUB_MLA):
            rows = pl.ds(i * Q_SUB_MLA, Q_SUB_MLA)
            o = _softmax_pv(q_ref[0, h, rows, :], k_ref.at[0, h], v_ref.at[0, h], km_ref[0, h], vm_ref[0, h])
            o_ref[rows, V_DIM * h:V_DIM * (h + 1)] = o.astype(o_ref.dtype)


def _gqa_attn_body(q_ref, k_ref, v_ref, km_ref, vm_ref, o_ref):
    for g in range(2):
        for i in range(q_ref.shape[2] // Q_SUB_GQA):
            rows = pl.ds(i * Q_SUB_GQA, Q_SUB_GQA)
            o = _softmax_pv(q_ref[0, g, rows, :], k_ref.at[0, 0], v_ref.at[0, 0], km_ref[0, 0], vm_ref[0, 0])
            o_ref[rows, GQA_DIM * g:GQA_DIM * (g + 1)] = o.astype(o_ref.dtype)


def _mla_attention(qm, kmt, vm, kmeta_t, vmeta):
    b, h, s, dk = qm.shape
    tq = Q_TILE_MLA
    nq = s // tq
    hs = MLA_HEADS_PER_STEP
    return pl.pallas_call(
        _mla_attn_body,
        grid=(b, h // hs, nq),
        in_specs=[
            pl.BlockSpec((1, hs, tq, dk), lambda bi, hi, qi: (bi, hi, qi, 0)),
            pl.BlockSpec((1, hs, dk, s), lambda bi, hi, qi: (bi, hi, 0, 0)),
            pl.BlockSpec((1, hs, s, V_EXT), lambda bi, hi, qi: (bi, hi, 0, 0)),
            pl.BlockSpec((1, hs, dk, META_PAD), lambda bi, hi, qi: (0, hi, 0, 0)),
            pl.BlockSpec((1, hs, META_PAD, V_EXT), lambda bi, hi, qi: (0, hi, 0, 0)),
        ],
        out_specs=pl.BlockSpec((tq, hs * V_DIM), lambda bi, hi, qi: (bi * nq + qi, hi)),
        out_shape=jax.ShapeDtypeStruct((b * s, h * V_DIM), BF16),
        compiler_params=pltpu.CompilerParams(dimension_semantics=("parallel", "parallel", "parallel"),
                                             vmem_limit_bytes=VMEM_LIMIT),
        name="mla_attn",
    )(qm, kmt, vm, kmeta_t, vmeta)


def _gqa_attention(qg, kgt, vg, kmeta_t, vmeta):
    b, h, s, d = qg.shape
    hk = kgt.shape[1]
    tq = Q_TILE_GQA
    nq = s // tq
    return pl.pallas_call(
        _gqa_attn_body,
        grid=(b, hk, nq),
        in_specs=[
            pl.BlockSpec((1, 2, tq, d), lambda bi, ji, qi: (bi, ji, qi, 0)),
            pl.BlockSpec((1, 1, d, s), lambda bi, ji, qi: (bi, ji, 0, 0)),
            pl.BlockSpec((1, 1, s, V_EXT), lambda bi, ji, qi: (bi, ji, 0, 0)),
            pl.BlockSpec((1, 1, d, META_PAD), lambda bi, ji, qi: (0, ji, 0, 0)),
            pl.BlockSpec((1, 1, META_PAD, V_EXT), lambda bi, ji, qi: (0, ji, 0, 0)),
        ],
        out_specs=pl.BlockSpec((tq, 2 * d), lambda bi, ji, qi: (bi * nq + qi, ji)),
        out_shape=jax.ShapeDtypeStruct((b * s, h * d), BF16),
        compiler_params=pltpu.CompilerParams(dimension_semantics=("parallel", "parallel", "parallel"),
                                             vmem_limit_bytes=VMEM_LIMIT),
        name="gqa_attn",
    )(qg, kgt, vg, kmeta_t, vmeta)


def _to_tiles(ref, x):
    r = x.shape[0]
    for s in range(x.shape[1] // LANES):
        ref[:, s] = x[:, LANES * s:LANES * (s + 1)].reshape(r // SUBLANES, SUBLANES, LANES)


def _from_tiles(ref):
    r = ref.shape[0] * SUBLANES
    return jnp.concatenate([ref[:, s].reshape(r, LANES) for s in range(ref.shape[1])], axis=1)


MERGE_SUB = 256


def _pack_rows(v):
    c = v.shape[1] // 2
    lo = lax.bitcast_convert_type(v[:, 0:c].astype(BF16).astype(F32), I32)
    hi = lax.bitcast_convert_type(v[:, c:2 * c].astype(BF16).astype(F32), I32)
    return lax.shift_right_logical(lo, 16) | hi


def _unpack_words(w):
    return (lax.bitcast_convert_type(w << 16, F32), lax.bitcast_convert_type(w & jnp.int32(-65536), F32))


def _merge_body(om_ref, og_ref, x_ref, lng_ref, lnb_ref, gom_ref, gog_ref, wo_ref, l1g_ref, l1b_ref,
                wr_ref, br_ref, h1_ref, h1p_ref, topi_ref, gate_ref):
    for i in range(x_ref.shape[0] // MERGE_SUB):
        rows = pl.ds(i * MERGE_SUB, MERGE_SUB)
        h1, ti, gt = _merge_chain(om_ref[rows, :], og_ref[rows, :], x_ref[rows, :], lng_ref[...], lnb_ref[...],
                                  gom_ref[...], gog_ref[...], wo_ref, l1g_ref[...], l1b_ref[...], wr_ref, br_ref[...])
        blocks = pl.ds(i * (MERGE_SUB // SUBLANES), MERGE_SUB // SUBLANES)
        _to_tiles(h1_ref.at[blocks], h1)
        _to_tiles(h1p_ref.at[blocks], _pack_rows(h1))
        topi_ref[0, :, rows] = ti
        for k in range(TOP_K):
            gate_ref[0, :, pl.ds(k * TOKEN_TILE + i * MERGE_SUB, MERGE_SUB)] = gt[k:k + 1, :]


def _merge_chain(om, og, x, lng, lnb, gom, gog, wo_ref, l1g, l1b, wr_ref, br):
    t = x.shape[0]
    h0 = _layernorm(x, lng, lnb)
    nm = _rmsnorm(om.astype(F32), gom).astype(BF16)
    ng = _rmsnorm(og.astype(F32), gog).astype(BF16)
    half = nm.shape[1]
    mix = jnp.dot(nm, wo_ref[0:half, :], preferred_element_type=F32)
    mix = mix + jnp.dot(ng, wo_ref[half:2 * half, :], preferred_element_type=F32)
    h1 = _layernorm(DEEPNORM_ALPHA * h0 + mix, l1g, l1b)

    hi = h1.astype(BF16)
    lo = (h1 - hi.astype(F32)).astype(BF16)
    acc = jnp.dot(hi, wr_ref[...], preferred_element_type=F32) + jnp.dot(lo, wr_ref[...], preferred_element_type=F32)
    logits = acc[:, 0:LANES] + acc[:, LANES:2 * LANES] + br
    cur = logits.T[0:N_EXPERTS, :]
    eidx = lax.broadcasted_iota(I32, cur.shape, 0)
    vals, idxs = [], []
    for _ in range(TOP_K):
        m = jnp.max(cur, axis=0, keepdims=True)
        i = jnp.min(jnp.where(cur == m, eidx, N_EXPERTS), axis=0, keepdims=True)
        vals.append(m)
        idxs.append(i)
        cur = jnp.where(eidx == i, -jnp.inf, cur)
    ex = [jnp.exp(v - vals[0]) for v in vals]
    den = ex[0] + ex[1] + ex[2] + ex[3]
    sub = lax.broadcasted_iota(I32, (8, t), 0)
    ti = jnp.zeros((8, t), I32)
    gt = jnp.zeros((8, t), F32)
    for k in range(TOP_K):
        ti = jnp.where(sub == k, idxs[k], ti)
        gt = jnp.where(sub == k, ex[k] / den, gt)
    return h1, ti[0:TOP_K, :], gt[0:TOP_K, :]


def _merge(o_mla, o_gqa, x2, w):
    n, d = x2.shape
    tile = TOKEN_TILE
    half = o_mla.shape[1]
    full = lambda shape: pl.BlockSpec(shape, lambda i: (0,) * len(shape))
    row = lambda width: pl.BlockSpec((tile, width), lambda i: (i, 0))
    return pl.pallas_call(
        _merge_body,
        grid=(n // tile,),
        in_specs=[row(half), row(half), row(d), full((1, d)), full((1, d)), full((1, half)), full((1, half)),
                  full((d, d)), full((1, d)), full((1, d)), full((d, 2 * LANES)), full((1, LANES))],
        out_specs=[pl.BlockSpec((tile // SUBLANES, d // LANES, SUBLANES, LANES), lambda i: (i, 0, 0, 0)),
                   pl.BlockSpec((tile // SUBLANES, PACK_CHUNKS, SUBLANES, LANES), lambda i: (i, 0, 0, 0)),
                   pl.BlockSpec((1, TOP_K, tile), lambda i: (i, 0, 0)),
                   pl.BlockSpec((1, 1, TOP_K * tile), lambda i: (i, 0, 0))],
        out_shape=[jax.ShapeDtypeStruct((n // SUBLANES, d // LANES, SUBLANES, LANES), F32),
                   jax.ShapeDtypeStruct((n // SUBLANES, PACK_CHUNKS, SUBLANES, LANES), I32),
                   jax.ShapeDtypeStruct((n // tile, TOP_K, tile), I32),
                   jax.ShapeDtypeStruct((n // tile, 1, TOP_K * tile), F32)],
        compiler_params=pltpu.CompilerParams(dimension_semantics=("parallel",), vmem_limit_bytes=VMEM_LIMIT),
        name="merge",
    )(o_mla, o_gqa, x2, w["ln_emb_g"], w["ln_emb_b"], w["g_o_mla"], w["g_o_gqa"], w["w_o"],
      w["ln1_g"], w["ln1_b"], w["w_router"], w["b_router"])


def _lanes_from_sublanes(col):
    diag = lax.broadcasted_iota(I32, col.shape, 0) == lax.broadcasted_iota(I32, col.shape, 1)
    return jnp.sum(jnp.where(diag, col, 0.0), axis=0, keepdims=True)


def _positions_body(topi_ref, upper_ref, lpb_ref, gstart_ref, nslab_ref, lstart_ref, texp_ref, nvalid_ref, ecnt_ref,
                    eoff_ref, *, ntp):
    ntile, _, tl = topi_ref.shape
    r = lax.broadcasted_iota(I32, (N_EXPERTS, N_EXPERTS), 0)
    c = lax.broadcasted_iota(I32, (N_EXPERTS, N_EXPERTS), 1)
    lower = (c < r).astype(F32)
    eidx = lax.broadcasted_iota(I32, (N_EXPERTS, tl), 0)

    def tile_onehots(j):
        topi = topi_ref[j]
        ohs = [eidx == topi[k:k + 1, :] for k in range(TOP_K)]
        onehot = ohs[0].astype(F32) + ohs[1].astype(F32) + ohs[2].astype(F32) + ohs[3].astype(F32)
        cnt = jnp.sum(onehot, axis=1, keepdims=True)
        run = jnp.floor((cnt + (SUBLANES - 1)) * (1.0 / SUBLANES)) * SUBLANES
        return ohs, onehot, jnp.broadcast_to(run, (N_EXPERTS, LANES))

    tot = lax.fori_loop(0, ntile, lambda j, acc: acc + tile_onehots(j)[2], jnp.zeros((N_EXPERTS, LANES), F32))
    pc = jnp.floor((tot + (MOE_TILE - 1)) * (1.0 / MOE_TILE)) * MOE_TILE
    off = jnp.dot(lower, pc, precision=lax.Precision.HIGHEST, preferred_element_type=F32)
    cumend = off + pc
    tstart = lax.broadcasted_iota(I32, (N_EXPERTS, ntp), 1).astype(F32) * MOE_TILE
    te = jnp.sum((jnp.broadcast_to(cumend[:, 0:1], (N_EXPERTS, ntp)) <= tstart).astype(I32), axis=0, keepdims=True)
    texp_ref[...] = jnp.minimum(te, N_EXPERTS - 1)
    nvalid_ref[...] = (cumend[N_EXPERTS - 1:N_EXPERTS, :] * (1.0 / MOE_TILE)).astype(I32)
    ecnt_ref[...] = _lanes_from_sublanes(tot).astype(I32)
    eoff_ref[...] = _lanes_from_sublanes(off).astype(I32)

    def place(j, carry):
        ohs, onehot, tile_run = tile_onehots(j)
        before = jnp.dot(onehot.astype(BF16), upper_ref[...], preferred_element_type=F32)
        loff = jnp.dot(lower, tile_run, precision=lax.Precision.HIGHEST, preferred_element_type=F32)
        base = before + loff[:, 0:1]
        sub = lax.broadcasted_iota(I32, (SUBLANES, tl), 0)
        out = jnp.zeros((SUBLANES, tl), F32)
        for k in range(TOP_K):
            pk = jnp.sum(jnp.where(ohs[k], base, 0.0), axis=0, keepdims=True)
            out = jnp.where(sub == k, pk, out)
        lp = out[0:TOP_K, :].astype(I32)
        lpb = (lp >> SUBLANE_SHIFT) * PSLAB + (lp & (SUBLANES - 1))
        lpb_ref[j] = jnp.concatenate([lpb[k:k + 1, :] for k in range(TOP_K)], axis=1)
        inv = 1.0 / SUBLANES
        gstart_ref[j] = (_lanes_from_sublanes(off + carry) * inv).astype(I32)
        nslab_ref[j] = (_lanes_from_sublanes(tile_run) * inv).astype(I32)
        lstart_ref[j] = (_lanes_from_sublanes(loff) * inv).astype(I32)
        return carry + tile_run

    lax.fori_loop(0, ntile, place, jnp.zeros((N_EXPERTS, LANES), F32))


def _positions(topi, ntp):
    ntile, _, tl = topi.shape
    tab = jax.ShapeDtypeStruct((ntile, 1, LANES), I32)
    small = jax.ShapeDtypeStruct((1, LANES), I32)
    upper = (jnp.arange(tl, dtype=I32)[:, None] < jnp.arange(tl, dtype=I32)[None, :]).astype(BF16)
    return pl.pallas_call(
        functools.partial(_positions_body, ntp=ntp),
        out_shape=[jax.ShapeDtypeStruct((ntile, 1, TOP_K * tl), I32), tab, tab, tab,
                   jax.ShapeDtypeStruct((1, ntp), I32), small, small, small],
        compiler_params=pltpu.CompilerParams(vmem_limit_bytes=VMEM_LIMIT),
        name="positions",
    )(topi, upper)


STAGE_ROWS = TOKEN_TILE * TOP_K + N_EXPERTS * SUBLANES
STAGE_FLAT = STAGE_ROWS * PACK_CHUNKS


def _load_row(ref, flat_start, chunks=SUBLANES):
    return ref[pl.ds(flat_start, chunks, stride=SUBLANES), :]


def _store_row(ref, flat_start, v):
    ref[pl.ds(flat_start, v.shape[0], stride=SUBLANES), :] = v


CHUNK_SHIFT = 2
CHUNK_SLABS = 1 << CHUNK_SHIFT


def _copy_run(src_ref, src0, dst_ref, dst0, nslab, sem):
    big = CHUNK_SLABS * PSLAB
    nbig = nslab >> CHUNK_SHIFT
    nsmall = nslab & (CHUNK_SLABS - 1)

    def start_big(s, c):
        pltpu.make_async_copy(src_ref.at[pl.ds(src0 + s * big, big)], dst_ref.at[pl.ds(dst0 + s * big, big)], sem).start()
        return c

    def start_small(s, c):
        o = nbig * big + s * PSLAB
        pltpu.make_async_copy(src_ref.at[pl.ds(src0 + o, PSLAB)], dst_ref.at[pl.ds(dst0 + o, PSLAB)], sem).start()
        return c

    lax.fori_loop(0, nbig, start_big, 0)
    lax.fori_loop(0, nsmall, start_small, 0)


def _wait_runs(src_ref, dst_ref, nbig, nsmall, sem):
    big = CHUNK_SLABS * PSLAB

    def wait_big(s, c):
        pltpu.make_async_copy(src_ref.at[pl.ds(0, big)], dst_ref.at[pl.ds(0, big)], sem).wait()
        return c

    def wait_small(s, c):
        pltpu.make_async_copy(src_ref.at[pl.ds(0, PSLAB)], dst_ref.at[pl.ds(0, PSLAB)], sem).wait()
        return c

    lax.fori_loop(0, nbig, wait_big, 0)
    lax.fori_loop(0, nsmall, wait_small, 0)


def _chunk_counts(nslab_ref, j):
    def add(e, c):
        n = nslab_ref[j, 0, e]
        return c[0] + (n >> CHUNK_SHIFT), c[1] + (n & (CHUNK_SLABS - 1))

    return lax.fori_loop(0, N_EXPERTS, add, (jnp.int32(0), jnp.int32(0)))


def _dispatch_body(ecnt_ref, eoff_ref, lpb_ref, gstart_ref, nslab_ref, lstart_ref, h_ref, xs_ref,
                   stg0_ref, stg1_ref, zero_ref, issued_ref, sem, zsem):
    g = pl.program_id(0)
    ng = pl.num_programs(0)
    stages = (stg0_ref, stg1_ref)

    def drain(j):
        _wait_runs(stages[j], xs_ref, issued_ref[2 * j], issued_ref[2 * j + 1], sem.at[j])

    def run(j):
        stg_ref = stages[j]

        @pl.when(g >= 2)
        def _():
            drain(j)

        def zero_last(e, c):
            last = jnp.maximum(lstart_ref[0, 0, e] + nslab_ref[0, 0, e] - 1, 0)
            stg_ref[pl.ds(last * PSLAB, PSLAB), :] = jnp.zeros((PSLAB, LANES), I32)
            return c

        lax.fori_loop(0, N_EXPERTS, zero_last, 0)

        def move(i, c):
            for u in range(SUBLANES):
                v = _load_row(h_ref, i * PSLAB + u, PACK_CHUNKS)
                for k in range(TOP_K):
                    _store_row(stg_ref, lpb_ref[0, 0, k * TOKEN_TILE + i * SUBLANES + u], v)
            return c

        lax.fori_loop(0, TOKEN_TILE // SUBLANES, move, 0)

        def send_run(e, c):
            _copy_run(stg_ref, lstart_ref[0, 0, e] * PSLAB, xs_ref, gstart_ref[0, 0, e] * PSLAB,
                      nslab_ref[0, 0, e], sem.at[j])
            return c

        lax.fori_loop(0, N_EXPERTS, send_run, 0)
        nbig, nsmall = _chunk_counts(nslab_ref, 0)
        issued_ref[2 * j] = nbig
        issued_ref[2 * j + 1] = nsmall

    for j in range(2):
        @pl.when(lax.rem(g, 2) == j)
        def _():
            run(j)

    @pl.when(g == ng - 1)
    def _():
        zero_ref[...] = jnp.zeros_like(zero_ref)

        def zero_copy(dst_slab):
            return pltpu.make_async_copy(zero_ref, xs_ref.at[pl.ds(dst_slab * PSLAB, PSLAB)], zsem)

        def pad_segment(e, c):
            rows = ecnt_ref[e]
            first = (eoff_ref[e] + rows) >> SUBLANE_SHIFT
            npad = lax.rem(MOE_TILE - lax.rem(rows, MOE_TILE), MOE_TILE) >> SUBLANE_SHIFT
            lax.fori_loop(0, npad, lambda s, cc: (zero_copy(first + s).start(), cc)[1], 0)
            lax.fori_loop(0, npad, lambda s, cc: (zero_copy(first + s).wait(), cc)[1], 0)
            return c

        lax.fori_loop(0, N_EXPERTS, pad_segment, 0)
        for j in range(2):
            @pl.when(jnp.logical_and(ng > 1, lax.rem(g, 2) != j))
            def _():
                drain(j)

        for j in range(2):
            @pl.when(lax.rem(g, 2) == j)
            def _():
                drain(j)


def _dispatch(h1_flat, lpb, gstart, nslab, lstart, ecnt, eoff, rows_pad):
    nflat = h1_flat.shape[0]
    t = TOKEN_TILE
    smem = lambda shape, imap: pl.BlockSpec(shape, imap, memory_space=pltpu.SMEM)
    per_tile = smem((1, 1, LANES), lambda i, c, o: (i, 0, 0))
    grid_spec = pltpu.PrefetchScalarGridSpec(
        num_scalar_prefetch=2,
        grid=(nflat // (t * PACK_CHUNKS),),
        in_specs=[smem((1, 1, TOP_K * t), lambda i, c, o: (i, 0, 0)), per_tile, per_tile, per_tile,
                  pl.BlockSpec((t * PACK_CHUNKS, LANES), lambda i, c, o: (i, 0))],
        out_specs=pl.BlockSpec(memory_space=pl.ANY),
        scratch_shapes=[pltpu.VMEM((STAGE_FLAT, LANES), I32), pltpu.VMEM((STAGE_FLAT, LANES), I32),
                        pltpu.VMEM((PSLAB, LANES), I32),
                        pltpu.SMEM((4,), I32), pltpu.SemaphoreType.DMA((2,)), pltpu.SemaphoreType.DMA],
    )
    return pl.pallas_call(
        _dispatch_body,
        grid_spec=grid_spec,
        out_shape=jax.ShapeDtypeStruct((rows_pad * PACK_CHUNKS, LANES), I32),
        compiler_params=pltpu.CompilerParams(dimension_semantics=("arbitrary",), vmem_limit_bytes=VMEM_LIMIT),
        name="dispatch",
    )(ecnt, eoff, lpb, gstart, nslab, lstart, h1_flat)


def _ffn_body(texp_ref, nvalid_ref, ecnt_ref, eoff_ref, x_ref, wgu_hbm, wd_hbm, bg_ref, bl_ref, bd_ref, y_ref,
              wgu_buf, wd_buf, wg_sc, wl_sc, wd_sc, seg_ref, sem):
    i = pl.program_id(0)
    valid = i < nvalid_ref[0]
    e = texp_ref[i]
    first_of_expert = jnp.logical_or(i == 0, e != texp_ref[jnp.maximum(i - 1, 0)])

    def weight_copies(expert, slot):
        return (pltpu.make_async_copy(wgu_hbm.at[expert], wgu_buf.at[slot], sem.at[slot]),
                pltpu.make_async_copy(wd_hbm.at[expert], wd_buf.at[slot], sem.at[slot]))

    def prepare(slot):
        blk = 2 * LANES
        r = lax.broadcasted_iota(I32, (blk, blk), 0)
        c = lax.broadcasted_iota(I32, (blk, blk), 1)
        sel = (r == jnp.where(c < LANES, 2 * c, 2 * (c - LANES) + 1)).astype(BF16)
        for b in range(wgu_buf.shape[2] // blk):
            wb = wgu_buf[slot, :, blk * b:blk * (b + 1)].astype(BF16)
            y = jnp.dot(wb, sel, preferred_element_type=F32).astype(BF16)
            wg_sc[:, LANES * b:LANES * (b + 1)] = y[:, 0:LANES]
            wl_sc[:, LANES * b:LANES * (b + 1)] = y[:, LANES:blk]
        wd_sc[...] = wd_buf[slot].astype(BF16)

    @pl.when(jnp.logical_and(valid, first_of_expert))
    def _():
        @pl.when(i == 0)
        def _():
            seg_ref[0] = 0
            for cp in weight_copies(e, 0):
                cp.start()

        seg = seg_ref[0]
        seg_ref[0] = seg + 1
        nxt = lax.while_loop(lambda n: jnp.logical_and(n < N_EXPERTS, ecnt_ref[jnp.minimum(n, N_EXPERTS - 1)] == 0),
                             lambda n: n + 1, e + 1)
        for slot in range(2):
            @pl.when(lax.rem(seg, 2) == slot)
            def _():
                for cp in weight_copies(e, slot):
                    cp.wait()

                @pl.when(nxt < N_EXPERTS)
                def _():
                    for cp in weight_copies(nxt, 1 - slot):
                        cp.start()

                prepare(slot)

    def chain(b0, nb):
        blocks = pl.ds(b0, nb)
        x = jnp.concatenate(_unpack_words(_from_tiles(x_ref.at[blocks])), axis=1).astype(BF16)
        hg = jnp.dot(x, wg_sc[...], preferred_element_type=F32) + bg_ref[0]
        hl = jnp.dot(x, wl_sc[...], preferred_element_type=F32) + bl_ref[0]
        g = jnp.minimum(hg, SWIGLU_LIMIT)
        lin = jnp.clip(hl, -SWIGLU_LIMIT, SWIGLU_LIMIT)
        act = g * (1.0 / (1.0 + jnp.exp(-SWIGLU_ALPHA * g))) * (lin + 1.0)
        y = jnp.dot(act.astype(BF16), wd_sc[...], preferred_element_type=F32) + bd_ref[0]
        _to_tiles(y_ref.at[blocks], _pack_rows(y))

    def compute(nrows):
        done = 0
        while done < nrows:
            n = min(FFN_CHAIN, nrows - done)
            chain(done // SUBLANES, n // SUBLANES)
            done += n
        rest = y_ref.shape[0] - nrows // SUBLANES
        if rest:
            y_ref[pl.ds(nrows // SUBLANES, rest)] = jnp.zeros((rest,) + y_ref.shape[1:], I32)

    used = eoff_ref[e] + ecnt_ref[e] - i * MOE_TILE
    quarter = MOE_TILE // 4
    for q in range(1, 5):
        lo, hi = (q - 1) * quarter, q * quarter
        in_range = jnp.logical_and(used > lo, used <= hi) if q < 4 else used > lo

        @pl.when(jnp.logical_and(valid, in_range))
        def _():
            compute(hi)


def _grouped_ffn(xs_tiles, texp, nvalid, ecnt, eoff, wgu, wd, bg, bl, bd, ntiles):
    d = wgu.shape[1]
    tm = MOE_TILE
    f = wd.shape[1]
    xmap = lambda i, te, nv, ec, eo: (jnp.minimum(i, nv[0] - 1), 0, 0, 0)
    wmap = lambda i, te, nv, ec, eo: (te[i], 0, 0)
    rows_blk = pl.BlockSpec((tm // SUBLANES, PACK_CHUNKS, SUBLANES, LANES), xmap)
    hbm = pl.BlockSpec(memory_space=pl.ANY)
    grid_spec = pltpu.PrefetchScalarGridSpec(
        num_scalar_prefetch=4,
        grid=(ntiles,),
        in_specs=[rows_blk, hbm, hbm,
                  pl.BlockSpec((1, 1, f), wmap), pl.BlockSpec((1, 1, f), wmap), pl.BlockSpec((1, 1, d), wmap)],
        out_specs=rows_blk,
        scratch_shapes=[pltpu.VMEM((2, d, 2 * f), F32), pltpu.VMEM((2, f, d), F32),
                        pltpu.VMEM((d, f), BF16), pltpu.VMEM((d, f), BF16), pltpu.VMEM((f, d), BF16),
                        pltpu.SMEM((1,), I32), pltpu.SemaphoreType.DMA((2,))],
    )
    return pl.pallas_call(
        _ffn_body,
        grid_spec=grid_spec,
        out_shape=jax.ShapeDtypeStruct(xs_tiles.shape, I32),
        compiler_params=pltpu.CompilerParams(dimension_semantics=("arbitrary",), vmem_limit_bytes=VMEM_LIMIT),
        name="ffn",
    )(texp, nvalid, ecnt, eoff, xs_tiles, wgu, wd, bg, bl, bd)


COMBINE_SUB = 512


def _combine_body(lpb_ref, gate_ref, gstart_ref, nslab_ref, lstart_ref, gstart2_ref, nslab2_ref, lstart2_ref,
                  h_ref, l2g_ref, l2b_ref, ys_ref, o_ref, stg0_ref, stg1_ref, moe_ref, sem):
    g = pl.program_id(0)
    ng = pl.num_programs(0)
    stages = (stg0_ref, stg1_ref)

    def fetch(gs_ref, ns_ref, ls_ref, j):
        def per_expert(e, c):
            _copy_run(ys_ref, gs_ref[0, 0, e] * PSLAB, stages[j], ls_ref[0, 0, e] * PSLAB, ns_ref[0, 0, e], sem.at[j])
            return c

        lax.fori_loop(0, N_EXPERTS, per_expert, 0)

    def wait_tile(j):
        nbig, nsmall = _chunk_counts(nslab_ref, 0)
        _wait_runs(ys_ref, stages[j], nbig, nsmall, sem.at[j])

    def gather(j):
        stg_ref = stages[j]

        def body(i, c):
            for u in range(SUBLANES):
                tok = i * SUBLANES + u
                acc_lo = acc_hi = None
                for k in range(TOP_K):
                    gk = gate_ref[0, 0, k * TOKEN_TILE + tok]
                    lo, hi = _unpack_words(_load_row(stg_ref, lpb_ref[0, 0, k * TOKEN_TILE + tok], PACK_CHUNKS))
                    acc_lo = gk * lo if k == 0 else acc_lo + gk * lo
                    acc_hi = gk * hi if k == 0 else acc_hi + gk * hi
                _store_row(moe_ref, i * SLAB + u, acc_lo)
                _store_row(moe_ref, i * SLAB + PSLAB + u, acc_hi)
            return c

        lax.fori_loop(0, TOKEN_TILE // SUBLANES, body, 0)

    def finish():
        nblk = COMBINE_SUB // SUBLANES
        for part in range(TOKEN_TILE // COMBINE_SUB):
            moe = jnp.concatenate(
                [jnp.concatenate([moe_ref[pl.ds((part * nblk + i) * SLAB + s * SUBLANES, SUBLANES), :]
                                  for i in range(nblk)], axis=0) for s in range(SUBLANES)], axis=1)
            h1 = _from_tiles(h_ref.at[pl.ds(part * nblk, nblk)])
            o_ref[pl.ds(part * COMBINE_SUB, COMBINE_SUB), :] = _layernorm(DEEPNORM_ALPHA * h1 + moe, l2g_ref[...],
                                                                          l2b_ref[...])

    @pl.when(g == 0)
    def _():
        fetch(gstart_ref, nslab_ref, lstart_ref, 0)

    for j in range(2):
        @pl.when(lax.rem(g, 2) == j)
        def _():
            @pl.when(g + 1 < ng)
            def _():
                fetch(gstart2_ref, nslab2_ref, lstart2_ref, 1 - j)

            wait_tile(j)
            gather(j)

    finish()


def _combine(lpb, gates, gstart, nslab, lstart, h1_tiles, ys_flat, w):
    d = h1_tiles.shape[1] * LANES
    n = h1_tiles.shape[0] * SUBLANES
    t = TOKEN_TILE
    ng = n // t
    full = lambda shape: pl.BlockSpec(shape, lambda i: (0,) * len(shape))
    smem = lambda shape, imap: pl.BlockSpec(shape, imap, memory_space=pltpu.SMEM)
    vec = smem((1, 1, TOP_K * t), lambda i: (i, 0, 0))
    this_step = smem((1, 1, LANES), lambda i: (i, 0, 0))
    next_step = smem((1, 1, LANES), lambda i: (jnp.minimum(i + 1, ng - 1), 0, 0))
    return pl.pallas_call(
        _combine_body,
        grid=(ng,),
        in_specs=[vec, vec, this_step, this_step, this_step, next_step, next_step, next_step,
                  pl.BlockSpec((t // SUBLANES, d // LANES, SUBLANES, LANES), lambda i: (i, 0, 0, 0)),
                  full((1, d)), full((1, d)),
                  pl.BlockSpec(memory_space=pl.ANY)],
        out_specs=pl.BlockSpec((t, d), lambda i: (i, 0)),
        out_shape=jax.ShapeDtypeStruct((n, d), F32),
        scratch_shapes=[pltpu.VMEM((STAGE_FLAT, LANES), I32), pltpu.VMEM((STAGE_FLAT, LANES), I32),
                        pltpu.VMEM((TOKEN_TILE * SUBLANES, LANES), F32), pltpu.SemaphoreType.DMA((2,))],
        compiler_params=pltpu.CompilerParams(dimension_semantics=("arbitrary",), vmem_limit_bytes=VMEM_LIMIT),
        name="combine",
    )(lpb, gates, gstart, nslab, lstart, gstart, nslab, lstart, h1_tiles, w["ln2_g"], w["ln2_b"], ys_flat)


def _rope_tables(pos_1d, row, col, g_k):
    inv = ROPE_THETA ** (-jnp.arange(0, ROPE_DIM, 2, dtype=F32) / ROPE_DIM)

    def cs(p):
        ang = p.astype(F32)[:, None] * inv[None, :]
        return jnp.cos(ang), jnp.sin(ang)

    c1, s1 = cs(pos_1d)
    cr, sr = cs(row)
    cc, sc = cs(col)
    c1w, s1w = jnp.concatenate([c1, c1, c1, c1], axis=1), jnp.concatenate([-s1, s1, -s1, s1], axis=1)
    caw, saw = jnp.concatenate([cr, cr, cc, cc], axis=1), jnp.concatenate([-sr, sr, -sc, sc], axis=1)
    ckt = caw.T * g_k[:, None]
    skt = saw.T * _swap_halves64(g_k)[:, None]
    return (c1w, s1w, caw, saw, c1w[:, :ROPE_DIM].T, s1w[:, :ROPE_DIM].T, ckt, skt)


def _prep_weights(ln_emb_g, ln_emb_b, w_in, g_q_a, w_q_b, g_kv_a, w_kv_b, g_q_gqa, g_k_gqa, g_o_mla, g_o_gqa, w_o,
                  ln1_g, ln1_b, w_router, b_router, ln2_g, ln2_b):
    r2 = lambda v: v.reshape(1, -1).astype(F32)
    o = np.cumsum([0, Q_LORA, KV_LORA, ROPE_DIM, GQA_HEADS * GQA_DIM, GQA_KV_HEADS * GQA_DIM, GQA_KV_HEADS * GQA_DIM])
    wi = w_in[0]
    seg = [wi[:, o[i]:o[i + 1]] for i in range(6)]
    w_in_p = jnp.concatenate([seg[0], seg[1], seg[3], _swap_halves64(seg[3]), seg[5]], axis=1).astype(BF16)
    w_kt = jnp.concatenate([seg[2], seg[4]], axis=1).T.astype(BF16)
    wq = w_q_b[0].reshape(Q_LORA, MLA_HEADS, QK_DIM)
    wq_rope = wq[:, :, NOPE_DIM:].reshape(Q_LORA, -1)
    w_qb = jnp.concatenate([wq[:, :, :NOPE_DIM].reshape(Q_LORA, -1), wq_rope, _swap_halves64(wq_rope)],
                           axis=1).astype(BF16)
    wk = w_kv_b[0].reshape(KV_LORA, MLA_HEADS, NOPE_DIM + V_DIM)
    w_kvb_kt = wk[:, :, :NOPE_DIM].reshape(KV_LORA, -1).T.astype(BF16)
    w_kvb_v = wk[:, :, NOPE_DIM:].reshape(KV_LORA, -1).astype(BF16)
    wr = jnp.pad(w_router[0].astype(F32), ((0, 0), (0, LANES - N_EXPERTS)))
    wr_hi = wr.astype(BF16)
    w_router_p = jnp.concatenate([wr_hi, (wr - wr_hi.astype(F32)).astype(BF16)], axis=1)
    b_router_p = jnp.pad(b_router[0].astype(F32), (0, LANES - N_EXPERTS), constant_values=NEG_BIG).reshape(1, LANES)
    return dict(
        ln_emb_g=r2(ln_emb_g), ln_emb_b=r2(ln_emb_b), w_in=w_in_p, w_kt=w_kt, g_q_a=r2(g_q_a[0]), w_qb=w_qb,
        g_kv_a=r2(g_kv_a[0]), w_kvb_v=w_kvb_v, w_kvb_kt=w_kvb_kt, g_q_gqa=r2(g_q_gqa[0]),
        g_q_gqa_sw=_swap_halves64(r2(g_q_gqa[0])),
        g_o_mla=r2(g_o_mla[0]), g_o_gqa=r2(g_o_gqa[0]), w_o=w_o[0].astype(BF16), ln1_g=r2(ln1_g[0]),
        ln1_b=r2(ln1_b[0]), w_router=w_router_p, b_router=b_router_p, ln2_g=r2(ln2_g[0]), ln2_b=r2(ln2_b[0]))


def kernel(x, meta_tokens, ln_emb_g, ln_emb_b, w_in, g_q_a, w_q_b, g_kv_a, w_kv_b, g_q_gqa, g_k_gqa, g_o_mla, g_o_gqa,
           w_o, ln1_g, ln1_b, w_router, b_router, w_gate_up, b_gate_up, w_down, b_down, ln2_g, ln2_b):
    b, s, d = x.shape
    n = b * s
    assert d == D_MODEL and meta_tokens.shape == (N_META, d) and s % GRID_W == 0
    assert s % ROW_TILE == 0 and s % Q_TILE_MLA == 0 and s % Q_TILE_GQA == 0 and s % KEY_BLOCK == 0
    assert n % TOKEN_TILE == 0 and TOKEN_TILE % MERGE_SUB == 0 and TOKEN_TILE % COMBINE_SUB == 0
    w = _prep_weights(ln_emb_g, ln_emb_b, w_in, g_q_a, w_q_b, g_kv_a, w_kv_b, g_q_gqa, g_k_gqa, g_o_mla, g_o_gqa,
                      w_o, ln1_g, ln1_b, w_router, b_router, ln2_g, ln2_b)

    tok = jnp.arange(s, dtype=I32)
    g_k = g_k_gqa[0].astype(F32)
    tabs_real = _rope_tables(tok + N_META, tok // GRID_W, tok % GRID_W, g_k)
    mt = jnp.arange(META_PAD, dtype=I32)
    tabs_meta = _rope_tables(mt, jnp.full((META_PAD,), -1, I32), mt, g_k)

    qm, km, vm, qg, kg, vg = _project(x, tabs_real, w, ROW_TILE)
    meta = jnp.pad(meta_tokens.astype(x.dtype), ((0, META_PAD - N_META), (0, 0))).reshape(1, META_PAD, d)
    _, km_m, vm_m, _, kg_m, vg_m = _project(meta, tabs_meta, w, META_PAD)

    o_mla = _mla_attention(qm, km, vm, km_m, vm_m)
    o_gqa = _gqa_attention(qg, kg, vg, kg_m, vg_m)

    h1_tiles, h1_packed, topi, gates = _merge(o_mla, o_gqa, x.reshape(n, d), w)

    run_rows = n * TOP_K + (n // TOKEN_TILE) * N_EXPERTS * (SUBLANES - 1)
    ntiles = -(-run_rows // MOE_TILE) + N_EXPERTS
    ntp = -(-ntiles // LANES) * LANES
    rows_pad = ntiles * MOE_TILE
    lpb, gstart, nslab, lstart, texp, nvalid, ecnt, eoff = _positions(topi, ntp)
    flat = lambda a: a.reshape(-1, LANES)
    xs_flat = _dispatch(flat(h1_packed), lpb, gstart, nslab, lstart, ecnt.reshape(-1), eoff.reshape(-1), rows_pad)

    bgu = b_gate_up[0].astype(F32).reshape(N_EXPERTS, D_FF, 2).transpose(2, 0, 1).reshape(2, N_EXPERTS, 1, D_FF)
    tiles = lambda a: a.reshape(-1, PACK_CHUNKS, SUBLANES, LANES)
    ys_tiles = _grouped_ffn(tiles(xs_flat), texp.reshape(-1), nvalid.reshape(-1), ecnt.reshape(-1), eoff.reshape(-1),
                            w_gate_up[0], w_down[0], bgu[0], bgu[1],
                            b_down[0].astype(F32).reshape(N_EXPERTS, 1, d), ntiles)

    out = _combine(lpb, gates, gstart, nslab, lstart, h1_tiles, flat(ys_tiles), w)
    return out.reshape(b, s, d)
```

```python
import functools

import jax
import jax.numpy as jnp
import numpy as np
from jax import lax
from jax.experimental import pallas as pl
from jax.experimental.pallas import tpu as pltpu

D_MODEL = 1024
N_META = 16
GRID_W = 64
ROPE_THETA = 10000.0
MLA_HEADS = 4
Q_LORA = 256
KV_LORA = 128
NOPE_DIM = 128
ROPE_DIM = 64
V_DIM = 128
QK_DIM = NOPE_DIM + ROPE_DIM
GQA_HEADS = 4
GQA_KV_HEADS = 2
GQA_DIM = 128
N_EXPERTS = 32
TOP_K = 4
D_FF = D_MODEL
SWIGLU_LIMIT = 7.0
SWIGLU_ALPHA = 1.702
RMS_EPS = 1e-6
LN_EPS = 1e-5
DEPTH = 1
DEEPNORM_ALPHA = (2.0 * DEPTH) ** 0.25

LANES = 128
SUBLANE_SHIFT = 3
SUBLANES = 1 << SUBLANE_SHIFT
SLAB = SUBLANES * SUBLANES
PACK_CHUNKS = D_MODEL // 2 // LANES
PSLAB = SUBLANES * PACK_CHUNKS
META_PAD = 128
MLA_K = 2 * LANES
NEG_BIG = -1e30
LOG2E = 1.4426950408889634
V_EXT = 2 * LANES

ROW_TILE = 1024
Q_TILE_MLA = 2048
Q_TILE_GQA = 2048
MLA_HEADS_PER_STEP = 2
Q_SUB_MLA = 128
Q_SUB_GQA = 512
KEY_BLOCK = 512
TOKEN_TILE = 1024
MOE_TILE = 1024
FFN_CHAIN = 512
VMEM_LIMIT = 56 * 1024 * 1024

F32 = jnp.float32
BF16 = jnp.bfloat16
I32 = jnp.int32


def _layernorm(x, g, b):
    mu = jnp.mean(x, axis=-1, keepdims=True)
    xc = x - mu
    var = jnp.mean(xc * xc, axis=-1, keepdims=True)
    return xc * lax.rsqrt(var + LN_EPS) * g + b


def _rmsnorm(x, g):
    return x * lax.rsqrt(jnp.mean(x * x, axis=-1, keepdims=True) + RMS_EPS) * g


def _swap_halves64(v):
    shp = v.shape
    return v.reshape(shp[:-1] + (shp[-1] // 64, 2, 32))[..., ::-1, :].reshape(shp)


PROJ_SUB = 256


def _swap_blocks32(xt):
    return jnp.concatenate([xt[o + b:o + b + 32] for o in range(0, xt.shape[0], 64) for b in (32, 0)], axis=0)


def _proj_chain(x, lng, lnb, win_ref, wkt_ref, gqa, wqb_ref, gkva, wkvbv_ref, wkvbkt_ref, gqg, gqgs,
                c1, s1, ca, sa, c1t, s1t, ckt, skt, put):
    r = x.shape[0]
    nt = (((1,), (1,)), ((), ()))
    h0b = _layernorm(x, lng, lnb).astype(BF16)
    z = jnp.dot(h0b, win_ref[...], preferred_element_type=F32)
    q_a = z[:, 0:256]
    kv_a = z[:, 256:384]
    q_g = z[:, 384:896]
    q_gs = z[:, 896:1408]
    v_g = z[:, 1408:1664]
    kt = lax.dot_general(wkt_ref[...], h0b, nt, preferred_element_type=F32)

    q = jnp.dot(_rmsnorm(q_a, gqa).astype(BF16), wqb_ref[...], preferred_element_type=F32)
    kvn = _rmsnorm(kv_a, gkva).astype(BF16)
    v_a = jnp.dot(kvn, wkvbv_ref[...], preferred_element_type=F32)
    knt = lax.dot_general(wkvbkt_ref[...], kvn, nt, preferred_element_type=F32)
    kpet = kt[0:ROPE_DIM]
    krot = kpet * c1t + _swap_blocks32(kpet) * s1t
    lane = lax.broadcasted_iota(I32, (r, LANES), 1)
    scale_a = QK_DIM ** -0.5 * LOG2E
    ones = jnp.ones((r, LANES), F32)
    for c in range(MLA_HEADS // 2):
        lo = LANES * c
        qr = q[:, 512 + lo:512 + lo + LANES] * c1 + q[:, 768 + lo:768 + lo + LANES] * s1
        for hh in range(2):
            h = 2 * c + hh
            slot = jnp.where((lane // 64) == hh, qr, 0.0)
            put("qm", h, (jnp.concatenate([q[:, LANES * h:LANES * (h + 1)], slot], axis=1) * scale_a).astype(BF16))
    for h in range(MLA_HEADS):
        put("km", h, jnp.concatenate([knt[LANES * h:LANES * (h + 1)], krot, krot], axis=0).astype(BF16))
        put("vm", h, jnp.concatenate([v_a[:, LANES * h:LANES * (h + 1)], ones], axis=1).astype(BF16))

    scale_b = GQA_DIM ** -0.5 * LOG2E
    cq, sq = ca * gqg, sa * gqgs
    for h in range(GQA_HEADS):
        sl = slice(LANES * h, LANES * (h + 1))
        xh = q_g[:, sl]
        inv = lax.rsqrt(jnp.mean(xh * xh, axis=-1, keepdims=True) + RMS_EPS)
        put("qg", h, ((xh * cq + q_gs[:, sl] * sq) * (inv * scale_b)).astype(BF16))
    for j in range(GQA_KV_HEADS):
        xt = kt[ROPE_DIM + LANES * j:ROPE_DIM + LANES * (j + 1)]
        inv = lax.rsqrt(jnp.mean(xt * xt, axis=0, keepdims=True) + RMS_EPS)
        put("kg", j, ((xt * ckt + _swap_blocks32(xt) * skt) * inv).astype(BF16))
        put("vg", j, jnp.concatenate([v_g[:, LANES * j:LANES * (j + 1)], ones], axis=1).astype(BF16))


def _proj_body(x_ref, lng_ref, lnb_ref, win_ref, wkt_ref, gqa_ref, wqb_ref, gkva_ref, wkvbv_ref, wkvbkt_ref,
               gqg_ref, gqgs_ref, c1_ref, s1_ref, ca_ref, sa_ref, c1t_ref, s1t_ref, ckt_ref, skt_ref,
               qm_ref, km_ref, vm_ref, qg_ref, kg_ref, vg_ref):
    outs = dict(qm=qm_ref, km=km_ref, vm=vm_ref, qg=qg_ref, kg=kg_ref, vg=vg_ref)
    tile = x_ref.shape[1]
    sub = min(PROJ_SUB, tile)
    for i in range(tile // sub):
        rows = pl.ds(i * sub, sub)

        def put(name, h, val):
            if name in ("km", "kg"):
                outs[name][0, h, :, rows] = val
            else:
                outs[name][0, h, rows, :] = val

        _proj_chain(x_ref[0, rows, :], lng_ref[...], lnb_ref[...], win_ref, wkt_ref, gqa_ref[...], wqb_ref,
                    gkva_ref[...], wkvbv_ref, wkvbkt_ref, gqg_ref[...], gqgs_ref[...],
                    c1_ref[rows, :], s1_ref[rows, :], ca_ref[rows, :], sa_ref[rows, :],
                    c1t_ref[:, rows], s1t_ref[:, rows], ckt_ref[:, rows], skt_ref[:, rows], put)


def _project(x3, tabs, w, tile):
    b, s, d = x3.shape
    nst = s // tile
    full = lambda shape: pl.BlockSpec(shape, lambda bi, si: (0,) * len(shape))
    tab = pl.BlockSpec((tile, LANES), lambda bi, si: (si, 0))
    tabt = lambda dims: pl.BlockSpec((dims, tile), lambda bi, si: (0, si))
    hm = lambda nh, dd: pl.BlockSpec((1, nh, tile, dd), lambda bi, si: (bi, 0, si, 0))
    hmt = lambda nh, dd: pl.BlockSpec((1, nh, dd, tile), lambda bi, si: (bi, 0, 0, si))
    out_shape = [
        jax.ShapeDtypeStruct((b, MLA_HEADS, s, MLA_K), BF16),
        jax.ShapeDtypeStruct((b, MLA_HEADS, MLA_K, s), BF16),
        jax.ShapeDtypeStruct((b, MLA_HEADS, s, V_EXT), BF16),
        jax.ShapeDtypeStruct((b, GQA_HEADS, s, GQA_DIM), BF16),
        jax.ShapeDtypeStruct((b, GQA_KV_HEADS, GQA_DIM, s), BF16),
        jax.ShapeDtypeStruct((b, GQA_KV_HEADS, s, V_EXT), BF16),
    ]
    return pl.pallas_call(
        _proj_body,
        grid=(b, nst),
        in_specs=[
            pl.BlockSpec((1, tile, d), lambda bi, si: (bi, si, 0)),
            full((1, d)), full((1, d)),
            full(w["w_in"].shape), full(w["w_kt"].shape), full((1, Q_LORA)), full(w["w_qb"].shape),
            full((1, KV_LORA)), full(w["w_kvb_v"].shape), full(w["w_kvb_kt"].shape),
            full((1, GQA_DIM)), full((1, GQA_DIM)),
            tab, tab, tab, tab, tabt(ROPE_DIM), tabt(ROPE_DIM), tabt(GQA_DIM), tabt(GQA_DIM),
        ],
        out_specs=[hm(MLA_HEADS, MLA_K), hmt(MLA_HEADS, MLA_K), hm(MLA_HEADS, V_EXT),
                   hm(GQA_HEADS, GQA_DIM), hmt(GQA_KV_HEADS, GQA_DIM), hm(GQA_KV_HEADS, V_EXT)],
        out_shape=out_shape,
        compiler_params=pltpu.CompilerParams(dimension_semantics=("parallel", "parallel"),
                                             vmem_limit_bytes=VMEM_LIMIT),
        name="proj",
    )(x3, w["ln_emb_g"], w["ln_emb_b"], w["w_in"], w["w_kt"], w["g_q_a"], w["w_qb"], w["g_kv_a"], w["w_kvb_v"],
      w["w_kvb_kt"], w["g_q_gqa"], w["g_q_gqa_sw"], *tabs)


def _softmax_pv(q, kt, v, kmt, vm):
    sm = jnp.dot(q, kmt, preferred_element_type=F32)
    col = lax.broadcasted_iota(I32, sm.shape, 1)
    sm = jnp.where(col < N_META, sm, NEG_BIG)
    nblk = kt.shape[1] // KEY_BLOCK
    blocks = [jnp.dot(q, kt[:, KEY_BLOCK * c:KEY_BLOCK * (c + 1)], preferred_element_type=F32) for c in range(nblk)]
    mx = blocks[0]
    for c in range(1, nblk):
        mx = jnp.maximum(mx, blocks[c])
    m = jnp.maximum(jnp.max(mx, axis=1, keepdims=True), jnp.max(sm, axis=1, keepdims=True))
    acc = jnp.dot(jnp.exp2(sm - m).astype(BF16), vm, preferred_element_type=F32)
    for c in range(nblk):
        p = jnp.exp2(blocks[c] - m).astype(BF16)
        acc = acc + jnp.dot(p, v[KEY_BLOCK * c:KEY_BLOCK * (c + 1), :], preferred_element_type=F32)
    return acc[:, 0:V_DIM] / acc[:, V_DIM:V_EXT]


def _mla_attn_body(q_ref, k_ref, v_ref, km_ref, vm_ref, o_ref):
    for h in range(q_ref.shape[1]):
        for i in range(q_ref.shape[2] // Q_SUB_MLA):
            rows = pl.ds(i * Q_SUB_MLA, Q_SUB_MLA)
            o = _softmax_pv(q_ref[0, h, rows, :], k_ref.at[0, h], v_ref.at[0, h], km_ref[0, h], vm_ref[0, h])
            o_ref[rows, V_DIM * h:V_DIM * (h + 1)] = o.astype(o_ref.dtype)


def _gqa_attn_body(q_ref, k_ref, v_ref, km_ref, vm_ref, o_ref):
    for g in range(2):
        for i in range(q_ref.shape[2] // Q_SUB_GQA):
            rows = pl.ds(i * Q_SUB_GQA, Q_SUB_GQA)
            o = _softmax_pv(q_ref[0, g, rows, :], k_ref.at[0, 0], v_ref.at[0, 0], km_ref[0, 0], vm_ref[0, 0])
            o_ref[rows, GQA_DIM * g:GQA_DIM * (g + 1)] = o.astype(o_ref.dtype)


def _mla_attention(qm, kmt, vm, kmeta_t, vmeta):
    b, h, s, dk = qm.shape
    tq = Q_TILE_MLA
    nq = s // tq
    hs = MLA_HEADS_PER_STEP
    return pl.pallas_call(
        _mla_attn_body,
        grid=(b, h // hs, nq),
        in_specs=[
            pl.BlockSpec((1, hs, tq, dk), lambda bi, hi, qi: (bi, hi, qi, 0)),
            pl.BlockSpec((1, hs, dk, s), lambda bi, hi, qi: (bi, hi, 0, 0)),
            pl.BlockSpec((1, hs, s, V_EXT), lambda bi, hi, qi: (bi, hi, 0, 0)),
            pl.BlockSpec((1, hs, dk, META_PAD), lambda bi, hi, qi: (0, hi, 0, 0)),
            pl.BlockSpec((1, hs, META_PAD, V_EXT), lambda bi, hi, qi: (0, hi, 0, 0)),
        ],
        out_specs=pl.BlockSpec((tq, hs * V_DIM), lambda bi, hi, qi: (bi * nq + qi, hi)),
        out_shape=jax.ShapeDtypeStruct((b * s, h * V_DIM), BF16),
        compiler_params=pltpu.CompilerParams(dimension_semantics=("parallel", "parallel", "parallel"),
                                             vmem_limit_bytes=VMEM_LIMIT),
        name="mla_attn",
    )(qm, kmt, vm, kmeta_t, vmeta)


def _gqa_attention(qg, kgt, vg, kmeta_t, vmeta):
    b, h, s, d = qg.shape
    hk = kgt.shape[1]
    tq = Q_TILE_GQA
    nq = s // tq
    return pl.pallas_call(
        _gqa_attn_body,
        grid=(b, hk, nq),
        in_specs=[
            pl.BlockSpec((1, 2, tq, d), lambda bi, ji, qi: (bi, ji, qi, 0)),
            pl.BlockSpec((1, 1, d, s), lambda bi, ji, qi: (bi, ji, 0, 0)),
            pl.BlockSpec((1, 1, s, V_EXT), lambda bi, ji, qi: (bi, ji, 0, 0)),
            pl.BlockSpec((1, 1, d, META_PAD), lambda bi, ji, qi: (0, ji, 0, 0)),
            pl.BlockSpec((1, 1, META_PAD, V_EXT), lambda bi, ji, qi: (0, ji, 0, 0)),
        ],
        out_specs=pl.BlockSpec((tq, 2 * d), lambda bi, ji, qi: (bi * nq + qi, ji)),
        out_shape=jax.ShapeDtypeStruct((b * s, h * d), BF16),
        compiler_params=pltpu.CompilerParams(dimension_semantics=("parallel", "parallel", "parallel"),
                                             vmem_limit_bytes=VMEM_LIMIT),
        name="gqa_attn",
    )(qg, kgt, vg, kmeta_t, vmeta)


def _to_tiles(ref, x):
    r = x.shape[0]
    for s in range(x.shape[1] // LANES):
        ref[:, s] = x[:, LANES * s:LANES * (s + 1)].reshape(r // SUBLANES, SUBLANES, LANES)


def _from_tiles(ref):
    r = ref.shape[0] * SUBLANES
    return jnp.concatenate([ref[:, s].reshape(r, LANES) for s in range(ref.shape[1])], axis=1)


MERGE_SUB = 256


def _pack_rows(v):
    c = v.shape[1] // 2
    lo = lax.bitcast_convert_type(v[:, 0:c].astype(BF16).astype(F32), I32)
    hi = lax.bitcast_convert_type(v[:, c:2 * c].astype(BF16).astype(F32), I32)
    return lax.shift_right_logical(lo, 16) | hi


def _unpack_words(w):
    return (lax.bitcast_convert_type(w << 16, F32), lax.bitcast_convert_type(w & jnp.int32(-65536), F32))


def _merge_body(om_ref, og_ref, x_ref, lng_ref, lnb_ref, gom_ref, gog_ref, wo_ref, l1g_ref, l1b_ref,
                wr_ref, br_ref, h1_ref, h1p_ref, topi_ref, gate_ref):
    for i in range(x_ref.shape[0] // MERGE_SUB):
        rows = pl.ds(i * MERGE_SUB, MERGE_SUB)
        h1, ti, gt = _merge_chain(om_ref[rows, :], og_ref[rows, :], x_ref[rows, :], lng_ref[...], lnb_ref[...],
                                  gom_ref[...], gog_ref[...], wo_ref, l1g_ref[...], l1b_ref[...], wr_ref, br_ref[...])
        blocks = pl.ds(i * (MERGE_SUB // SUBLANES), MERGE_SUB // SUBLANES)
        _to_tiles(h1_ref.at[blocks], h1)
        _to_tiles(h1p_ref.at[blocks], _pack_rows(h1))
        topi_ref[0, :, rows] = ti
        for k in range(TOP_K):
            gate_ref[0, :, pl.ds(k * TOKEN_TILE + i * MERGE_SUB, MERGE_SUB)] = gt[k:k + 1, :]


def _merge_chain(om, og, x, lng, lnb, gom, gog, wo_ref, l1g, l1b, wr_ref, br):
    t = x.shape[0]
    h0 = _layernorm(x, lng, lnb)
    nm = _rmsnorm(om.astype(F32), gom).astype(BF16)
    ng = _rmsnorm(og.astype(F32), gog).astype(BF16)
    half = nm.shape[1]
    mix = jnp.dot(nm, wo_ref[0:half, :], preferred_element_type=F32)
    mix = mix + jnp.dot(ng, wo_ref[half:2 * half, :], preferred_element_type=F32)
    h1 = _layernorm(DEEPNORM_ALPHA * h0 + mix, l1g, l1b)

    hi = h1.astype(BF16)
    lo = (h1 - hi.astype(F32)).astype(BF16)
    acc = jnp.dot(hi, wr_ref[...], preferred_element_type=F32) + jnp.dot(lo, wr_ref[...], preferred_element_type=F32)
    logits = acc[:, 0:LANES] + acc[:, LANES:2 * LANES] + br
    cur = logits.T[0:N_EXPERTS, :]
    eidx = lax.broadcasted_iota(I32, cur.shape, 0)
    vals, idxs = [], []
    for _ in range(TOP_K):
        m = jnp.max(cur, axis=0, keepdims=True)
        i = jnp.min(jnp.where(cur == m, eidx, N_EXPERTS), axis=0, keepdims=True)
        vals.append(m)
        idxs.append(i)
        cur = jnp.where(eidx == i, -jnp.inf, cur)
    ex = [jnp.exp(v - vals[0]) for v in vals]
    den = ex[0] + ex[1] + ex[2] + ex[3]
    sub = lax.broadcasted_iota(I32, (8, t), 0)
    ti = jnp.zeros((8, t), I32)
    gt = jnp.zeros((8, t), F32)
    for k in range(TOP_K):
        ti = jnp.where(sub == k, idxs[k], ti)
        gt = jnp.where(sub == k, ex[k] / den, gt)
    return h1, ti[0:TOP_K, :], gt[0:TOP_K, :]


def _merge(o_mla, o_gqa, x2, w):
    n, d = x2.shape
    tile = TOKEN_TILE
    half = o_mla.shape[1]
    full = lambda shape: pl.BlockSpec(shape, lambda i: (0,) * len(shape))
    row = lambda width: pl.BlockSpec((tile, width), lambda i: (i, 0))
    return pl.pallas_call(
        _merge_body,
        grid=(n // tile,),
        in_specs=[row(half), row(half), row(d), full((1, d)), full((1, d)), full((1, half)), full((1, half)),
                  full((d, d)), full((1, d)), full((1, d)), full((d, 2 * LANES)), full((1, LANES))],
        out_specs=[pl.BlockSpec((tile // SUBLANES, d // LANES, SUBLANES, LANES), lambda i: (i, 0, 0, 0)),
                   pl.BlockSpec((tile // SUBLANES, PACK_CHUNKS, SUBLANES, LANES), lambda i: (i, 0, 0, 0)),
                   pl.BlockSpec((1, TOP_K, tile), lambda i: (i, 0, 0)),
                   pl.BlockSpec((1, 1, TOP_K * tile), lambda i: (i, 0, 0))],
        out_shape=[jax.ShapeDtypeStruct((n // SUBLANES, d // LANES, SUBLANES, LANES), F32),
                   jax.ShapeDtypeStruct((n // SUBLANES, PACK_CHUNKS, SUBLANES, LANES), I32),
                   jax.ShapeDtypeStruct((n // tile, TOP_K, tile), I32),
                   jax.ShapeDtypeStruct((n // tile, 1, TOP_K * tile), F32)],
        compiler_params=pltpu.CompilerParams(dimension_semantics=("parallel",), vmem_limit_bytes=VMEM_LIMIT),
        name="merge",
    )(o_mla, o_gqa, x2, w["ln_emb_g"], w["ln_emb_b"], w["g_o_mla"], w["g_o_gqa"], w["w_o"],
      w["ln1_g"], w["ln1_b"], w["w_router"], w["b_router"])


def _lanes_from_sublanes(col):
    diag = lax.broadcasted_iota(I32, col.shape, 0) == lax.broadcasted_iota(I32, col.shape, 1)
    return jnp.sum(jnp.where(diag, col, 0.0), axis=0, keepdims=True)


def _positions_body(topi_ref, upper_ref, lpb_ref, gstart_ref, nslab_ref, lstart_ref, texp_ref, nvalid_ref, ecnt_ref,
                    eoff_ref, *, ntp):
    ntile, _, tl = topi_ref.shape
    r = lax.broadcasted_iota(I32, (N_EXPERTS, N_EXPERTS), 0)
    c = lax.broadcasted_iota(I32, (N_EXPERTS, N_EXPERTS), 1)
    lower = (c < r).astype(F32)
    eidx = lax.broadcasted_iota(I32, (N_EXPERTS, tl), 0)

    def tile_onehots(j):
        topi = topi_ref[j]
        ohs = [eidx == topi[k:k + 1, :] for k in range(TOP_K)]
        onehot = ohs[0].astype(F32) + ohs[1].astype(F32) + ohs[2].astype(F32) + ohs[3].astype(F32)
        cnt = jnp.sum(onehot, axis=1, keepdims=True)
        run = jnp.floor((cnt + (SUBLANES - 1)) * (1.0 / SUBLANES)) * SUBLANES
        return ohs, onehot, jnp.broadcast_to(run, (N_EXPERTS, LANES))

    tot = lax.fori_loop(0, ntile, lambda j, acc: acc + tile_onehots(j)[2], jnp.zeros((N_EXPERTS, LANES), F32))
    pc = jnp.floor((tot + (MOE_TILE - 1)) * (1.0 / MOE_TILE)) * MOE_TILE
    off = jnp.dot(lower, pc, precision=lax.Precision.HIGHEST, preferred_element_type=F32)
    cumend = off + pc
    tstart = lax.broadcasted_iota(I32, (N_EXPERTS, ntp), 1).astype(F32) * MOE_TILE
    te = jnp.sum((jnp.broadcast_to(cumend[:, 0:1], (N_EXPERTS, ntp)) <= tstart).astype(I32), axis=0, keepdims=True)
    texp_ref[...] = jnp.minimum(te, N_EXPERTS - 1)
    nvalid_ref[...] = (cumend[N_EXPERTS - 1:N_EXPERTS, :] * (1.0 / MOE_TILE)).astype(I32)
    ecnt_ref[...] = _lanes_from_sublanes(tot).astype(I32)
    eoff_ref[...] = _lanes_from_sublanes(off).astype(I32)

    def place(j, carry):
        ohs, onehot, tile_run = tile_onehots(j)
        before = jnp.dot(onehot.astype(BF16), upper_ref[...], preferred_element_type=F32)
        loff = jnp.dot(lower, tile_run, precision=lax.Precision.HIGHEST, preferred_element_type=F32)
        base = before + loff[:, 0:1]
        sub = lax.broadcasted_iota(I32, (SUBLANES, tl), 0)
        out = jnp.zeros((SUBLANES, tl), F32)
        for k in range(TOP_K):
            pk = jnp.sum(jnp.where(ohs[k], base, 0.0), axis=0, keepdims=True)
            out = jnp.where(sub == k, pk, out)
        lp = out[0:TOP_K, :].astype(I32)
        lpb = (lp >> SUBLANE_SHIFT) * PSLAB + (lp & (SUBLANES - 1))
        lpb_ref[j] = jnp.concatenate([lpb[k:k + 1, :] for k in range(TOP_K)], axis=1)
        inv = 1.0 / SUBLANES
        gstart_ref[j] = (_lanes_from_sublanes(off + carry) * inv).astype(I32)
        nslab_ref[j] = (_lanes_from_sublanes(tile_run) * inv).astype(I32)
        lstart_ref[j] = (_lanes_from_sublanes(loff) * inv).astype(I32)
        return carry + tile_run

    lax.fori_loop(0, ntile, place, jnp.zeros((N_EXPERTS, LANES), F32))


def _positions(topi, ntp):
    ntile, _, tl = topi.shape
    tab = jax.ShapeDtypeStruct((ntile, 1, LANES), I32)
    small = jax.ShapeDtypeStruct((1, LANES), I32)
    upper = (jnp.arange(tl, dtype=I32)[:, None] < jnp.arange(tl, dtype=I32)[None, :]).astype(BF16)
    return pl.pallas_call(
        functools.partial(_positions_body, ntp=ntp),
        out_shape=[jax.ShapeDtypeStruct((ntile, 1, TOP_K * tl), I32), tab, tab, tab,
                   jax.ShapeDtypeStruct((1, ntp), I32), small, small, small],
        compiler_params=pltpu.CompilerParams(vmem_limit_bytes=VMEM_LIMIT),
        name="positions",
    )(topi, upper)


STAGE_ROWS = TOKEN_TILE * TOP_K + N_EXPERTS * SUBLANES
STAGE_FLAT = STAGE_ROWS * PACK_CHUNKS


def _load_row(ref, flat_start, chunks=SUBLANES):
    return ref[pl.ds(flat_start, chunks, stride=SUBLANES), :]


def _store_row(ref, flat_start, v):
    ref[pl.ds(flat_start, v.shape[0], stride=SUBLANES), :] = v


CHUNK_SHIFT = 2
CHUNK_SLABS = 1 << CHUNK_SHIFT


def _copy_run(src_ref, src0, dst_ref, dst0, nslab, sem):
    big = CHUNK_SLABS * PSLAB
    nbig = nslab >> CHUNK_SHIFT
    nsmall = nslab & (CHUNK_SLABS - 1)

    def start_big(s, c):
        pltpu.make_async_copy(src_ref.at[pl.ds(src0 + s * big, big)], dst_ref.at[pl.ds(dst0 + s * big, big)], sem).start()
        return c

    def start_small(s, c):
        o = nbig * big + s * PSLAB
        pltpu.make_async_copy(src_ref.at[pl.ds(src0 + o, PSLAB)], dst_ref.at[pl.ds(dst0 + o, PSLAB)], sem).start()
        return c

    lax.fori_loop(0, nbig, start_big, 0)
    lax.fori_loop(0, nsmall, start_small, 0)


WAIT_SHIFT = 4


def _wait_slabs(src_ref, dst_ref, nslab, sem):
    big = (1 << WAIT_SHIFT) * PSLAB

    def wait_big(s, c):
        pltpu.make_async_copy(src_ref.at[pl.ds(0, big)], dst_ref.at[pl.ds(0, big)], sem).wait()
        return c

    def wait_small(s, c):
        pltpu.make_async_copy(src_ref.at[pl.ds(0, PSLAB)], dst_ref.at[pl.ds(0, PSLAB)], sem).wait()
        return c

    lax.fori_loop(0, nslab >> WAIT_SHIFT, wait_big, 0)
    lax.fori_loop(0, nslab & ((1 << WAIT_SHIFT) - 1), wait_small, 0)


def _total_slabs(nslab_ref):
    return lax.fori_loop(0, N_EXPERTS, lambda e, c: c + nslab_ref[0, 0, e], jnp.int32(0))


def _dispatch_body(ecnt_ref, eoff_ref, lpb_ref, gstart_ref, nslab_ref, lstart_ref, h_ref, xs_ref,
                   stg0_ref, stg1_ref, zero_ref, issued_ref, sem, zsem):
    g = pl.program_id(0)
    ng = pl.num_programs(0)
    stages = (stg0_ref, stg1_ref)

    def drain(j):
        _wait_slabs(stages[j], xs_ref, issued_ref[j], sem.at[j])

    def run(j):
        stg_ref = stages[j]

        @pl.when(g >= 2)
        def _():
            drain(j)

        def zero_last(e, c):
            last = jnp.maximum(lstart_ref[0, 0, e] + nslab_ref[0, 0, e] - 1, 0)
            stg_ref[pl.ds(last * PSLAB, PSLAB), :] = jnp.zeros((PSLAB, LANES), I32)
            return c

        lax.fori_loop(0, N_EXPERTS, zero_last, 0)

        def move(i, c):
            for u in range(SUBLANES):
                v = _load_row(h_ref, i * PSLAB + u, PACK_CHUNKS)
                for k in range(TOP_K):
                    _store_row(stg_ref, lpb_ref[0, 0, k * TOKEN_TILE + i * SUBLANES + u], v)
            return c

        lax.fori_loop(0, TOKEN_TILE // SUBLANES, move, 0)

        def send_run(e, c):
            _copy_run(stg_ref, lstart_ref[0, 0, e] * PSLAB, xs_ref, gstart_ref[0, 0, e] * PSLAB,
                      nslab_ref[0, 0, e], sem.at[j])
            return c

        lax.fori_loop(0, N_EXPERTS, send_run, 0)
        issued_ref[j] = _total_slabs(nslab_ref)

    for j in range(2):
        @pl.when(lax.rem(g, 2) == j)
        def _():
            run(j)

    @pl.when(g == ng - 1)
    def _():
        zero_ref[...] = jnp.zeros_like(zero_ref)

        def zero_copy(dst_slab):
            return pltpu.make_async_copy(zero_ref, xs_ref.at[pl.ds(dst_slab * PSLAB, PSLAB)], zsem)

        def pad_segment(e, c):
            rows = ecnt_ref[e]
            first = (eoff_ref[e] + rows) >> SUBLANE_SHIFT
            npad = lax.rem(MOE_TILE - lax.rem(rows, MOE_TILE), MOE_TILE) >> SUBLANE_SHIFT
            lax.fori_loop(0, npad, lambda s, cc: (zero_copy(first + s).start(), cc)[1], 0)
            lax.fori_loop(0, npad, lambda s, cc: (zero_copy(first + s).wait(), cc)[1], 0)
            return c

        lax.fori_loop(0, N_EXPERTS, pad_segment, 0)
        for j in range(2):
            @pl.when(jnp.logical_and(ng > 1, lax.rem(g, 2) != j))
            def _():
                drain(j)

        for j in range(2):
            @pl.when(lax.rem(g, 2) == j)
            def _():
                drain(j)


def _dispatch(h1_flat, lpb, gstart, nslab, lstart, ecnt, eoff, rows_pad):
    nflat = h1_flat.shape[0]
    t = TOKEN_TILE
    smem = lambda shape, imap: pl.BlockSpec(shape, imap, memory_space=pltpu.SMEM)
    per_tile = smem((1, 1, LANES), lambda i, c, o: (i, 0, 0))
    grid_spec = pltpu.PrefetchScalarGridSpec(
        num_scalar_prefetch=2,
        grid=(nflat // (t * PACK_CHUNKS),),
        in_specs=[smem((1, 1, TOP_K * t), lambda i, c, o: (i, 0, 0)), per_tile, per_tile, per_tile,
                  pl.BlockSpec((t * PACK_CHUNKS, LANES), lambda i, c, o: (i, 0))],
        out_specs=pl.BlockSpec(memory_space=pl.ANY),
        scratch_shapes=[pltpu.VMEM((STAGE_FLAT, LANES), I32), pltpu.VMEM((STAGE_FLAT, LANES), I32),
                        pltpu.VMEM((PSLAB, LANES), I32),
                        pltpu.SMEM((2,), I32), pltpu.SemaphoreType.DMA((2,)), pltpu.SemaphoreType.DMA],
    )
    return pl.pallas_call(
        _dispatch_body,
        grid_spec=grid_spec,
        out_shape=jax.ShapeDtypeStruct((rows_pad * PACK_CHUNKS, LANES), I32),
        compiler_params=pltpu.CompilerParams(dimension_semantics=("arbitrary",), vmem_limit_bytes=VMEM_LIMIT),
        name="dispatch",
    )(ecnt, eoff, lpb, gstart, nslab, lstart, h1_flat)


def _ffn_body(texp_ref, nvalid_ref, ecnt_ref, eoff_ref, x_ref, wgu_hbm, wd_hbm, bg_ref, bl_ref, bd_ref, y_ref,
              wgu_buf, wd_buf, wg_sc, wl_sc, wd_sc, seg_ref, sem):
    i = pl.program_id(0)
    valid = i < nvalid_ref[0]
    e = texp_ref[i]
    first_of_expert = jnp.logical_or(i == 0, e != texp_ref[jnp.maximum(i - 1, 0)])

    def weight_copies(expert, slot):
        return (pltpu.make_async_copy(wgu_hbm.at[expert], wgu_buf.at[slot], sem.at[slot]),
                pltpu.make_async_copy(wd_hbm.at[expert], wd_buf.at[slot], sem.at[slot]))

    def prepare(slot):
        blk = 2 * LANES
        r = lax.broadcasted_iota(I32, (blk, blk), 0)
        c = lax.broadcasted_iota(I32, (blk, blk), 1)
        sel = (r == jnp.where(c < LANES, 2 * c, 2 * (c - LANES) + 1)).astype(BF16)
        for b in range(wgu_buf.shape[2] // blk):
            wb = wgu_buf[slot, :, blk * b:blk * (b + 1)].astype(BF16)
            y = jnp.dot(wb, sel, preferred_element_type=F32).astype(BF16)
            wg_sc[:, LANES * b:LANES * (b + 1)] = y[:, 0:LANES]
            wl_sc[:, LANES * b:LANES * (b + 1)] = y[:, LANES:blk]
        wd_sc[...] = wd_buf[slot].astype(BF16)

    @pl.when(jnp.logical_and(valid, first_of_expert))
    def _():
        @pl.when(i == 0)
        def _():
            seg_ref[0] = 0
            for cp in weight_copies(e, 0):
                cp.start()

        seg = seg_ref[0]
        seg_ref[0] = seg + 1
        nxt = lax.while_loop(lambda n: jnp.logical_and(n < N_EXPERTS, ecnt_ref[jnp.minimum(n, N_EXPERTS - 1)] == 0),
                             lambda n: n + 1, e + 1)
        for slot in range(2):
            @pl.when(lax.rem(seg, 2) == slot)
            def _():
                for cp in weight_copies(e, slot):
                    cp.wait()

                @pl.when(nxt < N_EXPERTS)
                def _():
                    for cp in weight_copies(nxt, 1 - slot):
                        cp.start()

                prepare(slot)

    def chain(b0, nb):
        blocks = pl.ds(b0, nb)
        x = jnp.concatenate(_unpack_words(_from_tiles(x_ref.at[blocks])), axis=1).astype(BF16)
        hg = jnp.dot(x, wg_sc[...], preferred_element_type=F32) + bg_ref[0]
        hl = jnp.dot(x, wl_sc[...], preferred_element_type=F32) + bl_ref[0]
        g = jnp.minimum(hg, SWIGLU_LIMIT)
        lin = jnp.clip(hl, -SWIGLU_LIMIT, SWIGLU_LIMIT)
        act = g * (1.0 / (1.0 + jnp.exp(-SWIGLU_ALPHA * g))) * (lin + 1.0)
        y = jnp.dot(act.astype(BF16), wd_sc[...], preferred_element_type=F32) + bd_ref[0]
        _to_tiles(y_ref.at[blocks], _pack_rows(y))

    def compute(nrows):
        done = 0
        while done < nrows:
            n = min(FFN_CHAIN, nrows - done)
            chain(done // SUBLANES, n // SUBLANES)
            done += n
        rest = y_ref.shape[0] - nrows // SUBLANES
        if rest:
            y_ref[pl.ds(nrows // SUBLANES, rest)] = jnp.zeros((rest,) + y_ref.shape[1:], I32)

    used = eoff_ref[e] + ecnt_ref[e] - i * MOE_TILE
    quarter = MOE_TILE // 4
    for q in range(1, 5):
        lo, hi = (q - 1) * quarter, q * quarter
        in_range = jnp.logical_and(used > lo, used <= hi) if q < 4 else used > lo

        @pl.when(jnp.logical_and(valid, in_range))
        def _():
            compute(hi)


def _grouped_ffn(xs_tiles, texp, nvalid, ecnt, eoff, wgu, wd, bg, bl, bd, ntiles):
    d = wgu.shape[1]
    tm = MOE_TILE
    f = wd.shape[1]
    xmap = lambda i, te, nv, ec, eo: (jnp.minimum(i, nv[0] - 1), 0, 0, 0)
    wmap = lambda i, te, nv, ec, eo: (te[i], 0, 0)
    rows_blk = pl.BlockSpec((tm // SUBLANES, PACK_CHUNKS, SUBLANES, LANES), xmap)
    hbm = pl.BlockSpec(memory_space=pl.ANY)
    grid_spec = pltpu.PrefetchScalarGridSpec(
        num_scalar_prefetch=4,
        grid=(ntiles,),
        in_specs=[rows_blk, hbm, hbm,
                  pl.BlockSpec((1, 1, f), wmap), pl.BlockSpec((1, 1, f), wmap), pl.BlockSpec((1, 1, d), wmap)],
        out_specs=rows_blk,
        scratch_shapes=[pltpu.VMEM((2, d, 2 * f), F32), pltpu.VMEM((2, f, d), F32),
                        pltpu.VMEM((d, f), BF16), pltpu.VMEM((d, f), BF16), pltpu.VMEM((f, d), BF16),
                        pltpu.SMEM((1,), I32), pltpu.SemaphoreType.DMA((2,))],
    )
    return pl.pallas_call(
        _ffn_body,
        grid_spec=grid_spec,
        out_shape=jax.ShapeDtypeStruct(xs_tiles.shape, I32),
        compiler_params=pltpu.CompilerParams(dimension_semantics=("arbitrary",), vmem_limit_bytes=VMEM_LIMIT),
        name="ffn",
    )(texp, nvalid, ecnt, eoff, xs_tiles, wgu, wd, bg, bl, bd)


COMBINE_SUB = 512


def _combine_body(lpb_ref, gate_ref, gstart_ref, nslab_ref, lstart_ref, gstart2_ref, nslab2_ref, lstart2_ref,
                  h_ref, l2g_ref, l2b_ref, ys_ref, o_ref, stg0_ref, stg1_ref, moe_ref, sem):
    g = pl.program_id(0)
    ng = pl.num_programs(0)
    stages = (stg0_ref, stg1_ref)

    def fetch(gs_ref, ns_ref, ls_ref, j):
        def per_expert(e, c):
            _copy_run(ys_ref, gs_ref[0, 0, e] * PSLAB, stages[j], ls_ref[0, 0, e] * PSLAB, ns_ref[0, 0, e], sem.at[j])
            return c

        lax.fori_loop(0, N_EXPERTS, per_expert, 0)

    def wait_tile(j):
        _wait_slabs(ys_ref, stages[j], _total_slabs(nslab_ref), sem.at[j])

    def gather(j):
        stg_ref = stages[j]

        def body(i, c):
            for u in range(SUBLANES):
                tok = i * SUBLANES + u
                acc_lo = acc_hi = None
                for k in range(TOP_K):
                    gk = gate_ref[0, 0, k * TOKEN_TILE + tok]
                    lo, hi = _unpack_words(_load_row(stg_ref, lpb_ref[0, 0, k * TOKEN_TILE + tok], PACK_CHUNKS))
                    acc_lo = gk * lo if k == 0 else acc_lo + gk * lo
                    acc_hi = gk * hi if k == 0 else acc_hi + gk * hi
                _store_row(moe_ref, i * SLAB + u, acc_lo)
                _store_row(moe_ref, i * SLAB + PSLAB + u, acc_hi)
            return c

        lax.fori_loop(0, TOKEN_TILE // SUBLANES, body, 0)

    def finish():
        nblk = COMBINE_SUB // SUBLANES
        for part in range(TOKEN_TILE // COMBINE_SUB):
            moe = jnp.concatenate(
                [jnp.concatenate([moe_ref[pl.ds((part * nblk + i) * SLAB + s * SUBLANES, SUBLANES), :]
                                  for i in range(nblk)], axis=0) for s in range(SUBLANES)], axis=1)
            h1 = _from_tiles(h_ref.at[pl.ds(part * nblk, nblk)])
            o_ref[pl.ds(part * COMBINE_SUB, COMBINE_SUB), :] = _layernorm(DEEPNORM_ALPHA * h1 + moe, l2g_ref[...],
                                                                          l2b_ref[...])

    @pl.when(g == 0)
    def _():
        fetch(gstart_ref, nslab_ref, lstart_ref, 0)

    for j in range(2):
        @pl.when(lax.rem(g, 2) == j)
        def _():
            @pl.when(g + 1 < ng)
            def _():
                fetch(gstart2_ref, nslab2_ref, lstart2_ref, 1 - j)

            wait_tile(j)
            gather(j)

    finish()


def _combine(lpb, gates, gstart, nslab, lstart, h1_tiles, ys_flat, w):
    d = h1_tiles.shape[1] * LANES
    n = h1_tiles.shape[0] * SUBLANES
    t = TOKEN_TILE
    ng = n // t
    full = lambda shape: pl.BlockSpec(shape, lambda i: (0,) * len(shape))
    smem = lambda shape, imap: pl.BlockSpec(shape, imap, memory_space=pltpu.SMEM)
    vec = smem((1, 1, TOP_K * t), lambda i: (i, 0, 0))
    this_step = smem((1, 1, LANES), lambda i: (i, 0, 0))
    next_step = smem((1, 1, LANES), lambda i: (jnp.minimum(i + 1, ng - 1), 0, 0))
    return pl.pallas_call(
        _combine_body,
        grid=(ng,),
        in_specs=[vec, vec, this_step, this_step, this_step, next_step, next_step, next_step,
                  pl.BlockSpec((t // SUBLANES, d // LANES, SUBLANES, LANES), lambda i: (i, 0, 0, 0)),
                  full((1, d)), full((1, d)),
                  pl.BlockSpec(memory_space=pl.ANY)],
        out_specs=pl.BlockSpec((t, d), lambda i: (i, 0)),
        out_shape=jax.ShapeDtypeStruct((n, d), F32),
        scratch_shapes=[pltpu.VMEM((STAGE_FLAT, LANES), I32), pltpu.VMEM((STAGE_FLAT, LANES), I32),
                        pltpu.VMEM((TOKEN_TILE * SUBLANES, LANES), F32), pltpu.SemaphoreType.DMA((2,))],
        compiler_params=pltpu.CompilerParams(dimension_semantics=("arbitrary",), vmem_limit_bytes=VMEM_LIMIT),
        name="combine",
    )(lpb, gates, gstart, nslab, lstart, gstart, nslab, lstart, h1_tiles, w["ln2_g"], w["ln2_b"], ys_flat)


def _rope_tables(pos_1d, row, col, g_k):
    inv = ROPE_THETA ** (-jnp.arange(0, ROPE_DIM, 2, dtype=F32) / ROPE_DIM)

    def cs(p):
        ang = p.astype(F32)[:, None] * inv[None, :]
        return jnp.cos(ang), jnp.sin(ang)

    c1, s1 = cs(pos_1d)
    cr, sr = cs(row)
    cc, sc = cs(col)
    c1w, s1w = jnp.concatenate([c1, c1, c1, c1], axis=1), jnp.concatenate([-s1, s1, -s1, s1], axis=1)
    caw, saw = jnp.concatenate([cr, cr, cc, cc], axis=1), jnp.concatenate([-sr, sr, -sc, sc], axis=1)
    ckt = caw.T * g_k[:, None]
    skt = saw.T * _swap_halves64(g_k)[:, None]
    return (c1w, s1w, caw, saw, c1w[:, :ROPE_DIM].T, s1w[:, :ROPE_DIM].T, ckt, skt)


def _prep_weights(ln_emb_g, ln_emb_b, w_in, g_q_a, w_q_b, g_kv_a, w_kv_b, g_q_gqa, g_k_gqa, g_o_mla, g_o_gqa, w_o,
                  ln1_g, ln1_b, w_router, b_router, ln2_g, ln2_b):
    r2 = lambda v: v.reshape(1, -1).astype(F32)
    o = np.cumsum([0, Q_LORA, KV_LORA, ROPE_DIM, GQA_HEADS * GQA_DIM, GQA_KV_HEADS * GQA_DIM, GQA_KV_HEADS * GQA_DIM])
    wi = w_in[0]
    seg = [wi[:, o[i]:o[i + 1]] for i in range(6)]
    w_in_p = jnp.concatenate([seg[0], seg[1], seg[3], _swap_halves64(seg[3]), seg[5]], axis=1).astype(BF16)
    w_kt = jnp.concatenate([seg[2], seg[4]], axis=1).T.astype(BF16)
    wq = w_q_b[0].reshape(Q_LORA, MLA_HEADS, QK_DIM)
    wq_rope = wq[:, :, NOPE_DIM:].reshape(Q_LORA, -1)
    w_qb = jnp.concatenate([wq[:, :, :NOPE_DIM].reshape(Q_LORA, -1), wq_rope, _swap_halves64(wq_rope)],
                           axis=1).astype(BF16)
    wk = w_kv_b[0].reshape(KV_LORA, MLA_HEADS, NOPE_DIM + V_DIM)
    w_kvb_kt = wk[:, :, :NOPE_DIM].reshape(KV_LORA, -1).T.astype(BF16)
    w_kvb_v = wk[:, :, NOPE_DIM:].reshape(KV_LORA, -1).astype(BF16)
    wr = jnp.pad(w_router[0].astype(F32), ((0, 0), (0, LANES - N_EXPERTS)))
    wr_hi = wr.astype(BF16)
    w_router_p = jnp.concatenate([wr_hi, (wr - wr_hi.astype(F32)).astype(BF16)], axis=1)
    b_router_p = jnp.pad(b_router[0].astype(F32), (0, LANES - N_EXPERTS), constant_values=NEG_BIG).reshape(1, LANES)
    return dict(
        ln_emb_g=r2(ln_emb_g), ln_emb_b=r2(ln_emb_b), w_in=w_in_p, w_kt=w_kt, g_q_a=r2(g_q_a[0]), w_qb=w_qb,
        g_kv_a=r2(g_kv_a[0]), w_kvb_v=w_kvb_v, w_kvb_kt=w_kvb_kt, g_q_gqa=r2(g_q_gqa[0]),
        g_q_gqa_sw=_swap_halves64(r2(g_q_gqa[0])),
        g_o_mla=r2(g_o_mla[0]), g_o_gqa=r2(g_o_gqa[0]), w_o=w_o[0].astype(BF16), ln1_g=r2(ln1_g[0]),
        ln1_b=r2(ln1_b[0]), w_router=w_router_p, b_router=b_router_p, ln2_g=r2(ln2_g[0]), ln2_b=r2(ln2_b[0]))


def kernel(x, meta_tokens, ln_emb_g, ln_emb_b, w_in, g_q_a, w_q_b, g_kv_a, w_kv_b, g_q_gqa, g_k_gqa, g_o_mla, g_o_gqa,
           w_o, ln1_g, ln1_b, w_router, b_router, w_gate_up, b_gate_up, w_down, b_down, ln2_g, ln2_b):
    b, s, d = x.shape
    n = b * s
    assert d == D_MODEL and meta_tokens.shape == (N_META, d) and s % GRID_W == 0
    assert s % ROW_TILE == 0 and s % Q_TILE_MLA == 0 and s % Q_TILE_GQA == 0 and s % KEY_BLOCK == 0
    assert n % TOKEN_TILE == 0 and TOKEN_TILE % MERGE_SUB == 0 and TOKEN_TILE % COMBINE_SUB == 0
    w = _prep_weights(ln_emb_g, ln_emb_b, w_in, g_q_a, w_q_b, g_kv_a, w_kv_b, g_q_gqa, g_k_gqa, g_o_mla, g_o_gqa,
                      w_o, ln1_g, ln1_b, w_router, b_router, ln2_g, ln2_b)

    tok = jnp.arange(s, dtype=I32)
    g_k = g_k_gqa[0].astype(F32)
    tabs_real = _rope_tables(tok + N_META, tok // GRID_W, tok % GRID_W, g_k)
    mt = jnp.arange(META_PAD, dtype=I32)
    tabs_meta = _rope_tables(mt, jnp.full((META_PAD,), -1, I32), mt, g_k)

    qm, km, vm, qg, kg, vg = _project(x, tabs_real, w, ROW_TILE)
    meta = jnp.pad(meta_tokens.astype(x.dtype), ((0, META_PAD - N_META), (0, 0))).reshape(1, META_PAD, d)
    _, km_m, vm_m, _, kg_m, vg_m = _project(meta, tabs_meta, w, META_PAD)

    o_mla = _mla_attention(qm, km, vm, km_m, vm_m)
    o_gqa = _gqa_attention(qg, kg, vg, kg_m, vg_m)

    h1_tiles, h1_packed, topi, gates = _merge(o_mla, o_gqa, x.reshape(n, d), w)

    run_rows = n * TOP_K + (n // TOKEN_TILE) * N_EXPERTS * (SUBLANES - 1)
    ntiles = -(-run_rows // MOE_TILE) + N_EXPERTS
    ntp = -(-ntiles // LANES) * LANES
    rows_pad = ntiles * MOE_TILE
    lpb, gstart, nslab, lstart, texp, nvalid, ecnt, eoff = _positions(topi, ntp)
    flat = lambda a: a.reshape(-1, LANES)
    xs_flat = _dispatch(flat(h1_packed), lpb, gstart, nslab, lstart, ecnt.reshape(-1), eoff.reshape(-1), rows_pad)

    bgu = b_gate_up[0].astype(F32).reshape(N_EXPERTS, D_FF, 2).transpose(2, 0, 1).reshape(2, N_EXPERTS, 1, D_FF)
    tiles = lambda a: a.reshape(-1, PACK_CHUNKS, SUBLANES, LANES)
    ys_tiles = _grouped_ffn(tiles(xs_flat), texp.reshape(-1), nvalid.reshape(-1), ecnt.reshape(-1), eoff.reshape(-1),
                            w_gate_up[0], w_down[0], bgu[0], bgu[1],
                            b_down[0].astype(F32).reshape(N_EXPERTS, 1, d), ntiles)

    out = _combine(lpb, gates, gstart, nslab, lstart, h1_tiles, flat(ys_tiles), w)
    return out.reshape(b, s, d)
```

```python
import functools

import jax
import jax.numpy as jnp
import numpy as np
from jax import lax
from jax.experimental import pallas as pl
from jax.experimental.pallas import tpu as pltpu

D_MODEL = 1024
N_META = 16
GRID_W = 64
ROPE_THETA = 10000.0
MLA_HEADS = 4
Q_LORA = 256
KV_LORA = 128
NOPE_DIM = 128
ROPE_DIM = 64
V_DIM = 128
QK_DIM = NOPE_DIM + ROPE_DIM
GQA_HEADS = 4
GQA_KV_HEADS = 2
GQA_DIM = 128
N_EXPERTS = 32
TOP_K = 4
D_FF = D_MODEL
SWIGLU_LIMIT = 7.0
SWIGLU_ALPHA = 1.702
RMS_EPS = 1e-6
LN_EPS = 1e-5
DEPTH = 1
DEEPNORM_ALPHA = (2.0 * DEPTH) ** 0.25

LANES = 128
SUBLANE_SHIFT = 3
SUBLANES = 1 << SUBLANE_SHIFT
SLAB = SUBLANES * SUBLANES
PACK_CHUNKS = D_MODEL // 2 // LANES
PSLAB = SUBLANES * PACK_CHUNKS
META_PAD = 128
MLA_K = 2 * LANES
NEG_BIG = -1e30
LOG2E = 1.4426950408889634
V_EXT = 2 * LANES

ROW_TILE = 1024
Q_TILE_MLA = 2048
Q_TILE_GQA = 2048
MLA_HEADS_PER_STEP = 2
Q_SUB_MLA = 128
Q_SUB_GQA = 512
KEY_BLOCK = 512
TOKEN_TILE = 1024
MOE_TILE = 1024
FFN_CHAIN = 512
VMEM_LIMIT = 56 * 1024 * 1024

F32 = jnp.float32
BF16 = jnp.bfloat16
I32 = jnp.int32


def _layernorm(x, g, b):
    mu = jnp.mean(x, axis=-1, keepdims=True)
    xc = x - mu
    var = jnp.mean(xc * xc, axis=-1, keepdims=True)
    return xc * lax.rsqrt(var + LN_EPS) * g + b


def _rmsnorm(x, g):
    return x * lax.rsqrt(jnp.mean(x * x, axis=-1, keepdims=True) + RMS_EPS) * g


def _swap_halves64(v):
    shp = v.shape
    return v.reshape(shp[:-1] + (shp[-1] // 64, 2, 32))[..., ::-1, :].reshape(shp)


PROJ_SUB = 256


def _swap_blocks32(xt):
    return jnp.concatenate([xt[o + b:o + b + 32] for o in range(0, xt.shape[0], 64) for b in (32, 0)], axis=0)


def _proj_chain(x, lng, lnb, win_ref, wkt_ref, gqa, wqb_ref, gkva, wkvbv_ref, wkvbkt_ref, gqg, gqgs,
                c1, s1, ca, sa, c1t, s1t, ckt, skt, put):
    r = x.shape[0]
    nt = (((1,), (1,)), ((), ()))
    h0b = _layernorm(x, lng, lnb).astype(BF16)
    z = jnp.dot(h0b, win_ref[...], preferred_element_type=F32)
    q_a = z[:, 0:256]
    kv_a = z[:, 256:384]
    q_g = z[:, 384:896]
    q_gs = z[:, 896:1408]
    v_g = z[:, 1408:1664]
    kt = lax.dot_general(wkt_ref[...], h0b, nt, preferred_element_type=F32)

    q = jnp.dot(_rmsnorm(q_a, gqa).astype(BF16), wqb_ref[...], preferred_element_type=F32)
    kvn = _rmsnorm(kv_a, gkva).astype(BF16)
    v_a = jnp.dot(kvn, wkvbv_ref[...], preferred_element_type=F32)
    knt = lax.dot_general(wkvbkt_ref[...], kvn, nt, preferred_element_type=F32)
    kpet = kt[0:ROPE_DIM]
    krot = kpet * c1t + _swap_blocks32(kpet) * s1t
    lane = lax.broadcasted_iota(I32, (r, LANES), 1)
    scale_a = QK_DIM ** -0.5 * LOG2E
    ones = jnp.ones((r, LANES), F32)
    for c in range(MLA_HEADS // 2):
        lo = LANES * c
        qr = q[:, 512 + lo:512 + lo + LANES] * c1 + q[:, 768 + lo:768 + lo + LANES] * s1
        for hh in range(2):
            h = 2 * c + hh
            slot = jnp.where((lane // 64) == hh, qr, 0.0)
            put("qm", h, (jnp.concatenate([q[:, LANES * h:LANES * (h + 1)], slot], axis=1) * scale_a).astype(BF16))
    for h in range(MLA_HEADS):
        put("km", h, jnp.concatenate([knt[LANES * h:LANES * (h + 1)], krot, krot], axis=0).astype(BF16))
        put("vm", h, jnp.concatenate([v_a[:, LANES * h:LANES * (h + 1)], ones], axis=1).astype(BF16))

    scale_b = GQA_DIM ** -0.5 * LOG2E
    cq, sq = ca * gqg, sa * gqgs
    for h in range(GQA_HEADS):
        sl = slice(LANES * h, LANES * (h + 1))
        xh = q_g[:, sl]
        inv = lax.rsqrt(jnp.mean(xh * xh, axis=-1, keepdims=True) + RMS_EPS)
        put("qg", h, ((xh * cq + q_gs[:, sl] * sq) * (inv * scale_b)).astype(BF16))
    for j in range(GQA_KV_HEADS):
        xt = kt[ROPE_DIM + LANES * j:ROPE_DIM + LANES * (j + 1)]
        inv = lax.rsqrt(jnp.mean(xt * xt, axis=0, keepdims=True) + RMS_EPS)
        put("kg", j, ((xt * ckt + _swap_blocks32(xt) * skt) * inv).astype(BF16))
        put("vg", j, jnp.concatenate([v_g[:, LANES * j:LANES * (j + 1)], ones], axis=1).astype(BF16))


def _proj_body(x_ref, lng_ref, lnb_ref, win_ref, wkt_ref, gqa_ref, wqb_ref, gkva_ref, wkvbv_ref, wkvbkt_ref,
               gqg_ref, gqgs_ref, c1_ref, s1_ref, ca_ref, sa_ref, c1t_ref, s1t_ref, ckt_ref, skt_ref,
               qm_ref, km_ref, vm_ref, qg_ref, kg_ref, vg_ref):
    outs = dict(qm=qm_ref, km=km_ref, vm=vm_ref, qg=qg_ref, kg=kg_ref, vg=vg_ref)
    tile = x_ref.shape[1]
    sub = min(PROJ_SUB, tile)
    for i in range(tile // sub):
        rows = pl.ds(i * sub, sub)

        def put(name, h, val):
            if name in ("km", "kg"):
                outs[name][0, h, :, rows] = val
            else:
                outs[name][0, h, rows, :] = val

        _proj_chain(x_ref[0, rows, :], lng_ref[...], lnb_ref[...], win_ref, wkt_ref, gqa_ref[...], wqb_ref,
                    gkva_ref[...], wkvbv_ref, wkvbkt_ref, gqg_ref[...], gqgs_ref[...],
                    c1_ref[rows, :], s1_ref[rows, :], ca_ref[rows, :], sa_ref[rows, :],
                    c1t_ref[:, rows], s1t_ref[:, rows], ckt_ref[:, rows], skt_ref[:, rows], put)


def _project(x3, tabs, w, tile):
    b, s, d = x3.shape
    nst = s // tile
    full = lambda shape: pl.BlockSpec(shape, lambda bi, si: (0,) * len(shape))
    tab = pl.BlockSpec((tile, LANES), lambda bi, si: (si, 0))
    tabt = lambda dims: pl.BlockSpec((dims, tile), lambda bi, si: (0, si))
    hm = lambda nh, dd: pl.BlockSpec((1, nh, tile, dd), lambda bi, si: (bi, 0, si, 0))
    hmt = lambda nh, dd: pl.BlockSpec((1, nh, dd, tile), lambda bi, si: (bi, 0, 0, si))
    out_shape = [
        jax.ShapeDtypeStruct((b, MLA_HEADS, s, MLA_K), BF16),
        jax.ShapeDtypeStruct((b, MLA_HEADS, MLA_K, s), BF16),
        jax.ShapeDtypeStruct((b, MLA_HEADS, s, V_EXT), BF16),
        jax.ShapeDtypeStruct((b, GQA_HEADS, s, GQA_DIM), BF16),
        jax.ShapeDtypeStruct((b, GQA_KV_HEADS, GQA_DIM, s), BF16),
        jax.ShapeDtypeStruct((b, GQA_KV_HEADS, s, V_EXT), BF16),
    ]
    return pl.pallas_call(
        _proj_body,
        grid=(b, nst),
        in_specs=[
            pl.BlockSpec((1, tile, d), lambda bi, si: (bi, si, 0)),
            full((1, d)), full((1, d)),
            full(w["w_in"].shape), full(w["w_kt"].shape), full((1, Q_LORA)), full(w["w_qb"].shape),
            full((1, KV_LORA)), full(w["w_kvb_v"].shape), full(w["w_kvb_kt"].shape),
            full((1, GQA_DIM)), full((1, GQA_DIM)),
            tab, tab, tab, tab, tabt(ROPE_DIM), tabt(ROPE_DIM), tabt(GQA_DIM), tabt(GQA_DIM),
        ],
        out_specs=[hm(MLA_HEADS, MLA_K), hmt(MLA_HEADS, MLA_K), hm(MLA_HEADS, V_EXT),
                   hm(GQA_HEADS, GQA_DIM), hmt(GQA_KV_HEADS, GQA_DIM), hm(GQA_KV_HEADS, V_EXT)],
        out_shape=out_shape,
        compiler_params=pltpu.CompilerParams(dimension_semantics=("parallel", "parallel"),
                                             vmem_limit_bytes=VMEM_LIMIT),
        name="proj",
    )(x3, w["ln_emb_g"], w["ln_emb_b"], w["w_in"], w["w_kt"], w["g_q_a"], w["w_qb"], w["g_kv_a"], w["w_kvb_v"],
      w["w_kvb_kt"], w["g_q_gqa"], w["g_q_gqa_sw"], *tabs)


def _softmax_pv(q, kt, v, kmt, vm):
    sm = jnp.dot(q, kmt, preferred_element_type=F32)
    col = lax.broadcasted_iota(I32, sm.shape, 1)
    sm = jnp.where(col < N_META, sm, NEG_BIG)
    nblk = kt.shape[1] // KEY_BLOCK
    blocks = [jnp.dot(q, kt[:, KEY_BLOCK * c:KEY_BLOCK * (c + 1)], preferred_element_type=F32) for c in range(nblk)]
    mx = blocks[0]
    for c in range(1, nblk):
        mx = jnp.maximum(mx, blocks[c])
    m = jnp.maximum(jnp.max(mx, axis=1, keepdims=True), jnp.max(sm, axis=1, keepdims=True))
    acc = jnp.dot(jnp.exp2(sm - m).astype(BF16), vm, preferred_element_type=F32)
    for c in range(nblk):
        p = jnp.exp2(blocks[c] - m).astype(BF16)
        acc = acc + jnp.dot(p, v[KEY_BLOCK * c:KEY_BLOCK * (c + 1), :], preferred_element_type=F32)
    return acc[:, 0:V_DIM] / acc[:, V_DIM:V_EXT]


def _mla_attn_body(q_ref, k_ref, v_ref, km_ref, vm_ref, o_ref):
    for h in range(q_ref.shape[1]):
        for i in range(q_ref.shape[2] // Q_SUB_MLA):
            rows = pl.ds(i * Q_SUB_MLA, Q_SUB_MLA)
            o = _softmax_pv(q_ref[0, h, rows, :], k_ref.at[0, h], v_ref.at[0, h], km_ref[0, h], vm_ref[0, h])
            o_ref[rows, V_DIM * h:V_DIM * (h + 1)] = o.astype(o_ref.dtype)


def _gqa_attn_body(q_ref, k_ref, v_ref, km_ref, vm_ref, o_ref):
    for g in range(2):
        for i in range(q_ref.shape[2] // Q_SUB_GQA):
            rows = pl.ds(i * Q_SUB_GQA, Q_SUB_GQA)
            o = _softmax_pv(q_ref[0, g, rows, :], k_ref.at[0, 0], v_ref.at[0, 0], km_ref[0, 0], vm_ref[0, 0])
            o_ref[rows, GQA_DIM * g:GQA_DIM * (g + 1)] = o.astype(o_ref.dtype)


def _mla_attention(qm, kmt, vm, kmeta_t, vmeta):
    b, h, s, dk = qm.shape
    tq = Q_TILE_MLA
    nq = s // tq
    hs = MLA_HEADS_PER_STEP
    return pl.pallas_call(
        _mla_attn_body,
        grid=(b, h // hs, nq),
        in_specs=[
            pl.BlockSpec((1, hs, tq, dk), lambda bi, hi, qi: (bi, hi, qi, 0)),
            pl.BlockSpec((1, hs, dk, s), lambda bi, hi, qi: (bi, hi, 0, 0)),
            pl.BlockSpec((1, hs, s, V_EXT), lambda bi, hi, qi: (bi, hi, 0, 0)),
            pl.BlockSpec((1, hs, dk, META_PAD), lambda bi, hi, qi: (0, hi, 0, 0)),
            pl.BlockSpec((1, hs, META_PAD, V_EXT), lambda bi, hi, qi: (0, hi, 0, 0)),
        ],
        out_specs=pl.BlockSpec((tq, hs * V_DIM), lambda bi, hi, qi: (bi * nq + qi, hi)),
        out_shape=jax.ShapeDtypeStruct((b * s, h * V_DIM), BF16),
        compiler_params=pltpu.CompilerParams(dimension_semantics=("parallel", "parallel", "parallel"),
                                             vmem_limit_bytes=VMEM_LIMIT),
        name="mla_attn",
    )(qm, kmt, vm, kmeta_t, vmeta)


def _gqa_attention(qg, kgt, vg, kmeta_t, vmeta):
    b, h, s, d = qg.shape
    hk = kgt.shape[1]
    tq = Q_TILE_GQA
    nq = s // tq
    return pl.pallas_call(
        _gqa_attn_body,
        grid=(b, hk, nq),
        in_specs=[
            pl.BlockSpec((1, 2, tq, d), lambda bi, ji, qi: (bi, ji, qi, 0)),
            pl.BlockSpec((1, 1, d, s), lambda bi, ji, qi: (bi, ji, 0, 0)),
            pl.BlockSpec((1, 1, s, V_EXT), lambda bi, ji, qi: (bi, ji, 0, 0)),
            pl.BlockSpec((1, 1, d, META_PAD), lambda bi, ji, qi: (0, ji, 0, 0)),
            pl.BlockSpec((1, 1, META_PAD, V_EXT), lambda bi, ji, qi: (0, ji, 0, 0)),
        ],
        out_specs=pl.BlockSpec((tq, 2 * d), lambda bi, ji, qi: (bi * nq + qi, ji)),
        out_shape=jax.ShapeDtypeStruct((b * s, h * d), BF16),
        compiler_params=pltpu.CompilerParams(dimension_semantics=("parallel", "parallel", "parallel"),
                                             vmem_limit_bytes=VMEM_LIMIT),
        name="gqa_attn",
    )(qg, kgt, vg, kmeta_t, vmeta)


def _to_tiles(ref, x):
    r = x.shape[0]
    for s in range(x.shape[1] // LANES):
        ref[:, s] = x[:, LANES * s:LANES * (s + 1)].reshape(r // SUBLANES, SUBLANES, LANES)


def _from_tiles(ref):
    r = ref.shape[0] * SUBLANES
    return jnp.concatenate([ref[:, s].reshape(r, LANES) for s in range(ref.shape[1])], axis=1)


MERGE_SUB = 256


def _pack_rows(v):
    c = v.shape[1] // 2
    lo = lax.bitcast_convert_type(v[:, 0:c].astype(BF16).astype(F32), I32)
    hi = lax.bitcast_convert_type(v[:, c:2 * c].astype(BF16).astype(F32), I32)
    return lax.shift_right_logical(lo, 16) | hi


def _unpack_words(w):
    return (lax.bitcast_convert_type(w << 16, F32), lax.bitcast_convert_type(w & jnp.int32(-65536), F32))


def _merge_body(om_ref, og_ref, x_ref, lng_ref, lnb_ref, gom_ref, gog_ref, wo_ref, l1g_ref, l1b_ref,
                wr_ref, br_ref, h1_ref, h1p_ref, topi_ref, gate_ref):
    for i in range(x_ref.shape[0] // MERGE_SUB):
        rows = pl.ds(i * MERGE_SUB, MERGE_SUB)
        h1, ti, gt = _merge_chain(om_ref[rows, :], og_ref[rows, :], x_ref[rows, :], lng_ref[...], lnb_ref[...],
                                  gom_ref[...], gog_ref[...], wo_ref, l1g_ref[...], l1b_ref[...], wr_ref, br_ref[...])
        blocks = pl.ds(i * (MERGE_SUB // SUBLANES), MERGE_SUB // SUBLANES)
        _to_tiles(h1_ref.at[blocks], h1)
        _to_tiles(h1p_ref.at[blocks], _pack_rows(h1))
        topi_ref[0, :, rows] = ti
        for k in range(TOP_K):
            gate_ref[0, :, pl.ds(k * TOKEN_TILE + i * MERGE_SUB, MERGE_SUB)] = gt[k:k + 1, :]


def _merge_chain(om, og, x, lng, lnb, gom, gog, wo_ref, l1g, l1b, wr_ref, br):
    t = x.shape[0]
    h0 = _layernorm(x, lng, lnb)
    nm = _rmsnorm(om.astype(F32), gom).astype(BF16)
    ng = _rmsnorm(og.astype(F32), gog).astype(BF16)
    half = nm.shape[1]
    mix = jnp.dot(nm, wo_ref[0:half, :], preferred_element_type=F32)
    mix = mix + jnp.dot(ng, wo_ref[half:2 * half, :], preferred_element_type=F32)
    h1 = _layernorm(DEEPNORM_ALPHA * h0 + mix, l1g, l1b)

    hi = h1.astype(BF16)
    lo = (h1 - hi.astype(F32)).astype(BF16)
    acc = jnp.dot(hi, wr_ref[...], preferred_element_type=F32) + jnp.dot(lo, wr_ref[...], preferred_element_type=F32)
    logits = acc[:, 0:LANES] + acc[:, LANES:2 * LANES] + br
    cur = logits.T[0:N_EXPERTS, :]
    eidx = lax.broadcasted_iota(I32, cur.shape, 0)
    vals, idxs = [], []
    for _ in range(TOP_K):
        m = jnp.max(cur, axis=0, keepdims=True)
        i = jnp.min(jnp.where(cur == m, eidx, N_EXPERTS), axis=0, keepdims=True)
        vals.append(m)
        idxs.append(i)
        cur = jnp.where(eidx == i, -jnp.inf, cur)
    ex = [jnp.exp(v - vals[0]) for v in vals]
    den = ex[0] + ex[1] + ex[2] + ex[3]
    sub = lax.broadcasted_iota(I32, (8, t), 0)
    ti = jnp.zeros((8, t), I32)
    gt = jnp.zeros((8, t), F32)
    for k in range(TOP_K):
        ti = jnp.where(sub == k, idxs[k], ti)
        gt = jnp.where(sub == k, ex[k] / den, gt)
    return h1, ti[0:TOP_K, :], gt[0:TOP_K, :]


def _merge(o_mla, o_gqa, x2, w):
    n, d = x2.shape
    tile = TOKEN_TILE
    half = o_mla.shape[1]
    full = lambda shape: pl.BlockSpec(shape, lambda i: (0,) * len(shape))
    row = lambda width: pl.BlockSpec((tile, width), lambda i: (i, 0))
    return pl.pallas_call(
        _merge_body,
        grid=(n // tile,),
        in_specs=[row(half), row(half), row(d), full((1, d)), full((1, d)), full((1, half)), full((1, half)),
                  full((d, d)), full((1, d)), full((1, d)), full((d, 2 * LANES)), full((1, LANES))],
        out_specs=[pl.BlockSpec((tile // SUBLANES, d // LANES, SUBLANES, LANES), lambda i: (i, 0, 0, 0)),
                   pl.BlockSpec((tile // SUBLANES, PACK_CHUNKS, SUBLANES, LANES), lambda i: (i, 0, 0, 0)),
                   pl.BlockSpec((1, TOP_K, tile), lambda i: (i, 0, 0)),
                   pl.BlockSpec((1, 1, TOP_K * tile), lambda i: (i, 0, 0))],
        out_shape=[jax.ShapeDtypeStruct((n // SUBLANES, d // LANES, SUBLANES, LANES), F32),
                   jax.ShapeDtypeStruct((n // SUBLANES, PACK_CHUNKS, SUBLANES, LANES), I32),
                   jax.ShapeDtypeStruct((n // tile, TOP_K, tile), I32),
                   jax.ShapeDtypeStruct((n // tile, 1, TOP_K * tile), F32)],
        compiler_params=pltpu.CompilerParams(dimension_semantics=("parallel",), vmem_limit_bytes=VMEM_LIMIT),
        name="merge",
    )(o_mla, o_gqa, x2, w["ln_emb_g"], w["ln_emb_b"], w["g_o_mla"], w["g_o_gqa"], w["w_o"],
      w["ln1_g"], w["ln1_b"], w["w_router"], w["b_router"])


def _lanes_from_sublanes(col):
    diag = lax.broadcasted_iota(I32, col.shape, 0) == lax.broadcasted_iota(I32, col.shape, 1)
    return jnp.sum(jnp.where(diag, col, 0.0), axis=0, keepdims=True)


def _positions_body(topi_ref, upper_ref, lpb_ref, gstart_ref, nslab_ref, lstart_ref, texp_ref, nvalid_ref, ecnt_ref,
                    eoff_ref, *, ntp):
    ntile, _, tl = topi_ref.shape
    r = lax.broadcasted_iota(I32, (N_EXPERTS, N_EXPERTS), 0)
    c = lax.broadcasted_iota(I32, (N_EXPERTS, N_EXPERTS), 1)
    lower = (c < r).astype(F32)
    eidx = lax.broadcasted_iota(I32, (N_EXPERTS, tl), 0)

    def tile_onehots(j):
        topi = topi_ref[j]
        ohs = [eidx == topi[k:k + 1, :] for k in range(TOP_K)]
        onehot = ohs[0].astype(F32) + ohs[1].astype(F32) + ohs[2].astype(F32) + ohs[3].astype(F32)
        cnt = jnp.sum(onehot, axis=1, keepdims=True)
        run = jnp.floor((cnt + (SUBLANES - 1)) * (1.0 / SUBLANES)) * SUBLANES
        return ohs, onehot, jnp.broadcast_to(run, (N_EXPERTS, LANES))

    tot = lax.fori_loop(0, ntile, lambda j, acc: acc + tile_onehots(j)[2], jnp.zeros((N_EXPERTS, LANES), F32))
    pc = jnp.floor((tot + (MOE_TILE - 1)) * (1.0 / MOE_TILE)) * MOE_TILE
    off = jnp.dot(lower, pc, precision=lax.Precision.HIGHEST, preferred_element_type=F32)
    cumend = off + pc
    tstart = lax.broadcasted_iota(I32, (N_EXPERTS, ntp), 1).astype(F32) * MOE_TILE
    te = jnp.sum((jnp.broadcast_to(cumend[:, 0:1], (N_EXPERTS, ntp)) <= tstart).astype(I32), axis=0, keepdims=True)
    texp_ref[...] = jnp.minimum(te, N_EXPERTS - 1)
    nvalid_ref[...] = (cumend[N_EXPERTS - 1:N_EXPERTS, :] * (1.0 / MOE_TILE)).astype(I32)
    ecnt_ref[...] = _lanes_from_sublanes(tot).astype(I32)
    eoff_ref[...] = _lanes_from_sublanes(off).astype(I32)

    def place(j, carry):
        ohs, onehot, tile_run = tile_onehots(j)
        before = jnp.dot(onehot.astype(BF16), upper_ref[...], preferred_element_type=F32)
        loff = jnp.dot(lower, tile_run, precision=lax.Precision.HIGHEST, preferred_element_type=F32)
        base = before + loff[:, 0:1]
        sub = lax.broadcasted_iota(I32, (SUBLANES, tl), 0)
        out = jnp.zeros((SUBLANES, tl), F32)
        for k in range(TOP_K):
            pk = jnp.sum(jnp.where(ohs[k], base, 0.0), axis=0, keepdims=True)
            out = jnp.where(sub == k, pk, out)
        lp = out[0:TOP_K, :].astype(I32)
        lpb = (lp >> SUBLANE_SHIFT) * PSLAB + (lp & (SUBLANES - 1))
        lpb_ref[j] = jnp.concatenate([lpb[k:k + 1, :] for k in range(TOP_K)], axis=1)
        inv = 1.0 / SUBLANES
        gstart_ref[j] = (_lanes_from_sublanes(off + carry) * inv).astype(I32)
        nslab_ref[j] = (_lanes_from_sublanes(tile_run) * inv).astype(I32)
        lstart_ref[j] = (_lanes_from_sublanes(loff) * inv).astype(I32)
        return carry + tile_run

    lax.fori_loop(0, ntile, place, jnp.zeros((N_EXPERTS, LANES), F32))


def _positions(topi, ntp):
    ntile, _, tl = topi.shape
    tab = jax.ShapeDtypeStruct((ntile, 1, LANES), I32)
    small = jax.ShapeDtypeStruct((1, LANES), I32)
    upper = (jnp.arange(tl, dtype=I32)[:, None] < jnp.arange(tl, dtype=I32)[None, :]).astype(BF16)
    return pl.pallas_call(
        functools.partial(_positions_body, ntp=ntp),
        out_shape=[jax.ShapeDtypeStruct((ntile, 1, TOP_K * tl), I32), tab, tab, tab,
                   jax.ShapeDtypeStruct((1, ntp), I32), small, small, small],
        compiler_params=pltpu.CompilerParams(vmem_limit_bytes=VMEM_LIMIT),
        name="positions",
    )(topi, upper)


STAGE_ROWS = TOKEN_TILE * TOP_K + N_EXPERTS * SUBLANES
STAGE_FLAT = STAGE_ROWS * PACK_CHUNKS


def _load_row(ref, flat_start, chunks=SUBLANES):
    return ref[pl.ds(flat_start, chunks, stride=SUBLANES), :]


def _store_row(ref, flat_start, v):
    ref[pl.ds(flat_start, v.shape[0], stride=SUBLANES), :] = v


CHUNK_SHIFT = 2
CHUNK_SLABS = 1 << CHUNK_SHIFT


def _copy_run(src_ref, src0, dst_ref, dst0, nslab, sem):
    done = jnp.int32(0)
    left = nslab
    for shift in (2 * CHUNK_SHIFT, CHUNK_SHIFT, 0):
        size = (1 << shift) * PSLAB
        count = left >> shift

        def start(s, c, size=size, base=done):
            o = base + s * size
            pltpu.make_async_copy(src_ref.at[pl.ds(src0 + o, size)], dst_ref.at[pl.ds(dst0 + o, size)], sem).start()
            return c

        lax.fori_loop(0, count, start, 0)
        done = done + count * size
        left = left & ((1 << shift) - 1)


WAIT_SHIFT = 4


def _wait_slabs(src_ref, dst_ref, nslab, sem):
    big = (1 << WAIT_SHIFT) * PSLAB

    def wait_big(s, c):
        pltpu.make_async_copy(src_ref.at[pl.ds(0, big)], dst_ref.at[pl.ds(0, big)], sem).wait()
        return c

    def wait_small(s, c):
        pltpu.make_async_copy(src_ref.at[pl.ds(0, PSLAB)], dst_ref.at[pl.ds(0, PSLAB)], sem).wait()
        return c

    lax.fori_loop(0, nslab >> WAIT_SHIFT, wait_big, 0)
    lax.fori_loop(0, nslab & ((1 << WAIT_SHIFT) - 1), wait_small, 0)


def _total_slabs(nslab_ref):
    return lax.fori_loop(0, N_EXPERTS, lambda e, c: c + nslab_ref[0, 0, e], jnp.int32(0))


def _dispatch_body(ecnt_ref, eoff_ref, lpb_ref, gstart_ref, nslab_ref, lstart_ref, h_ref, xs_ref,
                   stg0_ref, stg1_ref, zero_ref, issued_ref, sem, zsem):
    g = pl.program_id(0)
    ng = pl.num_programs(0)
    stages = (stg0_ref, stg1_ref)

    def drain(j):
        _wait_slabs(stages[j], xs_ref, issued_ref[j], sem.at[j])

    def run(j):
        stg_ref = stages[j]

        @pl.when(g >= 2)
        def _():
            drain(j)

        def zero_last(e, c):
            last = jnp.maximum(lstart_ref[0, 0, e] + nslab_ref[0, 0, e] - 1, 0)
            stg_ref[pl.ds(last * PSLAB, PSLAB), :] = jnp.zeros((PSLAB, LANES), I32)
            return c

        lax.fori_loop(0, N_EXPERTS, zero_last, 0)

        def move(i, c):
            for u in range(SUBLANES):
                v = _load_row(h_ref, i * PSLAB + u, PACK_CHUNKS)
                for k in range(TOP_K):
                    _store_row(stg_ref, lpb_ref[0, 0, k * TOKEN_TILE + i * SUBLANES + u], v)
            return c

        lax.fori_loop(0, TOKEN_TILE // SUBLANES, move, 0)

        def send_run(e, c):
            _copy_run(stg_ref, lstart_ref[0, 0, e] * PSLAB, xs_ref, gstart_ref[0, 0, e] * PSLAB,
                      nslab_ref[0, 0, e], sem.at[j])
            return c

        lax.fori_loop(0, N_EXPERTS, send_run, 0)
        issued_ref[j] = _total_slabs(nslab_ref)

    for j in range(2):
        @pl.when(lax.rem(g, 2) == j)
        def _():
            run(j)

    @pl.when(g == ng - 1)
    def _():
        zero_ref[...] = jnp.zeros_like(zero_ref)

        def zero_copy(dst_slab):
            return pltpu.make_async_copy(zero_ref, xs_ref.at[pl.ds(dst_slab * PSLAB, PSLAB)], zsem)

        def pad_segment(e, c):
            rows = ecnt_ref[e]
            first = (eoff_ref[e] + rows) >> SUBLANE_SHIFT
            npad = lax.rem(MOE_TILE - lax.rem(rows, MOE_TILE), MOE_TILE) >> SUBLANE_SHIFT
            lax.fori_loop(0, npad, lambda s, cc: (zero_copy(first + s).start(), cc)[1], 0)
            lax.fori_loop(0, npad, lambda s, cc: (zero_copy(first + s).wait(), cc)[1], 0)
            return c

        lax.fori_loop(0, N_EXPERTS, pad_segment, 0)
        for j in range(2):
            @pl.when(jnp.logical_and(ng > 1, lax.rem(g, 2) != j))
            def _():
                drain(j)

        for j in range(2):
            @pl.when(lax.rem(g, 2) == j)
            def _():
                drain(j)


def _dispatch(h1_flat, lpb, gstart, nslab, lstart, ecnt, eoff, rows_pad):
    nflat = h1_flat.shape[0]
    t = TOKEN_TILE
    smem = lambda shape, imap: pl.BlockSpec(shape, imap, memory_space=pltpu.SMEM)
    per_tile = smem((1, 1, LANES), lambda i, c, o: (i, 0, 0))
    grid_spec = pltpu.PrefetchScalarGridSpec(
        num_scalar_prefetch=2,
        grid=(nflat // (t * PACK_CHUNKS),),
        in_specs=[smem((1, 1, TOP_K * t), lambda i, c, o: (i, 0, 0)), per_tile, per_tile, per_tile,
                  pl.BlockSpec((t * PACK_CHUNKS, LANES), lambda i, c, o: (i, 0))],
        out_specs=pl.BlockSpec(memory_space=pl.ANY),
        scratch_shapes=[pltpu.VMEM((STAGE_FLAT, LANES), I32), pltpu.VMEM((STAGE_FLAT, LANES), I32),
                        pltpu.VMEM((PSLAB, LANES), I32),
                        pltpu.SMEM((2,), I32), pltpu.SemaphoreType.DMA((2,)), pltpu.SemaphoreType.DMA],
    )
    return pl.pallas_call(
        _dispatch_body,
        grid_spec=grid_spec,
        out_shape=jax.ShapeDtypeStruct((rows_pad * PACK_CHUNKS, LANES), I32),
        compiler_params=pltpu.CompilerParams(dimension_semantics=("arbitrary",), vmem_limit_bytes=VMEM_LIMIT),
        name="dispatch",
    )(ecnt, eoff, lpb, gstart, nslab, lstart, h1_flat)


def _ffn_body(texp_ref, nvalid_ref, ecnt_ref, eoff_ref, x_ref, wgu_hbm, wd_hbm, bg_ref, bl_ref, bd_ref, y_ref,
              wgu_buf, wd_buf, wg_sc, wl_sc, wd_sc, seg_ref, sem):
    i = pl.program_id(0)
    valid = i < nvalid_ref[0]
    e = texp_ref[i]
    first_of_expert = jnp.logical_or(i == 0, e != texp_ref[jnp.maximum(i - 1, 0)])

    def weight_copies(expert, slot):
        return (pltpu.make_async_copy(wgu_hbm.at[expert], wgu_buf.at[slot], sem.at[slot]),
                pltpu.make_async_copy(wd_hbm.at[expert], wd_buf.at[slot], sem.at[slot]))

    def prepare(slot):
        blk = 2 * LANES
        r = lax.broadcasted_iota(I32, (blk, blk), 0)
        c = lax.broadcasted_iota(I32, (blk, blk), 1)
        sel = (r == jnp.where(c < LANES, 2 * c, 2 * (c - LANES) + 1)).astype(BF16)
        for b in range(wgu_buf.shape[2] // blk):
            wb = wgu_buf[slot, :, blk * b:blk * (b + 1)].astype(BF16)
            y = jnp.dot(wb, sel, preferred_element_type=F32).astype(BF16)
            wg_sc[:, LANES * b:LANES * (b + 1)] = y[:, 0:LANES]
            wl_sc[:, LANES * b:LANES * (b + 1)] = y[:, LANES:blk]
        wd_sc[...] = wd_buf[slot].astype(BF16)

    @pl.when(jnp.logical_and(valid, first_of_expert))
    def _():
        @pl.when(i == 0)
        def _():
            seg_ref[0] = 0
            for cp in weight_copies(e, 0):
                cp.start()

        seg = seg_ref[0]
        seg_ref[0] = seg + 1
        nxt = lax.while_loop(lambda n: jnp.logical_and(n < N_EXPERTS, ecnt_ref[jnp.minimum(n, N_EXPERTS - 1)] == 0),
                             lambda n: n + 1, e + 1)
        for slot in range(2):
            @pl.when(lax.rem(seg, 2) == slot)
            def _():
                for cp in weight_copies(e, slot):
                    cp.wait()

                @pl.when(nxt < N_EXPERTS)
                def _():
                    for cp in weight_copies(nxt, 1 - slot):
                        cp.start()

                prepare(slot)

    def chain(b0, nb):
        blocks = pl.ds(b0, nb)
        x = jnp.concatenate(_unpack_words(_from_tiles(x_ref.at[blocks])), axis=1).astype(BF16)
        hg = jnp.dot(x, wg_sc[...], preferred_element_type=F32) + bg_ref[0]
        hl = jnp.dot(x, wl_sc[...], preferred_element_type=F32) + bl_ref[0]
        g = jnp.minimum(hg, SWIGLU_LIMIT)
        lin = jnp.clip(hl, -SWIGLU_LIMIT, SWIGLU_LIMIT)
        act = g * (1.0 / (1.0 + jnp.exp(-SWIGLU_ALPHA * g))) * (lin + 1.0)
        y = jnp.dot(act.astype(BF16), wd_sc[...], preferred_element_type=F32) + bd_ref[0]
        _to_tiles(y_ref.at[blocks], _pack_rows(y))

    def compute(nrows):
        done = 0
        while done < nrows:
            n = min(FFN_CHAIN, nrows - done)
            chain(done // SUBLANES, n // SUBLANES)
            done += n
        rest = y_ref.shape[0] - nrows // SUBLANES
        if rest:
            y_ref[pl.ds(nrows // SUBLANES, rest)] = jnp.zeros((rest,) + y_ref.shape[1:], I32)

    used = eoff_ref[e] + ecnt_ref[e] - i * MOE_TILE
    quarter = MOE_TILE // 4
    for q in range(1, 5):
        lo, hi = (q - 1) * quarter, q * quarter
        in_range = jnp.logical_and(used > lo, used <= hi) if q < 4 else used > lo

        @pl.when(jnp.logical_and(valid, in_range))
        def _():
            compute(hi)


def _grouped_ffn(xs_tiles, texp, nvalid, ecnt, eoff, wgu, wd, bg, bl, bd, ntiles):
    d = wgu.shape[1]
    tm = MOE_TILE
    f = wd.shape[1]
    xmap = lambda i, te, nv, ec, eo: (jnp.minimum(i, nv[0] - 1), 0, 0, 0)
    wmap = lambda i, te, nv, ec, eo: (te[i], 0, 0)
    rows_blk = pl.BlockSpec((tm // SUBLANES, PACK_CHUNKS, SUBLANES, LANES), xmap)
    hbm = pl.BlockSpec(memory_space=pl.ANY)
    grid_spec = pltpu.PrefetchScalarGridSpec(
        num_scalar_prefetch=4,
        grid=(ntiles,),
        in_specs=[rows_blk, hbm, hbm,
                  pl.BlockSpec((1, 1, f), wmap), pl.BlockSpec((1, 1, f), wmap), pl.BlockSpec((1, 1, d), wmap)],
        out_specs=rows_blk,
        scratch_shapes=[pltpu.VMEM((2, d, 2 * f), F32), pltpu.VMEM((2, f, d), F32),
                        pltpu.VMEM((d, f), BF16), pltpu.VMEM((d, f), BF16), pltpu.VMEM((f, d), BF16),
                        pltpu.SMEM((1,), I32), pltpu.SemaphoreType.DMA((2,))],
    )
    return pl.pallas_call(
        _ffn_body,
        grid_spec=grid_spec,
        out_shape=jax.ShapeDtypeStruct(xs_tiles.shape, I32),
        compiler_params=pltpu.CompilerParams(dimension_semantics=("arbitrary",), vmem_limit_bytes=VMEM_LIMIT),
        name="ffn",
    )(texp, nvalid, ecnt, eoff, xs_tiles, wgu, wd, bg, bl, bd)


COMBINE_SUB = 512


def _combine_body(lpb_ref, gate_ref, gstart_ref, nslab_ref, lstart_ref, gstart2_ref, nslab2_ref, lstart2_ref,
                  h_ref, l2g_ref, l2b_ref, ys_ref, o_ref, stg0_ref, stg1_ref, moe_ref, sem):
    g = pl.program_id(0)
    ng = pl.num_programs(0)
    stages = (stg0_ref, stg1_ref)

    def fetch(gs_ref, ns_ref, ls_ref, j):
        def per_expert(e, c):
            _copy_run(ys_ref, gs_ref[0, 0, e] * PSLAB, stages[j], ls_ref[0, 0, e] * PSLAB, ns_ref[0, 0, e], sem.at[j])
            return c

        lax.fori_loop(0, N_EXPERTS, per_expert, 0)

    def wait_tile(j):
        _wait_slabs(ys_ref, stages[j], _total_slabs(nslab_ref), sem.at[j])

    def gather(j):
        stg_ref = stages[j]

        def body(i, c):
            for u in range(SUBLANES):
                tok = i * SUBLANES + u
                acc_lo = acc_hi = None
                for k in range(TOP_K):
                    gk = gate_ref[0, 0, k * TOKEN_TILE + tok]
                    lo, hi = _unpack_words(_load_row(stg_ref, lpb_ref[0, 0, k * TOKEN_TILE + tok], PACK_CHUNKS))
                    acc_lo = gk * lo if k == 0 else acc_lo + gk * lo
                    acc_hi = gk * hi if k == 0 else acc_hi + gk * hi
                _store_row(moe_ref, i * SLAB + u, acc_lo)
                _store_row(moe_ref, i * SLAB + PSLAB + u, acc_hi)
            return c

        lax.fori_loop(0, TOKEN_TILE // SUBLANES, body, 0)

    def finish():
        nblk = COMBINE_SUB // SUBLANES
        for part in range(TOKEN_TILE // COMBINE_SUB):
            moe = jnp.concatenate(
                [jnp.concatenate([moe_ref[pl.ds((part * nblk + i) * SLAB + s * SUBLANES, SUBLANES), :]
                                  for i in range(nblk)], axis=0) for s in range(SUBLANES)], axis=1)
            h1 = _from_tiles(h_ref.at[pl.ds(part * nblk, nblk)])
            o_ref[pl.ds(part * COMBINE_SUB, COMBINE_SUB), :] = _layernorm(DEEPNORM_ALPHA * h1 + moe, l2g_ref[...],
                                                                          l2b_ref[...])

    @pl.when(g == 0)
    def _():
        fetch(gstart_ref, nslab_ref, lstart_ref, 0)

    for j in range(2):
        @pl.when(lax.rem(g, 2) == j)
        def _():
            @pl.when(g + 1 < ng)
            def _():
                fetch(gstart2_ref, nslab2_ref, lstart2_ref, 1 - j)

            wait_tile(j)
            gather(j)

    finish()


def _combine(lpb, gates, gstart, nslab, lstart, h1_tiles, ys_flat, w):
    d = h1_tiles.shape[1] * LANES
    n = h1_tiles.shape[0] * SUBLANES
    t = TOKEN_TILE
    ng = n // t
    full = lambda shape: pl.BlockSpec(shape, lambda i: (0,) * len(shape))
    smem = lambda shape, imap: pl.BlockSpec(shape, imap, memory_space=pltpu.SMEM)
    vec = smem((1, 1, TOP_K * t), lambda i: (i, 0, 0))
    this_step = smem((1, 1, LANES), lambda i: (i, 0, 0))
    next_step = smem((1, 1, LANES), lambda i: (jnp.minimum(i + 1, ng - 1), 0, 0))
    return pl.pallas_call(
        _combine_body,
        grid=(ng,),
        in_specs=[vec, vec, this_step, this_step, this_step, next_step, next_step, next_step,
                  pl.BlockSpec((t // SUBLANES, d // LANES, SUBLANES, LANES), lambda i: (i, 0, 0, 0)),
                  full((1, d)), full((1, d)),
                  pl.BlockSpec(memory_space=pl.ANY)],
        out_specs=pl.BlockSpec((t, d), lambda i: (i, 0)),
        out_shape=jax.ShapeDtypeStruct((n, d), F32),
        scratch_shapes=[pltpu.VMEM((STAGE_FLAT, LANES), I32), pltpu.VMEM((STAGE_FLAT, LANES), I32),
                        pltpu.VMEM((TOKEN_TILE * SUBLANES, LANES), F32), pltpu.SemaphoreType.DMA((2,))],
        compiler_params=pltpu.CompilerParams(dimension_semantics=("arbitrary",), vmem_limit_bytes=VMEM_LIMIT),
        name="combine",
    )(lpb, gates, gstart, nslab, lstart, gstart, nslab, lstart, h1_tiles, w["ln2_g"], w["ln2_b"], ys_flat)


def _rope_tables(pos_1d, row, col, g_k):
    inv = ROPE_THETA ** (-jnp.arange(0, ROPE_DIM, 2, dtype=F32) / ROPE_DIM)

    def cs(p):
        ang = p.astype(F32)[:, None] * inv[None, :]
        return jnp.cos(ang), jnp.sin(ang)

    c1, s1 = cs(pos_1d)
    cr, sr = cs(row)
    cc, sc = cs(col)
    c1w, s1w = jnp.concatenate([c1, c1, c1, c1], axis=1), jnp.concatenate([-s1, s1, -s1, s1], axis=1)
    caw, saw = jnp.concatenate([cr, cr, cc, cc], axis=1), jnp.concatenate([-sr, sr, -sc, sc], axis=1)
    ckt = caw.T * g_k[:, None]
    skt = saw.T * _swap_halves64(g_k)[:, None]
    return (c1w, s1w, caw, saw, c1w[:, :ROPE_DIM].T, s1w[:, :ROPE_DIM].T, ckt, skt)


def _prep_weights(ln_emb_g, ln_emb_b, w_in, g_q_a, w_q_b, g_kv_a, w_kv_b, g_q_gqa, g_k_gqa, g_o_mla, g_o_gqa, w_o,
                  ln1_g, ln1_b, w_router, b_router, ln2_g, ln2_b):
    r2 = lambda v: v.reshape(1, -1).astype(F32)
    o = np.cumsum([0, Q_LORA, KV_LORA, ROPE_DIM, GQA_HEADS * GQA_DIM, GQA_KV_HEADS * GQA_DIM, GQA_KV_HEADS * GQA_DIM])
    wi = w_in[0]
    seg = [wi[:, o[i]:o[i + 1]] for i in range(6)]
    w_in_p = jnp.concatenate([seg[0], seg[1], seg[3], _swap_halves64(seg[3]), seg[5]], axis=1).astype(BF16)
    w_kt = jnp.concatenate([seg[2], seg[4]], axis=1).T.astype(BF16)
    wq = w_q_b[0].reshape(Q_LORA, MLA_HEADS, QK_DIM)
    wq_rope = wq[:, :, NOPE_DIM:].reshape(Q_LORA, -1)
    w_qb = jnp.concatenate([wq[:, :, :NOPE_DIM].reshape(Q_LORA, -1), wq_rope, _swap_halves64(wq_rope)],
                           axis=1).astype(BF16)
    wk = w_kv_b[0].reshape(KV_LORA, MLA_HEADS, NOPE_DIM + V_DIM)
    w_kvb_kt = wk[:, :, :NOPE_DIM].reshape(KV_LORA, -1).T.astype(BF16)
    w_kvb_v = wk[:, :, NOPE_DIM:].reshape(KV_LORA, -1).astype(BF16)
    wr = jnp.pad(w_router[0].astype(F32), ((0, 0), (0, LANES - N_EXPERTS)))
    wr_hi = wr.astype(BF16)
    w_router_p = jnp.concatenate([wr_hi, (wr - wr_hi.astype(F32)).astype(BF16)], axis=1)
    b_router_p = jnp.pad(b_router[0].astype(F32), (0, LANES - N_EXPERTS), constant_values=NEG_BIG).reshape(1, LANES)
    return dict(
        ln_emb_g=r2(ln_emb_g), ln_emb_b=r2(ln_emb_b), w_in=w_in_p, w_kt=w_kt, g_q_a=r2(g_q_a[0]), w_qb=w_qb,
        g_kv_a=r2(g_kv_a[0]), w_kvb_v=w_kvb_v, w_kvb_kt=w_kvb_kt, g_q_gqa=r2(g_q_gqa[0]),
        g_q_gqa_sw=_swap_halves64(r2(g_q_gqa[0])),
        g_o_mla=r2(g_o_mla[0]), g_o_gqa=r2(g_o_gqa[0]), w_o=w_o[0].astype(BF16), ln1_g=r2(ln1_g[0]),
        ln1_b=r2(ln1_b[0]), w_router=w_router_p, b_router=b_router_p, ln2_g=r2(ln2_g[0]), ln2_b=r2(ln2_b[0]))


def kernel(x, meta_tokens, ln_emb_g, ln_emb_b, w_in, g_q_a, w_q_b, g_kv_a, w_kv_b, g_q_gqa, g_k_gqa, g_o_mla, g_o_gqa,
           w_o, ln1_g, ln1_b, w_router, b_router, w_gate_up, b_gate_up, w_down, b_down, ln2_g, ln2_b):
    b, s, d = x.shape
    n = b * s
    assert d == D_MODEL and meta_tokens.shape == (N_META, d) and s % GRID_W == 0
    assert s % ROW_TILE == 0 and s % Q_TILE_MLA == 0 and s % Q_TILE_GQA == 0 and s % KEY_BLOCK == 0
    assert n % TOKEN_TILE == 0 and TOKEN_TILE % MERGE_SUB == 0 and TOKEN_TILE % COMBINE_SUB == 0
    w = _prep_weights(ln_emb_g, ln_emb_b, w_in, g_q_a, w_q_b, g_kv_a, w_kv_b, g_q_gqa, g_k_gqa, g_o_mla, g_o_gqa,
                      w_o, ln1_g, ln1_b, w_router, b_router, ln2_g, ln2_b)

    tok = jnp.arange(s, dtype=I32)
    g_k = g_k_gqa[0].astype(F32)
    tabs_real = _rope_tables(tok + N_META, tok // GRID_W, tok % GRID_W, g_k)
    mt = jnp.arange(META_PAD, dtype=I32)
    tabs_meta = _rope_tables(mt, jnp.full((META_PAD,), -1, I32), mt, g_k)

    qm, km, vm, qg, kg, vg = _project(x, tabs_real, w, ROW_TILE)
    meta = jnp.pad(meta_tokens.astype(x.dtype), ((0, META_PAD - N_META), (0, 0))).reshape(1, META_PAD, d)
    _, km_m, vm_m, _, kg_m, vg_m = _project(meta, tabs_meta, w, META_PAD)

    o_mla = _mla_attention(qm, km, vm, km_m, vm_m)
    o_gqa = _gqa_attention(qg, kg, vg, kg_m, vg_m)

    h1_tiles, h1_packed, topi, gates = _merge(o_mla, o_gqa, x.reshape(n, d), w)

    run_rows = n * TOP_K + (n // TOKEN_TILE) * N_EXPERTS * (SUBLANES - 1)
    ntiles = -(-run_rows // MOE_TILE) + N_EXPERTS
    ntp = -(-ntiles // LANES) * LANES
    rows_pad = ntiles * MOE_TILE
    lpb, gstart, nslab, lstart, texp, nvalid, ecnt, eoff = _positions(topi, ntp)
    flat = lambda a: a.reshape(-1, LANES)
    xs_flat = _dispatch(flat(h1_packed), lpb, gstart, nslab, lstart, ecnt.reshape(-1), eoff.reshape(-1), rows_pad)

    bgu = b_gate_up[0].astype(F32).reshape(N_EXPERTS, D_FF, 2).transpose(2, 0, 1).reshape(2, N_EXPERTS, 1, D_FF)
    tiles = lambda a: a.reshape(-1, PACK_CHUNKS, SUBLANES, LANES)
    ys_tiles = _grouped_ffn(tiles(xs_flat), texp.reshape(-1), nvalid.reshape(-1), ecnt.reshape(-1), eoff.reshape(-1),
                            w_gate_up[0], w_down[0], bgu[0], bgu[1],
                            b_down[0].astype(F32).reshape(N_EXPERTS, 1, d), ntiles)

    out = _combine(lpb, gates, gstart, nslab, lstart, h1_tiles, flat(ys_tiles), w)
    return out.reshape(b, s, d)
```

```python
import functools

import jax
import jax.numpy as jnp
import numpy as np
from jax import lax
from jax.experimental import pallas as pl
from jax.experimental.pallas import tpu as pltpu

D_MODEL = 1024
N_META = 16
GRID_W = 64
ROPE_THETA = 10000.0
MLA_HEADS = 4
Q_LORA = 256
KV_LORA = 128
NOPE_DIM = 128
ROPE_DIM = 64
V_DIM = 128
QK_DIM = NOPE_DIM + ROPE_DIM
GQA_HEADS = 4
GQA_KV_HEADS = 2
GQA_DIM = 128
N_EXPERTS = 32
TOP_K = 4
D_FF = D_MODEL
SWIGLU_LIMIT = 7.0
SWIGLU_ALPHA = 1.702
RMS_EPS = 1e-6
LN_EPS = 1e-5
DEPTH = 1
DEEPNORM_ALPHA = (2.0 * DEPTH) ** 0.25

LANES = 128
SUBLANE_SHIFT = 3
SUBLANES = 1 << SUBLANE_SHIFT
SLAB = SUBLANES * SUBLANES
PACK_CHUNKS = D_MODEL // 2 // LANES
PSLAB = SUBLANES * PACK_CHUNKS
META_PAD = 128
MLA_K = 2 * LANES
NEG_BIG = -1e30
LOG2E = 1.4426950408889634
V_EXT = 2 * LANES

ROW_TILE = 1024
Q_TILE_MLA = 2048
Q_TILE_GQA = 2048
MLA_HEADS_PER_STEP = 2
Q_SUB_MLA = 128
Q_SUB_GQA = 512
KEY_BLOCK = 512
TOKEN_TILE = 1024
MOE_TILE = 1024
FFN_CHAIN = 512
TOKEN_UNROLL = 16
VMEM_LIMIT = 56 * 1024 * 1024

F32 = jnp.float32
BF16 = jnp.bfloat16
I32 = jnp.int32


def _layernorm(x, g, b):
    mu = jnp.mean(x, axis=-1, keepdims=True)
    xc = x - mu
    var = jnp.mean(xc * xc, axis=-1, keepdims=True)
    return xc * lax.rsqrt(var + LN_EPS) * g + b


def _rmsnorm(x, g):
    return x * lax.rsqrt(jnp.mean(x * x, axis=-1, keepdims=True) + RMS_EPS) * g


def _swap_halves64(v):
    shp = v.shape
    return v.reshape(shp[:-1] + (shp[-1] // 64, 2, 32))[..., ::-1, :].reshape(shp)


PROJ_SUB = 256


def _swap_blocks32(xt):
    return jnp.concatenate([xt[o + b:o + b + 32] for o in range(0, xt.shape[0], 64) for b in (32, 0)], axis=0)


def _proj_chain(x, lng, lnb, win_ref, wkt_ref, gqa, wqb_ref, gkva, wkvbv_ref, wkvbkt_ref, gqg, gqgs,
                c1, s1, ca, sa, c1t, s1t, ckt, skt, put):
    r = x.shape[0]
    nt = (((1,), (1,)), ((), ()))
    h0b = _layernorm(x, lng, lnb).astype(BF16)
    z = jnp.dot(h0b, win_ref[...], preferred_element_type=F32)
    q_a = z[:, 0:256]
    kv_a = z[:, 256:384]
    q_g = z[:, 384:896]
    q_gs = z[:, 896:1408]
    v_g = z[:, 1408:1664]
    kt = lax.dot_general(wkt_ref[...], h0b, nt, preferred_element_type=F32)

    q = jnp.dot(_rmsnorm(q_a, gqa).astype(BF16), wqb_ref[...], preferred_element_type=F32)
    kvn = _rmsnorm(kv_a, gkva).astype(BF16)
    v_a = jnp.dot(kvn, wkvbv_ref[...], preferred_element_type=F32)
    knt = lax.dot_general(wkvbkt_ref[...], kvn, nt, preferred_element_type=F32)
    kpet = kt[0:ROPE_DIM]
    krot = kpet * c1t + _swap_blocks32(kpet) * s1t
    lane = lax.broadcasted_iota(I32, (r, LANES), 1)
    scale_a = QK_DIM ** -0.5 * LOG2E
    ones = jnp.ones((r, LANES), F32)
    for c in range(MLA_HEADS // 2):
        lo = LANES * c
        qr = q[:, 512 + lo:512 + lo + LANES] * c1 + q[:, 768 + lo:768 + lo + LANES] * s1
        for hh in range(2):
            h = 2 * c + hh
            slot = jnp.where((lane // 64) == hh, qr, 0.0)
            put("qm", h, (jnp.concatenate([q[:, LANES * h:LANES * (h + 1)], slot], axis=1) * scale_a).astype(BF16))
    for h in range(MLA_HEADS):
        put("km", h, jnp.concatenate([knt[LANES * h:LANES * (h + 1)], krot, krot], axis=0).astype(BF16))
        put("vm", h, jnp.concatenate([v_a[:, LANES * h:LANES * (h + 1)], ones], axis=1).astype(BF16))

    scale_b = GQA_DIM ** -0.5 * LOG2E
    cq, sq = ca * gqg, sa * gqgs
    for h in range(GQA_HEADS):
        sl = slice(LANES * h, LANES * (h + 1))
        xh = q_g[:, sl]
        inv = lax.rsqrt(jnp.mean(xh * xh, axis=-1, keepdims=True) + RMS_EPS)
        put("qg", h, ((xh * cq + q_gs[:, sl] * sq) * (inv * scale_b)).astype(BF16))
    for j in range(GQA_KV_HEADS):
        xt = kt[ROPE_DIM + LANES * j:ROPE_DIM + LANES * (j + 1)]
        inv = lax.rsqrt(jnp.mean(xt * xt, axis=0, keepdims=True) + RMS_EPS)
        put("kg", j, ((xt * ckt + _swap_blocks32(xt) * skt) * inv).astype(BF16))
        put("vg", j, jnp.concatenate([v_g[:, LANES * j:LANES * (j + 1)], ones], axis=1).astype(BF16))


def _proj_body(x_ref, lng_ref, lnb_ref, win_ref, wkt_ref, gqa_ref, wqb_ref, gkva_ref, wkvbv_ref, wkvbkt_ref,
               gqg_ref, gqgs_ref, c1_ref, s1_ref, ca_ref, sa_ref, c1t_ref, s1t_ref, ckt_ref, skt_ref,
               qm_ref, km_ref, vm_ref, qg_ref, kg_ref, vg_ref):
    outs = dict(qm=qm_ref, km=km_ref, vm=vm_ref, qg=qg_ref, kg=kg_ref, vg=vg_ref)
    tile = x_ref.shape[1]
    sub = min(PROJ_SUB, tile)
    for i in range(tile // sub):
        rows = pl.ds(i * sub, sub)

        def put(name, h, val):
            if name in ("km", "kg"):
                outs[name][0, h, :, rows] = val
            else:
                outs[name][0, h, rows, :] = val

        _proj_chain(x_ref[0, rows, :], lng_ref[...], lnb_ref[...], win_ref, wkt_ref, gqa_ref[...], wqb_ref,
                    gkva_ref[...], wkvbv_ref, wkvbkt_ref, gqg_ref[...], gqgs_ref[...],
                    c1_ref[rows, :], s1_ref[rows, :], ca_ref[rows, :], sa_ref[rows, :],
                    c1t_ref[:, rows], s1t_ref[:, rows], ckt_ref[:, rows], skt_ref[:, rows], put)


def _project(x3, tabs, w, tile):
    b, s, d = x3.shape
    nst = s // tile
    full = lambda shape: pl.BlockSpec(shape, lambda bi, si: (0,) * len(shape))
    tab = pl.BlockSpec((tile, LANES), lambda bi, si: (si, 0))
    tabt = lambda dims: pl.BlockSpec((dims, tile), lambda bi, si: (0, si))
    hm = lambda nh, dd: pl.BlockSpec((1, nh, tile, dd), lambda bi, si: (bi, 0, si, 0))
    hmt = lambda nh, dd: pl.BlockSpec((1, nh, dd, tile), lambda bi, si: (bi, 0, 0, si))
    out_shape = [
        jax.ShapeDtypeStruct((b, MLA_HEADS, s, MLA_K), BF16),
        jax.ShapeDtypeStruct((b, MLA_HEADS, MLA_K, s), BF16),
        jax.ShapeDtypeStruct((b, MLA_HEADS, s, V_EXT), BF16),
        jax.ShapeDtypeStruct((b, GQA_HEADS, s, GQA_DIM), BF16),
        jax.ShapeDtypeStruct((b, GQA_KV_HEADS, GQA_DIM, s), BF16),
        jax.ShapeDtypeStruct((b, GQA_KV_HEADS, s, V_EXT), BF16),
    ]
    return pl.pallas_call(
        _proj_body,
        grid=(b, nst),
        in_specs=[
            pl.BlockSpec((1, tile, d), lambda bi, si: (bi, si, 0)),
            full((1, d)), full((1, d)),
            full(w["w_in"].shape), full(w["w_kt"].shape), full((1, Q_LORA)), full(w["w_qb"].shape),
            full((1, KV_LORA)), full(w["w_kvb_v"].shape), full(w["w_kvb_kt"].shape),
            full((1, GQA_DIM)), full((1, GQA_DIM)),
            tab, tab, tab, tab, tabt(ROPE_DIM), tabt(ROPE_DIM), tabt(GQA_DIM), tabt(GQA_DIM),
        ],
        out_specs=[hm(MLA_HEADS, MLA_K), hmt(MLA_HEADS, MLA_K), hm(MLA_HEADS, V_EXT),
                   hm(GQA_HEADS, GQA_DIM), hmt(GQA_KV_HEADS, GQA_DIM), hm(GQA_KV_HEADS, V_EXT)],
        out_shape=out_shape,
        compiler_params=pltpu.CompilerParams(dimension_semantics=("parallel", "parallel"),
                                             vmem_limit_bytes=VMEM_LIMIT),
        name="proj",
    )(x3, w["ln_emb_g"], w["ln_emb_b"], w["w_in"], w["w_kt"], w["g_q_a"], w["w_qb"], w["g_kv_a"], w["w_kvb_v"],
      w["w_kvb_kt"], w["g_q_gqa"], w["g_q_gqa_sw"], *tabs)


def _softmax_pv(q, kt, v, kmt, vm):
    sm = jnp.dot(q, kmt, preferred_element_type=F32)
    col = lax.broadcasted_iota(I32, sm.shape, 1)
    sm = jnp.where(col < N_META, sm, NEG_BIG)
    nblk = kt.shape[1] // KEY_BLOCK
    blocks = [jnp.dot(q, kt[:, KEY_BLOCK * c:KEY_BLOCK * (c + 1)], preferred_element_type=F32) for c in range(nblk)]
    mx = blocks[0]
    for c in range(1, nblk):
        mx = jnp.maximum(mx, blocks[c])
    m = jnp.maximum(jnp.max(mx, axis=1, keepdims=True), jnp.max(sm, axis=1, keepdims=True))
    acc = jnp.dot(jnp.exp2(sm - m).astype(BF16), vm, preferred_element_type=F32)
    for c in range(nblk):
        p = jnp.exp2(blocks[c] - m).astype(BF16)
        acc = acc + jnp.dot(p, v[KEY_BLOCK * c:KEY_BLOCK * (c + 1), :], preferred_element_type=F32)
    return acc[:, 0:V_DIM] / acc[:, V_DIM:V_EXT]


def _mla_attn_body(q_ref, k_ref, v_ref, km_ref, vm_ref, o_ref):
    for h in range(q_ref.shape[1]):
        for i in range(q_ref.shape[2] // Q_SUB_MLA):
            rows = pl.ds(i * Q_SUB_MLA, Q_SUB_MLA)
            o = _softmax_pv(q_ref[0, h, rows, :], k_ref.at[0, h], v_ref.at[0, h], km_ref[0, h], vm_ref[0, h])
            o_ref[rows, V_DIM * h:V_DIM * (h + 1)] = o.astype(o_ref.dtype)


def _gqa_attn_body(q_ref, k_ref, v_ref, km_ref, vm_ref, o_ref):
    for g in range(2):
        for i in range(q_ref.shape[2] // Q_SUB_GQA):
            rows = pl.ds(i * Q_SUB_GQA, Q_SUB_GQA)
            o = _softmax_pv(q_ref[0, g, rows, :], k_ref.at[0, 0], v_ref.at[0, 0], km_ref[0, 0], vm_ref[0, 0])
            o_ref[rows, GQA_DIM * g:GQA_DIM * (g + 1)] = o.astype(o_ref.dtype)


def _mla_attention(qm, kmt, vm, kmeta_t, vmeta):
    b, h, s, dk = qm.shape
    tq = Q_TILE_MLA
    nq = s // tq
    hs = MLA_HEADS_PER_STEP
    return pl.pallas_call(
        _mla_attn_body,
        grid=(b, h // hs, nq),
        in_specs=[
            pl.BlockSpec((1, hs, tq, dk), lambda bi, hi, qi: (bi, hi, qi, 0)),
            pl.BlockSpec((1, hs, dk, s), lambda bi, hi, qi: (bi, hi, 0, 0)),
            pl.BlockSpec((1, hs, s, V_EXT), lambda bi, hi, qi: (bi, hi, 0, 0)),
            pl.BlockSpec((1, hs, dk, META_PAD), lambda bi, hi, qi: (0, hi, 0, 0)),
            pl.BlockSpec((1, hs, META_PAD, V_EXT), lambda bi, hi, qi: (0, hi, 0, 0)),
        ],
        out_specs=pl.BlockSpec((tq, hs * V_DIM), lambda bi, hi, qi: (bi * nq + qi, hi)),
        out_shape=jax.ShapeDtypeStruct((b * s, h * V_DIM), BF16),
        compiler_params=pltpu.CompilerParams(dimension_semantics=("parallel", "parallel", "parallel"),
                                             vmem_limit_bytes=VMEM_LIMIT),
        name="mla_attn",
    )(qm, kmt, vm, kmeta_t, vmeta)


def _gqa_attention(qg, kgt, vg, kmeta_t, vmeta):
    b, h, s, d = qg.shape
    hk = kgt.shape[1]
    tq = Q_TILE_GQA
    nq = s // tq
    return pl.pallas_call(
        _gqa_attn_body,
        grid=(b, hk, nq),
        in_specs=[
            pl.BlockSpec((1, 2, tq, d), lambda bi, ji, qi: (bi, ji, qi, 0)),
            pl.BlockSpec((1, 1, d, s), lambda bi, ji, qi: (bi, ji, 0, 0)),
            pl.BlockSpec((1, 1, s, V_EXT), lambda bi, ji, qi: (bi, ji, 0, 0)),
            pl.BlockSpec((1, 1, d, META_PAD), lambda bi, ji, qi: (0, ji, 0, 0)),
            pl.BlockSpec((1, 1, META_PAD, V_EXT), lambda bi, ji, qi: (0, ji, 0, 0)),
        ],
        out_specs=pl.BlockSpec((tq, 2 * d), lambda bi, ji, qi: (bi * nq + qi, ji)),
        out_shape=jax.ShapeDtypeStruct((b * s, h * d), BF16),
        compiler_params=pltpu.CompilerParams(dimension_semantics=("parallel", "parallel", "parallel"),
                                             vmem_limit_bytes=VMEM_LIMIT),
        name="gqa_attn",
    )(qg, kgt, vg, kmeta_t, vmeta)


def _to_tiles(ref, x):
    r = x.shape[0]
    for s in range(x.shape[1] // LANES):
        ref[:, s] = x[:, LANES * s:LANES * (s + 1)].reshape(r // SUBLANES, SUBLANES, LANES)


def _from_tiles(ref):
    r = ref.shape[0] * SUBLANES
    return jnp.concatenate([ref[:, s].reshape(r, LANES) for s in range(ref.shape[1])], axis=1)


MERGE_SUB = 256


def _pack_rows(v):
    c = v.shape[1] // 2
    lo = lax.bitcast_convert_type(v[:, 0:c].astype(BF16).astype(F32), I32)
    hi = lax.bitcast_convert_type(v[:, c:2 * c].astype(BF16).astype(F32), I32)
    return lax.shift_right_logical(lo, 16) | hi


def _unpack_words(w):
    return (lax.bitcast_convert_type(w << 16, F32), lax.bitcast_convert_type(w & jnp.int32(-65536), F32))


def _merge_body(om_ref, og_ref, x_ref, lng_ref, lnb_ref, gom_ref, gog_ref, wo_ref, l1g_ref, l1b_ref,
                wr_ref, br_ref, h1_ref, h1p_ref, topi_ref, gate_ref):
    for i in range(x_ref.shape[0] // MERGE_SUB):
        rows = pl.ds(i * MERGE_SUB, MERGE_SUB)
        h1, ti, gt = _merge_chain(om_ref[rows, :], og_ref[rows, :], x_ref[rows, :], lng_ref[...], lnb_ref[...],
                                  gom_ref[...], gog_ref[...], wo_ref, l1g_ref[...], l1b_ref[...], wr_ref, br_ref[...])
        blocks = pl.ds(i * (MERGE_SUB // SUBLANES), MERGE_SUB // SUBLANES)
        _to_tiles(h1_ref.at[blocks], h1)
        _to_tiles(h1p_ref.at[blocks], _pack_rows(h1))
        topi_ref[0, :, rows] = ti
        for k in range(TOP_K):
            gate_ref[0, :, pl.ds(k * TOKEN_TILE + i * MERGE_SUB, MERGE_SUB)] = gt[k:k + 1, :]


def _merge_chain(om, og, x, lng, lnb, gom, gog, wo_ref, l1g, l1b, wr_ref, br):
    t = x.shape[0]
    h0 = _layernorm(x, lng, lnb)
    nm = _rmsnorm(om.astype(F32), gom).astype(BF16)
    ng = _rmsnorm(og.astype(F32), gog).astype(BF16)
    half = nm.shape[1]
    mix = jnp.dot(nm, wo_ref[0:half, :], preferred_element_type=F32)
    mix = mix + jnp.dot(ng, wo_ref[half:2 * half, :], preferred_element_type=F32)
    h1 = _layernorm(DEEPNORM_ALPHA * h0 + mix, l1g, l1b)

    hi = h1.astype(BF16)
    lo = (h1 - hi.astype(F32)).astype(BF16)
    acc = jnp.dot(hi, wr_ref[...], preferred_element_type=F32) + jnp.dot(lo, wr_ref[...], preferred_element_type=F32)
    logits = acc[:, 0:LANES] + acc[:, LANES:2 * LANES] + br
    cur = logits.T[0:N_EXPERTS, :]
    eidx = lax.broadcasted_iota(I32, cur.shape, 0)
    vals, idxs = [], []
    for _ in range(TOP_K):
        m = jnp.max(cur, axis=0, keepdims=True)
        i = jnp.min(jnp.where(cur == m, eidx, N_EXPERTS), axis=0, keepdims=True)
        vals.append(m)
        idxs.append(i)
        cur = jnp.where(eidx == i, -jnp.inf, cur)
    ex = [jnp.exp(v - vals[0]) for v in vals]
    den = ex[0] + ex[1] + ex[2] + ex[3]
    sub = lax.broadcasted_iota(I32, (8, t), 0)
    ti = jnp.zeros((8, t), I32)
    gt = jnp.zeros((8, t), F32)
    for k in range(TOP_K):
        ti = jnp.where(sub == k, idxs[k], ti)
        gt = jnp.where(sub == k, ex[k] / den, gt)
    return h1, ti[0:TOP_K, :], gt[0:TOP_K, :]


def _merge(o_mla, o_gqa, x2, w):
    n, d = x2.shape
    tile = TOKEN_TILE
    half = o_mla.shape[1]
    full = lambda shape: pl.BlockSpec(shape, lambda i: (0,) * len(shape))
    row = lambda width: pl.BlockSpec((tile, width), lambda i: (i, 0))
    return pl.pallas_call(
        _merge_body,
        grid=(n // tile,),
        in_specs=[row(half), row(half), row(d), full((1, d)), full((1, d)), full((1, half)), full((1, half)),
                  full((d, d)), full((1, d)), full((1, d)), full((d, 2 * LANES)), full((1, LANES))],
        out_specs=[pl.BlockSpec((tile // SUBLANES, d // LANES, SUBLANES, LANES), lambda i: (i, 0, 0, 0)),
                   pl.BlockSpec((tile // SUBLANES, PACK_CHUNKS, SUBLANES, LANES), lambda i: (i, 0, 0, 0)),
                   pl.BlockSpec((1, TOP_K, tile), lambda i: (i, 0, 0)),
                   pl.BlockSpec((1, 1, TOP_K * tile), lambda i: (i, 0, 0))],
        out_shape=[jax.ShapeDtypeStruct((n // SUBLANES, d // LANES, SUBLANES, LANES), F32),
                   jax.ShapeDtypeStruct((n // SUBLANES, PACK_CHUNKS, SUBLANES, LANES), I32),
                   jax.ShapeDtypeStruct((n // tile, TOP_K, tile), I32),
                   jax.ShapeDtypeStruct((n // tile, 1, TOP_K * tile), F32)],
        compiler_params=pltpu.CompilerParams(dimension_semantics=("parallel",), vmem_limit_bytes=VMEM_LIMIT),
        name="merge",
    )(o_mla, o_gqa, x2, w["ln_emb_g"], w["ln_emb_b"], w["g_o_mla"], w["g_o_gqa"], w["w_o"],
      w["ln1_g"], w["ln1_b"], w["w_router"], w["b_router"])


def _lanes_from_sublanes(col):
    diag = lax.broadcasted_iota(I32, col.shape, 0) == lax.broadcasted_iota(I32, col.shape, 1)
    return jnp.sum(jnp.where(diag, col, 0.0), axis=0, keepdims=True)


def _positions_body(topi_ref, upper_ref, lpb_ref, gstart_ref, nslab_ref, lstart_ref, texp_ref, nvalid_ref, ecnt_ref,
                    eoff_ref, *, ntp):
    ntile, _, tl = topi_ref.shape
    r = lax.broadcasted_iota(I32, (N_EXPERTS, N_EXPERTS), 0)
    c = lax.broadcasted_iota(I32, (N_EXPERTS, N_EXPERTS), 1)
    lower = (c < r).astype(F32)
    eidx = lax.broadcasted_iota(I32, (N_EXPERTS, tl), 0)

    def tile_onehots(j):
        topi = topi_ref[j]
        ohs = [eidx == topi[k:k + 1, :] for k in range(TOP_K)]
        onehot = ohs[0].astype(F32) + ohs[1].astype(F32) + ohs[2].astype(F32) + ohs[3].astype(F32)
        cnt = jnp.sum(onehot, axis=1, keepdims=True)
        run = jnp.floor((cnt + (SUBLANES - 1)) * (1.0 / SUBLANES)) * SUBLANES
        return ohs, onehot, jnp.broadcast_to(run, (N_EXPERTS, LANES))

    tot = lax.fori_loop(0, ntile, lambda j, acc: acc + tile_onehots(j)[2], jnp.zeros((N_EXPERTS, LANES), F32))
    pc = jnp.floor((tot + (MOE_TILE - 1)) * (1.0 / MOE_TILE)) * MOE_TILE
    off = jnp.dot(lower, pc, precision=lax.Precision.HIGHEST, preferred_element_type=F32)
    cumend = off + pc
    tstart = lax.broadcasted_iota(I32, (N_EXPERTS, ntp), 1).astype(F32) * MOE_TILE
    te = jnp.sum((jnp.broadcast_to(cumend[:, 0:1], (N_EXPERTS, ntp)) <= tstart).astype(I32), axis=0, keepdims=True)
    texp_ref[...] = jnp.minimum(te, N_EXPERTS - 1)
    nvalid_ref[...] = (cumend[N_EXPERTS - 1:N_EXPERTS, :] * (1.0 / MOE_TILE)).astype(I32)
    ecnt_ref[...] = _lanes_from_sublanes(tot).astype(I32)
    eoff_ref[...] = _lanes_from_sublanes(off).astype(I32)

    def place(j, carry):
        ohs, onehot, tile_run = tile_onehots(j)
        before = jnp.dot(onehot.astype(BF16), upper_ref[...], preferred_element_type=F32)
        loff = jnp.dot(lower, tile_run, precision=lax.Precision.HIGHEST, preferred_element_type=F32)
        base = before + loff[:, 0:1]
        sub = lax.broadcasted_iota(I32, (SUBLANES, tl), 0)
        out = jnp.zeros((SUBLANES, tl), F32)
        for k in range(TOP_K):
            pk = jnp.sum(jnp.where(ohs[k], base, 0.0), axis=0, keepdims=True)
            out = jnp.where(sub == k, pk, out)
        lp = out[0:TOP_K, :].astype(I32)
        lpb = (lp >> SUBLANE_SHIFT) * PSLAB + (lp & (SUBLANES - 1))
        lpb_ref[j] = jnp.concatenate([lpb[k:k + 1, :] for k in range(TOP_K)], axis=1)
        inv = 1.0 / SUBLANES
        gstart_ref[j] = (_lanes_from_sublanes(off + carry) * inv).astype(I32)
        nslab_ref[j] = (_lanes_from_sublanes(tile_run) * inv).astype(I32)
        lstart_ref[j] = (_lanes_from_sublanes(loff) * inv).astype(I32)
        return carry + tile_run

    lax.fori_loop(0, ntile, place, jnp.zeros((N_EXPERTS, LANES), F32))


def _positions(topi, ntp):
    ntile, _, tl = topi.shape
    tab = jax.ShapeDtypeStruct((ntile, 1, LANES), I32)
    small = jax.ShapeDtypeStruct((1, LANES), I32)
    upper = (jnp.arange(tl, dtype=I32)[:, None] < jnp.arange(tl, dtype=I32)[None, :]).astype(BF16)
    return pl.pallas_call(
        functools.partial(_positions_body, ntp=ntp),
        out_shape=[jax.ShapeDtypeStruct((ntile, 1, TOP_K * tl), I32), tab, tab, tab,
                   jax.ShapeDtypeStruct((1, ntp), I32), small, small, small],
        compiler_params=pltpu.CompilerParams(vmem_limit_bytes=VMEM_LIMIT),
        name="positions",
    )(topi, upper)


STAGE_ROWS = TOKEN_TILE * TOP_K + N_EXPERTS * SUBLANES
STAGE_FLAT = STAGE_ROWS * PACK_CHUNKS


def _load_row(ref, flat_start, chunks=SUBLANES):
    return ref[pl.ds(flat_start, chunks, stride=SUBLANES), :]


def _store_row(ref, flat_start, v):
    ref[pl.ds(flat_start, v.shape[0], stride=SUBLANES), :] = v


CHUNK_SHIFT = 2
CHUNK_SLABS = 1 << CHUNK_SHIFT


def _copy_run(src_ref, src0, dst_ref, dst0, nslab, sem):
    done = jnp.int32(0)
    left = nslab
    for shift in (2 * CHUNK_SHIFT, CHUNK_SHIFT, 0):
        size = (1 << shift) * PSLAB
        count = left >> shift

        def start(s, c, size=size, base=done):
            o = base + s * size
            pltpu.make_async_copy(src_ref.at[pl.ds(src0 + o, size)], dst_ref.at[pl.ds(dst0 + o, size)], sem).start()
            return c

        lax.fori_loop(0, count, start, 0)
        done = done + count * size
        left = left & ((1 << shift) - 1)


WAIT_SHIFT = 4


def _wait_slabs(src_ref, dst_ref, nslab, sem):
    big = (1 << WAIT_SHIFT) * PSLAB

    def wait_big(s, c):
        pltpu.make_async_copy(src_ref.at[pl.ds(0, big)], dst_ref.at[pl.ds(0, big)], sem).wait()
        return c

    def wait_small(s, c):
        pltpu.make_async_copy(src_ref.at[pl.ds(0, PSLAB)], dst_ref.at[pl.ds(0, PSLAB)], sem).wait()
        return c

    lax.fori_loop(0, nslab >> WAIT_SHIFT, wait_big, 0)
    lax.fori_loop(0, nslab & ((1 << WAIT_SHIFT) - 1), wait_small, 0)


def _total_slabs(nslab_ref):
    return lax.fori_loop(0, N_EXPERTS, lambda e, c: c + nslab_ref[0, 0, e], jnp.int32(0))


def _dispatch_body(ecnt_ref, eoff_ref, lpb_ref, gstart_ref, nslab_ref, lstart_ref, h_ref, xs_ref,
                   stg0_ref, stg1_ref, zero_ref, issued_ref, sem, zsem):
    g = pl.program_id(0)
    ng = pl.num_programs(0)
    stages = (stg0_ref, stg1_ref)

    def drain(j):
        _wait_slabs(stages[j], xs_ref, issued_ref[j], sem.at[j])

    def run(j):
        stg_ref = stages[j]

        @pl.when(g >= 2)
        def _():
            drain(j)

        def zero_last(e, c):
            last = jnp.maximum(lstart_ref[0, 0, e] + nslab_ref[0, 0, e] - 1, 0)
            stg_ref[pl.ds(last * PSLAB, PSLAB), :] = jnp.zeros((PSLAB, LANES), I32)
            return c

        lax.fori_loop(0, N_EXPERTS, zero_last, 0)

        def move(i, c):
            for u in range(TOKEN_UNROLL):
                blk = i * (TOKEN_UNROLL // SUBLANES) + u // SUBLANES
                v = _load_row(h_ref, blk * PSLAB + u % SUBLANES, PACK_CHUNKS)
                for k in range(TOP_K):
                    _store_row(stg_ref, lpb_ref[0, 0, k * TOKEN_TILE + i * TOKEN_UNROLL + u], v)
            return c

        lax.fori_loop(0, TOKEN_TILE // TOKEN_UNROLL, move, 0)

        def send_run(e, c):
            _copy_run(stg_ref, lstart_ref[0, 0, e] * PSLAB, xs_ref, gstart_ref[0, 0, e] * PSLAB,
                      nslab_ref[0, 0, e], sem.at[j])
            return c

        lax.fori_loop(0, N_EXPERTS, send_run, 0)
        issued_ref[j] = _total_slabs(nslab_ref)

    for j in range(2):
        @pl.when(lax.rem(g, 2) == j)
        def _():
            run(j)

    @pl.when(g == ng - 1)
    def _():
        zero_ref[...] = jnp.zeros_like(zero_ref)

        def zero_copy(dst_slab):
            return pltpu.make_async_copy(zero_ref, xs_ref.at[pl.ds(dst_slab * PSLAB, PSLAB)], zsem)

        def pad_segment(e, c):
            rows = ecnt_ref[e]
            first = (eoff_ref[e] + rows) >> SUBLANE_SHIFT
            npad = lax.rem(MOE_TILE - lax.rem(rows, MOE_TILE), MOE_TILE) >> SUBLANE_SHIFT
            lax.fori_loop(0, npad, lambda s, cc: (zero_copy(first + s).start(), cc)[1], 0)
            lax.fori_loop(0, npad, lambda s, cc: (zero_copy(first + s).wait(), cc)[1], 0)
            return c

        lax.fori_loop(0, N_EXPERTS, pad_segment, 0)
        for j in range(2):
            @pl.when(jnp.logical_and(ng > 1, lax.rem(g, 2) != j))
            def _():
                drain(j)

        for j in range(2):
            @pl.when(lax.rem(g, 2) == j)
            def _():
                drain(j)


def _dispatch(h1_flat, lpb, gstart, nslab, lstart, ecnt, eoff, rows_pad):
    nflat = h1_flat.shape[0]
    t = TOKEN_TILE
    smem = lambda shape, imap: pl.BlockSpec(shape, imap, memory_space=pltpu.SMEM)
    per_tile = smem((1, 1, LANES), lambda i, c, o: (i, 0, 0))
    grid_spec = pltpu.PrefetchScalarGridSpec(
        num_scalar_prefetch=2,
        grid=(nflat // (t * PACK_CHUNKS),),
        in_specs=[smem((1, 1, TOP_K * t), lambda i, c, o: (i, 0, 0)), per_tile, per_tile, per_tile,
                  pl.BlockSpec((t * PACK_CHUNKS, LANES), lambda i, c, o: (i, 0))],
        out_specs=pl.BlockSpec(memory_space=pl.ANY),
        scratch_shapes=[pltpu.VMEM((STAGE_FLAT, LANES), I32), pltpu.VMEM((STAGE_FLAT, LANES), I32),
                        pltpu.VMEM((PSLAB, LANES), I32),
                        pltpu.SMEM((2,), I32), pltpu.SemaphoreType.DMA((2,)), pltpu.SemaphoreType.DMA],
    )
    return pl.pallas_call(
        _dispatch_body,
        grid_spec=grid_spec,
        out_shape=jax.ShapeDtypeStruct((rows_pad * PACK_CHUNKS, LANES), I32),
        compiler_params=pltpu.CompilerParams(dimension_semantics=("arbitrary",), vmem_limit_bytes=VMEM_LIMIT),
        name="dispatch",
    )(ecnt, eoff, lpb, gstart, nslab, lstart, h1_flat)


def _ffn_body(texp_ref, nvalid_ref, ecnt_ref, eoff_ref, x_ref, wgu_hbm, wd_hbm, bg_ref, bl_ref, bd_ref, y_ref,
              wgu_buf, wd_buf, wg_sc, wl_sc, wd_sc, seg_ref, sem):
    i = pl.program_id(0)
    valid = i < nvalid_ref[0]
    e = texp_ref[i]
    first_of_expert = jnp.logical_or(i == 0, e != texp_ref[jnp.maximum(i - 1, 0)])

    def weight_copies(expert, slot):
        return (pltpu.make_async_copy(wgu_hbm.at[expert], wgu_buf.at[slot], sem.at[slot]),
                pltpu.make_async_copy(wd_hbm.at[expert], wd_buf.at[slot], sem.at[slot]))

    def prepare(slot):
        blk = 2 * LANES
        r = lax.broadcasted_iota(I32, (blk, blk), 0)
        c = lax.broadcasted_iota(I32, (blk, blk), 1)
        sel = (r == jnp.where(c < LANES, 2 * c, 2 * (c - LANES) + 1)).astype(BF16)
        for b in range(wgu_buf.shape[2] // blk):
            wb = wgu_buf[slot, :, blk * b:blk * (b + 1)].astype(BF16)
            y = jnp.dot(wb, sel, preferred_element_type=F32).astype(BF16)
            wg_sc[:, LANES * b:LANES * (b + 1)] = y[:, 0:LANES]
            wl_sc[:, LANES * b:LANES * (b + 1)] = y[:, LANES:blk]
        wd_sc[...] = wd_buf[slot].astype(BF16)

    @pl.when(jnp.logical_and(valid, first_of_expert))
    def _():
        @pl.when(i == 0)
        def _():
            seg_ref[0] = 0
            for cp in weight_copies(e, 0):
                cp.start()

        seg = seg_ref[0]
        seg_ref[0] = seg + 1
        nxt = lax.while_loop(lambda n: jnp.logical_and(n < N_EXPERTS, ecnt_ref[jnp.minimum(n, N_EXPERTS - 1)] == 0),
                             lambda n: n + 1, e + 1)
        for slot in range(2):
            @pl.when(lax.rem(seg, 2) == slot)
            def _():
                for cp in weight_copies(e, slot):
                    cp.wait()

                @pl.when(nxt < N_EXPERTS)
                def _():
                    for cp in weight_copies(nxt, 1 - slot):
                        cp.start()

                prepare(slot)

    def chain(b0, nb):
        blocks = pl.ds(b0, nb)
        x = jnp.concatenate(_unpack_words(_from_tiles(x_ref.at[blocks])), axis=1).astype(BF16)
        hg = jnp.dot(x, wg_sc[...], preferred_element_type=F32) + bg_ref[0]
        hl = jnp.dot(x, wl_sc[...], preferred_element_type=F32) + bl_ref[0]
        g = jnp.minimum(hg, SWIGLU_LIMIT)
        lin = jnp.clip(hl, -SWIGLU_LIMIT, SWIGLU_LIMIT)
        act = g * (1.0 / (1.0 + jnp.exp(-SWIGLU_ALPHA * g))) * (lin + 1.0)
        y = jnp.dot(act.astype(BF16), wd_sc[...], preferred_element_type=F32) + bd_ref[0]
        _to_tiles(y_ref.at[blocks], _pack_rows(y))

    def compute(nrows):
        done = 0
        while done < nrows:
            n = min(FFN_CHAIN, nrows - done)
            chain(done // SUBLANES, n // SUBLANES)
            done += n
        rest = y_ref.shape[0] - nrows // SUBLANES
        if rest:
            y_ref[pl.ds(nrows // SUBLANES, rest)] = jnp.zeros((rest,) + y_ref.shape[1:], I32)

    used = eoff_ref[e] + ecnt_ref[e] - i * MOE_TILE
    quarter = MOE_TILE // 4
    for q in range(1, 5):
        lo, hi = (q - 1) * quarter, q * quarter
        in_range = jnp.logical_and(used > lo, used <= hi) if q < 4 else used > lo

        @pl.when(jnp.logical_and(valid, in_range))
        def _():
            compute(hi)


def _grouped_ffn(xs_tiles, texp, nvalid, ecnt, eoff, wgu, wd, bg, bl, bd, ntiles):
    d = wgu.shape[1]
    tm = MOE_TILE
    f = wd.shape[1]
    xmap = lambda i, te, nv, ec, eo: (jnp.minimum(i, nv[0] - 1), 0, 0, 0)
    wmap = lambda i, te, nv, ec, eo: (te[i], 0, 0)
    rows_blk = pl.BlockSpec((tm // SUBLANES, PACK_CHUNKS, SUBLANES, LANES), xmap)
    hbm = pl.BlockSpec(memory_space=pl.ANY)
    grid_spec = pltpu.PrefetchScalarGridSpec(
        num_scalar_prefetch=4,
        grid=(ntiles,),
        in_specs=[rows_blk, hbm, hbm,
                  pl.BlockSpec((1, 1, f), wmap), pl.BlockSpec((1, 1, f), wmap), pl.BlockSpec((1, 1, d), wmap)],
        out_specs=rows_blk,
        scratch_shapes=[pltpu.VMEM((2, d, 2 * f), F32), pltpu.VMEM((2, f, d), F32),
                        pltpu.VMEM((d, f), BF16), pltpu.VMEM((d, f), BF16), pltpu.VMEM((f, d), BF16),
                        pltpu.SMEM((1,), I32), pltpu.SemaphoreType.DMA((2,))],
    )
    return pl.pallas_call(
        _ffn_body,
        grid_spec=grid_spec,
        out_shape=jax.ShapeDtypeStruct(xs_tiles.shape, I32),
        compiler_params=pltpu.CompilerParams(dimension_semantics=("arbitrary",), vmem_limit_bytes=VMEM_LIMIT),
        name="ffn",
    )(texp, nvalid, ecnt, eoff, xs_tiles, wgu, wd, bg, bl, bd)


COMBINE_SUB = 512


def _combine_body(lpb_ref, gate_ref, gstart_ref, nslab_ref, lstart_ref, gstart2_ref, nslab2_ref, lstart2_ref,
                  h_ref, l2g_ref, l2b_ref, ys_ref, o_ref, stg0_ref, stg1_ref, moe_ref, sem):
    g = pl.program_id(0)
    ng = pl.num_programs(0)
    stages = (stg0_ref, stg1_ref)

    def fetch(gs_ref, ns_ref, ls_ref, j):
        def per_expert(e, c):
            _copy_run(ys_ref, gs_ref[0, 0, e] * PSLAB, stages[j], ls_ref[0, 0, e] * PSLAB, ns_ref[0, 0, e], sem.at[j])
            return c

        lax.fori_loop(0, N_EXPERTS, per_expert, 0)

    def wait_tile(j):
        _wait_slabs(ys_ref, stages[j], _total_slabs(nslab_ref), sem.at[j])

    def gather(j):
        stg_ref = stages[j]

        def body(i, c):
            for u in range(TOKEN_UNROLL):
                tok = i * TOKEN_UNROLL + u
                row0 = (i * (TOKEN_UNROLL // SUBLANES) + u // SUBLANES) * SLAB + u % SUBLANES
                acc_lo = acc_hi = None
                for k in range(TOP_K):
                    gk = gate_ref[0, 0, k * TOKEN_TILE + tok]
                    lo, hi = _unpack_words(_load_row(stg_ref, lpb_ref[0, 0, k * TOKEN_TILE + tok], PACK_CHUNKS))
                    acc_lo = gk * lo if k == 0 else acc_lo + gk * lo
                    acc_hi = gk * hi if k == 0 else acc_hi + gk * hi
                _store_row(moe_ref, row0, acc_lo)
                _store_row(moe_ref, row0 + PSLAB, acc_hi)
            return c

        lax.fori_loop(0, TOKEN_TILE // TOKEN_UNROLL, body, 0)

    def finish():
        nblk = COMBINE_SUB // SUBLANES
        for part in range(TOKEN_TILE // COMBINE_SUB):
            moe = jnp.concatenate(
                [jnp.concatenate([moe_ref[pl.ds((part * nblk + i) * SLAB + s * SUBLANES, SUBLANES), :]
                                  for i in range(nblk)], axis=0) for s in range(SUBLANES)], axis=1)
            h1 = _from_tiles(h_ref.at[pl.ds(part * nblk, nblk)])
            o_ref[pl.ds(part * COMBINE_SUB, COMBINE_SUB), :] = _layernorm(DEEPNORM_ALPHA * h1 + moe, l2g_ref[...],
                                                                          l2b_ref[...])

    @pl.when(g == 0)
    def _():
        fetch(gstart_ref, nslab_ref, lstart_ref, 0)

    for j in range(2):
        @pl.when(lax.rem(g, 2) == j)
        def _():
            @pl.when(g + 1 < ng)
            def _():
                fetch(gstart2_ref, nslab2_ref, lstart2_ref, 1 - j)

            wait_tile(j)
            gather(j)

    finish()


def _combine(lpb, gates, gstart, nslab, lstart, h1_tiles, ys_flat, w):
    d = h1_tiles.shape[1] * LANES
    n = h1_tiles.shape[0] * SUBLANES
    t = TOKEN_TILE
    ng = n // t
    full = lambda shape: pl.BlockSpec(shape, lambda i: (0,) * len(shape))
    smem = lambda shape, imap: pl.BlockSpec(shape, imap, memory_space=pltpu.SMEM)
    vec = smem((1, 1, TOP_K * t), lambda i: (i, 0, 0))
    this_step = smem((1, 1, LANES), lambda i: (i, 0, 0))
    next_step = smem((1, 1, LANES), lambda i: (jnp.minimum(i + 1, ng - 1), 0, 0))
    return pl.pallas_call(
        _combine_body,
        grid=(ng,),
        in_specs=[vec, vec, this_step, this_step, this_step, next_step, next_step, next_step,
                  pl.BlockSpec((t // SUBLANES, d // LANES, SUBLANES, LANES), lambda i: (i, 0, 0, 0)),
                  full((1, d)), full((1, d)),
                  pl.BlockSpec(memory_space=pl.ANY)],
        out_specs=pl.BlockSpec((t, d), lambda i: (i, 0)),
        out_shape=jax.ShapeDtypeStruct((n, d), F32),
        scratch_shapes=[pltpu.VMEM((STAGE_FLAT, LANES), I32), pltpu.VMEM((STAGE_FLAT, LANES), I32),
                        pltpu.VMEM((TOKEN_TILE * SUBLANES, LANES), F32), pltpu.SemaphoreType.DMA((2,))],
        compiler_params=pltpu.CompilerParams(dimension_semantics=("arbitrary",), vmem_limit_bytes=VMEM_LIMIT),
        name="combine",
    )(lpb, gates, gstart, nslab, lstart, gstart, nslab, lstart, h1_tiles, w["ln2_g"], w["ln2_b"], ys_flat)


def _rope_tables(pos_1d, row, col, g_k):
    inv = ROPE_THETA ** (-jnp.arange(0, ROPE_DIM, 2, dtype=F32) / ROPE_DIM)

    def cs(p):
        ang = p.astype(F32)[:, None] * inv[None, :]
        return jnp.cos(ang), jnp.sin(ang)

    c1, s1 = cs(pos_1d)
    cr, sr = cs(row)
    cc, sc = cs(col)
    c1w, s1w = jnp.concatenate([c1, c1, c1, c1], axis=1), jnp.concatenate([-s1, s1, -s1, s1], axis=1)
    caw, saw = jnp.concatenate([cr, cr, cc, cc], axis=1), jnp.concatenate([-sr, sr, -sc, sc], axis=1)
    ckt = caw.T * g_k[:, None]
    skt = saw.T * _swap_halves64(g_k)[:, None]
    return (c1w, s1w, caw, saw, c1w[:, :ROPE_DIM].T, s1w[:, :ROPE_DIM].T, ckt, skt)


def _prep_weights(ln_emb_g, ln_emb_b, w_in, g_q_a, w_q_b, g_kv_a, w_kv_b, g_q_gqa, g_k_gqa, g_o_mla, g_o_gqa, w_o,
                  ln1_g, ln1_b, w_router, b_router, ln2_g, ln2_b):
    r2 = lambda v: v.reshape(1, -1).astype(F32)
    o = np.cumsum([0, Q_LORA, KV_LORA, ROPE_DIM, GQA_HEADS * GQA_DIM, GQA_KV_HEADS * GQA_DIM, GQA_KV_HEADS * GQA_DIM])
    wi = w_in[0]
    seg = [wi[:, o[i]:o[i + 1]] for i in range(6)]
    w_in_p = jnp.concatenate([seg[0], seg[1], seg[3], _swap_halves64(seg[3]), seg[5]], axis=1).astype(BF16)
    w_kt = jnp.concatenate([seg[2], seg[4]], axis=1).T.astype(BF16)
    wq = w_q_b[0].reshape(Q_LORA, MLA_HEADS, QK_DIM)
    wq_rope = wq[:, :, NOPE_DIM:].reshape(Q_LORA, -1)
    w_qb = jnp.concatenate([wq[:, :, :NOPE_DIM].reshape(Q_LORA, -1), wq_rope, _swap_halves64(wq_rope)],
                           axis=1).astype(BF16)
    wk = w_kv_b[0].reshape(KV_LORA, MLA_HEADS, NOPE_DIM + V_DIM)
    w_kvb_kt = wk[:, :, :NOPE_DIM].reshape(KV_LORA, -1).T.astype(BF16)
    w_kvb_v = wk[:, :, NOPE_DIM:].reshape(KV_LORA, -1).astype(BF16)
    wr = jnp.pad(w_router[0].astype(F32), ((0, 0), (0, LANES - N_EXPERTS)))
    wr_hi = wr.astype(BF16)
    w_router_p = jnp.concatenate([wr_hi, (wr - wr_hi.astype(F32)).astype(BF16)], axis=1)
    b_router_p = jnp.pad(b_router[0].astype(F32), (0, LANES - N_EXPERTS), constant_values=NEG_BIG).reshape(1, LANES)
    return dict(
        ln_emb_g=r2(ln_emb_g), ln_emb_b=r2(ln_emb_b), w_in=w_in_p, w_kt=w_kt, g_q_a=r2(g_q_a[0]), w_qb=w_qb,
        g_kv_a=r2(g_kv_a[0]), w_kvb_v=w_kvb_v, w_kvb_kt=w_kvb_kt, g_q_gqa=r2(g_q_gqa[0]),
        g_q_gqa_sw=_swap_halves64(r2(g_q_gqa[0])),
        g_o_mla=r2(g_o_mla[0]), g_o_gqa=r2(g_o_gqa[0]), w_o=w_o[0].astype(BF16), ln1_g=r2(ln1_g[0]),
        ln1_b=r2(ln1_b[0]), w_router=w_router_p, b_router=b_router_p, ln2_g=r2(ln2_g[0]), ln2_b=r2(ln2_b[0]))


def kernel(x, meta_tokens, ln_emb_g, ln_emb_b, w_in, g_q_a, w_q_b, g_kv_a, w_kv_b, g_q_gqa, g_k_gqa, g_o_mla, g_o_gqa,
           w_o, ln1_g, ln1_b, w_router, b_router, w_gate_up, b_gate_up, w_down, b_down, ln2_g, ln2_b):
    b, s, d = x.shape
    n = b * s
    assert d == D_MODEL and meta_tokens.shape == (N_META, d) and s % GRID_W == 0
    assert s % ROW_TILE == 0 and s % Q_TILE_MLA == 0 and s % Q_TILE_GQA == 0 and s % KEY_BLOCK == 0
    assert n % TOKEN_TILE == 0 and TOKEN_TILE % MERGE_SUB == 0 and TOKEN_TILE % COMBINE_SUB == 0
    w = _prep_weights(ln_emb_g, ln_emb_b, w_in, g_q_a, w_q_b, g_kv_a, w_kv_b, g_q_gqa, g_k_gqa, g_o_mla, g_o_gqa,
                      w_o, ln1_g, ln1_b, w_router, b_router, ln2_g, ln2_b)

    tok = jnp.arange(s, dtype=I32)
    g_k = g_k_gqa[0].astype(F32)
    tabs_real = _rope_tables(tok + N_META, tok // GRID_W, tok % GRID_W, g_k)
    mt = jnp.arange(META_PAD, dtype=I32)
    tabs_meta = _rope_tables(mt, jnp.full((META_PAD,), -1, I32), mt, g_k)

    qm, km, vm, qg, kg, vg = _project(x, tabs_real, w, ROW_TILE)
    meta = jnp.pad(meta_tokens.astype(x.dtype), ((0, META_PAD - N_META), (0, 0))).reshape(1, META_PAD, d)
    _, km_m, vm_m, _, kg_m, vg_m = _project(meta, tabs_meta, w, META_PAD)

    o_mla = _mla_attention(qm, km, vm, km_m, vm_m)
    o_gqa = _gqa_attention(qg, kg, vg, kg_m, vg_m)

    h1_tiles, h1_packed, topi, gates = _merge(o_mla, o_gqa, x.reshape(n, d), w)

    run_rows = n * TOP_K + (n // TOKEN_TILE) * N_EXPERTS * (SUBLANES - 1)
    ntiles = -(-run_rows // MOE_TILE) + N_EXPERTS
    ntp = -(-ntiles // LANES) * LANES
    rows_pad = ntiles * MOE_TILE
    lpb, gstart, nslab, lstart, texp, nvalid, ecnt, eoff = _positions(topi, ntp)
    flat = lambda a: a.reshape(-1, LANES)
    xs_flat = _dispatch(flat(h1_packed), lpb, gstart, nslab, lstart, ecnt.reshape(-1), eoff.reshape(-1), rows_pad)

    bgu = b_gate_up[0].astype(F32).reshape(N_EXPERTS, D_FF, 2).transpose(2, 0, 1).reshape(2, N_EXPERTS, 1, D_FF)
    tiles = lambda a: a.reshape(-1, PACK_CHUNKS, SUBLANES, LANES)
    ys_tiles = _grouped_ffn(tiles(xs_flat), texp.reshape(-1), nvalid.reshape(-1), ecnt.reshape(-1), eoff.reshape(-1),
                            w_gate_up[0], w_down[0], bgu[0], bgu[1],
                            b_down[0].astype(F32).reshape(N_EXPERTS, 1, d), ntiles)

    out = _combine(lpb, gates, gstart, nslab, lstart, h1_tiles, flat(ys_tiles), w)
    return out.reshape(b, s, d)
```
